```python
import math
import jax, jax.numpy as jnp
from jax import lax
import numpy as np

D_MODEL = 2048
BATCH = 4
SEQ = 4096
DEPTH = 1

HEAD_DIM = 128
FOX_HEADS = 8
DIFF_HEADS = 4
DIFF_V_DIM = 2 * HEAD_DIM
FOX_WIDTH = FOX_HEADS * HEAD_DIM
DIFF_QK_WIDTH = DIFF_HEADS * 2 * HEAD_DIM
DIFF_V_WIDTH = DIFF_HEADS * DIFF_V_DIM
N_BRANCHES = 2
IN_SIZES = (FOX_WIDTH, FOX_WIDTH, FOX_WIDTH, FOX_HEADS,
            DIFF_QK_WIDTH, DIFF_QK_WIDTH, DIFF_V_WIDTH, N_BRANCHES * D_MODEL)
IN_WIDTH = sum(IN_SIZES)
ROPE_THETA = 500000.0
ROPE_DIM = HEAD_DIM // 4
Q_BLOCK = 128
N_EXPERTS = 32
TOP_K = 4
D_FF = D_MODEL
SWIGLU_ALPHA = 1.702
SWIGLU_LIMIT = 7.0
MOE_BLOCK = 512
FORGET_BIAS_INIT = 3.0
EPS = 1e-5

kernel_name = "hybrid_fox_diffattn_moe"


def rmsnorm(x, g):
    xf = x.astype(jnp.float32)
    y = xf * lax.rsqrt(jnp.mean(xf * xf, axis=-1, keepdims=True) + EPS)
    return (y * g.astype(jnp.float32)).astype(x.dtype)


def rotary_tables(positions):
    inv_freq = ROPE_THETA ** (-jnp.arange(0, ROPE_DIM, 2, dtype=jnp.float32) / ROPE_DIM)
    ang = positions.astype(jnp.float32)[:, None, :, None] * inv_freq
    return jnp.cos(ang), jnp.sin(ang)


def partial_rotary(t, cos, sin):
    half = ROPE_DIM // 2
    t1 = t[..., :half].astype(jnp.float32)
    t2 = t[..., half:ROPE_DIM].astype(jnp.float32)
    rot = jnp.concatenate([t1 * cos - t2 * sin, t2 * cos + t1 * sin], axis=-1)
    return jnp.concatenate([rot.astype(t.dtype), t[..., ROPE_DIM:]], axis=-1)


def to_heads(t, n_heads, d):
    b, s = t.shape[:2]
    return t.reshape(b, s, n_heads, d).transpose(0, 2, 1, 3)


def to_blocks(t):
    b, h, s = t.shape[:3]
    t = t.reshape(b, h, s // Q_BLOCK, Q_BLOCK, *t.shape[3:])
    return jnp.moveaxis(t, 2, 0)


def from_blocks(t):
    t = jnp.moveaxis(t, 0, 2)
    return t.reshape(t.shape[0], t.shape[1], -1, t.shape[-1])


def causal_block_mask(i, s):
    qpos = i * Q_BLOCK + jnp.arange(Q_BLOCK)
    return jnp.arange(s)[None, :] <= qpos[:, None]


def fox_attention(q, k, v, log_f):
    s = q.shape[2]
    scale = 1.0 / math.sqrt(HEAD_DIM)
    c = jnp.cumsum(log_f, axis=-1)
    nb = s // Q_BLOCK

    def one(args):
        qi, ci, i = args
        sc = jnp.einsum('bhqd,bhkd->bhqk', qi, k, preferred_element_type=jnp.float32) * scale
        sc = sc + ci[..., None] - c[:, :, None, :]
        sc = jnp.where(causal_block_mask(i, s), sc, -jnp.inf)
        p = jax.nn.softmax(sc, axis=-1)
        return jnp.einsum('bhqk,bhkd->bhqd', p.astype(v.dtype), v)

    out = lax.map(one, (to_blocks(q), to_blocks(c), jnp.arange(nb)))
    return from_blocks(out)


def diff_attention(q1, q2, k1, k2, v, lam):
    s = q1.shape[2]
    scale = 1.0 / math.sqrt(HEAD_DIM)
    nb = s // Q_BLOCK

    def one(args):
        q1i, q2i, i = args
        mask = causal_block_mask(i, s)
        s1 = jnp.einsum('bhqd,bhkd->bhqk', q1i, k1, preferred_element_type=jnp.float32) * scale
        s2 = jnp.einsum('bhqd,bhkd->bhqk', q2i, k2, preferred_element_type=jnp.float32) * scale
        p = (jax.nn.softmax(jnp.where(mask, s1, -jnp.inf), axis=-1)
             - lam * jax.nn.softmax(jnp.where(mask, s2, -jnp.inf), axis=-1))
        return jnp.einsum('bhqk,bhkd->bhqd', p.astype(v.dtype), v)

    out = lax.map(one, (to_blocks(q1), to_blocks(q2), jnp.arange(nb)))
    return from_blocks(out)


def mixer_block(u, cos, sin, layer, w_in, b_fgate, b_gate, lam_q1, lam_k1, lam_q2, lam_k2,
                subln_g, w_branch_fox, w_branch_diff, w_out):
    b, s, _ = u.shape
    proj = jnp.dot(u, w_in)
    offs = np.cumsum(IN_SIZES)[:-1].tolist()
    fq, fk, fv, flog, dq, dk, dv, glog = jnp.split(proj, offs, axis=-1)

    log_f = jax.nn.log_sigmoid(flog.astype(jnp.float32) + b_fgate.astype(jnp.float32))
    y_fox = fox_attention(to_heads(fq, FOX_HEADS, HEAD_DIM), to_heads(fk, FOX_HEADS, HEAD_DIM),
                          to_heads(fv, FOX_HEADS, HEAD_DIM), log_f.transpose(0, 2, 1))
    y_fox = y_fox.transpose(0, 2, 1, 3).reshape(b, s, FOX_WIDTH)
    y_a = jnp.dot(y_fox, w_branch_fox)

    dq = dq.reshape(b, s, DIFF_HEADS, 2, HEAD_DIM).transpose(0, 2, 3, 1, 4)
    dk = dk.reshape(b, s, DIFF_HEADS, 2, HEAD_DIM).transpose(0, 2, 3, 1, 4)
    q1 = partial_rotary(dq[:, :, 0], cos, sin)
    q2 = partial_rotary(dq[:, :, 1], cos, sin)
    k1 = partial_rotary(dk[:, :, 0], cos, sin)
    k2 = partial_rotary(dk[:, :, 1], cos, sin)
    lam_init = 0.8 - 0.6 * math.exp(-0.3 * layer)
    lam = (jnp.exp(jnp.sum(lam_q1.astype(jnp.float32) * lam_k1.astype(jnp.float32)))
           - jnp.exp(jnp.sum(lam_q2.astype(jnp.float32) * lam_k2.astype(jnp.float32))) + lam_init)
    y_diff = diff_attention(q1, q2, k1, k2, to_heads(dv, DIFF_HEADS, DIFF_V_DIM), lam)
    y_diff = rmsnorm(y_diff, subln_g) * (1.0 - lam_init)
    y_diff = y_diff.transpose(0, 2, 1, 3).reshape(b, s, DIFF_V_WIDTH)
    y_b = jnp.dot(y_diff, w_branch_diff)

    g = jax.nn.sigmoid(glog + b_gate).reshape(b, s, N_BRANCHES, D_MODEL)
    merged = g[:, :, 0] * y_a + g[:, :, 1] * y_b
    return jnp.dot(merged, w_out)


def moe_ffn(u, router_w, router_b, w_gu, b_gu, w_down, b_down):
    b, s, d = u.shape
    t = b * s
    ut = u.reshape(t, d)
    logits = jnp.dot(ut, router_w, preferred_element_type=jnp.float32) + router_b.astype(jnp.float32)
    top_val, top_idx = lax.top_k(logits, TOP_K)
    gate = jax.nn.softmax(top_val, axis=-1)

    a = t * TOP_K
    flat_e = top_idx.reshape(a).astype(jnp.int32)
    order = jnp.argsort(flat_e)
    sorted_e = flat_e[order]
    counts = jnp.bincount(flat_e, length=N_EXPERTS).astype(jnp.int32)
    padded = (counts + MOE_BLOCK - 1) // MOE_BLOCK * MOE_BLOCK
    start = jnp.cumsum(counts) - counts
    pad_end = jnp.cumsum(padded)
    pad_start = pad_end - padded
    dest = (pad_start[sorted_e] + jnp.arange(a, dtype=jnp.int32) - start[sorted_e]).astype(jnp.int32)
    n_blocks = -(-a // MOE_BLOCK) + N_EXPERTS
    n_rows = n_blocks * MOE_BLOCK
    row_tok = jnp.zeros((n_rows,), jnp.int32).at[dest].set((order // TOP_K).astype(jnp.int32))
    block_expert = jnp.minimum(
        jnp.searchsorted(pad_end, jnp.arange(n_blocks, dtype=jnp.int32) * MOE_BLOCK, side='right'),
        N_EXPERTS - 1)

    def expert_block(args):
        toks, e = args
        xb = ut[toks]
        hgu = jnp.dot(xb, w_gu[e]) + b_gu[e]
        hg = jnp.minimum(hgu[:, :D_FF], SWIGLU_LIMIT)
        hl = jnp.clip(hgu[:, D_FF:], -SWIGLU_LIMIT, SWIGLU_LIMIT)
        act = hg * jax.nn.sigmoid(SWIGLU_ALPHA * hg) * (hl + 1.0)
        return jnp.dot(act, w_down[e]) + b_down[e]

    y_rows = lax.map(expert_block, (row_tok.reshape(n_blocks, MOE_BLOCK), block_expert))
    y_rows = y_rows.reshape(n_rows, d)
    row_of_assign = jnp.zeros((a,), jnp.int32).at[order].set(dest)
    y_sel = y_rows[row_of_assign].reshape(t, TOP_K, d)
    y = jnp.einsum('tk,tkd->td', gate.astype(u.dtype), y_sel)
    return y.reshape(b, s, d)


def setup_inputs(seed: int = 0) -> dict:
    key = jax.random.key(seed)
    ks = jax.random.split(key, 24)
    f32 = jnp.float32
    nrm = lambda k, shape, scale: jax.random.normal(k, shape, f32) * scale
    L = DEPTH
    return {
        "x": jax.random.normal(ks[0], (BATCH, SEQ, D_MODEL), f32),
        "positions": jnp.broadcast_to(jnp.arange(SEQ, dtype=jnp.int32)[None, :], (BATCH, SEQ)),
        "norm1_g": 1.0 + nrm(ks[1], (L, D_MODEL), 0.02),
        "w_in": nrm(ks[2], (L, D_MODEL, IN_WIDTH), D_MODEL ** -0.5),
        "b_fgate": FORGET_BIAS_INIT + nrm(ks[3], (L, FOX_HEADS), 0.5),
        "b_gate": nrm(ks[4], (L, N_BRANCHES * D_MODEL), 0.02),
        "lam_q1": nrm(ks[5], (L, HEAD_DIM), 0.1),
        "lam_k1": nrm(ks[6], (L, HEAD_DIM), 0.1),
        "lam_q2": nrm(ks[7], (L, HEAD_DIM), 0.1),
        "lam_k2": nrm(ks[8], (L, HEAD_DIM), 0.1),
        "subln_g": 1.0 + nrm(ks[9], (L, DIFF_V_DIM), 0.02),
        "w_branch_fox": nrm(ks[10], (L, FOX_WIDTH, D_MODEL), FOX_WIDTH ** -0.5),
        "w_branch_diff": nrm(ks[11], (L, DIFF_V_WIDTH, D_MODEL), DIFF_V_WIDTH ** -0.5),
        "w_out": nrm(ks[12], (L, D_MODEL, D_MODEL), D_MODEL ** -0.5),
        "norm2_g": 1.0 + nrm(ks[13], (L, D_MODEL), 0.02),
        "router_w": nrm(ks[14], (L, D_MODEL, N_EXPERTS), D_MODEL ** -0.5),
        "router_b": nrm(ks[15], (L, N_EXPERTS), 0.01),
        "w_gu": nrm(ks[16], (L, N_EXPERTS, D_MODEL, 2 * D_FF), D_MODEL ** -0.5),
        "b_gu": nrm(ks[17], (L, N_EXPERTS, 2 * D_FF), 0.01),
        "w_down": nrm(ks[18], (L, N_EXPERTS, D_FF, D_MODEL), D_FF ** -0.5),
        "b_down": nrm(ks[19], (L, N_EXPERTS, D_MODEL), 0.01),
        "normf_g": 1.0 + nrm(ks[20], (D_MODEL,), 0.02),
    }


def reference(x, positions, norm1_g, w_in, b_fgate, b_gate, lam_q1, lam_k1, lam_q2, lam_k2,
              subln_g, w_branch_fox, w_branch_diff, w_out, norm2_g, router_w, router_b,
              w_gu, b_gu, w_down, b_down, normf_g):
    cos, sin = rotary_tables(positions)
    h = x
    for l in range(DEPTH):
        u = rmsnorm(h, norm1_g[l])
        h = h + mixer_block(u, cos, sin, l, w_in[l], b_fgate[l], b_gate[l], lam_q1[l], lam_k1[l],
                            lam_q2[l], lam_k2[l], subln_g[l], w_branch_fox[l], w_branch_diff[l],
                            w_out[l])
        u = rmsnorm(h, norm2_g[l])
        h = h + moe_ffn(u, router_w[l], router_b[l], w_gu[l], b_gu[l], w_down[l], b_down[l])
    return rmsnorm(h, normf_g)
```

```python
import functools
import math

import jax
import jax.numpy as jnp
from jax import lax
from jax.experimental import pallas as pl
from jax.experimental.pallas import tpu as pltpu

HEAD_DIM = 128
FOX_HEADS = 8
DIFF_HEADS = 4
DIFF_V_DIM = 2 * HEAD_DIM
ROPE_THETA = 500000.0
ROPE_DIM = HEAD_DIM // 4
N_EXPERTS = 32
TOP_K = 4
SWIGLU_ALPHA = 1.702
SWIGLU_LIMIT = 7.0
EPS = 1e-5
LANES = 128

F32 = jnp.float32
BF16 = jnp.bfloat16
NEG_BIG = -1e30
VMEM_LIMIT = 56 * 1024 * 1024


def _cparams(sem, vmem=VMEM_LIMIT):
    return pltpu.CompilerParams(dimension_semantics=sem, vmem_limit_bytes=vmem)


def _pick(n, pref):
    t = min(n, pref)
    while n % t:
        t //= 2
    return t


def _inproj_kernel(x_ref, g1_ref, w_ref, wf_ref, pos_ref, tab_ref, bg_ref,
                   qkv_ref, gate_ref, flog_ref, u_sc, cos_sc, sa_sc, sb_sc, *, scale, n_qkv):
    j = pl.program_id(1)

    @pl.when(j == 0)
    def _():
        xf = x_ref[...]
        ms = jnp.mean(xf * xf, axis=-1, keepdims=True)
        u = (xf * lax.rsqrt(ms + EPS)) * g1_ref[...]
        ub = u.astype(BF16)
        u_sc[...] = ub
        flog_ref[...] = jnp.dot(ub, wf_ref[...], preferred_element_type=F32)
        ang = pos_ref[...] * tab_ref[0:1, :]
        sn = jnp.sin(ang)
        cos_sc[...] = jnp.cos(ang)
        sa_sc[...] = sn * tab_ref[1:2, :]
        sb_sc[...] = sn * tab_ref[2:3, :]

    acc = jnp.dot(u_sc[...], w_ref[...], preferred_element_type=F32)

    def rotary(mult):
        for c in range(acc.shape[1] // HEAD_DIM):
            t = acc[:, c * HEAD_DIM:(c + 1) * HEAD_DIM]
            r = (t * cos_sc[...]
                 + pltpu.roll(t, HEAD_DIM - ROPE_DIM // 2, 1) * sa_sc[...]
                 + pltpu.roll(t, ROPE_DIM // 2, 1) * sb_sc[...])
            if mult != 1.0:
                r = r * mult
            qkv_ref[:, c * HEAD_DIM:(c + 1) * HEAD_DIM] = r.astype(BF16)

    @pl.when(j == 0)
    def _():
        qkv_ref[...] = (acc * scale).astype(BF16)

    @pl.when((j == 1) | (j == 2) | (j == 5))
    def _():
        qkv_ref[...] = acc.astype(BF16)

    @pl.when(j == 3)
    def _():
        rotary(scale)

    @pl.when(j == 4)
    def _():
        rotary(1.0)

    @pl.when(j >= n_qkv)
    def _():
        gate_ref[...] = jax.nn.sigmoid(acc + bg_ref[...])


def _inproj(x2, pos, g1, w_main, w_f, rope_tab, b_gate, tm):
    t, d = x2.shape
    tn = 1024
    n_qkv = 6
    n_gate = (w_main.shape[1] - n_qkv * tn) // tn
    grid = (t // tm, n_qkv + n_gate)
    kern = functools.partial(_inproj_kernel, scale=1.0 / math.sqrt(HEAD_DIM), n_qkv=n_qkv)
    return pl.pallas_call(
        kern,
        grid=grid,
        in_specs=[
            pl.BlockSpec((tm, d), lambda i, j: (i, 0)),
            pl.BlockSpec((1, d), lambda i, j: (0, 0)),
            pl.BlockSpec((d, tn), lambda i, j: (0, j)),
            pl.BlockSpec((d, LANES), lambda i, j: (0, 0)),
            pl.BlockSpec((tm, 1), lambda i, j: (i, 0)),
            pl.BlockSpec((8, LANES), lambda i, j: (0, 0)),
            pl.BlockSpec((1, tn), lambda i, j: (0, jnp.maximum(j - n_qkv, 0))),
        ],
        out_specs=[
            pl.BlockSpec((tm, tn), lambda i, j: (i, jnp.minimum(j, n_qkv - 1))),
            pl.BlockSpec((tm, tn), lambda i, j: (i, jnp.maximum(j - n_qkv, 0))),
            pl.BlockSpec((tm, LANES), lambda i, j: (i, 0)),
        ],
        out_shape=[
            jax.ShapeDtypeStruct((t, n_qkv * tn), BF16),
            jax.ShapeDtypeStruct((t, n_gate * tn), F32),
            jax.ShapeDtypeStruct((t, LANES), F32),
        ],
        scratch_shapes=[
            pltpu.VMEM((tm, d), BF16),
            pltpu.VMEM((tm, LANES), F32),
            pltpu.VMEM((tm, LANES), F32),
            pltpu.VMEM((tm, LANES), F32),
        ],
        compiler_params=_cparams(("parallel", "arbitrary")),
    )(x2, g1, w_main, w_f, pos, rope_tab, b_gate)


def _fgate_kernel(fl_ref, b_ref, c_ref):
    z = fl_ref[...] + b_ref[...]
    x = jnp.minimum(z, 0.0) - jnp.log(1.0 + jnp.exp(-jnp.abs(z)))
    n = x.shape[1]
    lane = lax.broadcasted_iota(jnp.int32, x.shape, 1)
    sh = 1
    while sh < n:
        x = x + jnp.where(lane >= sh, pltpu.roll(x, sh, 1), 0.0)
        sh *= 2
    c_ref[...] = x


def _fgate(fl, b):
    r, s = fl.shape
    return pl.pallas_call(
        _fgate_kernel,
        grid=(1,),
        in_specs=[pl.BlockSpec((r, s), lambda i: (0, 0)), pl.BlockSpec((r, 1), lambda i: (0, 0))],
        out_specs=pl.BlockSpec((r, s), lambda i: (0, 0)),
        out_shape=jax.ShapeDtypeStruct((r, s), F32),
        compiler_params=_cparams(("arbitrary",)),
    )(fl, b)


def _nt_dot(a, b):
    return lax.dot_general(a, b, (((1,), (1,)), ((), ())), preferred_element_type=F32)


def _fox_kernel(q_ref, k_ref, v_ref, c_ref, o_ref, m_sc, l_sc, acc_sc, *, tq, tk):
    qi = pl.program_id(2)
    q = q_ref[...]
    m_sc[...] = jnp.full(m_sc.shape, -jnp.inf, F32)
    l_sc[...] = jnp.zeros(l_sc.shape, F32)
    acc_sc[...] = jnp.zeros(acc_sc.shape, F32)

    def step(start, masked):
        k = k_ref[pl.ds(start, tk), :]
        v = v_ref[pl.ds(start, tk), :]
        s = _nt_dot(q, k) - c_ref[0, :, pl.ds(start, tk)]
        if masked:
            row = lax.broadcasted_iota(jnp.int32, s.shape, 0)
            col = lax.broadcasted_iota(jnp.int32, s.shape, 1)
            s = jnp.where(col <= row, s, -jnp.inf)
        m_old = m_sc[...]
        m_new = jnp.maximum(m_old, jnp.max(s, axis=-1, keepdims=True))
        alpha = jnp.exp(m_old - m_new)
        p = jnp.exp(s - m_new)
        l_sc[...] = alpha * l_sc[...] + jnp.sum(p, axis=-1, keepdims=True)
        acc_sc[...] = alpha * acc_sc[...] + jnp.dot(p.astype(BF16), v, preferred_element_type=F32)
        m_sc[...] = m_new

    def body(kc, carry):
        step(pl.multiple_of(kc * tk, tk), False)
        return carry

    lax.fori_loop(0, qi, body, 0)
    step(pl.multiple_of(qi * tq, tq), True)
    o_ref[...] = (acc_sc[...] / l_sc[...]).astype(o_ref.dtype)


def _fox_attn(qkv, c3, b, s, tq):
    t = b * s
    nq = s // tq
    kern = functools.partial(_fox_kernel, tq=tq, tk=tq)
    return pl.pallas_call(
        kern,
        grid=(b, FOX_HEADS, nq),
        in_specs=[
            pl.BlockSpec((tq, HEAD_DIM), lambda bi, h, qi: (bi * nq + qi, h)),
            pl.BlockSpec((s, HEAD_DIM), lambda bi, h, qi: (bi, FOX_HEADS + h)),
            pl.BlockSpec((s, HEAD_DIM), lambda bi, h, qi: (bi, 2 * FOX_HEADS + h)),
            pl.BlockSpec((1, 1, s), lambda bi, h, qi: (bi * FOX_HEADS + h, 0, 0)),
        ],
        out_specs=pl.BlockSpec((tq, HEAD_DIM), lambda bi, h, qi: (bi * nq + qi, h)),
        out_shape=jax.ShapeDtypeStruct((t, FOX_HEADS * HEAD_DIM), BF16),
        scratch_shapes=[
            pltpu.VMEM((tq, 1), F32),
            pltpu.VMEM((tq, 1), F32),
            pltpu.VMEM((tq, HEAD_DIM), F32),
        ],
        compiler_params=_cparams(("parallel", "parallel", "arbitrary")),
    )(qkv, qkv, qkv, c3)


def _diff_kernel(q1_ref, q2_ref, k1_ref, k2_ref, v_ref, lam_ref, g_ref, o_ref,
                 m1_sc, l1_sc, a1_sc, m2_sc, l2_sc, a2_sc, *, tq, tk, lam_init):
    qi = pl.program_id(2)
    for r in (m1_sc, m2_sc):
        r[...] = jnp.full(r.shape, -jnp.inf, F32)
    for r in (l1_sc, l2_sc, a1_sc, a2_sc):
        r[...] = jnp.zeros(r.shape, F32)

    def update(s, v, m_sc, l_sc, a_sc):
        m_old = m_sc[...]
        m_new = jnp.maximum(m_old, jnp.max(s, axis=-1, keepdims=True))
        alpha = jnp.exp(m_old - m_new)
        p = jnp.exp(s - m_new)
        l_sc[...] = alpha * l_sc[...] + jnp.sum(p, axis=-1, keepdims=True)
        a_sc[...] = alpha * a_sc[...] + jnp.dot(p.astype(BF16), v, preferred_element_type=F32)
        m_sc[...] = m_new

    def step(start, masked):
        v = v_ref[pl.ds(start, tk), :]
        s1 = _nt_dot(q1_ref[...], k1_ref[pl.ds(start, tk), :])
        s2 = _nt_dot(q2_ref[...], k2_ref[pl.ds(start, tk), :])
        if masked:
            row = lax.broadcasted_iota(jnp.int32, s1.shape, 0)
            col = lax.broadcasted_iota(jnp.int32, s1.shape, 1)
            keep = col <= row
            s1 = jnp.where(keep, s1, -jnp.inf)
            s2 = jnp.where(keep, s2, -jnp.inf)
        update(s1, v, m1_sc, l1_sc, a1_sc)
        update(s2, v, m2_sc, l2_sc, a2_sc)

    def body(kc, carry):
        step(pl.multiple_of(kc * tk, tk), False)
        return carry

    lax.fori_loop(0, qi, body, 0)
    step(pl.multiple_of(qi * tq, tq), True)

    lp = lam_ref[...]
    lam = (jnp.exp(jnp.sum(lp[0:1, :] * lp[1:2, :], axis=-1, keepdims=True))
           - jnp.exp(jnp.sum(lp[2:3, :] * lp[3:4, :], axis=-1, keepdims=True)) + lam_init)
    y = a1_sc[...] / l1_sc[...] - lam * (a2_sc[...] / l2_sc[...])
    ms = jnp.mean(y * y, axis=-1, keepdims=True)
    y = (y * lax.rsqrt(ms + EPS)) * g_ref[...]
    o_ref[...] = (y * (1.0 - lam_init)).astype(o_ref.dtype)


def _diff_attn(qkv, lam_p, subln_g, b, s, tq, lam_init):
    t = b * s
    nq = s // tq
    qb = 3 * FOX_HEADS
    kb = qb + 2 * DIFF_HEADS
    vb = (kb + 2 * DIFF_HEADS) // 2
    kern = functools.partial(_diff_kernel, tq=tq, tk=tq, lam_init=lam_init)
    return pl.pallas_call(
        kern,
        grid=(b, DIFF_HEADS, nq),
        in_specs=[
            pl.BlockSpec((tq, HEAD_DIM), lambda bi, h, qi: (bi * nq + qi, qb + 2 * h)),
            pl.BlockSpec((tq, HEAD_DIM), lambda bi, h, qi: (bi * nq + qi, qb + 2 * h + 1)),
            pl.BlockSpec((s, HEAD_DIM), lambda bi, h, qi: (bi, kb + 2 * h)),
            pl.BlockSpec((s, HEAD_DIM), lambda bi, h, qi: (bi, kb + 2 * h + 1)),
            pl.BlockSpec((s, DIFF_V_DIM), lambda bi, h, qi: (bi, vb + h)),
            pl.BlockSpec((4, HEAD_DIM), lambda bi, h, qi: (0, 0)),
            pl.BlockSpec((1, DIFF_V_DIM), lambda bi, h, qi: (0, 0)),
        ],
        out_specs=pl.BlockSpec((tq, DIFF_V_DIM), lambda bi, h, qi: (bi * nq + qi, h)),
        out_shape=jax.ShapeDtypeStruct((t, DIFF_HEADS * DIFF_V_DIM), BF16),
        scratch_shapes=[
            pltpu.VMEM((tq, 1), F32), pltpu.VMEM((tq, 1), F32), pltpu.VMEM((tq, DIFF_V_DIM), F32),
            pltpu.VMEM((tq, 1), F32), pltpu.VMEM((tq, 1), F32), pltpu.VMEM((tq, DIFF_V_DIM), F32),
        ],
        compiler_params=_cparams(("parallel", "parallel", "arbitrary")),
    )(qkv, qkv, qkv, qkv, qkv, lam_p, subln_g)


def _merge_kernel(yf_ref, yd_ref, g0_ref, g1_ref, x_ref, wbf_ref, wbd_ref, wo_ref, n2_ref,
                  rw_ref, rb_ref, h_ref, u_ref, idx_ref, gw_ref):
    ya = jnp.dot(yf_ref[...], wbf_ref[...], preferred_element_type=F32)
    yb = jnp.dot(yd_ref[...], wbd_ref[...], preferred_element_type=F32)
    merged = g0_ref[...] * ya + g1_ref[...] * yb
    h = x_ref[...] + jnp.dot(merged.astype(BF16), wo_ref[...], preferred_element_type=F32)
    h_ref[...] = h
    ms = jnp.mean(h * h, axis=-1, keepdims=True)
    u = (h * lax.rsqrt(ms + EPS)) * n2_ref[...]
    u_ref[...] = u
    logits = jnp.dot(u.astype(BF16), rw_ref[...], preferred_element_type=F32) + rb_ref[...]

    lane = lax.broadcasted_iota(jnp.int32, logits.shape, 1).astype(F32)
    work = logits
    vals, idxs = [], []
    for _ in range(TOP_K):
        m = jnp.max(work, axis=-1, keepdims=True)
        sel = jnp.min(jnp.where(work == m, lane, float(LANES)), axis=-1, keepdims=True)
        vals.append(m)
        idxs.append(sel)
        work = jnp.where(lane == sel, -jnp.inf, work)
    es = [jnp.exp(v - vals[0]) for v in vals]
    den = es[0]
    for e in es[1:]:
        den = den + e
    idx_out = jnp.zeros(logits.shape, F32)
    gw_out = jnp.zeros(logits.shape, F32)
    for k in range(TOP_K):
        idx_out = jnp.where(lane == float(k), idxs[k], idx_out)
        gw_out = jnp.where(lane == float(k), es[k] / den, gw_out)
    idx_ref[...] = idx_out.astype(jnp.int32)
    gw_ref[...] = gw_out


def _merge(yf, yd, gate, x2, wbf, wbd, wo, n2, rw, rb, tm):
    t, d = x2.shape
    fw = yf.shape[1]
    dw = yd.shape[1]
    const = dict(pipeline_mode=pl.Buffered(1))
    return pl.pallas_call(
        _merge_kernel,
        grid=(t // tm,),
        in_specs=[
            pl.BlockSpec((tm, fw), lambda i: (i, 0)),
            pl.BlockSpec((tm, dw), lambda i: (i, 0)),
            pl.BlockSpec((tm, d), lambda i: (i, 0)),
            pl.BlockSpec((tm, d), lambda i: (i, 1)),
            pl.BlockSpec((tm, d), lambda i: (i, 0)),
            pl.BlockSpec((fw, d), lambda i: (0, 0), **const),
            pl.BlockSpec((dw, d), lambda i: (0, 0), **const),
            pl.BlockSpec((d, d), lambda i: (0, 0), **const),
            pl.BlockSpec((1, d), lambda i: (0, 0)),
            pl.BlockSpec((d, LANES), lambda i: (0, 0), **const),
            pl.BlockSpec((1, LANES), lambda i: (0, 0)),
        ],
        out_specs=[
            pl.BlockSpec((tm, d), lambda i: (i, 0)),
            pl.BlockSpec((tm, d), lambda i: (i, 0)),
            pl.BlockSpec((tm, LANES), lambda i: (i, 0)),
            pl.BlockSpec((tm, LANES), lambda i: (i, 0)),
        ],
        out_shape=[
            jax.ShapeDtypeStruct((t, d), F32),
            jax.ShapeDtypeStruct((t, d), F32),
            jax.ShapeDtypeStruct((t, LANES), jnp.int32),
            jax.ShapeDtypeStruct((t, LANES), F32),
        ],
        compiler_params=_cparams(("parallel",)),
    )(yf, yd, gate, gate, x2, wbf, wbd, wo, n2, rw, rb)


def _dispatch_kernel(valid_ref, tok_ref, u_hbm, out_ref, stage, sems, *, rows):
    s = pl.program_id(0)
    n = pl.num_programs(0) - 1
    slot = s % 2

    def row_copy(t, r, sl):
        return pltpu.make_async_copy(u_hbm.at[pl.ds(t, 1), :], stage.at[sl, pl.ds(r, 1), :], sems.at[sl])

    @pl.when(jnp.logical_and(s < n, valid_ref[jnp.minimum(s, n - 1)] > 0))
    def _issue():
        def body(r, carry):
            row_copy(tok_ref[0, 0, r], r, slot).start()
            return carry
        lax.fori_loop(0, rows, body, 0)

    @pl.when(s > 0)
    def _drain():
        ps = 1 - slot
        was_valid = valid_ref[jnp.maximum(s - 1, 0)] > 0

        @pl.when(was_valid)
        def _():
            def body(r, carry):
                row_copy(0, r, ps).wait()
                return carry
            lax.fori_loop(0, rows, body, 0)
            out_ref[...] = stage[ps].astype(out_ref.dtype)

        @pl.when(jnp.logical_not(was_valid))
        def _():
            out_ref[...] = jnp.zeros(out_ref.shape, out_ref.dtype)


def _dispatch(sub_valid, row_tok3, u, rows):
    n_sub = row_tok3.shape[0]
    d = u.shape[1]
    kern = functools.partial(_dispatch_kernel, rows=rows)
    return pl.pallas_call(
        kern,
        grid_spec=pltpu.PrefetchScalarGridSpec(
            num_scalar_prefetch=1,
            grid=(n_sub + 1,),
            in_specs=[
                pl.BlockSpec((1, 1, rows), lambda s, v: (jnp.minimum(s, n_sub - 1), 0, 0),
                             memory_space=pltpu.SMEM),
                pl.BlockSpec(memory_space=pl.ANY),
            ],
            out_specs=pl.BlockSpec((rows, d), lambda s, v: (jnp.maximum(s - 1, 0), 0)),
            scratch_shapes=[pltpu.VMEM((2, rows, d), F32), pltpu.SemaphoreType.DMA((2,))],
        ),
        out_shape=jax.ShapeDtypeStruct((n_sub * rows, d), BF16),
        compiler_params=_cparams(("arbitrary",)),
    )(sub_valid, row_tok3, u)


def _expert_kernel(te_ref, ts_ref, nsub_ref, xs_ref, wg_ref, wu_ref, bg_ref, bu_ref, wd_ref, bd_ref,
                   out_ref, *, sub, spt):
    i = pl.program_id(0)
    j = pl.program_id(1)
    nsub = nsub_ref[i]

    @pl.when(jnp.logical_and(nsub == 0, j == 0))
    def _():
        out_ref[...] = jnp.zeros(out_ref.shape, out_ref.dtype)

    @pl.when(nsub > 0)
    def _():
        wg = wg_ref[0].astype(BF16)
        wu = wu_ref[0].astype(BF16)
        wd = wd_ref[0].astype(BF16)
        for s in range(spt):
            rs = slice(s * sub, (s + 1) * sub)

            @pl.when(s < nsub)
            def _():
                x = xs_ref[rs, :]
                hg = jnp.dot(x, wg, preferred_element_type=F32) + bg_ref[0]
                hu = jnp.dot(x, wu, preferred_element_type=F32) + bu_ref[0]
                hg = jnp.minimum(hg, SWIGLU_LIMIT)
                hl = jnp.clip(hu, -SWIGLU_LIMIT, SWIGLU_LIMIT)
                act = hg * jax.nn.sigmoid(SWIGLU_ALPHA * hg) * (hl + 1.0)
                y = jnp.dot(act.astype(BF16), wd, preferred_element_type=F32)

                @pl.when(j == 0)
                def _():
                    out_ref[rs, :] = y + bd_ref[0]

                @pl.when(j > 0)
                def _():
                    out_ref[rs, :] += y

            @pl.when(jnp.logical_and(s >= nsub, j == 0))
            def _():
                out_ref[rs, :] = jnp.zeros((sub, out_ref.shape[1]), out_ref.dtype)


def _experts(tile_e, tile_src, tile_nsub, xs, w_gu, b_gu3, w_down, b_down3, tm_e, sub, tf):
    n_rows, d = xs.shape
    n_tiles = n_rows // tm_e
    dff = w_down.shape[1]
    nf = dff // tf
    kern = functools.partial(_expert_kernel, sub=sub, spt=tm_e // sub)

    def jj(i, j, ns):
        return jnp.where(ns[i] > 0, j, nf - 1)

    return pl.pallas_call(
        kern,
        grid_spec=pltpu.PrefetchScalarGridSpec(
            num_scalar_prefetch=3,
            grid=(n_tiles, nf),
            in_specs=[
                pl.BlockSpec((tm_e, d), lambda i, j, te, ts, ns: (ts[i], 0)),
                pl.BlockSpec((1, d, tf), lambda i, j, te, ts, ns: (te[i], 0, jj(i, j, ns))),
                pl.BlockSpec((1, d, tf), lambda i, j, te, ts, ns: (te[i], 0, nf + jj(i, j, ns))),
                pl.BlockSpec((1, 1, tf), lambda i, j, te, ts, ns: (te[i], 0, jj(i, j, ns))),
                pl.BlockSpec((1, 1, tf), lambda i, j, te, ts, ns: (te[i], 0, nf + jj(i, j, ns))),
                pl.BlockSpec((1, tf, d), lambda i, j, te, ts, ns: (te[i], jj(i, j, ns), 0)),
                pl.BlockSpec((1, 1, d), lambda i, j, te, ts, ns: (te[i], 0, 0)),
            ],
            out_specs=pl.BlockSpec((tm_e, d), lambda i, j, te, ts, ns: (i, 0)),
        ),
        out_shape=jax.ShapeDtypeStruct((n_rows, d), F32),
        compiler_params=_cparams(("arbitrary", "arbitrary")),
    )(tile_e, tile_src, tile_nsub, xs, w_gu, w_gu, b_gu3, b_gu3, w_down, b_down3)


def _combine_kernel(cur_ref, nxt_ref, yr_hbm, h_ref, gw_ref, gf_ref, out_ref, ysel, sems, *, tmc,
                    final_norm):
    i = pl.program_id(0)
    n = pl.num_programs(0)
    slot = i % 2
    n_dma = TOP_K * tmc

    def row_copy(row, q, sl):
        return pltpu.make_async_copy(yr_hbm.at[pl.ds(row, 1), :], ysel.at[sl, pl.ds(q, 1), :], sems.at[sl])

    def issue(idx_ref, sl):
        def body(q, carry):
            row_copy(idx_ref[0, 0, q], q, sl).start()
            return carry
        lax.fori_loop(0, n_dma, body, 0)

    @pl.when(i == 0)
    def _():
        issue(cur_ref, 0)

    @pl.when(i + 1 < n)
    def _():
        issue(nxt_ref, 1 - slot)

    def wbody(q, carry):
        row_copy(0, q, slot).wait()
        return carry
    lax.fori_loop(0, n_dma, wbody, 0)

    gw = gw_ref[...]
    y = h_ref[...]
    for k in range(TOP_K):
        y = y + gw[:, k:k + 1] * ysel[slot, k * tmc:(k + 1) * tmc, :]
    if final_norm:
        ms = jnp.mean(y * y, axis=-1, keepdims=True)
        y = (y * lax.rsqrt(ms + EPS)) * gf_ref[...]
    out_ref[...] = y


def _combine(roa3, y_rows, h, gw, gf, tmc, final_norm):
    t, d = h.shape
    n = t // tmc
    kern = functools.partial(_combine_kernel, tmc=tmc, final_norm=final_norm)
    return pl.pallas_call(
        kern,
        grid=(n,),
        in_specs=[
            pl.BlockSpec((1, 1, TOP_K * tmc), lambda i: (i, 0, 0), memory_space=pltpu.SMEM),
            pl.BlockSpec((1, 1, TOP_K * tmc), lambda i: (jnp.minimum(i + 1, n - 1), 0, 0),
                         memory_space=pltpu.SMEM),
            pl.BlockSpec(memory_space=pl.ANY),
            pl.BlockSpec((tmc, d), lambda i: (i, 0)),
            pl.BlockSpec((tmc, LANES), lambda i: (i, 0)),
            pl.BlockSpec((1, d), lambda i: (0, 0)),
        ],
        out_specs=pl.BlockSpec((tmc, d), lambda i: (i, 0)),
        out_shape=jax.ShapeDtypeStruct((t, d), F32),
        scratch_shapes=[pltpu.VMEM((2, TOP_K * tmc, d), F32), pltpu.SemaphoreType.DMA((2,))],
        compiler_params=_cparams(("arbitrary",)),
    )(roa3, roa3, y_rows, h, gw, gf)


def _routing(top_idx, tm_e, sub):
    t = top_idx.shape[0]
    a = t * TOP_K
    flat_e = top_idx.reshape(a)
    onehot = (flat_e[:, None] == jnp.arange(N_EXPERTS, dtype=jnp.int32)[None, :]).astype(jnp.int32)
    csum = jnp.cumsum(onehot, axis=0)
    counts = csum[-1]
    rank = jnp.sum(onehot * csum, axis=1) - 1
    padded = (counts + tm_e - 1) // tm_e * tm_e
    pad_end = jnp.cumsum(padded)
    pad_start = pad_end - padded
    row_of_assign = (pad_start[flat_e] + rank).astype(jnp.int32)
    n_tiles = -(-a // tm_e) + N_EXPERTS
    n_rows = n_tiles * tm_e
    row_tok = jnp.zeros((n_rows,), jnp.int32).at[row_of_assign].set(
        jnp.arange(a, dtype=jnp.int32) // TOP_K)
    tile_start = jnp.arange(n_tiles, dtype=jnp.int32) * tm_e
    n_used = pad_end[-1] // tm_e
    tile_e_raw = jnp.minimum(jnp.searchsorted(pad_end, tile_start, side='right'), N_EXPERTS - 1)
    used = tile_start < pad_end[-1]
    last = jnp.maximum(n_used - 1, 0)
    tile_src = jnp.where(used, jnp.arange(n_tiles, dtype=jnp.int32), last).astype(jnp.int32)
    tile_e = tile_e_raw[tile_src].astype(jnp.int32)
    valid_rows = jnp.clip(counts[tile_e_raw] - (tile_start - pad_start[tile_e_raw]), 0, tm_e)
    tile_nsub = jnp.where(used, (valid_rows + sub - 1) // sub, 0).astype(jnp.int32)
    spt = tm_e // sub
    sub_valid = (jnp.arange(spt, dtype=jnp.int32)[None, :] < tile_nsub[:, None]).astype(jnp.int32)
    return row_tok, row_of_assign, tile_e, tile_src, tile_nsub, sub_valid.reshape(n_tiles * spt)


def kernel(x, positions, norm1_g, w_in, b_fgate, b_gate, lam_q1, lam_k1, lam_q2, lam_k2, subln_g,
           w_branch_fox, w_branch_diff, w_out, norm2_g, router_w, router_b, w_gu, b_gu, w_down,
           b_down, normf_g):
    b, s, d = x.shape
    t = b * s
    depth = norm1_g.shape[0]
    fox_w = FOX_HEADS * HEAD_DIM
    tm_a = _pick(t, 1024)
    tq = _pick(s, 512)
    tm_d = _pick(t, 256)
    tm_e, sub, tf = 1024, 512, 256
    tmc = _pick(t, 128)

    half = ROPE_DIM // 2
    inv_freq = ROPE_THETA ** (-jnp.arange(0, ROPE_DIM, 2, dtype=F32) / ROPE_DIM)
    zeros = jnp.zeros((LANES - ROPE_DIM,), F32)
    rope_tab = jnp.zeros((8, LANES), F32)
    rope_tab = rope_tab.at[0].set(jnp.concatenate([inv_freq, inv_freq, zeros]))
    rope_tab = rope_tab.at[1].set(jnp.concatenate([-jnp.ones((half,), F32), jnp.zeros((half,), F32), zeros]))
    rope_tab = rope_tab.at[2].set(jnp.concatenate([jnp.zeros((half,), F32), jnp.ones((half,), F32), zeros]))
    pos = positions.astype(F32).reshape(t, 1)

    h = x.reshape(t, d)
    for l in range(depth):
        wi = w_in[l]
        w_main = jnp.concatenate([wi[:, :3 * fox_w], wi[:, 3 * fox_w + FOX_HEADS:]], axis=1).astype(BF16)
        w_f = jnp.pad(wi[:, 3 * fox_w:3 * fox_w + FOX_HEADS], ((0, 0), (0, LANES - FOX_HEADS))).astype(BF16)
        qkv, gate, flog = _inproj(h, pos, norm1_g[l][None, :], w_main, w_f, rope_tab,
                                  b_gate[l][None, :], tm_a)

        fl = flog[:, :FOX_HEADS].reshape(b, s, FOX_HEADS).transpose(0, 2, 1).reshape(b * FOX_HEADS, s)
        bf = jnp.tile(b_fgate[l].astype(F32), b).reshape(b * FOX_HEADS, 1)
        c3 = _fgate(fl, bf).reshape(b * FOX_HEADS, 1, s)

        y_fox = _fox_attn(qkv, c3, b, s, tq)
        lam_init = 0.8 - 0.6 * math.exp(-0.3 * l)
        lam_p = jnp.stack([lam_q1[l], lam_k1[l], lam_q2[l], lam_k2[l]]).astype(F32)
        y_diff = _diff_attn(qkv, lam_p, subln_g[l][None, :].astype(F32), b, s, tq, lam_init)

        rw = jnp.pad(router_w[l], ((0, 0), (0, LANES - N_EXPERTS))).astype(BF16)
        rb = jnp.concatenate([router_b[l].astype(F32), jnp.full((LANES - N_EXPERTS,), NEG_BIG, F32)])[None, :]
        h, u2, idx128, gw128 = _merge(y_fox, y_diff, gate, h, w_branch_fox[l].astype(BF16),
                                      w_branch_diff[l].astype(BF16), w_out[l].astype(BF16),
                                      norm2_g[l][None, :], rw, rb, tm_d)

        row_tok, roa, tile_e, tile_src, tile_nsub, sub_valid = _routing(idx128[:, :TOP_K], tm_e, sub)
        n_sub = sub_valid.shape[0]
        xs = _dispatch(sub_valid, row_tok.reshape(n_sub, 1, sub), u2, sub)
        y_rows = _experts(tile_e, tile_src, tile_nsub, xs, w_gu[l], b_gu[l][:, None, :],
                          w_down[l], b_down[l][:, None, :], tm_e, sub, tf)
        roa3 = roa.reshape(t // tmc, tmc, TOP_K).transpose(0, 2, 1).reshape(t // tmc, 1, TOP_K * tmc)
        h = _combine(roa3, y_rows, h, gw128, normf_g[None, :], tmc, final_norm=(l == depth - 1))
    return h.reshape(b, s, d)
```

```python
import functools
import math

import jax
import jax.numpy as jnp
from jax import lax
from jax.experimental import pallas as pl
from jax.experimental.pallas import tpu as pltpu

HEAD_DIM = 128
FOX_HEADS = 8
DIFF_HEADS = 4
DIFF_V_DIM = 2 * HEAD_DIM
ROPE_THETA = 500000.0
ROPE_DIM = HEAD_DIM // 4
N_EXPERTS = 32
TOP_K = 4
SWIGLU_ALPHA = 1.702
SWIGLU_LIMIT = 7.0
EPS = 1e-5
LANES = 128

F32 = jnp.float32
BF16 = jnp.bfloat16
NEG_BIG = -1e30
LOG2E = math.log2(math.e)
ISSUE_UNROLL = 8
VMEM_LIMIT = 56 * 1024 * 1024


def _cparams(sem, vmem=VMEM_LIMIT):
    return pltpu.CompilerParams(dimension_semantics=sem, vmem_limit_bytes=vmem)


def _pick(n, pref):
    t = min(n, pref)
    while n % t:
        t //= 2
    return t


def _inproj_kernel(x_ref, g1_ref, w_ref, wf_ref, pos_ref, tab_ref, bg_ref,
                   qkv_ref, gate_ref, flog_ref, u_sc, cos_sc, sa_sc, sb_sc, *, scale, n_qkv):
    j = pl.program_id(1)

    @pl.when(j == 0)
    def _():
        xf = x_ref[...]
        ms = jnp.mean(xf * xf, axis=-1, keepdims=True)
        u = (xf * lax.rsqrt(ms + EPS)) * g1_ref[...]
        ub = u.astype(BF16)
        u_sc[...] = ub
        flog_ref[...] = jnp.dot(ub, wf_ref[...], preferred_element_type=F32)
        ang = pos_ref[...] * tab_ref[0:1, :]
        sn = jnp.sin(ang)
        cos_sc[...] = jnp.cos(ang)
        sa_sc[...] = sn * tab_ref[1:2, :]
        sb_sc[...] = sn * tab_ref[2:3, :]

    acc = jnp.dot(u_sc[...], w_ref[...], preferred_element_type=F32)

    def rotary(mult):
        for c in range(acc.shape[1] // HEAD_DIM):
            t = acc[:, c * HEAD_DIM:(c + 1) * HEAD_DIM]
            r = (t * cos_sc[...]
                 + pltpu.roll(t, HEAD_DIM - ROPE_DIM // 2, 1) * sa_sc[...]
                 + pltpu.roll(t, ROPE_DIM // 2, 1) * sb_sc[...])
            if mult != 1.0:
                r = r * mult
            qkv_ref[:, c * HEAD_DIM:(c + 1) * HEAD_DIM] = r.astype(BF16)

    @pl.when(j == 0)
    def _():
        qkv_ref[...] = (acc * scale).astype(BF16)

    @pl.when((j == 1) | (j == 2) | (j == 5))
    def _():
        qkv_ref[...] = acc.astype(BF16)

    @pl.when(j == 3)
    def _():
        rotary(scale)

    @pl.when(j == 4)
    def _():
        rotary(1.0)

    @pl.when(j >= n_qkv)
    def _():
        gate_ref[...] = jax.nn.sigmoid(acc + bg_ref[...])


def _inproj(x2, pos, g1, w_main, w_f, rope_tab, b_gate, tm):
    t, d = x2.shape
    tn = 1024
    n_qkv = 6
    n_gate = (w_main.shape[1] - n_qkv * tn) // tn
    grid = (t // tm, n_qkv + n_gate)
    kern = functools.partial(_inproj_kernel, scale=LOG2E / math.sqrt(HEAD_DIM), n_qkv=n_qkv)
    return pl.pallas_call(
        kern,
        grid=grid,
        in_specs=[
            pl.BlockSpec((tm, d), lambda i, j: (i, 0)),
            pl.BlockSpec((1, d), lambda i, j: (0, 0)),
            pl.BlockSpec((d, tn), lambda i, j: (0, j)),
            pl.BlockSpec((d, LANES), lambda i, j: (0, 0)),
            pl.BlockSpec((tm, 1), lambda i, j: (i, 0)),
            pl.BlockSpec((8, LANES), lambda i, j: (0, 0)),
            pl.BlockSpec((1, tn), lambda i, j: (0, jnp.maximum(j - n_qkv, 0))),
        ],
        out_specs=[
            pl.BlockSpec((tm, tn), lambda i, j: (i, jnp.minimum(j, n_qkv - 1))),
            pl.BlockSpec((tm, tn), lambda i, j: (i, jnp.maximum(j - n_qkv, 0))),
            pl.BlockSpec((tm, LANES), lambda i, j: (i, 0)),
        ],
        out_shape=[
            jax.ShapeDtypeStruct((t, n_qkv * tn), BF16),
            jax.ShapeDtypeStruct((t, n_gate * tn), F32),
            jax.ShapeDtypeStruct((t, LANES), F32),
        ],
        scratch_shapes=[
            pltpu.VMEM((tm, d), BF16),
            pltpu.VMEM((tm, LANES), F32),
            pltpu.VMEM((tm, LANES), F32),
            pltpu.VMEM((tm, LANES), F32),
        ],
        compiler_params=_cparams(("parallel", "arbitrary")),
    )(x2, g1, w_main, w_f, pos, rope_tab, b_gate)


def _fgate_kernel(fl_ref, b_ref, c_ref):
    z = fl_ref[...] + b_ref[...]
    x = jnp.minimum(z, 0.0) - jnp.log(1.0 + jnp.exp(-jnp.abs(z)))
    n = x.shape[1]
    lane = lax.broadcasted_iota(jnp.int32, x.shape, 1)
    sh = 1
    while sh < n:
        x = x + jnp.where(lane >= sh, pltpu.roll(x, sh, 1), 0.0)
        sh *= 2
    c_ref[...] = x


def _fgate(fl, b):
    r, s = fl.shape
    return pl.pallas_call(
        _fgate_kernel,
        grid=(1,),
        in_specs=[pl.BlockSpec((r, s), lambda i: (0, 0)), pl.BlockSpec((r, 1), lambda i: (0, 0))],
        out_specs=pl.BlockSpec((r, s), lambda i: (0, 0)),
        out_shape=jax.ShapeDtypeStruct((r, s), F32),
        compiler_params=_cparams(("arbitrary",)),
    )(fl, b)


def _transpose_bf16(x):
    return x.astype(F32).T.astype(BF16)


def _softmax_step(s, vt, m_sc, l_sc, acc_sc):
    m_old = m_sc[...]
    m_new = jnp.maximum(m_old, jnp.max(s, axis=0, keepdims=True))
    alpha = jnp.exp2(m_old - m_new)
    p = jnp.exp2(s - m_new)
    l_sc[...] = alpha * l_sc[...] + jnp.sum(p, axis=0, keepdims=True)
    acc_sc[...] = alpha * acc_sc[...] + jnp.dot(vt, p.astype(BF16), preferred_element_type=F32)
    m_sc[...] = m_new


def _causal_keep(shape):
    kv = lax.broadcasted_iota(jnp.int32, shape, 0)
    qq = lax.broadcasted_iota(jnp.int32, shape, 1)
    return kv <= qq


def _fox_kernel(q_ref, k_ref, v_ref, c_ref, o_ref, qt_sc, kaug_sc, vt_sc, m_sc, l_sc, acc_sc,
                *, tq, tk, s_len):
    qi = pl.program_id(2)

    @pl.when(qi == 0)
    def _():
        sub = lax.broadcasted_iota(jnp.int32, (HEAD_DIM, tk), 0)
        for ch in range(s_len // tk):
            rows = slice(ch * tk, (ch + 1) * tk)
            vt_sc[:, rows] = _transpose_bf16(v_ref[rows, :])
            c = c_ref[0, :, rows] * (-LOG2E)
            hi = c.astype(BF16).astype(F32)
            mid = (c - hi).astype(BF16).astype(F32)
            lo = (c - hi - mid).astype(BF16).astype(F32)
            tab = jnp.where(sub == 0, hi, jnp.where(sub == 1, mid, jnp.where(sub == 2, lo, 0.0)))
            kaug_sc[rows, HEAD_DIM:] = tab.T.astype(BF16)
            kaug_sc[rows, :HEAD_DIM] = k_ref[rows, :]

    sub_q = lax.broadcasted_iota(jnp.int32, (HEAD_DIM, tq), 0)
    qt_sc[:HEAD_DIM, :] = _transpose_bf16(q_ref[...])
    qt_sc[HEAD_DIM:, :] = jnp.where(sub_q < 3, 1.0, 0.0).astype(BF16)
    m_sc[...] = jnp.full(m_sc.shape, -jnp.inf, F32)
    l_sc[...] = jnp.zeros(l_sc.shape, F32)
    acc_sc[...] = jnp.zeros(acc_sc.shape, F32)

    def step(start, masked):
        s = jnp.dot(kaug_sc[pl.ds(start, tk), :], qt_sc[...], preferred_element_type=F32)
        if masked:
            s = jnp.where(_causal_keep(s.shape), s, -jnp.inf)
        _softmax_step(s, vt_sc[:, pl.ds(start, tk)], m_sc, l_sc, acc_sc)

    def body(kc, carry):
        step(pl.multiple_of(kc * tk, tk), False)
        return carry

    lax.fori_loop(0, qi, body, 0)
    step(pl.multiple_of(qi * tq, tq), True)
    o_ref[...] = (acc_sc[...] / l_sc[...]).T.astype(o_ref.dtype)


def _fox_attn(qkv, c3, b, s, tq):
    t = b * s
    nq = s // tq
    kern = functools.partial(_fox_kernel, tq=tq, tk=tq, s_len=s)
    return pl.pallas_call(
        kern,
        grid=(b, FOX_HEADS, nq),
        in_specs=[
            pl.BlockSpec((tq, HEAD_DIM), lambda bi, h, qi: (bi * nq + qi, h)),
            pl.BlockSpec((s, HEAD_DIM), lambda bi, h, qi: (bi, FOX_HEADS + h)),
            pl.BlockSpec((s, HEAD_DIM), lambda bi, h, qi: (bi, 2 * FOX_HEADS + h)),
            pl.BlockSpec((1, 1, s), lambda bi, h, qi: (bi * FOX_HEADS + h, 0, 0)),
        ],
        out_specs=pl.BlockSpec((tq, HEAD_DIM), lambda bi, h, qi: (bi * nq + qi, h)),
        out_shape=jax.ShapeDtypeStruct((t, FOX_HEADS * HEAD_DIM), BF16),
        scratch_shapes=[
            pltpu.VMEM((2 * HEAD_DIM, tq), BF16),
            pltpu.VMEM((s, 2 * HEAD_DIM), BF16),
            pltpu.VMEM((HEAD_DIM, s), BF16),
            pltpu.VMEM((1, tq), F32),
            pltpu.VMEM((1, tq), F32),
            pltpu.VMEM((HEAD_DIM, tq), F32),
        ],
        compiler_params=_cparams(("parallel", "parallel", "arbitrary")),
    )(qkv, qkv, qkv, c3)


def _diff_kernel(q1_ref, q2_ref, k1_ref, k2_ref, v_ref, lam_ref, g_ref, o_ref,
                 q1t_sc, q2t_sc, vt_sc, m1_sc, l1_sc, a1_sc, m2_sc, l2_sc, a2_sc,
                 *, tq, tk, s_len, lam_init):
    qi = pl.program_id(2)

    @pl.when(qi == 0)
    def _():
        for ch in range(s_len // tk):
            rows = slice(ch * tk, (ch + 1) * tk)
            vt_sc[:, rows] = _transpose_bf16(v_ref[rows, :])

    q1t_sc[...] = _transpose_bf16(q1_ref[...])
    q2t_sc[...] = _transpose_bf16(q2_ref[...])
    for r in (m1_sc, m2_sc):
        r[...] = jnp.full(r.shape, -jnp.inf, F32)
    for r in (l1_sc, l2_sc, a1_sc, a2_sc):
        r[...] = jnp.zeros(r.shape, F32)

    def step(start, masked):
        vt = vt_sc[:, pl.ds(start, tk)]
        s1 = jnp.dot(k1_ref[pl.ds(start, tk), :], q1t_sc[...], preferred_element_type=F32)
        s2 = jnp.dot(k2_ref[pl.ds(start, tk), :], q2t_sc[...], preferred_element_type=F32)
        if masked:
            keep = _causal_keep(s1.shape)
            s1 = jnp.where(keep, s1, -jnp.inf)
            s2 = jnp.where(keep, s2, -jnp.inf)
        _softmax_step(s1, vt, m1_sc, l1_sc, a1_sc)
        _softmax_step(s2, vt, m2_sc, l2_sc, a2_sc)

    def body(kc, carry):
        step(pl.multiple_of(kc * tk, tk), False)
        return carry

    lax.fori_loop(0, qi, body, 0)
    step(pl.multiple_of(qi * tq, tq), True)

    lp = lam_ref[...]
    lam = (jnp.exp(jnp.sum(lp[0:1, :] * lp[1:2, :], axis=-1, keepdims=True))
           - jnp.exp(jnp.sum(lp[2:3, :] * lp[3:4, :], axis=-1, keepdims=True)) + lam_init)
    y = (a1_sc[...] / l1_sc[...]).T - lam * (a2_sc[...] / l2_sc[...]).T
    ms = jnp.mean(y * y, axis=-1, keepdims=True)
    y = (y * lax.rsqrt(ms + EPS)) * g_ref[...]
    o_ref[...] = (y * (1.0 - lam_init)).astype(o_ref.dtype)


def _diff_attn(qkv, lam_p, subln_g, b, s, tq, lam_init):
    t = b * s
    nq = s // tq
    qb = 3 * FOX_HEADS
    kb = qb + 2 * DIFF_HEADS
    vb = (kb + 2 * DIFF_HEADS) // 2
    kern = functools.partial(_diff_kernel, tq=tq, tk=tq, s_len=s, lam_init=lam_init)
    return pl.pallas_call(
        kern,
        grid=(b, DIFF_HEADS, nq),
        in_specs=[
            pl.BlockSpec((tq, HEAD_DIM), lambda bi, h, qi: (bi * nq + qi, qb + 2 * h)),
            pl.BlockSpec((tq, HEAD_DIM), lambda bi, h, qi: (bi * nq + qi, qb + 2 * h + 1)),
            pl.BlockSpec((s, HEAD_DIM), lambda bi, h, qi: (bi, kb + 2 * h)),
            pl.BlockSpec((s, HEAD_DIM), lambda bi, h, qi: (bi, kb + 2 * h + 1)),
            pl.BlockSpec((s, DIFF_V_DIM), lambda bi, h, qi: (bi, vb + h)),
            pl.BlockSpec((4, HEAD_DIM), lambda bi, h, qi: (0, 0)),
            pl.BlockSpec((1, DIFF_V_DIM), lambda bi, h, qi: (0, 0)),
        ],
        out_specs=pl.BlockSpec((tq, DIFF_V_DIM), lambda bi, h, qi: (bi * nq + qi, h)),
        out_shape=jax.ShapeDtypeStruct((t, DIFF_HEADS * DIFF_V_DIM), BF16),
        scratch_shapes=[
            pltpu.VMEM((HEAD_DIM, tq), BF16), pltpu.VMEM((HEAD_DIM, tq), BF16),
            pltpu.VMEM((DIFF_V_DIM, s), BF16),
            pltpu.VMEM((1, tq), F32), pltpu.VMEM((1, tq), F32), pltpu.VMEM((DIFF_V_DIM, tq), F32),
            pltpu.VMEM((1, tq), F32), pltpu.VMEM((1, tq), F32), pltpu.VMEM((DIFF_V_DIM, tq), F32),
        ],
        compiler_params=_cparams(("parallel", "parallel", "arbitrary")),
    )(qkv, qkv, qkv, qkv, qkv, lam_p, subln_g)


def _merge_kernel(yf_ref, yd_ref, g0_ref, g1_ref, x_ref, wbf_ref, wbd_ref, wo_ref, n2_ref,
                  rw_ref, rb_ref, h_ref, u_ref, idx_ref, gw_ref):
    ya = jnp.dot(yf_ref[...], wbf_ref[...], preferred_element_type=F32)
    yb = jnp.dot(yd_ref[...], wbd_ref[...], preferred_element_type=F32)
    merged = g0_ref[...] * ya + g1_ref[...] * yb
    h = x_ref[...] + jnp.dot(merged.astype(BF16), wo_ref[...], preferred_element_type=F32)
    h_ref[...] = h
    ms = jnp.mean(h * h, axis=-1, keepdims=True)
    u = (h * lax.rsqrt(ms + EPS)) * n2_ref[...]
    u_ref[...] = u
    logits = jnp.dot(u.astype(BF16), rw_ref[...], preferred_element_type=F32) + rb_ref[...]

    lane = lax.broadcasted_iota(jnp.int32, logits.shape, 1).astype(F32)
    work = logits
    vals, idxs = [], []
    for _ in range(TOP_K):
        m = jnp.max(work, axis=-1, keepdims=True)
        sel = jnp.min(jnp.where(work == m, lane, float(LANES)), axis=-1, keepdims=True)
        vals.append(m)
        idxs.append(sel)
        work = jnp.where(lane == sel, -jnp.inf, work)
    es = [jnp.exp(v - vals[0]) for v in vals]
    den = es[0]
    for e in es[1:]:
        den = den + e
    idx_out = jnp.zeros(logits.shape, F32)
    gw_out = jnp.zeros(logits.shape, F32)
    for k in range(TOP_K):
        idx_out = jnp.where(lane == float(k), idxs[k], idx_out)
        gw_out = jnp.where(lane == float(k), es[k] / den, gw_out)
    idx_ref[...] = idx_out.astype(jnp.int32)
    gw_ref[...] = gw_out


def _merge(yf, yd, gate, x2, wbf, wbd, wo, n2, rw, rb, tm):
    t, d = x2.shape
    fw = yf.shape[1]
    dw = yd.shape[1]
    const = dict(pipeline_mode=pl.Buffered(1))
    return pl.pallas_call(
        _merge_kernel,
        grid=(t // tm,),
        in_specs=[
            pl.BlockSpec((tm, fw), lambda i: (i, 0)),
            pl.BlockSpec((tm, dw), lambda i: (i, 0)),
            pl.BlockSpec((tm, d), lambda i: (i, 0)),
            pl.BlockSpec((tm, d), lambda i: (i, 1)),
            pl.BlockSpec((tm, d), lambda i: (i, 0)),
            pl.BlockSpec((fw, d), lambda i: (0, 0), **const),
            pl.BlockSpec((dw, d), lambda i: (0, 0), **const),
            pl.BlockSpec((d, d), lambda i: (0, 0), **const),
            pl.BlockSpec((1, d), lambda i: (0, 0)),
            pl.BlockSpec((d, LANES), lambda i: (0, 0), **const),
            pl.BlockSpec((1, LANES), lambda i: (0, 0)),
        ],
        out_specs=[
            pl.BlockSpec((tm, d), lambda i: (i, 0)),
            pl.BlockSpec((tm, d), lambda i: (i, 0)),
            pl.BlockSpec((tm, LANES), lambda i: (i, 0)),
            pl.BlockSpec((tm, LANES), lambda i: (i, 0)),
        ],
        out_shape=[
            jax.ShapeDtypeStruct((t, d), F32),
            jax.ShapeDtypeStruct((t, d), F32),
            jax.ShapeDtypeStruct((t, LANES), jnp.int32),
            jax.ShapeDtypeStruct((t, LANES), F32),
        ],
        compiler_params=_cparams(("parallel",)),
    )(yf, yd, gate, gate, x2, wbf, wbd, wo, n2, rw, rb)


def _dispatch_kernel(valid_ref, tok_ref, u_hbm, out_ref, stage, sems, *, rows):
    s = pl.program_id(0)
    n = pl.num_programs(0) - 1
    slot = s % 2

    def row_copy(t, r, sl):
        return pltpu.make_async_copy(u_hbm.at[pl.ds(t, 1), :], stage.at[sl, pl.ds(r, 1), :], sems.at[sl])

    @pl.when(jnp.logical_and(s < n, valid_ref[jnp.minimum(s, n - 1)] > 0))
    def _issue():
        def body(g, carry):
            for r8 in range(ISSUE_UNROLL):
                r = g * ISSUE_UNROLL + r8
                row_copy(tok_ref[0, 0, r], r, slot).start()
            return carry
        lax.fori_loop(0, rows // ISSUE_UNROLL, body, 0)

    @pl.when(s > 0)
    def _drain():
        ps = 1 - slot
        was_valid = valid_ref[jnp.maximum(s - 1, 0)] > 0

        @pl.when(was_valid)
        def _():
            pltpu.make_async_copy(u_hbm.at[pl.ds(0, rows), :], stage.at[ps], sems.at[ps]).wait()
            out_ref[...] = stage[ps].astype(out_ref.dtype)

        @pl.when(jnp.logical_not(was_valid))
        def _():
            out_ref[...] = jnp.zeros(out_ref.shape, out_ref.dtype)


def _dispatch(sub_valid, row_tok3, u, rows):
    n_sub = row_tok3.shape[0]
    d = u.shape[1]
    kern = functools.partial(_dispatch_kernel, rows=rows)
    return pl.pallas_call(
        kern,
        grid_spec=pltpu.PrefetchScalarGridSpec(
            num_scalar_prefetch=1,
            grid=(n_sub + 1,),
            in_specs=[
                pl.BlockSpec((1, 1, rows), lambda s, v: (jnp.minimum(s, n_sub - 1), 0, 0),
                             memory_space=pltpu.SMEM),
                pl.BlockSpec(memory_space=pl.ANY),
            ],
            out_specs=pl.BlockSpec((rows, d), lambda s, v: (jnp.maximum(s - 1, 0), 0)),
            scratch_shapes=[pltpu.VMEM((2, rows, d), F32), pltpu.SemaphoreType.DMA((2,))],
        ),
        out_shape=jax.ShapeDtypeStruct((n_sub * rows, d), BF16),
        compiler_params=_cparams(("arbitrary",)),
    )(sub_valid, row_tok3, u)


def _expert_kernel(te_ref, ts_ref, nsub_ref, xs_ref, wg_ref, wu_ref, bg_ref, bu_ref, wd_ref, bd_ref,
                   out_ref, *, sub, spt):
    i = pl.program_id(0)
    j = pl.program_id(1)
    nsub = nsub_ref[i]

    @pl.when(jnp.logical_and(nsub == 0, j == 0))
    def _():
        out_ref[...] = jnp.zeros(out_ref.shape, out_ref.dtype)

    @pl.when(nsub > 0)
    def _():
        wg = wg_ref[0].astype(BF16)
        wu = wu_ref[0].astype(BF16)
        wd = wd_ref[0].astype(BF16)
        for s in range(spt):
            rs = slice(s * sub, (s + 1) * sub)

            @pl.when(s < nsub)
            def _():
                x = xs_ref[rs, :]
                hg = jnp.dot(x, wg, preferred_element_type=F32) + bg_ref[0]
                hu = jnp.dot(x, wu, preferred_element_type=F32) + bu_ref[0]
                hg = jnp.minimum(hg, SWIGLU_LIMIT)
                hl = jnp.clip(hu, -SWIGLU_LIMIT, SWIGLU_LIMIT)
                act = hg * jax.nn.sigmoid(SWIGLU_ALPHA * hg) * (hl + 1.0)
                y = jnp.dot(act.astype(BF16), wd, preferred_element_type=F32)

                @pl.when(j == 0)
                def _():
                    out_ref[rs, :] = y + bd_ref[0]

                @pl.when(j > 0)
                def _():
                    out_ref[rs, :] += y

            @pl.when(jnp.logical_and(s >= nsub, j == 0))
            def _():
                out_ref[rs, :] = jnp.zeros((sub, out_ref.shape[1]), out_ref.dtype)


def _experts(tile_e, tile_src, tile_nsub, xs, w_gu, b_gu3, w_down, b_down3, tm_e, sub, tf):
    n_rows, d = xs.shape
    n_tiles = n_rows // tm_e
    dff = w_down.shape[1]
    nf = dff // tf
    kern = functools.partial(_expert_kernel, sub=sub, spt=tm_e // sub)

    def jj(i, j, ns):
        return jnp.where(ns[i] > 0, j, nf - 1)

    return pl.pallas_call(
        kern,
        grid_spec=pltpu.PrefetchScalarGridSpec(
            num_scalar_prefetch=3,
            grid=(n_tiles, nf),
            in_specs=[
                pl.BlockSpec((tm_e, d), lambda i, j, te, ts, ns: (ts[i], 0)),
                pl.BlockSpec((1, d, tf), lambda i, j, te, ts, ns: (te[i], 0, jj(i, j, ns))),
                pl.BlockSpec((1, d, tf), lambda i, j, te, ts, ns: (te[i], 0, nf + jj(i, j, ns))),
                pl.BlockSpec((1, 1, tf), lambda i, j, te, ts, ns: (te[i], 0, jj(i, j, ns))),
                pl.BlockSpec((1, 1, tf), lambda i, j, te, ts, ns: (te[i], 0, nf + jj(i, j, ns))),
                pl.BlockSpec((1, tf, d), lambda i, j, te, ts, ns: (te[i], jj(i, j, ns), 0)),
                pl.BlockSpec((1, 1, d), lambda i, j, te, ts, ns: (te[i], 0, 0)),
            ],
            out_specs=pl.BlockSpec((tm_e, d), lambda i, j, te, ts, ns: (i, 0)),
        ),
        out_shape=jax.ShapeDtypeStruct((n_rows, d), F32),
        compiler_params=_cparams(("arbitrary", "arbitrary")),
    )(tile_e, tile_src, tile_nsub, xs, w_gu, w_gu, b_gu3, b_gu3, w_down, b_down3)


def _combine_kernel(cur_ref, nxt_ref, yr_hbm, h_ref, gw_ref, gf_ref, out_ref, ysel, sems, *, tmc,
                    final_norm):
    i = pl.program_id(0)
    n = pl.num_programs(0)
    slot = i % 2
    n_dma = TOP_K * tmc

    def row_copy(row, q, sl):
        return pltpu.make_async_copy(yr_hbm.at[pl.ds(row, 1), :], ysel.at[sl, pl.ds(q, 1), :], sems.at[sl])

    def issue(idx_ref, sl):
        def body(g, carry):
            for q8 in range(ISSUE_UNROLL):
                q = g * ISSUE_UNROLL + q8
                row_copy(idx_ref[0, 0, q], q, sl).start()
            return carry
        lax.fori_loop(0, n_dma // ISSUE_UNROLL, body, 0)

    @pl.when(i == 0)
    def _():
        issue(cur_ref, 0)

    @pl.when(i + 1 < n)
    def _():
        issue(nxt_ref, 1 - slot)

    pltpu.make_async_copy(yr_hbm.at[pl.ds(0, n_dma), :], ysel.at[slot], sems.at[slot]).wait()

    gw = gw_ref[...]
    y = h_ref[...]
    for k in range(TOP_K):
        y = y + gw[:, k:k + 1] * ysel[slot, k * tmc:(k + 1) * tmc, :]
    if final_norm:
        ms = jnp.mean(y * y, axis=-1, keepdims=True)
        y = (y * lax.rsqrt(ms + EPS)) * gf_ref[...]
    out_ref[...] = y


def _combine(roa3, y_rows, h, gw, gf, tmc, final_norm):
    t, d = h.shape
    n = t // tmc
    kern = functools.partial(_combine_kernel, tmc=tmc, final_norm=final_norm)
    return pl.pallas_call(
        kern,
        grid=(n,),
        in_specs=[
            pl.BlockSpec((1, 1, TOP_K * tmc), lambda i: (i, 0, 0), memory_space=pltpu.SMEM),
            pl.BlockSpec((1, 1, TOP_K * tmc), lambda i: (jnp.minimum(i + 1, n - 1), 0, 0),
                         memory_space=pltpu.SMEM),
            pl.BlockSpec(memory_space=pl.ANY),
            pl.BlockSpec((tmc, d), lambda i: (i, 0)),
            pl.BlockSpec((tmc, LANES), lambda i: (i, 0)),
            pl.BlockSpec((1, d), lambda i: (0, 0)),
        ],
        out_specs=pl.BlockSpec((tmc, d), lambda i: (i, 0)),
        out_shape=jax.ShapeDtypeStruct((t, d), F32),
        scratch_shapes=[pltpu.VMEM((2, TOP_K * tmc, d), F32), pltpu.SemaphoreType.DMA((2,))],
        compiler_params=_cparams(("arbitrary",)),
    )(roa3, roa3, y_rows, h, gw, gf)


def _routing(top_idx, tm_e, sub):
    t = top_idx.shape[0]
    a = t * TOP_K
    flat_e = top_idx.reshape(a)
    onehot = (flat_e[:, None] == jnp.arange(N_EXPERTS, dtype=jnp.int32)[None, :]).astype(jnp.int32)
    csum = jnp.cumsum(onehot, axis=0)
    counts = csum[-1]
    rank = jnp.sum(onehot * csum, axis=1) - 1
    padded = (counts + tm_e - 1) // tm_e * tm_e
    pad_end = jnp.cumsum(padded)
    pad_start = pad_end - padded
    row_of_assign = (pad_start[flat_e] + rank).astype(jnp.int32)
    n_tiles = -(-a // tm_e) + N_EXPERTS
    n_rows = n_tiles * tm_e
    row_tok = jnp.zeros((n_rows,), jnp.int32).at[row_of_assign].set(
        jnp.arange(a, dtype=jnp.int32) // TOP_K)
    tile_start = jnp.arange(n_tiles, dtype=jnp.int32) * tm_e
    n_used = pad_end[-1] // tm_e
    tile_e_raw = jnp.minimum(jnp.searchsorted(pad_end, tile_start, side='right'), N_EXPERTS - 1)
    used = tile_start < pad_end[-1]
    last = jnp.maximum(n_used - 1, 0)
    tile_src = jnp.where(used, jnp.arange(n_tiles, dtype=jnp.int32), last).astype(jnp.int32)
    tile_e = tile_e_raw[tile_src].astype(jnp.int32)
    valid_rows = jnp.clip(counts[tile_e_raw] - (tile_start - pad_start[tile_e_raw]), 0, tm_e)
    tile_nsub = jnp.where(used, (valid_rows + sub - 1) // sub, 0).astype(jnp.int32)
    spt = tm_e // sub
    sub_valid = (jnp.arange(spt, dtype=jnp.int32)[None, :] < tile_nsub[:, None]).astype(jnp.int32)
    return row_tok, row_of_assign, tile_e, tile_src, tile_nsub, sub_valid.reshape(n_tiles * spt)


def kernel(x, positions, norm1_g, w_in, b_fgate, b_gate, lam_q1, lam_k1, lam_q2, lam_k2, subln_g,
           w_branch_fox, w_branch_diff, w_out, norm2_g, router_w, router_b, w_gu, b_gu, w_down,
           b_down, normf_g):
    b, s, d = x.shape
    t = b * s
    depth = norm1_g.shape[0]
    fox_w = FOX_HEADS * HEAD_DIM
    tm_a = _pick(t, 1024)
    tq = _pick(s, 512)
    tm_d = _pick(t, 256)
    tm_e, sub, tf = 1024, 512, 256
    tmc = _pick(t, 128)

    half = ROPE_DIM // 2
    inv_freq = ROPE_THETA ** (-jnp.arange(0, ROPE_DIM, 2, dtype=F32) / ROPE_DIM)
    zeros = jnp.zeros((LANES - ROPE_DIM,), F32)
    rope_tab = jnp.zeros((8, LANES), F32)
    rope_tab = rope_tab.at[0].set(jnp.concatenate([inv_freq, inv_freq, zeros]))
    rope_tab = rope_tab.at[1].set(jnp.concatenate([-jnp.ones((half,), F32), jnp.zeros((half,), F32), zeros]))
    rope_tab = rope_tab.at[2].set(jnp.concatenate([jnp.zeros((half,), F32), jnp.ones((half,), F32), zeros]))
    pos = positions.astype(F32).reshape(t, 1)

    h = x.reshape(t, d)
    for l in range(depth):
        wi = w_in[l]
        w_main = jnp.concatenate([wi[:, :3 * fox_w], wi[:, 3 * fox_w + FOX_HEADS:]], axis=1).astype(BF16)
        w_f = jnp.pad(wi[:, 3 * fox_w:3 * fox_w + FOX_HEADS], ((0, 0), (0, LANES - FOX_HEADS))).astype(BF16)
        qkv, gate, flog = _inproj(h, pos, norm1_g[l][None, :], w_main, w_f, rope_tab,
                                  b_gate[l][None, :], tm_a)

        fl = flog[:, :FOX_HEADS].reshape(b, s, FOX_HEADS).transpose(0, 2, 1).reshape(b * FOX_HEADS, s)
        bf = jnp.tile(b_fgate[l].astype(F32), b).reshape(b * FOX_HEADS, 1)
        c3 = _fgate(fl, bf).reshape(b * FOX_HEADS, 1, s)

        y_fox = _fox_attn(qkv, c3, b, s, tq)
        lam_init = 0.8 - 0.6 * math.exp(-0.3 * l)
        lam_p = jnp.stack([lam_q1[l], lam_k1[l], lam_q2[l], lam_k2[l]]).astype(F32)
        y_diff = _diff_attn(qkv, lam_p, subln_g[l][None, :].astype(F32), b, s, tq, lam_init)

        rw = jnp.pad(router_w[l], ((0, 0), (0, LANES - N_EXPERTS))).astype(BF16)
        rb = jnp.concatenate([router_b[l].astype(F32), jnp.full((LANES - N_EXPERTS,), NEG_BIG, F32)])[None, :]
        h, u2, idx128, gw128 = _merge(y_fox, y_diff, gate, h, w_branch_fox[l].astype(BF16),
                                      w_branch_diff[l].astype(BF16), w_out[l].astype(BF16),
                                      norm2_g[l][None, :], rw, rb, tm_d)

        row_tok, roa, tile_e, tile_src, tile_nsub, sub_valid = _routing(idx128[:, :TOP_K], tm_e, sub)
        n_sub = sub_valid.shape[0]
        xs = _dispatch(sub_valid, row_tok.reshape(n_sub, 1, sub), u2, sub)
        y_rows = _experts(tile_e, tile_src, tile_nsub, xs, w_gu[l], b_gu[l][:, None, :],
                          w_down[l], b_down[l][:, None, :], tm_e, sub, tf)
        roa3 = roa.reshape(t // tmc, tmc, TOP_K).transpose(0, 2, 1).reshape(t // tmc, 1, TOP_K * tmc)
        h = _combine(roa3, y_rows, h, gw128, normf_g[None, :], tmc, final_norm=(l == depth - 1))
    return h.reshape(b, s, d)
```

```python
import functools
import math

import jax
import jax.numpy as jnp
from jax import lax
from jax.experimental import pallas as pl
from jax.experimental.pallas import tpu as pltpu

HEAD_DIM = 128
FOX_HEADS = 8
DIFF_HEADS = 4
DIFF_V_DIM = 2 * HEAD_DIM
ROPE_THETA = 500000.0
ROPE_DIM = HEAD_DIM // 4
N_EXPERTS = 32
TOP_K = 4
SWIGLU_ALPHA = 1.702
SWIGLU_LIMIT = 7.0
EPS = 1e-5
LANES = 128

F32 = jnp.float32
BF16 = jnp.bfloat16
NEG_BIG = -1e30
LOG2E = math.log2(math.e)
ISSUE_UNROLL = 8
VMEM_LIMIT = 56 * 1024 * 1024


def _cparams(sem, vmem=VMEM_LIMIT):
    return pltpu.CompilerParams(dimension_semantics=sem, vmem_limit_bytes=vmem)


def _pick(n, pref):
    t = min(n, pref)
    while n % t:
        t //= 2
    return t


def _inproj_kernel(x_ref, g1_ref, w_ref, wf_ref, pos_ref, tab_ref, bg_ref,
                   qkv_ref, gate_ref, flog_ref, u_sc, cos_sc, sa_sc, sb_sc, *, scale, n_qkv):
    j = pl.program_id(1)

    @pl.when(j == 0)
    def _():
        xf = x_ref[...]
        ms = jnp.mean(xf * xf, axis=-1, keepdims=True)
        u = (xf * lax.rsqrt(ms + EPS)) * g1_ref[...]
        ub = u.astype(BF16)
        u_sc[...] = ub
        flog_ref[...] = jnp.dot(ub, wf_ref[...], preferred_element_type=F32)
        ang = pos_ref[...] * tab_ref[0:1, :]
        sn = jnp.sin(ang)
        cos_sc[...] = jnp.cos(ang)
        sa_sc[...] = sn * tab_ref[1:2, :]
        sb_sc[...] = sn * tab_ref[2:3, :]

    acc = jnp.dot(u_sc[...], w_ref[...], preferred_element_type=F32)

    def rotary(mult):
        for c in range(acc.shape[1] // HEAD_DIM):
            t = acc[:, c * HEAD_DIM:(c + 1) * HEAD_DIM]
            r = (t * cos_sc[...]
                 + pltpu.roll(t, HEAD_DIM - ROPE_DIM // 2, 1) * sa_sc[...]
                 + pltpu.roll(t, ROPE_DIM // 2, 1) * sb_sc[...])
            if mult != 1.0:
                r = r * mult
            qkv_ref[:, c * HEAD_DIM:(c + 1) * HEAD_DIM] = r.astype(BF16)

    @pl.when(j == 0)
    def _():
        qkv_ref[...] = (acc * scale).astype(BF16)

    @pl.when((j == 1) | (j == 2) | (j == 5))
    def _():
        qkv_ref[...] = acc.astype(BF16)

    @pl.when(j == 3)
    def _():
        rotary(scale)

    @pl.when(j == 4)
    def _():
        rotary(1.0)

    @pl.when(j >= n_qkv)
    def _():
        gate_ref[...] = jax.nn.sigmoid(acc + bg_ref[...])


def _inproj(x2, pos, g1, w_main, w_f, rope_tab, b_gate, tm):
    t, d = x2.shape
    tn = 1024
    n_qkv = 6
    n_gate = (w_main.shape[1] - n_qkv * tn) // tn
    grid = (t // tm, n_qkv + n_gate)
    kern = functools.partial(_inproj_kernel, scale=LOG2E / math.sqrt(HEAD_DIM), n_qkv=n_qkv)
    return pl.pallas_call(
        kern,
        grid=grid,
        in_specs=[
            pl.BlockSpec((tm, d), lambda i, j: (i, 0)),
            pl.BlockSpec((1, d), lambda i, j: (0, 0)),
            pl.BlockSpec((d, tn), lambda i, j: (0, j)),
            pl.BlockSpec((d, LANES), lambda i, j: (0, 0)),
            pl.BlockSpec((tm, 1), lambda i, j: (i, 0)),
            pl.BlockSpec((8, LANES), lambda i, j: (0, 0)),
            pl.BlockSpec((1, tn), lambda i, j: (0, jnp.maximum(j - n_qkv, 0))),
        ],
        out_specs=[
            pl.BlockSpec((tm, tn), lambda i, j: (i, jnp.minimum(j, n_qkv - 1))),
            pl.BlockSpec((tm, tn), lambda i, j: (i, jnp.maximum(j - n_qkv, 0))),
            pl.BlockSpec((tm, LANES), lambda i, j: (i, 0)),
        ],
        out_shape=[
            jax.ShapeDtypeStruct((t, n_qkv * tn), BF16),
            jax.ShapeDtypeStruct((t, n_gate * tn), F32),
            jax.ShapeDtypeStruct((t, LANES), F32),
        ],
        scratch_shapes=[
            pltpu.VMEM((tm, d), BF16),
            pltpu.VMEM((tm, LANES), F32),
            pltpu.VMEM((tm, LANES), F32),
            pltpu.VMEM((tm, LANES), F32),
        ],
        compiler_params=_cparams(("parallel", "arbitrary")),
    )(x2, g1, w_main, w_f, pos, rope_tab, b_gate)


def _fgate_kernel(fl_ref, b_ref, c_ref):
    z = fl_ref[...] + b_ref[...]
    x = jnp.minimum(z, 0.0) - jnp.log(1.0 + jnp.exp(-jnp.abs(z)))
    n = x.shape[1]
    lane = lax.broadcasted_iota(jnp.int32, x.shape, 1)
    sh = 1
    while sh < n:
        x = x + jnp.where(lane >= sh, pltpu.roll(x, sh, 1), 0.0)
        sh *= 2
    c_ref[...] = x


def _fgate(fl, b):
    r, s = fl.shape
    return pl.pallas_call(
        _fgate_kernel,
        grid=(1,),
        in_specs=[pl.BlockSpec((r, s), lambda i: (0, 0)), pl.BlockSpec((r, 1), lambda i: (0, 0))],
        out_specs=pl.BlockSpec((r, s), lambda i: (0, 0)),
        out_shape=jax.ShapeDtypeStruct((r, s), F32),
        compiler_params=_cparams(("arbitrary",)),
    )(fl, b)


def _transpose_bf16(x):
    return x.astype(F32).T.astype(BF16)


def _softmax_step(s, vt, m_sc, l_sc, acc_sc):
    m_old = m_sc[...]
    m_new = jnp.maximum(m_old, jnp.max(s, axis=0, keepdims=True))
    alpha = jnp.exp2(m_old - m_new)
    p = jnp.exp2(s - m_new)
    l_sc[...] = alpha * l_sc[...] + jnp.sum(p, axis=0, keepdims=True)
    acc_sc[...] = alpha * acc_sc[...] + jnp.dot(vt, p.astype(BF16), preferred_element_type=F32)
    m_sc[...] = m_new


def _causal_keep(shape):
    kv = lax.broadcasted_iota(jnp.int32, shape, 0)
    qq = lax.broadcasted_iota(jnp.int32, shape, 1)
    return kv <= qq


FOX_HEADS_PER_STEP = 2


def _fox_kernel(q_ref, k_ref, v_ref, c_ref, o_ref, *scratch, tq, tk, s_len):
    hp = FOX_HEADS_PER_STEP
    qt_sc, kaug_sc, vt_sc, m_sc, l_sc, acc_sc = (scratch[n * hp:(n + 1) * hp] for n in range(6))
    qi = pl.program_id(2)
    head_cols = [slice(h * HEAD_DIM, (h + 1) * HEAD_DIM) for h in range(hp)]

    @pl.when(qi == 0)
    def _():
        sub = lax.broadcasted_iota(jnp.int32, (HEAD_DIM, tk), 0)
        for h in range(hp):
            for ch in range(s_len // tk):
                rows = slice(ch * tk, (ch + 1) * tk)
                vt_sc[h][:, rows] = _transpose_bf16(v_ref[rows, head_cols[h]])
                c = c_ref[h, :, rows] * (-LOG2E)
                hi = c.astype(BF16).astype(F32)
                mid = (c - hi).astype(BF16).astype(F32)
                lo = (c - hi - mid).astype(BF16).astype(F32)
                tab = jnp.where(sub == 0, hi, jnp.where(sub == 1, mid, jnp.where(sub == 2, lo, 0.0)))
                kaug_sc[h][rows, HEAD_DIM:] = tab.T.astype(BF16)
                kaug_sc[h][rows, :HEAD_DIM] = k_ref[rows, head_cols[h]]

    sub_q = lax.broadcasted_iota(jnp.int32, (HEAD_DIM, tq), 0)
    for h in range(hp):
        qt_sc[h][:HEAD_DIM, :] = _transpose_bf16(q_ref[:, head_cols[h]])
        qt_sc[h][HEAD_DIM:, :] = jnp.where(sub_q < 3, 1.0, 0.0).astype(BF16)
        m_sc[h][...] = jnp.full(m_sc[h].shape, -jnp.inf, F32)
        l_sc[h][...] = jnp.zeros(l_sc[h].shape, F32)
        acc_sc[h][...] = jnp.zeros(acc_sc[h].shape, F32)

    def step(start, masked):
        for h in range(hp):
            s = jnp.dot(kaug_sc[h][pl.ds(start, tk), :], qt_sc[h][...], preferred_element_type=F32)
            if masked:
                s = jnp.where(_causal_keep(s.shape), s, -jnp.inf)
            _softmax_step(s, vt_sc[h][:, pl.ds(start, tk)], m_sc[h], l_sc[h], acc_sc[h])

    def body(kc, carry):
        step(pl.multiple_of(kc * tk, tk), False)
        return carry

    lax.fori_loop(0, qi, body, 0)
    step(pl.multiple_of(qi * tq, tq), True)
    for h in range(hp):
        o_ref[:, head_cols[h]] = (acc_sc[h][...] / l_sc[h][...]).T.astype(o_ref.dtype)


def _fox_attn(qkv, c3, b, s, tq):
    t = b * s
    nq = s // tq
    hp = FOX_HEADS_PER_STEP
    groups = FOX_HEADS // hp
    width = hp * HEAD_DIM
    kern = functools.partial(_fox_kernel, tq=tq, tk=tq, s_len=s)

    def per_head(shape, dtype):
        return [pltpu.VMEM(shape, dtype) for _ in range(hp)]

    return pl.pallas_call(
        kern,
        grid=(b, groups, nq),
        in_specs=[
            pl.BlockSpec((tq, width), lambda bi, g, qi: (bi * nq + qi, g)),
            pl.BlockSpec((s, width), lambda bi, g, qi: (bi, groups + g)),
            pl.BlockSpec((s, width), lambda bi, g, qi: (bi, 2 * groups + g)),
            pl.BlockSpec((hp, 1, s), lambda bi, g, qi: (bi * groups + g, 0, 0)),
        ],
        out_specs=pl.BlockSpec((tq, width), lambda bi, g, qi: (bi * nq + qi, g)),
        out_shape=jax.ShapeDtypeStruct((t, FOX_HEADS * HEAD_DIM), BF16),
        scratch_shapes=(per_head((2 * HEAD_DIM, tq), BF16) + per_head((s, 2 * HEAD_DIM), BF16)
                        + per_head((HEAD_DIM, s), BF16) + per_head((1, tq), F32)
                        + per_head((1, tq), F32) + per_head((HEAD_DIM, tq), F32)),
        compiler_params=_cparams(("parallel", "parallel", "arbitrary")),
    )(qkv, qkv, qkv, c3)


def _diff_kernel(q1_ref, q2_ref, k1_ref, k2_ref, v_ref, lam_ref, g_ref, o_ref,
                 q1t_sc, q2t_sc, vt_sc, m1_sc, l1_sc, a1_sc, m2_sc, l2_sc, a2_sc,
                 *, tq, tk, s_len, lam_init):
    qi = pl.program_id(2)

    @pl.when(qi == 0)
    def _():
        for ch in range(s_len // tk):
            rows = slice(ch * tk, (ch + 1) * tk)
            vt_sc[:, rows] = _transpose_bf16(v_ref[rows, :])

    q1t_sc[...] = _transpose_bf16(q1_ref[...])
    q2t_sc[...] = _transpose_bf16(q2_ref[...])
    for r in (m1_sc, m2_sc):
        r[...] = jnp.full(r.shape, -jnp.inf, F32)
    for r in (l1_sc, l2_sc, a1_sc, a2_sc):
        r[...] = jnp.zeros(r.shape, F32)

    def step(start, masked):
        vt = vt_sc[:, pl.ds(start, tk)]
        s1 = jnp.dot(k1_ref[pl.ds(start, tk), :], q1t_sc[...], preferred_element_type=F32)
        s2 = jnp.dot(k2_ref[pl.ds(start, tk), :], q2t_sc[...], preferred_element_type=F32)
        if masked:
            keep = _causal_keep(s1.shape)
            s1 = jnp.where(keep, s1, -jnp.inf)
            s2 = jnp.where(keep, s2, -jnp.inf)
        _softmax_step(s1, vt, m1_sc, l1_sc, a1_sc)
        _softmax_step(s2, vt, m2_sc, l2_sc, a2_sc)

    def body(kc, carry):
        step(pl.multiple_of(kc * tk, tk), False)
        return carry

    lax.fori_loop(0, qi, body, 0)
    step(pl.multiple_of(qi * tq, tq), True)

    lp = lam_ref[...]
    lam = (jnp.exp(jnp.sum(lp[0:1, :] * lp[1:2, :], axis=-1, keepdims=True))
           - jnp.exp(jnp.sum(lp[2:3, :] * lp[3:4, :], axis=-1, keepdims=True)) + lam_init)
    y = (a1_sc[...] / l1_sc[...]).T - lam * (a2_sc[...] / l2_sc[...]).T
    ms = jnp.mean(y * y, axis=-1, keepdims=True)
    y = (y * lax.rsqrt(ms + EPS)) * g_ref[...]
    o_ref[...] = (y * (1.0 - lam_init)).astype(o_ref.dtype)


def _diff_attn(qkv, lam_p, subln_g, b, s, tq, lam_init):
    t = b * s
    nq = s // tq
    qb = 3 * FOX_HEADS
    kb = qb + 2 * DIFF_HEADS
    vb = (kb + 2 * DIFF_HEADS) // 2
    kern = functools.partial(_diff_kernel, tq=tq, tk=tq, s_len=s, lam_init=lam_init)
    return pl.pallas_call(
        kern,
        grid=(b, DIFF_HEADS, nq),
        in_specs=[
            pl.BlockSpec((tq, HEAD_DIM), lambda bi, h, qi: (bi * nq + qi, qb + 2 * h)),
            pl.BlockSpec((tq, HEAD_DIM), lambda bi, h, qi: (bi * nq + qi, qb + 2 * h + 1)),
            pl.BlockSpec((s, HEAD_DIM), lambda bi, h, qi: (bi, kb + 2 * h)),
            pl.BlockSpec((s, HEAD_DIM), lambda bi, h, qi: (bi, kb + 2 * h + 1)),
            pl.BlockSpec((s, DIFF_V_DIM), lambda bi, h, qi: (bi, vb + h)),
            pl.BlockSpec((4, HEAD_DIM), lambda bi, h, qi: (0, 0)),
            pl.BlockSpec((1, DIFF_V_DIM), lambda bi, h, qi: (0, 0)),
        ],
        out_specs=pl.BlockSpec((tq, DIFF_V_DIM), lambda bi, h, qi: (bi * nq + qi, h)),
        out_shape=jax.ShapeDtypeStruct((t, DIFF_HEADS * DIFF_V_DIM), BF16),
        scratch_shapes=[
            pltpu.VMEM((HEAD_DIM, tq), BF16), pltpu.VMEM((HEAD_DIM, tq), BF16),
            pltpu.VMEM((DIFF_V_DIM, s), BF16),
            pltpu.VMEM((1, tq), F32), pltpu.VMEM((1, tq), F32), pltpu.VMEM((DIFF_V_DIM, tq), F32),
            pltpu.VMEM((1, tq), F32), pltpu.VMEM((1, tq), F32), pltpu.VMEM((DIFF_V_DIM, tq), F32),
        ],
        compiler_params=_cparams(("parallel", "parallel", "arbitrary")),
    )(qkv, qkv, qkv, qkv, qkv, lam_p, subln_g)


def _merge_kernel(yf_ref, yd_ref, g0_ref, g1_ref, x_ref, wbf_ref, wbd_ref, wo_ref, n2_ref,
                  rw_ref, rb_ref, h_ref, u_ref, idx_ref, gw_ref):
    ya = jnp.dot(yf_ref[...], wbf_ref[...], preferred_element_type=F32)
    yb = jnp.dot(yd_ref[...], wbd_ref[...], preferred_element_type=F32)
    merged = g0_ref[...] * ya + g1_ref[...] * yb
    h = x_ref[...] + jnp.dot(merged.astype(BF16), wo_ref[...], preferred_element_type=F32)
    h_ref[...] = h
    ms = jnp.mean(h * h, axis=-1, keepdims=True)
    u = (h * lax.rsqrt(ms + EPS)) * n2_ref[...]
    u_ref[...] = u
    logits = jnp.dot(u.astype(BF16), rw_ref[...], preferred_element_type=F32) + rb_ref[...]

    lane = lax.broadcasted_iota(jnp.int32, logits.shape, 1).astype(F32)
    work = logits
    vals, idxs = [], []
    for _ in range(TOP_K):
        m = jnp.max(work, axis=-1, keepdims=True)
        sel = jnp.min(jnp.where(work == m, lane, float(LANES)), axis=-1, keepdims=True)
        vals.append(m)
        idxs.append(sel)
        work = jnp.where(lane == sel, -jnp.inf, work)
    es = [jnp.exp(v - vals[0]) for v in vals]
    den = es[0]
    for e in es[1:]:
        den = den + e
    idx_out = jnp.zeros(logits.shape, F32)
    gw_out = jnp.zeros(logits.shape, F32)
    for k in range(TOP_K):
        idx_out = jnp.where(lane == float(k), idxs[k], idx_out)
        gw_out = jnp.where(lane == float(k), es[k] / den, gw_out)
    idx_ref[...] = idx_out.astype(jnp.int32)
    gw_ref[...] = gw_out


def _merge(yf, yd, gate, x2, wbf, wbd, wo, n2, rw, rb, tm):
    t, d = x2.shape
    fw = yf.shape[1]
    dw = yd.shape[1]
    const = dict(pipeline_mode=pl.Buffered(1))
    return pl.pallas_call(
        _merge_kernel,
        grid=(t // tm,),
        in_specs=[
            pl.BlockSpec((tm, fw), lambda i: (i, 0)),
            pl.BlockSpec((tm, dw), lambda i: (i, 0)),
            pl.BlockSpec((tm, d), lambda i: (i, 0)),
            pl.BlockSpec((tm, d), lambda i: (i, 1)),
            pl.BlockSpec((tm, d), lambda i: (i, 0)),
            pl.BlockSpec((fw, d), lambda i: (0, 0), **const),
            pl.BlockSpec((dw, d), lambda i: (0, 0), **const),
            pl.BlockSpec((d, d), lambda i: (0, 0), **const),
            pl.BlockSpec((1, d), lambda i: (0, 0)),
            pl.BlockSpec((d, LANES), lambda i: (0, 0), **const),
            pl.BlockSpec((1, LANES), lambda i: (0, 0)),
        ],
        out_specs=[
            pl.BlockSpec((tm, d), lambda i: (i, 0)),
            pl.BlockSpec((tm, d), lambda i: (i, 0)),
            pl.BlockSpec((tm, LANES), lambda i: (i, 0)),
            pl.BlockSpec((tm, LANES), lambda i: (i, 0)),
        ],
        out_shape=[
            jax.ShapeDtypeStruct((t, d), F32),
            jax.ShapeDtypeStruct((t, d), F32),
            jax.ShapeDtypeStruct((t, LANES), jnp.int32),
            jax.ShapeDtypeStruct((t, LANES), F32),
        ],
        compiler_params=_cparams(("parallel",)),
    )(yf, yd, gate, gate, x2, wbf, wbd, wo, n2, rw, rb)


def _dispatch_kernel(valid_ref, tok_ref, u_hbm, out_ref, stage, sems, *, rows):
    s = pl.program_id(0)
    n = pl.num_programs(0) - 1
    slot = s % 2

    def row_copy(t, r, sl):
        return pltpu.make_async_copy(u_hbm.at[pl.ds(t, 1), :], stage.at[sl, pl.ds(r, 1), :], sems.at[sl])

    @pl.when(jnp.logical_and(s < n, valid_ref[jnp.minimum(s, n - 1)] > 0))
    def _issue():
        def body(g, carry):
            for r8 in range(ISSUE_UNROLL):
                r = g * ISSUE_UNROLL + r8
                row_copy(tok_ref[0, 0, r], r, slot).start()
            return carry
        lax.fori_loop(0, rows // ISSUE_UNROLL, body, 0)

    @pl.when(s > 0)
    def _drain():
        ps = 1 - slot
        was_valid = valid_ref[jnp.maximum(s - 1, 0)] > 0

        @pl.when(was_valid)
        def _():
            pltpu.make_async_copy(u_hbm.at[pl.ds(0, rows), :], stage.at[ps], sems.at[ps]).wait()
            out_ref[...] = stage[ps].astype(out_ref.dtype)

        @pl.when(jnp.logical_not(was_valid))
        def _():
            out_ref[...] = jnp.zeros(out_ref.shape, out_ref.dtype)


def _dispatch(sub_valid, row_tok3, u, rows):
    n_sub = row_tok3.shape[0]
    d = u.shape[1]
    kern = functools.partial(_dispatch_kernel, rows=rows)
    return pl.pallas_call(
        kern,
        grid_spec=pltpu.PrefetchScalarGridSpec(
            num_scalar_prefetch=1,
            grid=(n_sub + 1,),
            in_specs=[
                pl.BlockSpec((1, 1, rows), lambda s, v: (jnp.minimum(s, n_sub - 1), 0, 0),
                             memory_space=pltpu.SMEM),
                pl.BlockSpec(memory_space=pl.ANY),
            ],
            out_specs=pl.BlockSpec((rows, d), lambda s, v: (jnp.maximum(s - 1, 0), 0)),
            scratch_shapes=[pltpu.VMEM((2, rows, d), F32), pltpu.SemaphoreType.DMA((2,))],
        ),
        out_shape=jax.ShapeDtypeStruct((n_sub * rows, d), BF16),
        compiler_params=_cparams(("arbitrary",)),
    )(sub_valid, row_tok3, u)


def _expert_kernel(te_ref, ts_ref, nsub_ref, xs_ref, wg_ref, wu_ref, bg_ref, bu_ref, wd_ref, bd_ref,
                   out_ref, act_sc, *, sub, spt, nf):
    i = pl.program_id(0)
    j = pl.program_id(1)
    nsub = nsub_ref[i]

    full = nsub == spt

    def gate_up(rs, wg, wu):
        x = xs_ref[rs, :]
        hg = jnp.dot(x, wg, preferred_element_type=F32) + bg_ref[0]
        hu = jnp.dot(x, wu, preferred_element_type=F32) + bu_ref[0]
        hg = jnp.minimum(hg, SWIGLU_LIMIT)
        hl = jnp.clip(hu, -SWIGLU_LIMIT, SWIGLU_LIMIT)
        act = hg * jax.nn.sigmoid(SWIGLU_ALPHA * hg) * (hl + 1.0)
        act_sc[j, rs, :] = act.astype(BF16)

    def down(rs, wd):
        act = jnp.concatenate([act_sc[c, rs, :] for c in range(nf)], axis=1)
        out_ref[rs, :] = jnp.dot(act, wd, preferred_element_type=F32) + bd_ref[0]

    @pl.when(jnp.logical_and(j < nf, full))
    def _():
        gate_up(slice(None), wg_ref[0].astype(BF16), wu_ref[0].astype(BF16))

    @pl.when(jnp.logical_and(j < nf, jnp.logical_and(nsub > 0, nsub < spt)))
    def _():
        wg = wg_ref[0].astype(BF16)
        wu = wu_ref[0].astype(BF16)
        for s in range(spt - 1):
            pl.when(s < nsub)(functools.partial(gate_up, slice(s * sub, (s + 1) * sub), wg, wu))

    @pl.when(jnp.logical_and(j >= nf, full))
    def _():
        down(slice(None), wd_ref[0].astype(BF16))

    @pl.when(jnp.logical_and(j >= nf, nsub < spt))
    def _():
        wd = wd_ref[0].astype(BF16)
        for s in range(spt):
            rs = slice(s * sub, (s + 1) * sub)
            if s < spt - 1:
                pl.when(s < nsub)(functools.partial(down, rs, wd))

            @pl.when(s >= nsub)
            def _():
                out_ref[rs, :] = jnp.zeros((sub, out_ref.shape[1]), out_ref.dtype)


def _experts(tile_e, tile_src, tile_nsub, xs, w_gu, b_gu3, w_down, b_down3, tm_e, sub, tf):
    n_rows, d = xs.shape
    n_tiles = n_rows // tm_e
    dff = w_down.shape[1]
    nf = dff // tf
    nn = d // tf
    assert nn == nf
    kern = functools.partial(_expert_kernel, sub=sub, spt=tm_e // sub, nf=nf)

    def c1(i, j, ns):
        return jnp.where(ns[i] > 0, jnp.minimum(j, nf - 1), nf - 1)

    def c2(i, j, ns):
        return jnp.where(ns[i] > 0, jnp.maximum(j - nf, 0), nf - 1)

    return pl.pallas_call(
        kern,
        grid_spec=pltpu.PrefetchScalarGridSpec(
            num_scalar_prefetch=3,
            grid=(n_tiles, 2 * nf),
            in_specs=[
                pl.BlockSpec((tm_e, d), lambda i, j, te, ts, ns: (ts[i], 0)),
                pl.BlockSpec((1, d, tf), lambda i, j, te, ts, ns: (te[i], 0, c1(i, j, ns))),
                pl.BlockSpec((1, d, tf), lambda i, j, te, ts, ns: (te[i], 0, nf + c1(i, j, ns))),
                pl.BlockSpec((1, 1, tf), lambda i, j, te, ts, ns: (te[i], 0, c1(i, j, ns))),
                pl.BlockSpec((1, 1, tf), lambda i, j, te, ts, ns: (te[i], 0, nf + c1(i, j, ns))),
                pl.BlockSpec((1, dff, tf), lambda i, j, te, ts, ns: (te[i], 0, c2(i, j, ns))),
                pl.BlockSpec((1, 1, tf), lambda i, j, te, ts, ns: (te[i], 0, c2(i, j, ns))),
            ],
            out_specs=pl.BlockSpec((tm_e, tf), lambda i, j, te, ts, ns: (i, jnp.maximum(j - nf, 0))),
            scratch_shapes=[pltpu.VMEM((nf, tm_e, tf), BF16)],
        ),
        out_shape=jax.ShapeDtypeStruct((n_rows, d), F32),
        compiler_params=_cparams(("arbitrary", "arbitrary")),
    )(tile_e, tile_src, tile_nsub, xs, w_gu, w_gu, b_gu3, b_gu3, w_down, b_down3)


def _combine_kernel(cur_ref, nxt_ref, yr_hbm, h_ref, gw_ref, gf_ref, out_ref, ysel, sems, *, tmc,
                    final_norm):
    i = pl.program_id(0)
    n = pl.num_programs(0)
    slot = i % 2
    n_dma = TOP_K * tmc

    def row_copy(row, q, sl):
        return pltpu.make_async_copy(yr_hbm.at[pl.ds(row, 1), :], ysel.at[sl, pl.ds(q, 1), :], sems.at[sl])

    def issue(idx_ref, sl):
        def body(g, carry):
            for q8 in range(ISSUE_UNROLL):
                q = g * ISSUE_UNROLL + q8
                row_copy(idx_ref[0, 0, q], q, sl).start()
            return carry
        lax.fori_loop(0, n_dma // ISSUE_UNROLL, body, 0)

    @pl.when(i == 0)
    def _():
        issue(cur_ref, 0)

    @pl.when(i + 1 < n)
    def _():
        issue(nxt_ref, 1 - slot)

    pltpu.make_async_copy(yr_hbm.at[pl.ds(0, n_dma), :], ysel.at[slot], sems.at[slot]).wait()

    gw = gw_ref[...]
    y = h_ref[...]
    for k in range(TOP_K):
        y = y + gw[:, k:k + 1] * ysel[slot, k * tmc:(k + 1) * tmc, :]
    if final_norm:
        ms = jnp.mean(y * y, axis=-1, keepdims=True)
        y = (y * lax.rsqrt(ms + EPS)) * gf_ref[...]
    out_ref[...] = y


def _combine(roa3, y_rows, h, gw, gf, tmc, final_norm):
    t, d = h.shape
    n = t // tmc
    kern = functools.partial(_combine_kernel, tmc=tmc, final_norm=final_norm)
    return pl.pallas_call(
        kern,
        grid=(n,),
        in_specs=[
            pl.BlockSpec((1, 1, TOP_K * tmc), lambda i: (i, 0, 0), memory_space=pltpu.SMEM),
            pl.BlockSpec((1, 1, TOP_K * tmc), lambda i: (jnp.minimum(i + 1, n - 1), 0, 0),
                         memory_space=pltpu.SMEM),
            pl.BlockSpec(memory_space=pl.ANY),
            pl.BlockSpec((tmc, d), lambda i: (i, 0)),
            pl.BlockSpec((tmc, LANES), lambda i: (i, 0)),
            pl.BlockSpec((1, d), lambda i: (0, 0)),
        ],
        out_specs=pl.BlockSpec((tmc, d), lambda i: (i, 0)),
        out_shape=jax.ShapeDtypeStruct((t, d), F32),
        scratch_shapes=[pltpu.VMEM((2, TOP_K * tmc, d), F32), pltpu.SemaphoreType.DMA((2,))],
        compiler_params=_cparams(("arbitrary",)),
    )(roa3, roa3, y_rows, h, gw, gf)


def _routing(top_idx, tm_e, sub, blk):
    t = top_idx.shape[0]
    a = t * TOP_K
    flat_e = top_idx.reshape(a)
    onehot = (flat_e[:, None] == jnp.arange(N_EXPERTS, dtype=jnp.int32)[None, :]).astype(jnp.int32)
    csum = jnp.cumsum(onehot, axis=0)
    counts = csum[-1]
    rank = jnp.sum(onehot * csum, axis=1) - 1
    padded = (counts + tm_e - 1) // tm_e * tm_e
    pad_end = jnp.cumsum(padded)
    pad_start = pad_end - padded
    row_of_assign = (pad_start[flat_e] + rank).astype(jnp.int32)
    n_tiles = -(-a // tm_e) + N_EXPERTS
    n_rows = n_tiles * tm_e
    row_tok = jnp.zeros((n_rows,), jnp.int32).at[row_of_assign].set(
        jnp.arange(a, dtype=jnp.int32) // TOP_K)
    tile_start = jnp.arange(n_tiles, dtype=jnp.int32) * tm_e
    n_used = pad_end[-1] // tm_e
    tile_e_raw = jnp.minimum(jnp.searchsorted(pad_end, tile_start, side='right'), N_EXPERTS - 1)
    used = tile_start < pad_end[-1]
    last = jnp.maximum(n_used - 1, 0)
    tile_src = jnp.where(used, jnp.arange(n_tiles, dtype=jnp.int32), last).astype(jnp.int32)
    tile_e = tile_e_raw[tile_src].astype(jnp.int32)
    valid_rows = jnp.clip(counts[tile_e_raw] - (tile_start - pad_start[tile_e_raw]), 0, tm_e)
    tile_nsub = jnp.where(used, (valid_rows + sub - 1) // sub, 0).astype(jnp.int32)
    bpt = tm_e // blk
    blk_valid = (jnp.arange(bpt, dtype=jnp.int32)[None, :] * blk
                 < jnp.where(used, valid_rows, 0)[:, None]).astype(jnp.int32)
    return row_tok, row_of_assign, tile_e, tile_src, tile_nsub, blk_valid.reshape(n_tiles * bpt)


def kernel(x, positions, norm1_g, w_in, b_fgate, b_gate, lam_q1, lam_k1, lam_q2, lam_k2, subln_g,
           w_branch_fox, w_branch_diff, w_out, norm2_g, router_w, router_b, w_gu, b_gu, w_down,
           b_down, normf_g):
    b, s, d = x.shape
    t = b * s
    depth = norm1_g.shape[0]
    fox_w = FOX_HEADS * HEAD_DIM
    tm_a = _pick(t, 1024)
    tq = _pick(s, 512)
    tm_d = _pick(t, 256)
    tm_e, sub, tf, blk = 1024, 256, 512, 512
    tmc = _pick(t, 128)

    half = ROPE_DIM // 2
    inv_freq = ROPE_THETA ** (-jnp.arange(0, ROPE_DIM, 2, dtype=F32) / ROPE_DIM)
    zeros = jnp.zeros((LANES - ROPE_DIM,), F32)
    rope_tab = jnp.zeros((8, LANES), F32)
    rope_tab = rope_tab.at[0].set(jnp.concatenate([inv_freq, inv_freq, zeros]))
    rope_tab = rope_tab.at[1].set(jnp.concatenate([-jnp.ones((half,), F32), jnp.zeros((half,), F32), zeros]))
    rope_tab = rope_tab.at[2].set(jnp.concatenate([jnp.zeros((half,), F32), jnp.ones((half,), F32), zeros]))
    pos = positions.astype(F32).reshape(t, 1)

    h = x.reshape(t, d)
    for l in range(depth):
        wi = w_in[l]
        w_main = jnp.concatenate([wi[:, :3 * fox_w], wi[:, 3 * fox_w + FOX_HEADS:]], axis=1).astype(BF16)
        w_f = jnp.pad(wi[:, 3 * fox_w:3 * fox_w + FOX_HEADS], ((0, 0), (0, LANES - FOX_HEADS))).astype(BF16)
        qkv, gate, flog = _inproj(h, pos, norm1_g[l][None, :], w_main, w_f, rope_tab,
                                  b_gate[l][None, :], tm_a)

        fl = flog[:, :FOX_HEADS].reshape(b, s, FOX_HEADS).transpose(0, 2, 1).reshape(b * FOX_HEADS, s)
        bf = jnp.tile(b_fgate[l].astype(F32), b).reshape(b * FOX_HEADS, 1)
        c3 = _fgate(fl, bf).reshape(b * FOX_HEADS, 1, s)

        y_fox = _fox_attn(qkv, c3, b, s, tq)
        lam_init = 0.8 - 0.6 * math.exp(-0.3 * l)
        lam_p = jnp.stack([lam_q1[l], lam_k1[l], lam_q2[l], lam_k2[l]]).astype(F32)
        y_diff = _diff_attn(qkv, lam_p, subln_g[l][None, :].astype(F32), b, s, tq, lam_init)

        rw = jnp.pad(router_w[l], ((0, 0), (0, LANES - N_EXPERTS))).astype(BF16)
        rb = jnp.concatenate([router_b[l].astype(F32), jnp.full((LANES - N_EXPERTS,), NEG_BIG, F32)])[None, :]
        h, u2, idx128, gw128 = _merge(y_fox, y_diff, gate, h, w_branch_fox[l].astype(BF16),
                                      w_branch_diff[l].astype(BF16), w_out[l].astype(BF16),
                                      norm2_g[l][None, :], rw, rb, tm_d)

        row_tok, roa, tile_e, tile_src, tile_nsub, blk_valid = _routing(idx128[:, :TOP_K], tm_e, sub, blk)
        n_blk = blk_valid.shape[0]
        xs = _dispatch(blk_valid, row_tok.reshape(n_blk, 1, blk), u2, blk)
        y_rows = _experts(tile_e, tile_src, tile_nsub, xs, w_gu[l], b_gu[l][:, None, :],
                          w_down[l], b_down[l][:, None, :], tm_e, sub, tf)
        roa3 = roa.reshape(t // tmc, tmc, TOP_K).transpose(0, 2, 1).reshape(t // tmc, 1, TOP_K * tmc)
        h = _combine(roa3, y_rows, h, gw128, normf_g[None, :], tmc, final_norm=(l == depth - 1))
    return h.reshape(b, s, d)
```

```python
import functools
import math

import jax
import jax.numpy as jnp
from jax import lax
from jax.experimental import pallas as pl
from jax.experimental.pallas import tpu as pltpu

HEAD_DIM = 128
FOX_HEADS = 8
DIFF_HEADS = 4
DIFF_V_DIM = 2 * HEAD_DIM
ROPE_THETA = 500000.0
ROPE_DIM = HEAD_DIM // 4
N_EXPERTS = 32
TOP_K = 4
SWIGLU_ALPHA = 1.702
SWIGLU_LIMIT = 7.0
EPS = 1e-5
LANES = 128

F32 = jnp.float32
BF16 = jnp.bfloat16
NEG_BIG = -1e30
LOG2E = math.log2(math.e)
ISSUE_UNROLL = 8
VMEM_LIMIT = 56 * 1024 * 1024


def _cparams(sem, vmem=VMEM_LIMIT):
    return pltpu.CompilerParams(dimension_semantics=sem, vmem_limit_bytes=vmem)


def _pick(n, pref):
    t = min(n, pref)
    while n % t:
        t //= 2
    return t


def _inproj_kernel(x_ref, g1_ref, w_ref, wf_ref, pos_ref, tab_ref, bg_ref,
                   qkv_ref, gate_ref, flog_ref, u_sc, cos_sc, sa_sc, sb_sc, *, scale, n_qkv):
    j = pl.program_id(1)

    @pl.when(j == 0)
    def _():
        xf = x_ref[...]
        ms = jnp.mean(xf * xf, axis=-1, keepdims=True)
        u = (xf * lax.rsqrt(ms + EPS)) * g1_ref[...]
        ub = u.astype(BF16)
        u_sc[...] = ub
        flog_ref[...] = jnp.dot(ub, wf_ref[...], preferred_element_type=F32)
        ang = pos_ref[...] * tab_ref[0:1, :]
        sn = jnp.sin(ang)
        cos_sc[...] = jnp.cos(ang)
        sa_sc[...] = sn * tab_ref[1:2, :]
        sb_sc[...] = sn * tab_ref[2:3, :]

    acc = jnp.dot(u_sc[...], w_ref[...], preferred_element_type=F32)

    def rotary(mult):
        for c in range(acc.shape[1] // HEAD_DIM):
            t = acc[:, c * HEAD_DIM:(c + 1) * HEAD_DIM]
            r = (t * cos_sc[...]
                 + pltpu.roll(t, HEAD_DIM - ROPE_DIM // 2, 1) * sa_sc[...]
                 + pltpu.roll(t, ROPE_DIM // 2, 1) * sb_sc[...])
            if mult != 1.0:
                r = r * mult
            qkv_ref[:, c * HEAD_DIM:(c + 1) * HEAD_DIM] = r.astype(BF16)

    @pl.when(j == 0)
    def _():
        qkv_ref[...] = (acc * scale).astype(BF16)

    @pl.when((j == 1) | (j == 2) | (j == 5))
    def _():
        qkv_ref[...] = acc.astype(BF16)

    @pl.when(j == 3)
    def _():
        rotary(scale)

    @pl.when(j == 4)
    def _():
        rotary(1.0)

    @pl.when(j >= n_qkv)
    def _():
        gate_ref[...] = jax.nn.sigmoid(acc + bg_ref[...])


def _inproj(x2, pos, g1, w_main, w_f, rope_tab, b_gate, tm):
    t, d = x2.shape
    tn = 1024
    n_qkv = 6
    n_gate = (w_main.shape[1] - n_qkv * tn) // tn
    grid = (t // tm, n_qkv + n_gate)
    kern = functools.partial(_inproj_kernel, scale=LOG2E / math.sqrt(HEAD_DIM), n_qkv=n_qkv)
    return pl.pallas_call(
        kern,
        grid=grid,
        in_specs=[
            pl.BlockSpec((tm, d), lambda i, j: (i, 0)),
            pl.BlockSpec((1, d), lambda i, j: (0, 0)),
            pl.BlockSpec((d, tn), lambda i, j: (0, j)),
            pl.BlockSpec((d, LANES), lambda i, j: (0, 0)),
            pl.BlockSpec((tm, 1), lambda i, j: (i, 0)),
            pl.BlockSpec((8, LANES), lambda i, j: (0, 0)),
            pl.BlockSpec((1, tn), lambda i, j: (0, jnp.maximum(j - n_qkv, 0))),
        ],
        out_specs=[
            pl.BlockSpec((tm, tn), lambda i, j: (i, jnp.minimum(j, n_qkv - 1))),
            pl.BlockSpec((tm, tn), lambda i, j: (i, jnp.maximum(j - n_qkv, 0))),
            pl.BlockSpec((tm, LANES), lambda i, j: (i, 0)),
        ],
        out_shape=[
            jax.ShapeDtypeStruct((t, n_qkv * tn), BF16),
            jax.ShapeDtypeStruct((t, n_gate * tn), F32),
            jax.ShapeDtypeStruct((t, LANES), F32),
        ],
        scratch_shapes=[
            pltpu.VMEM((tm, d), BF16),
            pltpu.VMEM((tm, LANES), F32),
            pltpu.VMEM((tm, LANES), F32),
            pltpu.VMEM((tm, LANES), F32),
        ],
        compiler_params=_cparams(("parallel", "arbitrary")),
    )(x2, g1, w_main, w_f, pos, rope_tab, b_gate)


def _fgate_kernel(fl_ref, b_ref, c_ref):
    z = fl_ref[...] + b_ref[...]
    x = jnp.minimum(z, 0.0) - jnp.log(1.0 + jnp.exp(-jnp.abs(z)))
    n = x.shape[1]
    lane = lax.broadcasted_iota(jnp.int32, x.shape, 1)
    sh = 1
    while sh < n:
        x = x + jnp.where(lane >= sh, pltpu.roll(x, sh, 1), 0.0)
        sh *= 2
    c_ref[...] = x


def _fgate(fl, b):
    r, s = fl.shape
    return pl.pallas_call(
        _fgate_kernel,
        grid=(1,),
        in_specs=[pl.BlockSpec((r, s), lambda i: (0, 0)), pl.BlockSpec((r, 1), lambda i: (0, 0))],
        out_specs=pl.BlockSpec((r, s), lambda i: (0, 0)),
        out_shape=jax.ShapeDtypeStruct((r, s), F32),
        compiler_params=_cparams(("arbitrary",)),
    )(fl, b)


def _transpose_bf16(x):
    return x.astype(F32).T.astype(BF16)


def _softmax_step(s, vt, m_sc, l_sc, acc_sc):
    m_old = m_sc[...]
    m_new = jnp.maximum(m_old, jnp.max(s, axis=0, keepdims=True))
    alpha = jnp.exp2(m_old - m_new)
    p = jnp.exp2(s - m_new)
    l_sc[...] = alpha * l_sc[...] + jnp.sum(p, axis=0, keepdims=True)
    acc_sc[...] = alpha * acc_sc[...] + jnp.dot(vt, p.astype(BF16), preferred_element_type=F32)
    m_sc[...] = m_new


def _causal_keep(shape):
    kv = lax.broadcasted_iota(jnp.int32, shape, 0)
    qq = lax.broadcasted_iota(jnp.int32, shape, 1)
    return kv <= qq


FOX_HEADS_PER_STEP = 2


def _fox_kernel(q_ref, k_ref, v_ref, c_ref, o_ref, *scratch, tq, tk, s_len):
    hp = FOX_HEADS_PER_STEP
    qt_sc, kaug_sc, vt_sc, m_sc, l_sc, acc_sc = (scratch[n * hp:(n + 1) * hp] for n in range(6))
    qi = pl.program_id(2)
    head_cols = [slice(h * HEAD_DIM, (h + 1) * HEAD_DIM) for h in range(hp)]

    @pl.when(qi == 0)
    def _():
        sub = lax.broadcasted_iota(jnp.int32, (HEAD_DIM, tk), 0)
        for h in range(hp):
            for ch in range(s_len // tk):
                rows = slice(ch * tk, (ch + 1) * tk)
                vt_sc[h][:, rows] = _transpose_bf16(v_ref[rows, head_cols[h]])
                c = c_ref[h, :, rows] * (-LOG2E)
                hi = c.astype(BF16).astype(F32)
                mid = (c - hi).astype(BF16).astype(F32)
                lo = (c - hi - mid).astype(BF16).astype(F32)
                tab = jnp.where(sub == 0, hi, jnp.where(sub == 1, mid, jnp.where(sub == 2, lo, 0.0)))
                kaug_sc[h][rows, HEAD_DIM:] = tab.T.astype(BF16)
                kaug_sc[h][rows, :HEAD_DIM] = k_ref[rows, head_cols[h]]

    sub_q = lax.broadcasted_iota(jnp.int32, (HEAD_DIM, tq), 0)
    for h in range(hp):
        qt_sc[h][:HEAD_DIM, :] = _transpose_bf16(q_ref[:, head_cols[h]])
        qt_sc[h][HEAD_DIM:, :] = jnp.where(sub_q < 3, 1.0, 0.0).astype(BF16)
        m_sc[h][...] = jnp.full(m_sc[h].shape, -jnp.inf, F32)
        l_sc[h][...] = jnp.zeros(l_sc[h].shape, F32)
        acc_sc[h][...] = jnp.zeros(acc_sc[h].shape, F32)

    def step(start, masked):
        for h in range(hp):
            s = jnp.dot(kaug_sc[h][pl.ds(start, tk), :], qt_sc[h][...], preferred_element_type=F32)
            if masked:
                s = jnp.where(_causal_keep(s.shape), s, -jnp.inf)
            _softmax_step(s, vt_sc[h][:, pl.ds(start, tk)], m_sc[h], l_sc[h], acc_sc[h])

    def body(kc, carry):
        step(pl.multiple_of(kc * tk, tk), False)
        return carry

    lax.fori_loop(0, qi, body, 0)
    step(pl.multiple_of(qi * tq, tq), True)
    for h in range(hp):
        o_ref[:, head_cols[h]] = (acc_sc[h][...] / l_sc[h][...]).T.astype(o_ref.dtype)


def _fox_attn(qkv, c3, b, s, tq):
    t = b * s
    nq = s // tq
    hp = FOX_HEADS_PER_STEP
    groups = FOX_HEADS // hp
    width = hp * HEAD_DIM
    kern = functools.partial(_fox_kernel, tq=tq, tk=tq, s_len=s)

    def per_head(shape, dtype):
        return [pltpu.VMEM(shape, dtype) for _ in range(hp)]

    return pl.pallas_call(
        kern,
        grid=(b, groups, nq),
        in_specs=[
            pl.BlockSpec((tq, width), lambda bi, g, qi: (bi * nq + qi, g)),
            pl.BlockSpec((s, width), lambda bi, g, qi: (bi, groups + g)),
            pl.BlockSpec((s, width), lambda bi, g, qi: (bi, 2 * groups + g)),
            pl.BlockSpec((hp, 1, s), lambda bi, g, qi: (bi * groups + g, 0, 0)),
        ],
        out_specs=pl.BlockSpec((tq, width), lambda bi, g, qi: (bi * nq + qi, g)),
        out_shape=jax.ShapeDtypeStruct((t, FOX_HEADS * HEAD_DIM), BF16),
        scratch_shapes=(per_head((2 * HEAD_DIM, tq), BF16) + per_head((s, 2 * HEAD_DIM), BF16)
                        + per_head((HEAD_DIM, s), BF16) + per_head((1, tq), F32)
                        + per_head((1, tq), F32) + per_head((HEAD_DIM, tq), F32)),
        compiler_params=_cparams(("parallel", "parallel", "arbitrary")),
    )(qkv, qkv, qkv, c3)


def _diff_kernel(q1_ref, q2_ref, k1_ref, k2_ref, v_ref, lam_ref, g_ref, o_ref,
                 q1t_sc, q2t_sc, vt_sc, m1_sc, l1_sc, a1_sc, m2_sc, l2_sc, a2_sc,
                 *, tq, tk, s_len, lam_init):
    qi = pl.program_id(2)

    @pl.when(qi == 0)
    def _():
        for ch in range(s_len // tk):
            rows = slice(ch * tk, (ch + 1) * tk)
            vt_sc[:, rows] = _transpose_bf16(v_ref[rows, :])

    q1t_sc[...] = _transpose_bf16(q1_ref[...])
    q2t_sc[...] = _transpose_bf16(q2_ref[...])
    for r in (m1_sc, m2_sc):
        r[...] = jnp.full(r.shape, -jnp.inf, F32)
    for r in (l1_sc, l2_sc, a1_sc, a2_sc):
        r[...] = jnp.zeros(r.shape, F32)

    def step(start, masked):
        vt = vt_sc[:, pl.ds(start, tk)]
        s1 = jnp.dot(k1_ref[pl.ds(start, tk), :], q1t_sc[...], preferred_element_type=F32)
        s2 = jnp.dot(k2_ref[pl.ds(start, tk), :], q2t_sc[...], preferred_element_type=F32)
        if masked:
            keep = _causal_keep(s1.shape)
            s1 = jnp.where(keep, s1, -jnp.inf)
            s2 = jnp.where(keep, s2, -jnp.inf)
        _softmax_step(s1, vt, m1_sc, l1_sc, a1_sc)
        _softmax_step(s2, vt, m2_sc, l2_sc, a2_sc)

    def body(kc, carry):
        step(pl.multiple_of(kc * tk, tk), False)
        return carry

    lax.fori_loop(0, qi, body, 0)
    step(pl.multiple_of(qi * tq, tq), True)

    lp = lam_ref[...]
    lam = (jnp.exp(jnp.sum(lp[0:1, :] * lp[1:2, :], axis=-1, keepdims=True))
           - jnp.exp(jnp.sum(lp[2:3, :] * lp[3:4, :], axis=-1, keepdims=True)) + lam_init)
    y = (a1_sc[...] / l1_sc[...]).T - lam * (a2_sc[...] / l2_sc[...]).T
    ms = jnp.mean(y * y, axis=-1, keepdims=True)
    y = (y * lax.rsqrt(ms + EPS)) * g_ref[...]
    o_ref[...] = (y * (1.0 - lam_init)).astype(o_ref.dtype)


def _diff_attn(qkv, lam_p, subln_g, b, s, tq, lam_init):
    t = b * s
    nq = s // tq
    qb = 3 * FOX_HEADS
    kb = qb + 2 * DIFF_HEADS
    vb = (kb + 2 * DIFF_HEADS) // 2
    kern = functools.partial(_diff_kernel, tq=tq, tk=tq, s_len=s, lam_init=lam_init)
    return pl.pallas_call(
        kern,
        grid=(b, DIFF_HEADS, nq),
        in_specs=[
            pl.BlockSpec((tq, HEAD_DIM), lambda bi, h, qi: (bi * nq + qi, qb + 2 * h)),
            pl.BlockSpec((tq, HEAD_DIM), lambda bi, h, qi: (bi * nq + qi, qb + 2 * h + 1)),
            pl.BlockSpec((s, HEAD_DIM), lambda bi, h, qi: (bi, kb + 2 * h)),
            pl.BlockSpec((s, HEAD_DIM), lambda bi, h, qi: (bi, kb + 2 * h + 1)),
            pl.BlockSpec((s, DIFF_V_DIM), lambda bi, h, qi: (bi, vb + h)),
            pl.BlockSpec((4, HEAD_DIM), lambda bi, h, qi: (0, 0)),
            pl.BlockSpec((1, DIFF_V_DIM), lambda bi, h, qi: (0, 0)),
        ],
        out_specs=pl.BlockSpec((tq, DIFF_V_DIM), lambda bi, h, qi: (bi * nq + qi, h)),
        out_shape=jax.ShapeDtypeStruct((t, DIFF_HEADS * DIFF_V_DIM), BF16),
        scratch_shapes=[
            pltpu.VMEM((HEAD_DIM, tq), BF16), pltpu.VMEM((HEAD_DIM, tq), BF16),
            pltpu.VMEM((DIFF_V_DIM, s), BF16),
            pltpu.VMEM((1, tq), F32), pltpu.VMEM((1, tq), F32), pltpu.VMEM((DIFF_V_DIM, tq), F32),
            pltpu.VMEM((1, tq), F32), pltpu.VMEM((1, tq), F32), pltpu.VMEM((DIFF_V_DIM, tq), F32),
        ],
        compiler_params=_cparams(("parallel", "parallel", "arbitrary")),
    )(qkv, qkv, qkv, qkv, qkv, lam_p, subln_g)


def _pack_bf16_pairs(xb):
    w = xb.shape[1] // 2
    lo = lax.bitcast_convert_type(xb[:, :w].astype(F32), jnp.uint32)
    hi = lax.bitcast_convert_type(xb[:, w:].astype(F32), jnp.uint32)
    return (lo >> 16) | (hi & jnp.uint32(0xFFFF0000))


def _unpack_bf16_pairs(p):
    lo = lax.bitcast_convert_type(p << 16, F32).astype(BF16)
    hi = lax.bitcast_convert_type(p & jnp.uint32(0xFFFF0000), F32).astype(BF16)
    return lo, hi


def _merge_kernel(yf_ref, yd_ref, g0_ref, g1_ref, x_ref, wbf_ref, wbd_ref, wo_ref, n2_ref,
                  rw_ref, rb_ref, h_ref, u_ref, idx_ref, gw_ref):
    ya = jnp.dot(yf_ref[...], wbf_ref[...], preferred_element_type=F32)
    yb = jnp.dot(yd_ref[...], wbd_ref[...], preferred_element_type=F32)
    merged = g0_ref[...] * ya + g1_ref[...] * yb
    h = x_ref[...] + jnp.dot(merged.astype(BF16), wo_ref[...], preferred_element_type=F32)
    h_ref[...] = h
    ms = jnp.mean(h * h, axis=-1, keepdims=True)
    u = (h * lax.rsqrt(ms + EPS)) * n2_ref[...]
    ub = u.astype(BF16)
    u_ref[...] = _pack_bf16_pairs(ub)
    logits = jnp.dot(ub, rw_ref[...], preferred_element_type=F32) + rb_ref[...]

    lane = lax.broadcasted_iota(jnp.int32, logits.shape, 1).astype(F32)
    work = logits
    vals, idxs = [], []
    for _ in range(TOP_K):
        m = jnp.max(work, axis=-1, keepdims=True)
        sel = jnp.min(jnp.where(work == m, lane, float(LANES)), axis=-1, keepdims=True)
        vals.append(m)
        idxs.append(sel)
        work = jnp.where(lane == sel, -jnp.inf, work)
    es = [jnp.exp(v - vals[0]) for v in vals]
    den = es[0]
    for e in es[1:]:
        den = den + e
    idx_out = jnp.zeros(logits.shape, F32)
    gw_out = jnp.zeros(logits.shape, F32)
    for k in range(TOP_K):
        idx_out = jnp.where(lane == float(k), idxs[k], idx_out)
        gw_out = jnp.where(lane == float(k), es[k] / den, gw_out)
    idx_ref[...] = idx_out.astype(jnp.int32)
    gw_ref[...] = gw_out


def _merge(yf, yd, gate, x2, wbf, wbd, wo, n2, rw, rb, tm):
    t, d = x2.shape
    fw = yf.shape[1]
    dw = yd.shape[1]
    const = dict(pipeline_mode=pl.Buffered(1))
    return pl.pallas_call(
        _merge_kernel,
        grid=(t // tm,),
        in_specs=[
            pl.BlockSpec((tm, fw), lambda i: (i, 0)),
            pl.BlockSpec((tm, dw), lambda i: (i, 0)),
            pl.BlockSpec((tm, d), lambda i: (i, 0)),
            pl.BlockSpec((tm, d), lambda i: (i, 1)),
            pl.BlockSpec((tm, d), lambda i: (i, 0)),
            pl.BlockSpec((fw, d), lambda i: (0, 0), **const),
            pl.BlockSpec((dw, d), lambda i: (0, 0), **const),
            pl.BlockSpec((d, d), lambda i: (0, 0), **const),
            pl.BlockSpec((1, d), lambda i: (0, 0)),
            pl.BlockSpec((d, LANES), lambda i: (0, 0), **const),
            pl.BlockSpec((1, LANES), lambda i: (0, 0)),
        ],
        out_specs=[
            pl.BlockSpec((tm, d), lambda i: (i, 0)),
            pl.BlockSpec((tm, d // 2), lambda i: (i, 0)),
            pl.BlockSpec((tm, LANES), lambda i: (i, 0)),
            pl.BlockSpec((tm, LANES), lambda i: (i, 0)),
        ],
        out_shape=[
            jax.ShapeDtypeStruct((t, d), F32),
            jax.ShapeDtypeStruct((t, d // 2), jnp.uint32),
            jax.ShapeDtypeStruct((t, LANES), jnp.int32),
            jax.ShapeDtypeStruct((t, LANES), F32),
        ],
        compiler_params=_cparams(("parallel",)),
    )(yf, yd, gate, gate, x2, wbf, wbd, wo, n2, rw, rb)


def _expert_kernel(te_ref, nsub_ref, cur_ref, nxt_ref, u_hbm, wg_ref, wu_ref, bg_ref, bu_ref, wd_ref,
                   bd_ref, out_ref, stage, x_sc, act_sc, sems, *, sub, spt, nf):
    i = pl.program_id(0)
    j = pl.program_id(1)
    nsub = nsub_ref[i]
    slot = i % 2
    tm_e = x_sc.shape[0]
    per_step = tm_e // (2 * nf)
    half = x_sc.shape[1] // 2

    def start_rows(tok_ref, sl, first, count):
        for r in range(count):
            row = first + r
            pltpu.make_async_copy(u_hbm.at[pl.ds(tok_ref[0, 0, row], 1), :],
                                  stage.at[sl, pl.ds(row, 1), :], sems.at[sl]).start()

    def start_next_tile_rows():
        start_rows(nxt_ref, 1 - slot, j * per_step, per_step)

    @pl.when(jnp.logical_and(i == 0, j == 0))
    def _():
        def body(g, carry):
            start_rows(cur_ref, 0, g * ISSUE_UNROLL, ISSUE_UNROLL)
            return carry
        lax.fori_loop(0, tm_e // ISSUE_UNROLL, body, 0)

    fetched = jnp.logical_or(i == 0, nsub_ref[jnp.maximum(i - 1, 0)] > 0)

    @pl.when(jnp.logical_and(j == 0, fetched))
    def _():
        pltpu.make_async_copy(u_hbm.at[pl.ds(0, tm_e), :], stage.at[slot], sems.at[slot]).wait()

        @pl.when(nsub > 0)
        def _():
            lo, hi = _unpack_bf16_pairs(stage[slot])
            x_sc[:, :half] = lo
            x_sc[:, half:] = hi

    full = nsub == spt

    def gate_up(rs, wg, wu):
        x = x_sc[rs, :]
        hg = jnp.dot(x, wg, preferred_element_type=F32) + bg_ref[0]
        hu = jnp.dot(x, wu, preferred_element_type=F32) + bu_ref[0]
        hg = jnp.minimum(hg, SWIGLU_LIMIT)
        hl = jnp.clip(hu, -SWIGLU_LIMIT, SWIGLU_LIMIT)
        act = hg * jax.nn.sigmoid(SWIGLU_ALPHA * hg) * (hl + 1.0)
        act_sc[j, rs, :] = act.astype(BF16)

    def down(rs, wd):
        act = jnp.concatenate([act_sc[c, rs, :] for c in range(nf)], axis=1)
        out_ref[rs, :] = jnp.dot(act, wd, preferred_element_type=F32) + bd_ref[0]

    partial_tile = jnp.logical_and(nsub > 0, nsub < spt)

    @pl.when(jnp.logical_and(j < nf, full))
    def _():
        start_next_tile_rows()
        gate_up(slice(None), wg_ref[0].astype(BF16), wu_ref[0].astype(BF16))

    @pl.when(jnp.logical_and(j < nf, partial_tile))
    def _():
        start_next_tile_rows()
        wg = wg_ref[0].astype(BF16)
        wu = wu_ref[0].astype(BF16)
        for s in range(spt - 1):
            pl.when(s < nsub)(functools.partial(gate_up, slice(s * sub, (s + 1) * sub), wg, wu))

    @pl.when(jnp.logical_and(j >= nf, full))
    def _():
        start_next_tile_rows()
        down(slice(None), wd_ref[0].astype(BF16))

    @pl.when(jnp.logical_and(j >= nf, partial_tile))
    def _():
        start_next_tile_rows()
        wd = wd_ref[0].astype(BF16)
        for s in range(spt - 1):
            pl.when(s < nsub)(functools.partial(down, slice(s * sub, (s + 1) * sub), wd))

    @pl.when(jnp.logical_and(j >= nf, nsub < spt))
    def _():
        for s in range(spt):
            @pl.when(s >= nsub)
            def _():
                out_ref[s * sub:(s + 1) * sub, :] = jnp.zeros((sub, out_ref.shape[1]), out_ref.dtype)


def _experts(tile_e, tile_nsub, row_tok3, u_packed, w_gu, b_gu3, w_down, b_down3, tm_e, sub, tf):
    n_tiles = row_tok3.shape[0]
    n_rows = n_tiles * tm_e
    d = 2 * u_packed.shape[1]
    dff = w_down.shape[1]
    nf = dff // tf
    nn = d // tf
    assert nn == nf
    kern = functools.partial(_expert_kernel, sub=sub, spt=tm_e // sub, nf=nf)

    def c1(i, j, ns):
        return jnp.where(ns[i] > 0, jnp.minimum(j, nf - 1), nf - 1)

    def c2(i, j, ns):
        return jnp.where(ns[i] > 0, jnp.maximum(j - nf, 0), nf - 1)

    tok_block = (1, 1, tm_e)
    return pl.pallas_call(
        kern,
        grid_spec=pltpu.PrefetchScalarGridSpec(
            num_scalar_prefetch=2,
            grid=(n_tiles, 2 * nf),
            in_specs=[
                pl.BlockSpec(tok_block, lambda i, j, te, ns: (i, 0, 0), memory_space=pltpu.SMEM),
                pl.BlockSpec(tok_block, lambda i, j, te, ns: (jnp.minimum(i + 1, n_tiles - 1), 0, 0),
                             memory_space=pltpu.SMEM),
                pl.BlockSpec(memory_space=pl.ANY),
                pl.BlockSpec((1, d, tf), lambda i, j, te, ns: (te[i], 0, c1(i, j, ns))),
                pl.BlockSpec((1, d, tf), lambda i, j, te, ns: (te[i], 0, nf + c1(i, j, ns))),
                pl.BlockSpec((1, 1, tf), lambda i, j, te, ns: (te[i], 0, c1(i, j, ns))),
                pl.BlockSpec((1, 1, tf), lambda i, j, te, ns: (te[i], 0, nf + c1(i, j, ns))),
                pl.BlockSpec((1, dff, tf), lambda i, j, te, ns: (te[i], 0, c2(i, j, ns))),
                pl.BlockSpec((1, 1, tf), lambda i, j, te, ns: (te[i], 0, c2(i, j, ns))),
            ],
            out_specs=pl.BlockSpec((tm_e, tf), lambda i, j, te, ns: (i, jnp.maximum(j - nf, 0))),
            scratch_shapes=[
                pltpu.VMEM((2, tm_e, d // 2), jnp.uint32),
                pltpu.VMEM((tm_e, d), BF16),
                pltpu.VMEM((nf, tm_e, tf), BF16),
                pltpu.SemaphoreType.DMA((2,)),
            ],
        ),
        out_shape=jax.ShapeDtypeStruct((n_rows, d), F32),
        compiler_params=_cparams(("arbitrary", "arbitrary")),
    )(tile_e, tile_nsub, row_tok3, row_tok3, u_packed, w_gu, w_gu, b_gu3, b_gu3, w_down, b_down3)


def _combine_kernel(cur_ref, nxt_ref, yr_hbm, h_ref, gw_ref, gf_ref, out_ref, ysel, sems, *, tmc,
                    final_norm):
    i = pl.program_id(0)
    n = pl.num_programs(0)
    slot = i % 2
    n_dma = TOP_K * tmc

    def row_copy(row, q, sl):
        return pltpu.make_async_copy(yr_hbm.at[pl.ds(row, 1), :], ysel.at[sl, pl.ds(q, 1), :], sems.at[sl])

    def issue(idx_ref, sl):
        def body(g, carry):
            for q8 in range(ISSUE_UNROLL):
                q = g * ISSUE_UNROLL + q8
                row_copy(idx_ref[0, 0, q], q, sl).start()
            return carry
        lax.fori_loop(0, n_dma // ISSUE_UNROLL, body, 0)

    @pl.when(i == 0)
    def _():
        issue(cur_ref, 0)

    @pl.when(i + 1 < n)
    def _():
        issue(nxt_ref, 1 - slot)

    pltpu.make_async_copy(yr_hbm.at[pl.ds(0, n_dma), :], ysel.at[slot], sems.at[slot]).wait()

    gw = gw_ref[...]
    y = h_ref[...]
    for k in range(TOP_K):
        y = y + gw[:, k:k + 1] * ysel[slot, k * tmc:(k + 1) * tmc, :]
    if final_norm:
        ms = jnp.mean(y * y, axis=-1, keepdims=True)
        y = (y * lax.rsqrt(ms + EPS)) * gf_ref[...]
    out_ref[...] = y


def _combine(roa3, y_rows, h, gw, gf, tmc, final_norm):
    t, d = h.shape
    n = t // tmc
    kern = functools.partial(_combine_kernel, tmc=tmc, final_norm=final_norm)
    return pl.pallas_call(
        kern,
        grid=(n,),
        in_specs=[
            pl.BlockSpec((1, 1, TOP_K * tmc), lambda i: (i, 0, 0), memory_space=pltpu.SMEM),
            pl.BlockSpec((1, 1, TOP_K * tmc), lambda i: (jnp.minimum(i + 1, n - 1), 0, 0),
                         memory_space=pltpu.SMEM),
            pl.BlockSpec(memory_space=pl.ANY),
            pl.BlockSpec((tmc, d), lambda i: (i, 0)),
            pl.BlockSpec((tmc, LANES), lambda i: (i, 0)),
            pl.BlockSpec((1, d), lambda i: (0, 0)),
        ],
        out_specs=pl.BlockSpec((tmc, d), lambda i: (i, 0)),
        out_shape=jax.ShapeDtypeStruct((t, d), F32),
        scratch_shapes=[pltpu.VMEM((2, TOP_K * tmc, d), F32), pltpu.SemaphoreType.DMA((2,))],
        compiler_params=_cparams(("arbitrary",)),
    )(roa3, roa3, y_rows, h, gw, gf)


def _routing(top_idx, tm_e, sub):
    t = top_idx.shape[0]
    a = t * TOP_K
    flat_e = top_idx.reshape(a)
    onehot = (flat_e[:, None] == jnp.arange(N_EXPERTS, dtype=jnp.int32)[None, :]).astype(jnp.int32)
    csum = jnp.cumsum(onehot, axis=0)
    counts = csum[-1]
    rank = jnp.sum(onehot * csum, axis=1) - 1
    padded = (counts + tm_e - 1) // tm_e * tm_e
    pad_end = jnp.cumsum(padded)
    pad_start = pad_end - padded
    row_of_assign = (pad_start[flat_e] + rank).astype(jnp.int32)
    n_tiles = -(-a // tm_e) + N_EXPERTS
    n_rows = n_tiles * tm_e
    row_tok = jnp.zeros((n_rows,), jnp.int32).at[row_of_assign].set(
        jnp.arange(a, dtype=jnp.int32) // TOP_K)
    tile_start = jnp.arange(n_tiles, dtype=jnp.int32) * tm_e
    n_used = pad_end[-1] // tm_e
    tile_e_raw = jnp.minimum(jnp.searchsorted(pad_end, tile_start, side='right'), N_EXPERTS - 1)
    used = tile_start < pad_end[-1]
    last = jnp.maximum(n_used - 1, 0)
    tile_src = jnp.where(used, jnp.arange(n_tiles, dtype=jnp.int32), last).astype(jnp.int32)
    tile_e = tile_e_raw[tile_src].astype(jnp.int32)
    valid_rows = jnp.clip(counts[tile_e_raw] - (tile_start - pad_start[tile_e_raw]), 0, tm_e)
    tile_nsub = jnp.where(used, (valid_rows + sub - 1) // sub, 0).astype(jnp.int32)
    return row_tok.reshape(n_tiles, 1, tm_e), row_of_assign, tile_e, tile_nsub


def kernel(x, positions, norm1_g, w_in, b_fgate, b_gate, lam_q1, lam_k1, lam_q2, lam_k2, subln_g,
           w_branch_fox, w_branch_diff, w_out, norm2_g, router_w, router_b, w_gu, b_gu, w_down,
           b_down, normf_g):
    b, s, d = x.shape
    t = b * s
    depth = norm1_g.shape[0]
    fox_w = FOX_HEADS * HEAD_DIM
    tm_a = _pick(t, 1024)
    tq = _pick(s, 512)
    tm_d = _pick(t, 256)
    tm_e, sub, tf = 1024, 256, 512
    tmc = _pick(t, 128)

    half = ROPE_DIM // 2
    inv_freq = ROPE_THETA ** (-jnp.arange(0, ROPE_DIM, 2, dtype=F32) / ROPE_DIM)
    zeros = jnp.zeros((LANES - ROPE_DIM,), F32)
    rope_tab = jnp.zeros((8, LANES), F32)
    rope_tab = rope_tab.at[0].set(jnp.concatenate([inv_freq, inv_freq, zeros]))
    rope_tab = rope_tab.at[1].set(jnp.concatenate([-jnp.ones((half,), F32), jnp.zeros((half,), F32), zeros]))
    rope_tab = rope_tab.at[2].set(jnp.concatenate([jnp.zeros((half,), F32), jnp.ones((half,), F32), zeros]))
    pos = positions.astype(F32).reshape(t, 1)

    h = x.reshape(t, d)
    for l in range(depth):
        wi = w_in[l]
        w_main = jnp.concatenate([wi[:, :3 * fox_w], wi[:, 3 * fox_w + FOX_HEADS:]], axis=1).astype(BF16)
        w_f = jnp.pad(wi[:, 3 * fox_w:3 * fox_w + FOX_HEADS], ((0, 0), (0, LANES - FOX_HEADS))).astype(BF16)
        qkv, gate, flog = _inproj(h, pos, norm1_g[l][None, :], w_main, w_f, rope_tab,
                                  b_gate[l][None, :], tm_a)

        fl = flog[:, :FOX_HEADS].reshape(b, s, FOX_HEADS).transpose(0, 2, 1).reshape(b * FOX_HEADS, s)
        bf = jnp.tile(b_fgate[l].astype(F32), b).reshape(b * FOX_HEADS, 1)
        c3 = _fgate(fl, bf).reshape(b * FOX_HEADS, 1, s)

        y_fox = _fox_attn(qkv, c3, b, s, tq)
        lam_init = 0.8 - 0.6 * math.exp(-0.3 * l)
        lam_p = jnp.stack([lam_q1[l], lam_k1[l], lam_q2[l], lam_k2[l]]).astype(F32)
        y_diff = _diff_attn(qkv, lam_p, subln_g[l][None, :].astype(F32), b, s, tq, lam_init)

        rw = jnp.pad(router_w[l], ((0, 0), (0, LANES - N_EXPERTS))).astype(BF16)
        rb = jnp.concatenate([router_b[l].astype(F32), jnp.full((LANES - N_EXPERTS,), NEG_BIG, F32)])[None, :]
        h, u2, idx128, gw128 = _merge(y_fox, y_diff, gate, h, w_branch_fox[l].astype(BF16),
                                      w_branch_diff[l].astype(BF16), w_out[l].astype(BF16),
                                      norm2_g[l][None, :], rw, rb, tm_d)

        row_tok3, roa, tile_e, tile_nsub = _routing(idx128[:, :TOP_K], tm_e, sub)
        y_rows = _experts(tile_e, tile_nsub, row_tok3, u2, w_gu[l], b_gu[l][:, None, :],
                          w_down[l], b_down[l][:, None, :], tm_e, sub, tf)
        roa3 = roa.reshape(t // tmc, tmc, TOP_K).transpose(0, 2, 1).reshape(t // tmc, 1, TOP_K * tmc)
        h = _combine(roa3, y_rows, h, gw128, normf_g[None, :], tmc, final_norm=(l == depth - 1))
    return h.reshape(b, s, d)
```

```python
import functools
import math

import jax
import jax.numpy as jnp
from jax import lax
from jax.experimental import pallas as pl
from jax.experimental.pallas import tpu as pltpu

HEAD_DIM = 128
FOX_HEADS = 8
DIFF_HEADS = 4
DIFF_V_DIM = 2 * HEAD_DIM
ROPE_THETA = 500000.0
ROPE_DIM = HEAD_DIM // 4
N_EXPERTS = 32
TOP_K = 4
SWIGLU_ALPHA = 1.702
SWIGLU_LIMIT = 7.0
EPS = 1e-5
LANES = 128

F32 = jnp.float32
BF16 = jnp.bfloat16
NEG_BIG = -1e30
LOG2E = math.log2(math.e)
ISSUE_UNROLL = 8
VMEM_LIMIT = 56 * 1024 * 1024


def _cparams(sem, vmem=VMEM_LIMIT):
    return pltpu.CompilerParams(dimension_semantics=sem, vmem_limit_bytes=vmem)


def _pick(n, pref):
    t = min(n, pref)
    while n % t:
        t //= 2
    return t


def _inproj_kernel(x_ref, g1_ref, w_ref, wf_ref, pos_ref, tab_ref, bg_ref,
                   qkv_ref, gate_ref, flog_ref, u_sc, cos_sc, sa_sc, sb_sc, *, scale, n_qkv):
    j = pl.program_id(1)

    @pl.when(j == 0)
    def _():
        xf = x_ref[...]
        ms = jnp.mean(xf * xf, axis=-1, keepdims=True)
        u = (xf * lax.rsqrt(ms + EPS)) * g1_ref[...]
        ub = u.astype(BF16)
        u_sc[...] = ub
        flog_ref[...] = jnp.dot(ub, wf_ref[...], preferred_element_type=F32)
        ang = pos_ref[...] * tab_ref[0:1, :]
        sn = jnp.sin(ang)
        cos_sc[...] = jnp.cos(ang)
        sa_sc[...] = sn * tab_ref[1:2, :]
        sb_sc[...] = sn * tab_ref[2:3, :]

    acc = jnp.dot(u_sc[...], w_ref[...], preferred_element_type=F32)

    def rotary(mult):
        for c in range(acc.shape[1] // HEAD_DIM):
            t = acc[:, c * HEAD_DIM:(c + 1) * HEAD_DIM]
            r = (t * cos_sc[...]
                 + pltpu.roll(t, HEAD_DIM - ROPE_DIM // 2, 1) * sa_sc[...]
                 + pltpu.roll(t, ROPE_DIM // 2, 1) * sb_sc[...])
            if mult != 1.0:
                r = r * mult
            qkv_ref[:, c * HEAD_DIM:(c + 1) * HEAD_DIM] = r.astype(BF16)

    @pl.when(j == 0)
    def _():
        qkv_ref[...] = (acc * scale).astype(BF16)

    @pl.when((j == 1) | (j == 2) | (j == 5))
    def _():
        qkv_ref[...] = acc.astype(BF16)

    @pl.when(j == 3)
    def _():
        rotary(scale)

    @pl.when(j == 4)
    def _():
        rotary(1.0)

    @pl.when(j >= n_qkv)
    def _():
        gate_ref[...] = jax.nn.sigmoid(acc + bg_ref[...])


def _inproj(x2, pos, g1, w_main, w_f, rope_tab, b_gate, tm):
    t, d = x2.shape
    tn = 1024
    n_qkv = 6
    n_gate = (w_main.shape[1] - n_qkv * tn) // tn
    grid = (t // tm, n_qkv + n_gate)
    kern = functools.partial(_inproj_kernel, scale=LOG2E / math.sqrt(HEAD_DIM), n_qkv=n_qkv)
    return pl.pallas_call(
        kern,
        grid=grid,
        in_specs=[
            pl.BlockSpec((tm, d), lambda i, j: (i, 0)),
            pl.BlockSpec((1, d), lambda i, j: (0, 0)),
            pl.BlockSpec((d, tn), lambda i, j: (0, j)),
            pl.BlockSpec((d, LANES), lambda i, j: (0, 0)),
            pl.BlockSpec((tm, 1), lambda i, j: (i, 0)),
            pl.BlockSpec((8, LANES), lambda i, j: (0, 0)),
            pl.BlockSpec((1, tn), lambda i, j: (0, jnp.maximum(j - n_qkv, 0))),
        ],
        out_specs=[
            pl.BlockSpec((tm, tn), lambda i, j: (i, jnp.minimum(j, n_qkv - 1))),
            pl.BlockSpec((tm, tn), lambda i, j: (i, jnp.maximum(j - n_qkv, 0))),
            pl.BlockSpec((tm, LANES), lambda i, j: (i, 0)),
        ],
        out_shape=[
            jax.ShapeDtypeStruct((t, n_qkv * tn), BF16),
            jax.ShapeDtypeStruct((t, n_gate * tn), F32),
            jax.ShapeDtypeStruct((t, LANES), F32),
        ],
        scratch_shapes=[
            pltpu.VMEM((tm, d), BF16),
            pltpu.VMEM((tm, LANES), F32),
            pltpu.VMEM((tm, LANES), F32),
            pltpu.VMEM((tm, LANES), F32),
        ],
        compiler_params=_cparams(("parallel", "arbitrary")),
    )(x2, g1, w_main, w_f, pos, rope_tab, b_gate)


def _fgate_kernel(fl_ref, b_ref, c_ref):
    z = fl_ref[...] + b_ref[...]
    x = jnp.minimum(z, 0.0) - jnp.log(1.0 + jnp.exp(-jnp.abs(z)))
    n = x.shape[1]
    lane = lax.broadcasted_iota(jnp.int32, x.shape, 1)
    sh = 1
    while sh < n:
        x = x + jnp.where(lane >= sh, pltpu.roll(x, sh, 1), 0.0)
        sh *= 2
    c_ref[...] = x


def _fgate(fl, b):
    r, s = fl.shape
    return pl.pallas_call(
        _fgate_kernel,
        grid=(1,),
        in_specs=[pl.BlockSpec((r, s), lambda i: (0, 0)), pl.BlockSpec((r, 1), lambda i: (0, 0))],
        out_specs=pl.BlockSpec((r, s), lambda i: (0, 0)),
        out_shape=jax.ShapeDtypeStruct((r, s), F32),
        compiler_params=_cparams(("arbitrary",)),
    )(fl, b)


def _transpose_bf16(x):
    return x.astype(F32).T.astype(BF16)


def _softmax_step(s, vt, m_sc, l_sc, acc_sc):
    m_old = m_sc[...]
    m_new = jnp.maximum(m_old, jnp.max(s, axis=0, keepdims=True))
    alpha = jnp.exp2(m_old - m_new)
    p = jnp.exp2(s - m_new)
    l_sc[...] = alpha * l_sc[...] + jnp.sum(p, axis=0, keepdims=True)
    acc_sc[...] = alpha * acc_sc[...] + jnp.dot(vt, p.astype(BF16), preferred_element_type=F32)
    m_sc[...] = m_new


def _causal_keep(shape):
    kv = lax.broadcasted_iota(jnp.int32, shape, 0)
    qq = lax.broadcasted_iota(jnp.int32, shape, 1)
    return kv <= qq


FOX_HEADS_PER_STEP = 2


def _fox_kernel(q_ref, k_ref, v_ref, c_ref, o_ref, *scratch, tq, tk, s_len):
    hp = FOX_HEADS_PER_STEP
    qt_sc, kaug_sc, vt_sc, m_sc, l_sc, acc_sc = (scratch[n * hp:(n + 1) * hp] for n in range(6))
    qi = pl.program_id(2)
    head_cols = [slice(h * HEAD_DIM, (h + 1) * HEAD_DIM) for h in range(hp)]

    @pl.when(qi == 0)
    def _():
        sub = lax.broadcasted_iota(jnp.int32, (HEAD_DIM, tk), 0)
        for h in range(hp):
            for ch in range(s_len // tk):
                rows = slice(ch * tk, (ch + 1) * tk)
                vt_sc[h][:, rows] = _transpose_bf16(v_ref[rows, head_cols[h]])
                c = c_ref[h, :, rows] * (-LOG2E)
                hi = c.astype(BF16).astype(F32)
                mid = (c - hi).astype(BF16).astype(F32)
                lo = (c - hi - mid).astype(BF16).astype(F32)
                tab = jnp.where(sub == 0, hi, jnp.where(sub == 1, mid, jnp.where(sub == 2, lo, 0.0)))
                kaug_sc[h][rows, HEAD_DIM:] = tab.T.astype(BF16)
                kaug_sc[h][rows, :HEAD_DIM] = k_ref[rows, head_cols[h]]

    sub_q = lax.broadcasted_iota(jnp.int32, (HEAD_DIM, tq), 0)
    for h in range(hp):
        qt_sc[h][:HEAD_DIM, :] = _transpose_bf16(q_ref[:, head_cols[h]])
        qt_sc[h][HEAD_DIM:, :] = jnp.where(sub_q < 3, 1.0, 0.0).astype(BF16)
        m_sc[h][...] = jnp.full(m_sc[h].shape, -jnp.inf, F32)
        l_sc[h][...] = jnp.zeros(l_sc[h].shape, F32)
        acc_sc[h][...] = jnp.zeros(acc_sc[h].shape, F32)

    def step(start, masked):
        for h in range(hp):
            s = jnp.dot(kaug_sc[h][pl.ds(start, tk), :], qt_sc[h][...], preferred_element_type=F32)
            if masked:
                s = jnp.where(_causal_keep(s.shape), s, -jnp.inf)
            _softmax_step(s, vt_sc[h][:, pl.ds(start, tk)], m_sc[h], l_sc[h], acc_sc[h])

    def body(kc, carry):
        step(pl.multiple_of(kc * tk, tk), False)
        return carry

    lax.fori_loop(0, qi, body, 0)
    step(pl.multiple_of(qi * tq, tq), True)
    for h in range(hp):
        o_ref[:, head_cols[h]] = (acc_sc[h][...] / l_sc[h][...]).T.astype(o_ref.dtype)


def _fox_attn(qkv, c3, b, s, tq):
    t = b * s
    nq = s // tq
    hp = FOX_HEADS_PER_STEP
    groups = FOX_HEADS // hp
    width = hp * HEAD_DIM
    kern = functools.partial(_fox_kernel, tq=tq, tk=tq, s_len=s)

    def per_head(shape, dtype):
        return [pltpu.VMEM(shape, dtype) for _ in range(hp)]

    return pl.pallas_call(
        kern,
        grid=(b, groups, nq),
        in_specs=[
            pl.BlockSpec((tq, width), lambda bi, g, qi: (bi * nq + qi, g)),
            pl.BlockSpec((s, width), lambda bi, g, qi: (bi, groups + g)),
            pl.BlockSpec((s, width), lambda bi, g, qi: (bi, 2 * groups + g)),
            pl.BlockSpec((hp, 1, s), lambda bi, g, qi: (bi * groups + g, 0, 0)),
        ],
        out_specs=pl.BlockSpec((tq, width), lambda bi, g, qi: (bi * nq + qi, g)),
        out_shape=jax.ShapeDtypeStruct((t, FOX_HEADS * HEAD_DIM), BF16),
        scratch_shapes=(per_head((2 * HEAD_DIM, tq), BF16) + per_head((s, 2 * HEAD_DIM), BF16)
                        + per_head((HEAD_DIM, s), BF16) + per_head((1, tq), F32)
                        + per_head((1, tq), F32) + per_head((HEAD_DIM, tq), F32)),
        compiler_params=_cparams(("parallel", "parallel", "arbitrary")),
    )(qkv, qkv, qkv, c3)


def _diff_kernel(q1_ref, q2_ref, k1_ref, k2_ref, v_ref, lam_ref, g_ref, o_ref,
                 q1t_sc, q2t_sc, vt_sc, m1_sc, l1_sc, a1_sc, m2_sc, l2_sc, a2_sc,
                 *, tq, tk, s_len, lam_init):
    qi = pl.program_id(2)

    @pl.when(qi == 0)
    def _():
        for ch in range(s_len // tk):
            rows = slice(ch * tk, (ch + 1) * tk)
            vt_sc[:, rows] = _transpose_bf16(v_ref[rows, :])

    q1t_sc[...] = _transpose_bf16(q1_ref[...])
    q2t_sc[...] = _transpose_bf16(q2_ref[...])
    for r in (m1_sc, m2_sc):
        r[...] = jnp.full(r.shape, -jnp.inf, F32)
    for r in (l1_sc, l2_sc, a1_sc, a2_sc):
        r[...] = jnp.zeros(r.shape, F32)

    def step(start, masked):
        vt = vt_sc[:, pl.ds(start, tk)]
        s1 = jnp.dot(k1_ref[pl.ds(start, tk), :], q1t_sc[...], preferred_element_type=F32)
        s2 = jnp.dot(k2_ref[pl.ds(start, tk), :], q2t_sc[...], preferred_element_type=F32)
        if masked:
            keep = _causal_keep(s1.shape)
            s1 = jnp.where(keep, s1, -jnp.inf)
            s2 = jnp.where(keep, s2, -jnp.inf)
        _softmax_step(s1, vt, m1_sc, l1_sc, a1_sc)
        _softmax_step(s2, vt, m2_sc, l2_sc, a2_sc)

    def body(kc, carry):
        step(pl.multiple_of(kc * tk, tk), False)
        return carry

    lax.fori_loop(0, qi, body, 0)
    step(pl.multiple_of(qi * tq, tq), True)

    lp = lam_ref[...]
    lam = (jnp.exp(jnp.sum(lp[0:1, :] * lp[1:2, :], axis=-1, keepdims=True))
           - jnp.exp(jnp.sum(lp[2:3, :] * lp[3:4, :], axis=-1, keepdims=True)) + lam_init)
    y = (a1_sc[...] / l1_sc[...]).T - lam * (a2_sc[...] / l2_sc[...]).T
    ms = jnp.mean(y * y, axis=-1, keepdims=True)
    y = (y * lax.rsqrt(ms + EPS)) * g_ref[...]
    o_ref[...] = (y * (1.0 - lam_init)).astype(o_ref.dtype)


def _diff_attn(qkv, lam_p, subln_g, b, s, tq, lam_init):
    t = b * s
    nq = s // tq
    qb = 3 * FOX_HEADS
    kb = qb + 2 * DIFF_HEADS
    vb = (kb + 2 * DIFF_HEADS) // 2
    kern = functools.partial(_diff_kernel, tq=tq, tk=tq, s_len=s, lam_init=lam_init)
    return pl.pallas_call(
        kern,
        grid=(b, DIFF_HEADS, nq),
        in_specs=[
            pl.BlockSpec((tq, HEAD_DIM), lambda bi, h, qi: (bi * nq + qi, qb + 2 * h)),
            pl.BlockSpec((tq, HEAD_DIM), lambda bi, h, qi: (bi * nq + qi, qb + 2 * h + 1)),
            pl.BlockSpec((s, HEAD_DIM), lambda bi, h, qi: (bi, kb + 2 * h)),
            pl.BlockSpec((s, HEAD_DIM), lambda bi, h, qi: (bi, kb + 2 * h + 1)),
            pl.BlockSpec((s, DIFF_V_DIM), lambda bi, h, qi: (bi, vb + h)),
            pl.BlockSpec((4, HEAD_DIM), lambda bi, h, qi: (0, 0)),
            pl.BlockSpec((1, DIFF_V_DIM), lambda bi, h, qi: (0, 0)),
        ],
        out_specs=pl.BlockSpec((tq, DIFF_V_DIM), lambda bi, h, qi: (bi * nq + qi, h)),
        out_shape=jax.ShapeDtypeStruct((t, DIFF_HEADS * DIFF_V_DIM), BF16),
        scratch_shapes=[
            pltpu.VMEM((HEAD_DIM, tq), BF16), pltpu.VMEM((HEAD_DIM, tq), BF16),
            pltpu.VMEM((DIFF_V_DIM, s), BF16),
            pltpu.VMEM((1, tq), F32), pltpu.VMEM((1, tq), F32), pltpu.VMEM((DIFF_V_DIM, tq), F32),
            pltpu.VMEM((1, tq), F32), pltpu.VMEM((1, tq), F32), pltpu.VMEM((DIFF_V_DIM, tq), F32),
        ],
        compiler_params=_cparams(("parallel", "parallel", "arbitrary")),
    )(qkv, qkv, qkv, qkv, qkv, lam_p, subln_g)


def _pack_bf16_pairs(xb):
    w = xb.shape[1] // 2
    lo = lax.bitcast_convert_type(xb[:, :w].astype(F32), jnp.uint32)
    hi = lax.bitcast_convert_type(xb[:, w:].astype(F32), jnp.uint32)
    return (lo >> 16) | (hi & jnp.uint32(0xFFFF0000))


def _unpack_bf16_pairs(p):
    lo = lax.bitcast_convert_type(p << 16, F32).astype(BF16)
    hi = lax.bitcast_convert_type(p & jnp.uint32(0xFFFF0000), F32).astype(BF16)
    return lo, hi


def _merge_kernel(yf_ref, yd_ref, g0_ref, g1_ref, x_ref, wbf_ref, wbd_ref, wo_ref, n2_ref,
                  rw_ref, rb_ref, h_ref, u_ref, idx_ref, gw_ref):
    ya = jnp.dot(yf_ref[...], wbf_ref[...], preferred_element_type=F32)
    yb = jnp.dot(yd_ref[...], wbd_ref[...], preferred_element_type=F32)
    merged = g0_ref[...] * ya + g1_ref[...] * yb
    h = x_ref[...] + jnp.dot(merged.astype(BF16), wo_ref[...], preferred_element_type=F32)
    h_ref[...] = h
    ms = jnp.mean(h * h, axis=-1, keepdims=True)
    u = (h * lax.rsqrt(ms + EPS)) * n2_ref[...]
    ub = u.astype(BF16)
    u_ref[...] = _pack_bf16_pairs(ub)
    logits = jnp.dot(ub, rw_ref[...], preferred_element_type=F32) + rb_ref[...]

    lane = lax.broadcasted_iota(jnp.int32, logits.shape, 1).astype(F32)
    work = logits
    vals, idxs = [], []
    for _ in range(TOP_K):
        m = jnp.max(work, axis=-1, keepdims=True)
        sel = jnp.min(jnp.where(work == m, lane, float(LANES)), axis=-1, keepdims=True)
        vals.append(m)
        idxs.append(sel)
        work = jnp.where(lane == sel, -jnp.inf, work)
    es = [jnp.exp(v - vals[0]) for v in vals]
    den = es[0]
    for e in es[1:]:
        den = den + e
    idx_out = jnp.zeros(logits.shape, F32)
    gw_out = jnp.zeros(logits.shape, F32)
    for k in range(TOP_K):
        idx_out = jnp.where(lane == float(k), idxs[k], idx_out)
        gw_out = jnp.where(lane == float(k), es[k] / den, gw_out)
    idx_ref[...] = idx_out.astype(jnp.int32)
    gw_ref[...] = gw_out


def _merge(yf, yd, gate, x2, wbf, wbd, wo, n2, rw, rb, tm):
    t, d = x2.shape
    fw = yf.shape[1]
    dw = yd.shape[1]
    const = dict(pipeline_mode=pl.Buffered(1))
    return pl.pallas_call(
        _merge_kernel,
        grid=(t // tm,),
        in_specs=[
            pl.BlockSpec((tm, fw), lambda i: (i, 0)),
            pl.BlockSpec((tm, dw), lambda i: (i, 0)),
            pl.BlockSpec((tm, d), lambda i: (i, 0)),
            pl.BlockSpec((tm, d), lambda i: (i, 1)),
            pl.BlockSpec((tm, d), lambda i: (i, 0)),
            pl.BlockSpec((fw, d), lambda i: (0, 0), **const),
            pl.BlockSpec((dw, d), lambda i: (0, 0), **const),
            pl.BlockSpec((d, d), lambda i: (0, 0), **const),
            pl.BlockSpec((1, d), lambda i: (0, 0)),
            pl.BlockSpec((d, LANES), lambda i: (0, 0), **const),
            pl.BlockSpec((1, LANES), lambda i: (0, 0)),
        ],
        out_specs=[
            pl.BlockSpec((tm, d), lambda i: (i, 0)),
            pl.BlockSpec((tm, d // 2), lambda i: (i, 0)),
            pl.BlockSpec((tm, LANES), lambda i: (i, 0)),
            pl.BlockSpec((tm, LANES), lambda i: (i, 0)),
        ],
        out_shape=[
            jax.ShapeDtypeStruct((t, d), F32),
            jax.ShapeDtypeStruct((t, d // 2), jnp.uint32),
            jax.ShapeDtypeStruct((t, LANES), jnp.int32),
            jax.ShapeDtypeStruct((t, LANES), F32),
        ],
        compiler_params=_cparams(("parallel",)),
    )(yf, yd, gate, gate, x2, wbf, wbd, wo, n2, rw, rb)


def _expert_kernel(te_ref, nsub_ref, cur_ref, nxt_ref, u_hbm, wg_ref, wu_ref, bg_ref, bu_ref, wd_ref,
                   bd_ref, out_ref, stage, x_sc, act_sc, sems, *, sub, spt, nf):
    i = pl.program_id(0)
    j = pl.program_id(1)
    nsub = nsub_ref[i]
    slot = i % 2
    tm_e = x_sc.shape[0]
    per_step = tm_e // (2 * nf)
    half = x_sc.shape[1] // 2

    def start_rows(tok_ref, sl, first, count):
        for r in range(count):
            row = first + r
            pltpu.async_copy(u_hbm.at[pl.ds(tok_ref[0, 0, row], 1), :],
                             stage.at[sl, pl.ds(row, 1), :], sems.at[sl], priority=1)

    def start_next_tile_rows():
        start_rows(nxt_ref, 1 - slot, j * per_step, per_step)

    @pl.when(jnp.logical_and(i == 0, j == 0))
    def _():
        def body(g, carry):
            start_rows(cur_ref, 0, g * ISSUE_UNROLL, ISSUE_UNROLL)
            return carry
        lax.fori_loop(0, tm_e // ISSUE_UNROLL, body, 0)

    fetched = jnp.logical_or(i == 0, nsub_ref[jnp.maximum(i - 1, 0)] > 0)

    @pl.when(jnp.logical_and(j == 0, fetched))
    def _():
        pltpu.make_async_copy(u_hbm.at[pl.ds(0, tm_e), :], stage.at[slot], sems.at[slot]).wait()

        @pl.when(nsub > 0)
        def _():
            lo, hi = _unpack_bf16_pairs(stage[slot])
            x_sc[:, :half] = lo
            x_sc[:, half:] = hi

    full = nsub == spt

    def gate_up(rs, wg, wu):
        x = x_sc[rs, :]
        hg = jnp.dot(x, wg, preferred_element_type=F32) + bg_ref[0]
        hu = jnp.dot(x, wu, preferred_element_type=F32) + bu_ref[0]
        hg = jnp.minimum(hg, SWIGLU_LIMIT)
        hl = jnp.clip(hu, -SWIGLU_LIMIT, SWIGLU_LIMIT)
        act = hg * jax.nn.sigmoid(SWIGLU_ALPHA * hg) * (hl + 1.0)
        act_sc[j, rs, :] = act.astype(BF16)

    def down(rs, wd):
        act = jnp.concatenate([act_sc[c, rs, :] for c in range(nf)], axis=1)
        out_ref[rs, :] = jnp.dot(act, wd, preferred_element_type=F32) + bd_ref[0]

    partial_tile = jnp.logical_and(nsub > 0, nsub < spt)

    @pl.when(jnp.logical_and(j < nf, full))
    def _():
        start_next_tile_rows()
        gate_up(slice(None), wg_ref[0].astype(BF16), wu_ref[0].astype(BF16))

    @pl.when(jnp.logical_and(j < nf, partial_tile))
    def _():
        start_next_tile_rows()
        wg = wg_ref[0].astype(BF16)
        wu = wu_ref[0].astype(BF16)
        for s in range(spt - 1):
            pl.when(s < nsub)(functools.partial(gate_up, slice(s * sub, (s + 1) * sub), wg, wu))

    @pl.when(jnp.logical_and(j >= nf, full))
    def _():
        start_next_tile_rows()
        down(slice(None), wd_ref[0].astype(BF16))

    @pl.when(jnp.logical_and(j >= nf, partial_tile))
    def _():
        start_next_tile_rows()
        wd = wd_ref[0].astype(BF16)
        for s in range(spt - 1):
            pl.when(s < nsub)(functools.partial(down, slice(s * sub, (s + 1) * sub), wd))

    @pl.when(jnp.logical_and(j >= nf, nsub < spt))
    def _():
        for s in range(spt):
            @pl.when(s >= nsub)
            def _():
                out_ref[s * sub:(s + 1) * sub, :] = jnp.zeros((sub, out_ref.shape[1]), out_ref.dtype)


def _experts(tile_e, tile_nsub, row_tok3, u_packed, w_gu, b_gu3, w_down, b_down3, tm_e, sub, tf):
    n_tiles = row_tok3.shape[0]
    n_rows = n_tiles * tm_e
    d = 2 * u_packed.shape[1]
    dff = w_down.shape[1]
    nf = dff // tf
    nn = d // tf
    assert nn == nf
    kern = functools.partial(_expert_kernel, sub=sub, spt=tm_e // sub, nf=nf)

    def c1(i, j, ns):
        return jnp.where(ns[i] > 0, jnp.minimum(j, nf - 1), nf - 1)

    def c2(i, j, ns):
        return jnp.where(ns[i] > 0, jnp.maximum(j - nf, 0), nf - 1)

    tok_block = (1, 1, tm_e)
    return pl.pallas_call(
        kern,
        grid_spec=pltpu.PrefetchScalarGridSpec(
            num_scalar_prefetch=2,
            grid=(n_tiles, 2 * nf),
            in_specs=[
                pl.BlockSpec(tok_block, lambda i, j, te, ns: (i, 0, 0), memory_space=pltpu.SMEM),
                pl.BlockSpec(tok_block, lambda i, j, te, ns: (jnp.minimum(i + 1, n_tiles - 1), 0, 0),
                             memory_space=pltpu.SMEM),
                pl.BlockSpec(memory_space=pl.ANY),
                pl.BlockSpec((1, d, tf), lambda i, j, te, ns: (te[i], 0, c1(i, j, ns))),
                pl.BlockSpec((1, d, tf), lambda i, j, te, ns: (te[i], 0, nf + c1(i, j, ns))),
                pl.BlockSpec((1, 1, tf), lambda i, j, te, ns: (te[i], 0, c1(i, j, ns))),
                pl.BlockSpec((1, 1, tf), lambda i, j, te, ns: (te[i], 0, nf + c1(i, j, ns))),
                pl.BlockSpec((1, dff, tf), lambda i, j, te, ns: (te[i], 0, c2(i, j, ns))),
                pl.BlockSpec((1, 1, tf), lambda i, j, te, ns: (te[i], 0, c2(i, j, ns))),
            ],
            out_specs=pl.BlockSpec((tm_e, tf), lambda i, j, te, ns: (i, jnp.maximum(j - nf, 0))),
            scratch_shapes=[
                pltpu.VMEM((2, tm_e, d // 2), jnp.uint32),
                pltpu.VMEM((tm_e, d), BF16),
                pltpu.VMEM((nf, tm_e, tf), BF16),
                pltpu.SemaphoreType.DMA((2,)),
            ],
        ),
        out_shape=jax.ShapeDtypeStruct((n_rows, d), F32),
        compiler_params=_cparams(("arbitrary", "arbitrary")),
    )(tile_e, tile_nsub, row_tok3, row_tok3, u_packed, w_gu, w_gu, b_gu3, b_gu3, w_down, b_down3)


def _combine_kernel(cur_ref, nxt_ref, yr_hbm, h_ref, gw_ref, gf_ref, out_ref, ysel, sems, *, tmc,
                    final_norm):
    i = pl.program_id(0)
    n = pl.num_programs(0)
    slot = i % 2
    n_dma = TOP_K * tmc

    def row_copy(row, q, sl):
        return pltpu.make_async_copy(yr_hbm.at[pl.ds(row, 1), :], ysel.at[sl, pl.ds(q, 1), :], sems.at[sl])

    def issue(idx_ref, sl):
        def body(g, carry):
            for q8 in range(ISSUE_UNROLL):
                q = g * ISSUE_UNROLL + q8
                row_copy(idx_ref[0, 0, q], q, sl).start()
            return carry
        lax.fori_loop(0, n_dma // ISSUE_UNROLL, body, 0)

    @pl.when(i == 0)
    def _():
        issue(cur_ref, 0)

    @pl.when(i + 1 < n)
    def _():
        issue(nxt_ref, 1 - slot)

    pltpu.make_async_copy(yr_hbm.at[pl.ds(0, n_dma), :], ysel.at[slot], sems.at[slot]).wait()

    gw = gw_ref[...]
    y = h_ref[...]
    for k in range(TOP_K):
        y = y + gw[:, k:k + 1] * ysel[slot, k * tmc:(k + 1) * tmc, :]
    if final_norm:
        ms = jnp.mean(y * y, axis=-1, keepdims=True)
        y = (y * lax.rsqrt(ms + EPS)) * gf_ref[...]
    out_ref[...] = y


def _combine(roa3, y_rows, h, gw, gf, tmc, final_norm):
    t, d = h.shape
    n = t // tmc
    kern = functools.partial(_combine_kernel, tmc=tmc, final_norm=final_norm)
    return pl.pallas_call(
        kern,
        grid=(n,),
        in_specs=[
            pl.BlockSpec((1, 1, TOP_K * tmc), lambda i: (i, 0, 0), memory_space=pltpu.SMEM),
            pl.BlockSpec((1, 1, TOP_K * tmc), lambda i: (jnp.minimum(i + 1, n - 1), 0, 0),
                         memory_space=pltpu.SMEM),
            pl.BlockSpec(memory_space=pl.ANY),
            pl.BlockSpec((tmc, d), lambda i: (i, 0)),
            pl.BlockSpec((tmc, LANES), lambda i: (i, 0)),
            pl.BlockSpec((1, d), lambda i: (0, 0)),
        ],
        out_specs=pl.BlockSpec((tmc, d), lambda i: (i, 0)),
        out_shape=jax.ShapeDtypeStruct((t, d), F32),
        scratch_shapes=[pltpu.VMEM((2, TOP_K * tmc, d), F32), pltpu.SemaphoreType.DMA((2,))],
        compiler_params=_cparams(("arbitrary",)),
    )(roa3, roa3, y_rows, h, gw, gf)


def _routing(top_idx, tm_e, sub):
    t = top_idx.shape[0]
    a = t * TOP_K
    flat_e = top_idx.reshape(a)
    onehot = (flat_e[:, None] == jnp.arange(N_EXPERTS, dtype=jnp.int32)[None, :]).astype(jnp.int32)
    csum = jnp.cumsum(onehot, axis=0)
    counts = csum[-1]
    rank = jnp.sum(onehot * csum, axis=1) - 1
    padded = (counts + tm_e - 1) // tm_e * tm_e
    pad_end = jnp.cumsum(padded)
    pad_start = pad_end - padded
    row_of_assign = (pad_start[flat_e] + rank).astype(jnp.int32)
    n_tiles = -(-a // tm_e) + N_EXPERTS
    n_rows = n_tiles * tm_e
    row_tok = jnp.zeros((n_rows,), jnp.int32).at[row_of_assign].set(
        jnp.arange(a, dtype=jnp.int32) // TOP_K)
    tile_start = jnp.arange(n_tiles, dtype=jnp.int32) * tm_e
    n_used = pad_end[-1] // tm_e
    tile_e_raw = jnp.minimum(jnp.searchsorted(pad_end, tile_start, side='right'), N_EXPERTS - 1)
    used = tile_start < pad_end[-1]
    last = jnp.maximum(n_used - 1, 0)
    tile_src = jnp.where(used, jnp.arange(n_tiles, dtype=jnp.int32), last).astype(jnp.int32)
    tile_e = tile_e_raw[tile_src].astype(jnp.int32)
    valid_rows = jnp.clip(counts[tile_e_raw] - (tile_start - pad_start[tile_e_raw]), 0, tm_e)
    tile_nsub = jnp.where(used, (valid_rows + sub - 1) // sub, 0).astype(jnp.int32)
    return row_tok.reshape(n_tiles, 1, tm_e), row_of_assign, tile_e, tile_nsub


def kernel(x, positions, norm1_g, w_in, b_fgate, b_gate, lam_q1, lam_k1, lam_q2, lam_k2, subln_g,
           w_branch_fox, w_branch_diff, w_out, norm2_g, router_w, router_b, w_gu, b_gu, w_down,
           b_down, normf_g):
    b, s, d = x.shape
    t = b * s
    depth = norm1_g.shape[0]
    fox_w = FOX_HEADS * HEAD_DIM
    tm_a = _pick(t, 1024)
    tq = _pick(s, 512)
    tm_d = _pick(t, 256)
    tm_e, sub, tf = 1024, 256, 512
    tmc = _pick(t, 128)

    half = ROPE_DIM // 2
    inv_freq = ROPE_THETA ** (-jnp.arange(0, ROPE_DIM, 2, dtype=F32) / ROPE_DIM)
    zeros = jnp.zeros((LANES - ROPE_DIM,), F32)
    rope_tab = jnp.zeros((8, LANES), F32)
    rope_tab = rope_tab.at[0].set(jnp.concatenate([inv_freq, inv_freq, zeros]))
    rope_tab = rope_tab.at[1].set(jnp.concatenate([-jnp.ones((half,), F32), jnp.zeros((half,), F32), zeros]))
    rope_tab = rope_tab.at[2].set(jnp.concatenate([jnp.zeros((half,), F32), jnp.ones((half,), F32), zeros]))
    pos = positions.astype(F32).reshape(t, 1)

    h = x.reshape(t, d)
    for l in range(depth):
        wi = w_in[l]
        w_main = jnp.concatenate([wi[:, :3 * fox_w], wi[:, 3 * fox_w + FOX_HEADS:]], axis=1).astype(BF16)
        w_f = jnp.pad(wi[:, 3 * fox_w:3 * fox_w + FOX_HEADS], ((0, 0), (0, LANES - FOX_HEADS))).astype(BF16)
        qkv, gate, flog = _inproj(h, pos, norm1_g[l][None, :], w_main, w_f, rope_tab,
                                  b_gate[l][None, :], tm_a)

        fl = flog[:, :FOX_HEADS].reshape(b, s, FOX_HEADS).transpose(0, 2, 1).reshape(b * FOX_HEADS, s)
        bf = jnp.tile(b_fgate[l].astype(F32), b).reshape(b * FOX_HEADS, 1)
        c3 = _fgate(fl, bf).reshape(b * FOX_HEADS, 1, s)

        y_fox = _fox_attn(qkv, c3, b, s, tq)
        lam_init = 0.8 - 0.6 * math.exp(-0.3 * l)
        lam_p = jnp.stack([lam_q1[l], lam_k1[l], lam_q2[l], lam_k2[l]]).astype(F32)
        y_diff = _diff_attn(qkv, lam_p, subln_g[l][None, :].astype(F32), b, s, tq, lam_init)

        rw = jnp.pad(router_w[l], ((0, 0), (0, LANES - N_EXPERTS))).astype(BF16)
        rb = jnp.concatenate([router_b[l].astype(F32), jnp.full((LANES - N_EXPERTS,), NEG_BIG, F32)])[None, :]
        h, u2, idx128, gw128 = _merge(y_fox, y_diff, gate, h, w_branch_fox[l].astype(BF16),
                                      w_branch_diff[l].astype(BF16), w_out[l].astype(BF16),
                                      norm2_g[l][None, :], rw, rb, tm_d)

        row_tok3, roa, tile_e, tile_nsub = _routing(idx128[:, :TOP_K], tm_e, sub)
        y_rows = _experts(tile_e, tile_nsub, row_tok3, u2, w_gu[l], b_gu[l][:, None, :],
                          w_down[l], b_down[l][:, None, :], tm_e, sub, tf)
        roa3 = roa.reshape(t // tmc, tmc, TOP_K).transpose(0, 2, 1).reshape(t // tmc, 1, TOP_K * tmc)
        h = _combine(roa3, y_rows, h, gw128, normf_g[None, :], tmc, final_norm=(l == depth - 1))
    return h.reshape(b, s, d)
```

```python
import functools
import math

import jax
import jax.numpy as jnp
from jax import lax
from jax.experimental import pallas as pl
from jax.experimental.pallas import tpu as pltpu

HEAD_DIM = 128
FOX_HEADS = 8
DIFF_HEADS = 4
DIFF_V_DIM = 2 * HEAD_DIM
ROPE_THETA = 500000.0
ROPE_DIM = HEAD_DIM // 4
N_EXPERTS = 32
TOP_K = 4
SWIGLU_ALPHA = 1.702
SWIGLU_LIMIT = 7.0
EPS = 1e-5
LANES = 128

F32 = jnp.float32
BF16 = jnp.bfloat16
NEG_BIG = -1e30
LOG2E = math.log2(math.e)
ISSUE_UNROLL = 8
ROW_VISIT_STRIDE = 37
VMEM_LIMIT = 56 * 1024 * 1024


def _cparams(sem, vmem=VMEM_LIMIT):
    return pltpu.CompilerParams(dimension_semantics=sem, vmem_limit_bytes=vmem)


def _pick(n, pref):
    t = min(n, pref)
    while n % t:
        t //= 2
    return t


QKV_TILE = 1024
QKV_ROTARY_TILES = (3, 4)
QKV_SCALED_TILES = (0, 3)


def _qkv_kernel(x_ref, g1_ref, w_ref, wf_ref, pos_ref, tab_ref,
                qkv_ref, u_ref, flog_ref, cos_sc, sa_sc, sb_sc, *, scale):
    j = pl.program_id(1)

    @pl.when(j == 0)
    def _():
        xf = x_ref[...]
        ms = jnp.mean(xf * xf, axis=-1, keepdims=True)
        u = (xf * lax.rsqrt(ms + EPS)) * g1_ref[...]
        ub = u.astype(BF16)
        u_ref[...] = ub
        flog_ref[...] = jnp.dot(ub, wf_ref[...], preferred_element_type=F32)
        ang = pos_ref[...] * tab_ref[0:1, :]
        sn = jnp.sin(ang)
        cos_sc[...] = jnp.cos(ang)
        sa_sc[...] = sn * tab_ref[1:2, :]
        sb_sc[...] = sn * tab_ref[2:3, :]

    acc = jnp.dot(u_ref[...], w_ref[...], preferred_element_type=F32)

    rot = functools.reduce(jnp.logical_or, [j == q for q in QKV_ROTARY_TILES])
    scaled = functools.reduce(jnp.logical_or, [j == q for q in QKV_SCALED_TILES])
    mult = jnp.where(scaled, scale, 1.0).astype(F32)
    rot_f = jnp.where(rot, mult, 0.0).astype(F32)
    ca = cos_sc[...] * rot_f + (mult - rot_f)
    cb = sa_sc[...] * rot_f
    cc = sb_sc[...] * rot_f
    for c in range(acc.shape[1] // HEAD_DIM):
        t = acc[:, c * HEAD_DIM:(c + 1) * HEAD_DIM]
        r = (t * ca + pltpu.roll(t, HEAD_DIM - ROPE_DIM // 2, 1) * cb
             + pltpu.roll(t, ROPE_DIM // 2, 1) * cc)
        qkv_ref[:, c * HEAD_DIM:(c + 1) * HEAD_DIM] = r.astype(BF16)


def _gates_kernel(u_ref, w_ref, bg_ref, gate_ref):
    acc = jnp.dot(u_ref[...], w_ref[...], preferred_element_type=F32)
    gate_ref[...] = jax.nn.sigmoid(acc + bg_ref[...])


def _inproj(x2, pos, g1, w_main, w_f, rope_tab, b_gate, tm):
    t, d = x2.shape
    tn = QKV_TILE
    n_qkv = 6
    n_gate = (w_main.shape[1] - n_qkv * tn) // tn
    kern = functools.partial(_qkv_kernel, scale=LOG2E / math.sqrt(HEAD_DIM))
    qkv, u, flog = pl.pallas_call(
        kern,
        grid=(t // tm, n_qkv),
        in_specs=[
            pl.BlockSpec((tm, d), lambda i, j: (i, 0)),
            pl.BlockSpec((1, d), lambda i, j: (0, 0)),
            pl.BlockSpec((d, tn), lambda i, j: (0, j)),
            pl.BlockSpec((d, LANES), lambda i, j: (0, 0)),
            pl.BlockSpec((tm, 1), lambda i, j: (i, 0)),
            pl.BlockSpec((8, LANES), lambda i, j: (0, 0)),
        ],
        out_specs=[
            pl.BlockSpec((tm, tn), lambda i, j: (i, j)),
            pl.BlockSpec((tm, d), lambda i, j: (i, 0)),
            pl.BlockSpec((tm, LANES), lambda i, j: (i, 0)),
        ],
        out_shape=[
            jax.ShapeDtypeStruct((t, n_qkv * tn), BF16),
            jax.ShapeDtypeStruct((t, d), BF16),
            jax.ShapeDtypeStruct((t, LANES), F32),
        ],
        scratch_shapes=[
            pltpu.VMEM((tm, LANES), F32),
            pltpu.VMEM((tm, LANES), F32),
            pltpu.VMEM((tm, LANES), F32),
        ],
        compiler_params=_cparams(("parallel", "arbitrary")),
    )(x2, g1, w_main, w_f, pos, rope_tab)
    gate = pl.pallas_call(
        _gates_kernel,
        grid=(t // tm, n_gate),
        in_specs=[
            pl.BlockSpec((tm, d), lambda i, j: (i, 0)),
            pl.BlockSpec((d, tn), lambda i, j: (0, n_qkv + j)),
            pl.BlockSpec((1, tn), lambda i, j: (0, j)),
        ],
        out_specs=pl.BlockSpec((tm, tn), lambda i, j: (i, j)),
        out_shape=jax.ShapeDtypeStruct((t, n_gate * tn), F32),
        compiler_params=_cparams(("parallel", "arbitrary")),
    )(u, w_main, b_gate)
    return qkv, gate, flog


def _fgate_kernel(fl_ref, b_ref, c_ref):
    z = fl_ref[...] + b_ref[...]
    x = jnp.minimum(z, 0.0) - jnp.log(1.0 + jnp.exp(-jnp.abs(z)))
    n = x.shape[1]
    lane = lax.broadcasted_iota(jnp.int32, x.shape, 1)
    sh = 1
    while sh < n:
        x = x + jnp.where(lane >= sh, pltpu.roll(x, sh, 1), 0.0)
        sh *= 2
    c_ref[...] = x


def _fgate(fl, b):
    r, s = fl.shape
    return pl.pallas_call(
        _fgate_kernel,
        grid=(1,),
        in_specs=[pl.BlockSpec((r, s), lambda i: (0, 0)), pl.BlockSpec((r, 1), lambda i: (0, 0))],
        out_specs=pl.BlockSpec((r, s), lambda i: (0, 0)),
        out_shape=jax.ShapeDtypeStruct((r, s), F32),
        compiler_params=_cparams(("arbitrary",)),
    )(fl, b)


def _transpose_bf16(x):
    return x.astype(F32).T.astype(BF16)


def _softmax_step(s, vt, m_sc, l_sc, acc_sc):
    m_old = m_sc[...]
    m_new = jnp.maximum(m_old, jnp.max(s, axis=0, keepdims=True))
    alpha = jnp.exp2(m_old - m_new)
    p = jnp.exp2(s - m_new)
    l_sc[...] = alpha * l_sc[...] + jnp.sum(p, axis=0, keepdims=True)
    acc_sc[...] = alpha * acc_sc[...] + jnp.dot(vt, p.astype(BF16), preferred_element_type=F32)
    m_sc[...] = m_new


def _causal_keep(shape):
    kv = lax.broadcasted_iota(jnp.int32, shape, 0)
    qq = lax.broadcasted_iota(jnp.int32, shape, 1)
    return kv <= qq


FOX_HEADS_PER_STEP = 2


def _fox_kernel(q_ref, k_ref, v_ref, c_ref, o_ref, *scratch, tq, tk, s_len):
    hp = FOX_HEADS_PER_STEP
    qt_sc, kaug_sc, vt_sc, m_sc, l_sc, acc_sc = (scratch[n * hp:(n + 1) * hp] for n in range(6))
    qi = pl.program_id(2)
    head_cols = [slice(h * HEAD_DIM, (h + 1) * HEAD_DIM) for h in range(hp)]

    @pl.when(qi == 0)
    def _():
        sub = lax.broadcasted_iota(jnp.int32, (HEAD_DIM, tk), 0)
        for h in range(hp):
            for ch in range(s_len // tk):
                rows = slice(ch * tk, (ch + 1) * tk)
                vt_sc[h][:, rows] = _transpose_bf16(v_ref[rows, head_cols[h]])
                c = c_ref[h, :, rows] * (-LOG2E)
                hi = c.astype(BF16).astype(F32)
                mid = (c - hi).astype(BF16).astype(F32)
                lo = (c - hi - mid).astype(BF16).astype(F32)
                tab = jnp.where(sub == 0, hi, jnp.where(sub == 1, mid, jnp.where(sub == 2, lo, 0.0)))
                kaug_sc[h][rows, HEAD_DIM:] = tab.T.astype(BF16)
                kaug_sc[h][rows, :HEAD_DIM] = k_ref[rows, head_cols[h]]

    sub_q = lax.broadcasted_iota(jnp.int32, (HEAD_DIM, tq), 0)
    for h in range(hp):
        qt_sc[h][:HEAD_DIM, :] = _transpose_bf16(q_ref[:, head_cols[h]])
        qt_sc[h][HEAD_DIM:, :] = jnp.where(sub_q < 3, 1.0, 0.0).astype(BF16)
        m_sc[h][...] = jnp.full(m_sc[h].shape, -jnp.inf, F32)
        l_sc[h][...] = jnp.zeros(l_sc[h].shape, F32)
        acc_sc[h][...] = jnp.zeros(acc_sc[h].shape, F32)

    def step(start, masked):
        for h in range(hp):
            s = jnp.dot(kaug_sc[h][pl.ds(start, tk), :], qt_sc[h][...], preferred_element_type=F32)
            if masked:
                s = jnp.where(_causal_keep(s.shape), s, -jnp.inf)
            _softmax_step(s, vt_sc[h][:, pl.ds(start, tk)], m_sc[h], l_sc[h], acc_sc[h])

    def body(kc, carry):
        step(pl.multiple_of(kc * tk, tk), False)
        return carry

    lax.fori_loop(0, qi, body, 0)
    step(pl.multiple_of(qi * tq, tq), True)
    for h in range(hp):
        o_ref[:, head_cols[h]] = (acc_sc[h][...] / l_sc[h][...]).T.astype(o_ref.dtype)


def _fox_attn(qkv, c3, b, s, tq):
    t = b * s
    nq = s // tq
    hp = FOX_HEADS_PER_STEP
    groups = FOX_HEADS // hp
    width = hp * HEAD_DIM
    kern = functools.partial(_fox_kernel, tq=tq, tk=tq, s_len=s)

    def per_head(shape, dtype):
        return [pltpu.VMEM(shape, dtype) for _ in range(hp)]

    return pl.pallas_call(
        kern,
        grid=(b, groups, nq),
        in_specs=[
            pl.BlockSpec((tq, width), lambda bi, g, qi: (bi * nq + qi, g)),
            pl.BlockSpec((s, width), lambda bi, g, qi: (bi, groups + g)),
            pl.BlockSpec((s, width), lambda bi, g, qi: (bi, 2 * groups + g)),
            pl.BlockSpec((hp, 1, s), lambda bi, g, qi: (bi * groups + g, 0, 0)),
        ],
        out_specs=pl.BlockSpec((tq, width), lambda bi, g, qi: (bi * nq + qi, g)),
        out_shape=jax.ShapeDtypeStruct((t, FOX_HEADS * HEAD_DIM), BF16),
        scratch_shapes=(per_head((2 * HEAD_DIM, tq), BF16) + per_head((s, 2 * HEAD_DIM), BF16)
                        + per_head((HEAD_DIM, s), BF16) + per_head((1, tq), F32)
                        + per_head((1, tq), F32) + per_head((HEAD_DIM, tq), F32)),
        compiler_params=_cparams(("parallel", "parallel", "arbitrary")),
    )(qkv, qkv, qkv, c3)


def _diff_kernel(q1_ref, q2_ref, k1_ref, k2_ref, v_ref, lam_ref, g_ref, o_ref,
                 q1t_sc, q2t_sc, vt_sc, m1_sc, l1_sc, a1_sc, m2_sc, l2_sc, a2_sc,
                 *, tq, tk, s_len, lam_init):
    qi = pl.program_id(2)

    @pl.when(qi == 0)
    def _():
        for ch in range(s_len // tk):
            rows = slice(ch * tk, (ch + 1) * tk)
            vt_sc[:, rows] = _transpose_bf16(v_ref[rows, :])

    q1t_sc[...] = _transpose_bf16(q1_ref[...])
    q2t_sc[...] = _transpose_bf16(q2_ref[...])
    for r in (m1_sc, m2_sc):
        r[...] = jnp.full(r.shape, -jnp.inf, F32)
    for r in (l1_sc, l2_sc, a1_sc, a2_sc):
        r[...] = jnp.zeros(r.shape, F32)

    def step(start, masked):
        vt = vt_sc[:, pl.ds(start, tk)]
        s1 = jnp.dot(k1_ref[pl.ds(start, tk), :], q1t_sc[...], preferred_element_type=F32)
        s2 = jnp.dot(k2_ref[pl.ds(start, tk), :], q2t_sc[...], preferred_element_type=F32)
        if masked:
            keep = _causal_keep(s1.shape)
            s1 = jnp.where(keep, s1, -jnp.inf)
            s2 = jnp.where(keep, s2, -jnp.inf)
        _softmax_step(s1, vt, m1_sc, l1_sc, a1_sc)
        _softmax_step(s2, vt, m2_sc, l2_sc, a2_sc)

    def body(kc, carry):
        step(pl.multiple_of(kc * tk, tk), False)
        return carry

    lax.fori_loop(0, qi, body, 0)
    step(pl.multiple_of(qi * tq, tq), True)

    lp = lam_ref[...]
    lam = (jnp.exp(jnp.sum(lp[0:1, :] * lp[1:2, :], axis=-1, keepdims=True))
           - jnp.exp(jnp.sum(lp[2:3, :] * lp[3:4, :], axis=-1, keepdims=True)) + lam_init)
    y = (a1_sc[...] / l1_sc[...]).T - lam * (a2_sc[...] / l2_sc[...]).T
    ms = jnp.mean(y * y, axis=-1, keepdims=True)
    y = (y * lax.rsqrt(ms + EPS)) * g_ref[...]
    o_ref[...] = (y * (1.0 - lam_init)).astype(o_ref.dtype)


def _diff_attn(qkv, lam_p, subln_g, b, s, tq, lam_init):
    t = b * s
    nq = s // tq
    qb = 3 * FOX_HEADS
    kb = qb + 2 * DIFF_HEADS
    vb = (kb + 2 * DIFF_HEADS) // 2
    kern = functools.partial(_diff_kernel, tq=tq, tk=tq, s_len=s, lam_init=lam_init)
    return pl.pallas_call(
        kern,
        grid=(b, DIFF_HEADS, nq),
        in_specs=[
            pl.BlockSpec((tq, HEAD_DIM), lambda bi, h, qi: (bi * nq + qi, qb + 2 * h)),
            pl.BlockSpec((tq, HEAD_DIM), lambda bi, h, qi: (bi * nq + qi, qb + 2 * h + 1)),
            pl.BlockSpec((s, HEAD_DIM), lambda bi, h, qi: (bi, kb + 2 * h)),
            pl.BlockSpec((s, HEAD_DIM), lambda bi, h, qi: (bi, kb + 2 * h + 1)),
            pl.BlockSpec((s, DIFF_V_DIM), lambda bi, h, qi: (bi, vb + h)),
            pl.BlockSpec((4, HEAD_DIM), lambda bi, h, qi: (0, 0)),
            pl.BlockSpec((1, DIFF_V_DIM), lambda bi, h, qi: (0, 0)),
        ],
        out_specs=pl.BlockSpec((tq, DIFF_V_DIM), lambda bi, h, qi: (bi * nq + qi, h)),
        out_shape=jax.ShapeDtypeStruct((t, DIFF_HEADS * DIFF_V_DIM), BF16),
        scratch_shapes=[
            pltpu.VMEM((HEAD_DIM, tq), BF16), pltpu.VMEM((HEAD_DIM, tq), BF16),
            pltpu.VMEM((DIFF_V_DIM, s), BF16),
            pltpu.VMEM((1, tq), F32), pltpu.VMEM((1, tq), F32), pltpu.VMEM((DIFF_V_DIM, tq), F32),
            pltpu.VMEM((1, tq), F32), pltpu.VMEM((1, tq), F32), pltpu.VMEM((DIFF_V_DIM, tq), F32),
        ],
        compiler_params=_cparams(("parallel", "parallel", "arbitrary")),
    )(qkv, qkv, qkv, qkv, qkv, lam_p, subln_g)


def _pack_bf16_pairs(xb):
    w = xb.shape[1] // 2
    lo = lax.bitcast_convert_type(xb[:, :w].astype(F32), jnp.uint32)
    hi = lax.bitcast_convert_type(xb[:, w:].astype(F32), jnp.uint32)
    return (lo >> 16) | (hi & jnp.uint32(0xFFFF0000))


def _unpack_bf16_pairs(p):
    lo = lax.bitcast_convert_type(p << 16, F32).astype(BF16)
    hi = lax.bitcast_convert_type(p & jnp.uint32(0xFFFF0000), F32).astype(BF16)
    return lo, hi


def _merge_kernel(yf_ref, yd_ref, g0_ref, g1_ref, x_ref, wbf_ref, wbd_ref, wo_ref, n2_ref,
                  rw_ref, rb_ref, h_ref, u_ref, idx_ref, gw_ref):
    ya = jnp.dot(yf_ref[...], wbf_ref[...], preferred_element_type=F32)
    yb = jnp.dot(yd_ref[...], wbd_ref[...], preferred_element_type=F32)
    merged = g0_ref[...] * ya + g1_ref[...] * yb
    h = x_ref[...] + jnp.dot(merged.astype(BF16), wo_ref[...], preferred_element_type=F32)
    h_ref[...] = h
    ms = jnp.mean(h * h, axis=-1, keepdims=True)
    u = (h * lax.rsqrt(ms + EPS)) * n2_ref[...]
    ub = u.astype(BF16)
    u_ref[...] = _pack_bf16_pairs(ub)
    logits = jnp.dot(ub, rw_ref[...], preferred_element_type=F32) + rb_ref[...]

    lane = lax.broadcasted_iota(jnp.int32, logits.shape, 1).astype(F32)
    work = logits
    vals, idxs = [], []
    for _ in range(TOP_K):
        m = jnp.max(work, axis=-1, keepdims=True)
        sel = jnp.min(jnp.where(work == m, lane, float(LANES)), axis=-1, keepdims=True)
        vals.append(m)
        idxs.append(sel)
        work = jnp.where(lane == sel, -jnp.inf, work)
    es = [jnp.exp(v - vals[0]) for v in vals]
    den = es[0]
    for e in es[1:]:
        den = den + e
    idx_out = jnp.zeros(logits.shape, F32)
    gw_out = jnp.zeros(logits.shape, F32)
    for k in range(TOP_K):
        idx_out = jnp.where(lane == float(k), idxs[k], idx_out)
        gw_out = jnp.where(lane == float(k), es[k] / den, gw_out)
    idx_ref[...] = idx_out.astype(jnp.int32)
    gw_ref[...] = gw_out


def _merge(yf, yd, gate, x2, wbf, wbd, wo, n2, rw, rb, tm):
    t, d = x2.shape
    fw = yf.shape[1]
    dw = yd.shape[1]
    const = dict(pipeline_mode=pl.Buffered(1))
    return pl.pallas_call(
        _merge_kernel,
        grid=(t // tm,),
        in_specs=[
            pl.BlockSpec((tm, fw), lambda i: (i, 0)),
            pl.BlockSpec((tm, dw), lambda i: (i, 0)),
            pl.BlockSpec((tm, d), lambda i: (i, 0)),
            pl.BlockSpec((tm, d), lambda i: (i, 1)),
            pl.BlockSpec((tm, d), lambda i: (i, 0)),
            pl.BlockSpec((fw, d), lambda i: (0, 0), **const),
            pl.BlockSpec((dw, d), lambda i: (0, 0), **const),
            pl.BlockSpec((d, d), lambda i: (0, 0), **const),
            pl.BlockSpec((1, d), lambda i: (0, 0)),
            pl.BlockSpec((d, LANES), lambda i: (0, 0), **const),
            pl.BlockSpec((1, LANES), lambda i: (0, 0)),
        ],
        out_specs=[
            pl.BlockSpec((tm, d), lambda i: (i, 0)),
            pl.BlockSpec((tm, d // 2), lambda i: (i, 0)),
            pl.BlockSpec((tm, LANES), lambda i: (i, 0)),
            pl.BlockSpec((tm, LANES), lambda i: (i, 0)),
        ],
        out_shape=[
            jax.ShapeDtypeStruct((t, d), F32),
            jax.ShapeDtypeStruct((t, d // 2), jnp.uint32),
            jax.ShapeDtypeStruct((t, LANES), jnp.int32),
            jax.ShapeDtypeStruct((t, LANES), F32),
        ],
        compiler_params=_cparams(("parallel",)),
    )(yf, yd, gate, gate, x2, wbf, wbd, wo, n2, rw, rb)


def _expert_kernel(te_ref, nsub_ref, cur_ref, nxt_ref, u_hbm, wg_ref, wu_ref, bg_ref, bu_ref, wd_ref,
                   bd_ref, out_ref, stage, x_sc, act_sc, sems, *, sub, spt, nf):
    i = pl.program_id(0)
    j = pl.program_id(1)
    nsub = nsub_ref[i]
    slot = i % 2
    tm_e = x_sc.shape[0]
    per_step = tm_e // (2 * nf)
    half = x_sc.shape[1] // 2

    def start_rows(tok_ref, sl, first, count):
        for r in range(count):
            row = ((first + r) * ROW_VISIT_STRIDE) % tm_e
            pltpu.make_async_copy(u_hbm.at[pl.ds(tok_ref[0, 0, row], 1), :],
                                  stage.at[sl, pl.ds(row, 1), :], sems.at[sl]).start()

    def start_next_tile_rows():
        start_rows(nxt_ref, 1 - slot, j * per_step, per_step)

    @pl.when(jnp.logical_and(i == 0, j == 0))
    def _():
        def body(g, carry):
            start_rows(cur_ref, 0, g * ISSUE_UNROLL, ISSUE_UNROLL)
            return carry
        lax.fori_loop(0, tm_e // ISSUE_UNROLL, body, 0)

    fetched = jnp.logical_or(i == 0, nsub_ref[jnp.maximum(i - 1, 0)] > 0)

    @pl.when(jnp.logical_and(j == 0, fetched))
    def _():
        pltpu.make_async_copy(u_hbm.at[pl.ds(0, tm_e), :], stage.at[slot], sems.at[slot]).wait()

        @pl.when(nsub > 0)
        def _():
            lo, hi = _unpack_bf16_pairs(stage[slot])
            x_sc[:, :half] = lo
            x_sc[:, half:] = hi

    full = nsub == spt

    def gate_up(rs, wg, wu):
        x = x_sc[rs, :]
        hg = jnp.dot(x, wg, preferred_element_type=F32) + bg_ref[0]
        hu = jnp.dot(x, wu, preferred_element_type=F32) + bu_ref[0]
        hg = jnp.minimum(hg, SWIGLU_LIMIT)
        hl = jnp.clip(hu, -SWIGLU_LIMIT, SWIGLU_LIMIT)
        act = hg * jax.nn.sigmoid(SWIGLU_ALPHA * hg) * (hl + 1.0)
        act_sc[j, rs, :] = act.astype(BF16)

    def down(rs, wd):
        act = jnp.concatenate([act_sc[c, rs, :] for c in range(nf)], axis=1)
        out_ref[rs, :] = jnp.dot(act, wd, preferred_element_type=F32) + bd_ref[0]

    partial_tile = jnp.logical_and(nsub > 0, nsub < spt)

    @pl.when(jnp.logical_and(j < nf, full))
    def _():
        start_next_tile_rows()
        gate_up(slice(None), wg_ref[0].astype(BF16), wu_ref[0].astype(BF16))

    @pl.when(jnp.logical_and(j < nf, partial_tile))
    def _():
        start_next_tile_rows()
        wg = wg_ref[0].astype(BF16)
        wu = wu_ref[0].astype(BF16)
        for s in range(spt - 1):
            pl.when(s < nsub)(functools.partial(gate_up, slice(s * sub, (s + 1) * sub), wg, wu))

    @pl.when(jnp.logical_and(j >= nf, full))
    def _():
        start_next_tile_rows()
        down(slice(None), wd_ref[0].astype(BF16))

    @pl.when(jnp.logical_and(j >= nf, partial_tile))
    def _():
        start_next_tile_rows()
        wd = wd_ref[0].astype(BF16)
        for s in range(spt - 1):
            pl.when(s < nsub)(functools.partial(down, slice(s * sub, (s + 1) * sub), wd))

    @pl.when(jnp.logical_and(j >= nf, nsub < spt))
    def _():
        for s in range(spt):
            @pl.when(s >= nsub)
            def _():
                out_ref[s * sub:(s + 1) * sub, :] = jnp.zeros((sub, out_ref.shape[1]), out_ref.dtype)


def _experts(tile_e, tile_nsub, row_tok3, u_packed, w_gu, b_gu3, w_down, b_down3, tm_e, sub, tf):
    n_tiles = row_tok3.shape[0]
    n_rows = n_tiles * tm_e
    d = 2 * u_packed.shape[1]
    dff = w_down.shape[1]
    nf = dff // tf
    nn = d // tf
    assert nn == nf
    kern = functools.partial(_expert_kernel, sub=sub, spt=tm_e // sub, nf=nf)

    def c1(i, j, ns):
        return jnp.where(ns[i] > 0, jnp.minimum(j, nf - 1), nf - 1)

    def c2(i, j, ns):
        return jnp.where(ns[i] > 0, jnp.maximum(j - nf, 0), nf - 1)

    tok_block = (1, 1, tm_e)
    return pl.pallas_call(
        kern,
        grid_spec=pltpu.PrefetchScalarGridSpec(
            num_scalar_prefetch=2,
            grid=(n_tiles, 2 * nf),
            in_specs=[
                pl.BlockSpec(tok_block, lambda i, j, te, ns: (i, 0, 0), memory_space=pltpu.SMEM),
                pl.BlockSpec(tok_block, lambda i, j, te, ns: (jnp.minimum(i + 1, n_tiles - 1), 0, 0),
                             memory_space=pltpu.SMEM),
                pl.BlockSpec(memory_space=pl.ANY),
                pl.BlockSpec((1, d, tf), lambda i, j, te, ns: (te[i], 0, c1(i, j, ns))),
                pl.BlockSpec((1, d, tf), lambda i, j, te, ns: (te[i], 0, nf + c1(i, j, ns))),
                pl.BlockSpec((1, 1, tf), lambda i, j, te, ns: (te[i], 0, c1(i, j, ns))),
                pl.BlockSpec((1, 1, tf), lambda i, j, te, ns: (te[i], 0, nf + c1(i, j, ns))),
                pl.BlockSpec((1, dff, tf), lambda i, j, te, ns: (te[i], 0, c2(i, j, ns))),
                pl.BlockSpec((1, 1, tf), lambda i, j, te, ns: (te[i], 0, c2(i, j, ns))),
            ],
            out_specs=pl.BlockSpec((tm_e, tf), lambda i, j, te, ns: (i, jnp.maximum(j - nf, 0))),
            scratch_shapes=[
                pltpu.VMEM((2, tm_e, d // 2), jnp.uint32),
                pltpu.VMEM((tm_e, d), BF16),
                pltpu.VMEM((nf, tm_e, tf), BF16),
                pltpu.SemaphoreType.DMA((2,)),
            ],
        ),
        out_shape=jax.ShapeDtypeStruct((n_rows, d), F32),
        compiler_params=_cparams(("arbitrary", "arbitrary")),
    )(tile_e, tile_nsub, row_tok3, row_tok3, u_packed, w_gu, w_gu, b_gu3, b_gu3, w_down, b_down3)


def _combine_kernel(cur_ref, nxt_ref, yr_hbm, h_ref, gw_ref, gf_ref, out_ref, ysel, sems, *, tmc,
                    final_norm):
    i = pl.program_id(0)
    n = pl.num_programs(0)
    slot = i % 2
    n_dma = TOP_K * tmc

    def row_copy(row, q, sl):
        return pltpu.make_async_copy(yr_hbm.at[pl.ds(row, 1), :], ysel.at[sl, pl.ds(q, 1), :], sems.at[sl])

    def issue(idx_ref, sl):
        def body(g, carry):
            for q8 in range(ISSUE_UNROLL):
                q = g * ISSUE_UNROLL + q8
                row_copy(idx_ref[0, 0, q], q, sl).start()
            return carry
        lax.fori_loop(0, n_dma // ISSUE_UNROLL, body, 0)

    @pl.when(i == 0)
    def _():
        issue(cur_ref, 0)

    @pl.when(i + 1 < n)
    def _():
        issue(nxt_ref, 1 - slot)

    pltpu.make_async_copy(yr_hbm.at[pl.ds(0, n_dma), :], ysel.at[slot], sems.at[slot]).wait()

    gw = gw_ref[...]
    y = h_ref[...]
    for k in range(TOP_K):
        y = y + gw[:, k:k + 1] * ysel[slot, k * tmc:(k + 1) * tmc, :]
    if final_norm:
        ms = jnp.mean(y * y, axis=-1, keepdims=True)
        y = (y * lax.rsqrt(ms + EPS)) * gf_ref[...]
    out_ref[...] = y


def _combine(roa3, y_rows, h, gw, gf, tmc, final_norm):
    t, d = h.shape
    n = t // tmc
    kern = functools.partial(_combine_kernel, tmc=tmc, final_norm=final_norm)
    return pl.pallas_call(
        kern,
        grid=(n,),
        in_specs=[
            pl.BlockSpec((1, 1, TOP_K * tmc), lambda i: (i, 0, 0), memory_space=pltpu.SMEM),
            pl.BlockSpec((1, 1, TOP_K * tmc), lambda i: (jnp.minimum(i + 1, n - 1), 0, 0),
                         memory_space=pltpu.SMEM),
            pl.BlockSpec(memory_space=pl.ANY),
            pl.BlockSpec((tmc, d), lambda i: (i, 0)),
            pl.BlockSpec((tmc, LANES), lambda i: (i, 0)),
            pl.BlockSpec((1, d), lambda i: (0, 0)),
        ],
        out_specs=pl.BlockSpec((tmc, d), lambda i: (i, 0)),
        out_shape=jax.ShapeDtypeStruct((t, d), F32),
        scratch_shapes=[pltpu.VMEM((2, TOP_K * tmc, d), F32), pltpu.SemaphoreType.DMA((2,))],
        compiler_params=_cparams(("arbitrary",)),
    )(roa3, roa3, y_rows, h, gw, gf)


def _routing(top_idx, tm_e, sub):
    t = top_idx.shape[0]
    a = t * TOP_K
    flat_e = top_idx.reshape(a)
    onehot = (flat_e[:, None] == jnp.arange(N_EXPERTS, dtype=jnp.int32)[None, :]).astype(jnp.int32)
    csum = jnp.cumsum(onehot, axis=0)
    counts = csum[-1]
    rank = jnp.sum(onehot * csum, axis=1) - 1
    padded = (counts + tm_e - 1) // tm_e * tm_e
    pad_end = jnp.cumsum(padded)
    pad_start = pad_end - padded
    row_of_assign = (pad_start[flat_e] + rank).astype(jnp.int32)
    n_tiles = -(-a // tm_e) + N_EXPERTS
    n_rows = n_tiles * tm_e
    row_tok = jnp.zeros((n_rows,), jnp.int32).at[row_of_assign].set(
        jnp.arange(a, dtype=jnp.int32) // TOP_K, unique_indices=True, mode='promise_in_bounds')
    tile_start = jnp.arange(n_tiles, dtype=jnp.int32) * tm_e
    n_used = pad_end[-1] // tm_e
    tile_e_raw = jnp.minimum(jnp.searchsorted(pad_end, tile_start, side='right'), N_EXPERTS - 1)
    used = tile_start < pad_end[-1]
    last = jnp.maximum(n_used - 1, 0)
    tile_src = jnp.where(used, jnp.arange(n_tiles, dtype=jnp.int32), last).astype(jnp.int32)
    tile_e = tile_e_raw[tile_src].astype(jnp.int32)
    valid_rows = jnp.clip(counts[tile_e_raw] - (tile_start - pad_start[tile_e_raw]), 0, tm_e)
    tile_nsub = jnp.where(used, (valid_rows + sub - 1) // sub, 0).astype(jnp.int32)
    return row_tok.reshape(n_tiles, 1, tm_e), row_of_assign, tile_e, tile_nsub


def kernel(x, positions, norm1_g, w_in, b_fgate, b_gate, lam_q1, lam_k1, lam_q2, lam_k2, subln_g,
           w_branch_fox, w_branch_diff, w_out, norm2_g, router_w, router_b, w_gu, b_gu, w_down,
           b_down, normf_g):
    b, s, d = x.shape
    t = b * s
    depth = norm1_g.shape[0]
    fox_w = FOX_HEADS * HEAD_DIM
    tm_a = _pick(t, 1024)
    tq = _pick(s, 512)
    tm_d = _pick(t, 256)
    tm_e, sub, tf = 1024, 256, 512
    tmc = _pick(t, 128)

    half = ROPE_DIM // 2
    inv_freq = ROPE_THETA ** (-jnp.arange(0, ROPE_DIM, 2, dtype=F32) / ROPE_DIM)
    zeros = jnp.zeros((LANES - ROPE_DIM,), F32)
    rope_tab = jnp.zeros((8, LANES), F32)
    rope_tab = rope_tab.at[0].set(jnp.concatenate([inv_freq, inv_freq, zeros]))
    rope_tab = rope_tab.at[1].set(jnp.concatenate([-jnp.ones((half,), F32), jnp.zeros((half,), F32), zeros]))
    rope_tab = rope_tab.at[2].set(jnp.concatenate([jnp.zeros((half,), F32), jnp.ones((half,), F32), zeros]))
    pos = positions.astype(F32).reshape(t, 1)

    h = x.reshape(t, d)
    for l in range(depth):
        wi = w_in[l]
        w_main = jnp.concatenate([wi[:, :3 * fox_w], wi[:, 3 * fox_w + FOX_HEADS:]], axis=1).astype(BF16)
        w_f = jnp.pad(wi[:, 3 * fox_w:3 * fox_w + FOX_HEADS], ((0, 0), (0, LANES - FOX_HEADS))).astype(BF16)
        qkv, gate, flog = _inproj(h, pos, norm1_g[l][None, :], w_main, w_f, rope_tab,
                                  b_gate[l][None, :], tm_a)

        fl = flog[:, :FOX_HEADS].reshape(b, s, FOX_HEADS).transpose(0, 2, 1).reshape(b * FOX_HEADS, s)
        bf = jnp.tile(b_fgate[l].astype(F32), b).reshape(b * FOX_HEADS, 1)
        c3 = _fgate(fl, bf).reshape(b * FOX_HEADS, 1, s)

        y_fox = _fox_attn(qkv, c3, b, s, tq)
        lam_init = 0.8 - 0.6 * math.exp(-0.3 * l)
        lam_p = jnp.stack([lam_q1[l], lam_k1[l], lam_q2[l], lam_k2[l]]).astype(F32)
        y_diff = _diff_attn(qkv, lam_p, subln_g[l][None, :].astype(F32), b, s, tq, lam_init)

        rw = jnp.pad(router_w[l], ((0, 0), (0, LANES - N_EXPERTS))).astype(BF16)
        rb = jnp.concatenate([router_b[l].astype(F32), jnp.full((LANES - N_EXPERTS,), NEG_BIG, F32)])[None, :]
        h, u2, idx128, gw128 = _merge(y_fox, y_diff, gate, h, w_branch_fox[l].astype(BF16),
                                      w_branch_diff[l].astype(BF16), w_out[l].astype(BF16),
                                      norm2_g[l][None, :], rw, rb, tm_d)

        row_tok3, roa, tile_e, tile_nsub = _routing(idx128[:, :TOP_K], tm_e, sub)
        y_rows = _experts(tile_e, tile_nsub, row_tok3, u2, w_gu[l], b_gu[l][:, None, :],
                          w_down[l], b_down[l][:, None, :], tm_e, sub, tf)
        roa3 = roa.reshape(t // tmc, tmc, TOP_K).transpose(0, 2, 1).reshape(t // tmc, 1, TOP_K * tmc)
        h = _combine(roa3, y_rows, h, gw128, normf_g[None, :], tmc, final_norm=(l == depth - 1))
    return h.reshape(b, s, d)
```

```python
import functools
import math

import jax
import jax.numpy as jnp
from jax import lax
from jax.experimental import pallas as pl
from jax.experimental.pallas import tpu as pltpu

HEAD_DIM = 128
FOX_HEADS = 8
DIFF_HEADS = 4
DIFF_V_DIM = 2 * HEAD_DIM
ROPE_THETA = 500000.0
ROPE_DIM = HEAD_DIM // 4
N_EXPERTS = 32
TOP_K = 4
SWIGLU_ALPHA = 1.702
SWIGLU_LIMIT = 7.0
EPS = 1e-5
LANES = 128

F32 = jnp.float32
BF16 = jnp.bfloat16
NEG_BIG = -1e30
LOG2E = math.log2(math.e)
ISSUE_UNROLL = 8
VMEM_LIMIT = 56 * 1024 * 1024


def _cparams(sem, vmem=VMEM_LIMIT):
    return pltpu.CompilerParams(dimension_semantics=sem, vmem_limit_bytes=vmem)


def _pick(n, pref):
    t = min(n, pref)
    while n % t:
        t //= 2
    return t


QKV_TILE = 1024
QKV_ROTARY_TILES = (3, 4)
QKV_SCALED_TILES = (0, 3)


def _qkv_kernel(x_ref, g1_ref, w_ref, wf_ref, pos_ref, tab_ref,
                qkv_ref, u_ref, flog_ref, cos_sc, sa_sc, sb_sc, *, scale):
    j = pl.program_id(1)

    @pl.when(j == 0)
    def _():
        xf = x_ref[...]
        ms = jnp.mean(xf * xf, axis=-1, keepdims=True)
        u = (xf * lax.rsqrt(ms + EPS)) * g1_ref[...]
        ub = u.astype(BF16)
        u_ref[...] = ub
        flog_ref[...] = jnp.dot(ub, wf_ref[...], preferred_element_type=F32)
        ang = pos_ref[...] * tab_ref[0:1, :]
        sn = jnp.sin(ang)
        cos_sc[...] = jnp.cos(ang)
        sa_sc[...] = sn * tab_ref[1:2, :]
        sb_sc[...] = sn * tab_ref[2:3, :]

    acc = jnp.dot(u_ref[...], w_ref[...], preferred_element_type=F32)

    rot = functools.reduce(jnp.logical_or, [j == q for q in QKV_ROTARY_TILES])
    scaled = functools.reduce(jnp.logical_or, [j == q for q in QKV_SCALED_TILES])
    mult = jnp.where(scaled, scale, 1.0).astype(F32)
    rot_f = jnp.where(rot, mult, 0.0).astype(F32)
    ca = cos_sc[...] * rot_f + (mult - rot_f)
    cb = sa_sc[...] * rot_f
    cc = sb_sc[...] * rot_f
    for c in range(acc.shape[1] // HEAD_DIM):
        t = acc[:, c * HEAD_DIM:(c + 1) * HEAD_DIM]
        r = (t * ca + pltpu.roll(t, HEAD_DIM - ROPE_DIM // 2, 1) * cb
             + pltpu.roll(t, ROPE_DIM // 2, 1) * cc)
        qkv_ref[:, c * HEAD_DIM:(c + 1) * HEAD_DIM] = r.astype(BF16)


def _gates_kernel(u_ref, w_ref, bg_ref, gate_ref):
    acc = jnp.dot(u_ref[...], w_ref[...], preferred_element_type=F32)
    gate_ref[...] = jax.nn.sigmoid(acc + bg_ref[...])


def _inproj(x2, pos, g1, w_main, w_f, rope_tab, b_gate, tm):
    t, d = x2.shape
    tn = QKV_TILE
    n_qkv = 6
    n_gate = (w_main.shape[1] - n_qkv * tn) // tn
    kern = functools.partial(_qkv_kernel, scale=LOG2E / math.sqrt(HEAD_DIM))
    qkv, u, flog = pl.pallas_call(
        kern,
        grid=(t // tm, n_qkv),
        in_specs=[
            pl.BlockSpec((tm, d), lambda i, j: (i, 0)),
            pl.BlockSpec((1, d), lambda i, j: (0, 0)),
            pl.BlockSpec((d, tn), lambda i, j: (0, j)),
            pl.BlockSpec((d, LANES), lambda i, j: (0, 0)),
            pl.BlockSpec((tm, 1), lambda i, j: (i, 0)),
            pl.BlockSpec((8, LANES), lambda i, j: (0, 0)),
        ],
        out_specs=[
            pl.BlockSpec((tm, tn), lambda i, j: (i, j)),
            pl.BlockSpec((tm, d), lambda i, j: (i, 0)),
            pl.BlockSpec((tm, LANES), lambda i, j: (i, 0)),
        ],
        out_shape=[
            jax.ShapeDtypeStruct((t, n_qkv * tn), BF16),
            jax.ShapeDtypeStruct((t, d), BF16),
            jax.ShapeDtypeStruct((t, LANES), F32),
        ],
        scratch_shapes=[
            pltpu.VMEM((tm, LANES), F32),
            pltpu.VMEM((tm, LANES), F32),
            pltpu.VMEM((tm, LANES), F32),
        ],
        compiler_params=_cparams(("parallel", "arbitrary")),
    )(x2, g1, w_main, w_f, pos, rope_tab)
    gate = pl.pallas_call(
        _gates_kernel,
        grid=(t // tm, n_gate),
        in_specs=[
            pl.BlockSpec((tm, d), lambda i, j: (i, 0)),
            pl.BlockSpec((d, tn), lambda i, j: (0, n_qkv + j)),
            pl.BlockSpec((1, tn), lambda i, j: (0, j)),
        ],
        out_specs=pl.BlockSpec((tm, tn), lambda i, j: (i, j)),
        out_shape=jax.ShapeDtypeStruct((t, n_gate * tn), F32),
        compiler_params=_cparams(("parallel", "arbitrary")),
    )(u, w_main, b_gate)
    return qkv, gate, flog


def _fgate_kernel(fl_ref, b_ref, c_ref):
    z = fl_ref[...] + b_ref[...]
    x = jnp.minimum(z, 0.0) - jnp.log(1.0 + jnp.exp(-jnp.abs(z)))
    n = x.shape[1]
    lane = lax.broadcasted_iota(jnp.int32, x.shape, 1)
    sh = 1
    while sh < n:
        x = x + jnp.where(lane >= sh, pltpu.roll(x, sh, 1), 0.0)
        sh *= 2
    c_ref[...] = x


def _fgate(fl, b):
    r, s = fl.shape
    return pl.pallas_call(
        _fgate_kernel,
        grid=(1,),
        in_specs=[pl.BlockSpec((r, s), lambda i: (0, 0)), pl.BlockSpec((r, 1), lambda i: (0, 0))],
        out_specs=pl.BlockSpec((r, s), lambda i: (0, 0)),
        out_shape=jax.ShapeDtypeStruct((r, s), F32),
        compiler_params=_cparams(("arbitrary",)),
    )(fl, b)


def _transpose_bf16(x):
    return x.astype(F32).T.astype(BF16)


def _softmax_step(s, vt, m_sc, l_sc, acc_sc):
    m_old = m_sc[...]
    m_new = jnp.maximum(m_old, jnp.max(s, axis=0, keepdims=True))
    alpha = jnp.exp2(m_old - m_new)
    p = jnp.exp2(s - m_new)
    l_sc[...] = alpha * l_sc[...] + jnp.sum(p, axis=0, keepdims=True)
    acc_sc[...] = alpha * acc_sc[...] + jnp.dot(vt, p.astype(BF16), preferred_element_type=F32)
    m_sc[...] = m_new


def _causal_keep(shape):
    kv = lax.broadcasted_iota(jnp.int32, shape, 0)
    qq = lax.broadcasted_iota(jnp.int32, shape, 1)
    return kv <= qq


FOX_HEADS_PER_STEP = 2


def _fox_kernel(q_ref, k_ref, v_ref, c_ref, o_ref, *scratch, tq, tk, s_len):
    hp = FOX_HEADS_PER_STEP
    qt_sc, kaug_sc, vt_sc, m_sc, l_sc, acc_sc = (scratch[n * hp:(n + 1) * hp] for n in range(6))
    qi = pl.program_id(2)
    head_cols = [slice(h * HEAD_DIM, (h + 1) * HEAD_DIM) for h in range(hp)]

    @pl.when(qi == 0)
    def _():
        sub = lax.broadcasted_iota(jnp.int32, (HEAD_DIM, tk), 0)
        for h in range(hp):
            for ch in range(s_len // tk):
                rows = slice(ch * tk, (ch + 1) * tk)
                vt_sc[h][:, rows] = _transpose_bf16(v_ref[rows, head_cols[h]])
                c = c_ref[h, :, rows] * (-LOG2E)
                hi = c.astype(BF16).astype(F32)
                mid = (c - hi).astype(BF16).astype(F32)
                lo = (c - hi - mid).astype(BF16).astype(F32)
                tab = jnp.where(sub == 0, hi, jnp.where(sub == 1, mid, jnp.where(sub == 2, lo, 0.0)))
                kaug_sc[h][rows, HEAD_DIM:] = tab.T.astype(BF16)
                kaug_sc[h][rows, :HEAD_DIM] = k_ref[rows, head_cols[h]]

    sub_q = lax.broadcasted_iota(jnp.int32, (HEAD_DIM, tq), 0)
    for h in range(hp):
        qt_sc[h][:HEAD_DIM, :] = _transpose_bf16(q_ref[:, head_cols[h]])
        qt_sc[h][HEAD_DIM:, :] = jnp.where(sub_q < 3, 1.0, 0.0).astype(BF16)
        m_sc[h][...] = jnp.full(m_sc[h].shape, -jnp.inf, F32)
        l_sc[h][...] = jnp.zeros(l_sc[h].shape, F32)
        acc_sc[h][...] = jnp.zeros(acc_sc[h].shape, F32)

    def step(start, masked):
        for h in range(hp):
            s = jnp.dot(kaug_sc[h][pl.ds(start, tk), :], qt_sc[h][...], preferred_element_type=F32)
            if masked:
                s = jnp.where(_causal_keep(s.shape), s, -jnp.inf)
            _softmax_step(s, vt_sc[h][:, pl.ds(start, tk)], m_sc[h], l_sc[h], acc_sc[h])

    def body(kc, carry):
        step(pl.multiple_of(kc * tk, tk), False)
        return carry

    lax.fori_loop(0, qi, body, 0)
    step(pl.multiple_of(qi * tq, tq), True)
    for h in range(hp):
        o_ref[:, head_cols[h]] = (acc_sc[h][...] / l_sc[h][...]).T.astype(o_ref.dtype)


def _fox_attn(qkv, c3, b, s, tq):
    t = b * s
    nq = s // tq
    hp = FOX_HEADS_PER_STEP
    groups = FOX_HEADS // hp
    width = hp * HEAD_DIM
    kern = functools.partial(_fox_kernel, tq=tq, tk=tq, s_len=s)

    def per_head(shape, dtype):
        return [pltpu.VMEM(shape, dtype) for _ in range(hp)]

    return pl.pallas_call(
        kern,
        grid=(b, groups, nq),
        in_specs=[
            pl.BlockSpec((tq, width), lambda bi, g, qi: (bi * nq + qi, g)),
            pl.BlockSpec((s, width), lambda bi, g, qi: (bi, groups + g)),
            pl.BlockSpec((s, width), lambda bi, g, qi: (bi, 2 * groups + g)),
            pl.BlockSpec((hp, 1, s), lambda bi, g, qi: (bi * groups + g, 0, 0)),
        ],
        out_specs=pl.BlockSpec((tq, width), lambda bi, g, qi: (bi * nq + qi, g)),
        out_shape=jax.ShapeDtypeStruct((t, FOX_HEADS * HEAD_DIM), BF16),
        scratch_shapes=(per_head((2 * HEAD_DIM, tq), BF16) + per_head((s, 2 * HEAD_DIM), BF16)
                        + per_head((HEAD_DIM, s), BF16) + per_head((1, tq), F32)
                        + per_head((1, tq), F32) + per_head((HEAD_DIM, tq), F32)),
        compiler_params=_cparams(("parallel", "parallel", "arbitrary")),
    )(qkv, qkv, qkv, c3)


def _diff_kernel(q1_ref, q2_ref, k1_ref, k2_ref, v_ref, lam_ref, g_ref, o_ref,
                 q1t_sc, q2t_sc, vt_sc, m1_sc, l1_sc, a1_sc, m2_sc, l2_sc, a2_sc,
                 *, tq, tk, s_len, lam_init):
    qi = pl.program_id(2)

    @pl.when(qi == 0)
    def _():
        for ch in range(s_len // tk):
            rows = slice(ch * tk, (ch + 1) * tk)
            vt_sc[:, rows] = _transpose_bf16(v_ref[rows, :])

    q1t_sc[...] = _transpose_bf16(q1_ref[...])
    q2t_sc[...] = _transpose_bf16(q2_ref[...])
    for r in (m1_sc, m2_sc):
        r[...] = jnp.full(r.shape, -jnp.inf, F32)
    for r in (l1_sc, l2_sc, a1_sc, a2_sc):
        r[...] = jnp.zeros(r.shape, F32)

    def step(start, masked):
        vt = vt_sc[:, pl.ds(start, tk)]
        s1 = jnp.dot(k1_ref[pl.ds(start, tk), :], q1t_sc[...], preferred_element_type=F32)
        s2 = jnp.dot(k2_ref[pl.ds(start, tk), :], q2t_sc[...], preferred_element_type=F32)
        if masked:
            keep = _causal_keep(s1.shape)
            s1 = jnp.where(keep, s1, -jnp.inf)
            s2 = jnp.where(keep, s2, -jnp.inf)
        _softmax_step(s1, vt, m1_sc, l1_sc, a1_sc)
        _softmax_step(s2, vt, m2_sc, l2_sc, a2_sc)

    def body(kc, carry):
        step(pl.multiple_of(kc * tk, tk), False)
        return carry

    lax.fori_loop(0, qi, body, 0)
    step(pl.multiple_of(qi * tq, tq), True)

    lp = lam_ref[...]
    lam = (jnp.exp(jnp.sum(lp[0:1, :] * lp[1:2, :], axis=-1, keepdims=True))
           - jnp.exp(jnp.sum(lp[2:3, :] * lp[3:4, :], axis=-1, keepdims=True)) + lam_init)
    y = (a1_sc[...] / l1_sc[...]).T - lam * (a2_sc[...] / l2_sc[...]).T
    ms = jnp.mean(y * y, axis=-1, keepdims=True)
    y = (y * lax.rsqrt(ms + EPS)) * g_ref[...]
    o_ref[...] = (y * (1.0 - lam_init)).astype(o_ref.dtype)


def _diff_attn(qkv, lam_p, subln_g, b, s, tq, lam_init):
    t = b * s
    nq = s // tq
    qb = 3 * FOX_HEADS
    kb = qb + 2 * DIFF_HEADS
    vb = (kb + 2 * DIFF_HEADS) // 2
    kern = functools.partial(_diff_kernel, tq=tq, tk=tq, s_len=s, lam_init=lam_init)
    return pl.pallas_call(
        kern,
        grid=(b, DIFF_HEADS, nq),
        in_specs=[
            pl.BlockSpec((tq, HEAD_DIM), lambda bi, h, qi: (bi * nq + qi, qb + 2 * h)),
            pl.BlockSpec((tq, HEAD_DIM), lambda bi, h, qi: (bi * nq + qi, qb + 2 * h + 1)),
            pl.BlockSpec((s, HEAD_DIM), lambda bi, h, qi: (bi, kb + 2 * h)),
            pl.BlockSpec((s, HEAD_DIM), lambda bi, h, qi: (bi, kb + 2 * h + 1)),
            pl.BlockSpec((s, DIFF_V_DIM), lambda bi, h, qi: (bi, vb + h)),
            pl.BlockSpec((4, HEAD_DIM), lambda bi, h, qi: (0, 0)),
            pl.BlockSpec((1, DIFF_V_DIM), lambda bi, h, qi: (0, 0)),
        ],
        out_specs=pl.BlockSpec((tq, DIFF_V_DIM), lambda bi, h, qi: (bi * nq + qi, h)),
        out_shape=jax.ShapeDtypeStruct((t, DIFF_HEADS * DIFF_V_DIM), BF16),
        scratch_shapes=[
            pltpu.VMEM((HEAD_DIM, tq), BF16), pltpu.VMEM((HEAD_DIM, tq), BF16),
            pltpu.VMEM((DIFF_V_DIM, s), BF16),
            pltpu.VMEM((1, tq), F32), pltpu.VMEM((1, tq), F32), pltpu.VMEM((DIFF_V_DIM, tq), F32),
            pltpu.VMEM((1, tq), F32), pltpu.VMEM((1, tq), F32), pltpu.VMEM((DIFF_V_DIM, tq), F32),
        ],
        compiler_params=_cparams(("parallel", "parallel", "arbitrary")),
    )(qkv, qkv, qkv, qkv, qkv, lam_p, subln_g)


def _pack_bf16_pairs(xb):
    w = xb.shape[1] // 2
    lo = lax.bitcast_convert_type(xb[:, :w].astype(F32), jnp.uint32)
    hi = lax.bitcast_convert_type(xb[:, w:].astype(F32), jnp.uint32)
    return (lo >> 16) | (hi & jnp.uint32(0xFFFF0000))


def _unpack_bf16_pairs(p):
    lo = lax.bitcast_convert_type(p << 16, F32).astype(BF16)
    hi = lax.bitcast_convert_type(p & jnp.uint32(0xFFFF0000), F32).astype(BF16)
    return lo, hi


def _merge_kernel(yf_ref, yd_ref, g0_ref, g1_ref, x_ref, wbf_ref, wbd_ref, wo_ref, n2_ref,
                  rw_ref, rb_ref, h_ref, u_ref, idx_ref, gw_ref):
    ya = jnp.dot(yf_ref[...], wbf_ref[...], preferred_element_type=F32)
    yb = jnp.dot(yd_ref[...], wbd_ref[...], preferred_element_type=F32)
    merged = g0_ref[...] * ya + g1_ref[...] * yb
    h = x_ref[...] + jnp.dot(merged.astype(BF16), wo_ref[...], preferred_element_type=F32)
    h_ref[...] = h
    ms = jnp.mean(h * h, axis=-1, keepdims=True)
    u = (h * lax.rsqrt(ms + EPS)) * n2_ref[...]
    ub = u.astype(BF16)
    u_ref[...] = _pack_bf16_pairs(ub)
    logits = jnp.dot(ub, rw_ref[...], preferred_element_type=F32) + rb_ref[...]

    lane = lax.broadcasted_iota(jnp.int32, logits.shape, 1).astype(F32)
    work = logits
    vals, idxs = [], []
    for _ in range(TOP_K):
        m = jnp.max(work, axis=-1, keepdims=True)
        sel = jnp.min(jnp.where(work == m, lane, float(LANES)), axis=-1, keepdims=True)
        vals.append(m)
        idxs.append(sel)
        work = jnp.where(lane == sel, -jnp.inf, work)
    es = [jnp.exp(v - vals[0]) for v in vals]
    den = es[0]
    for e in es[1:]:
        den = den + e
    idx_out = jnp.zeros(logits.shape, F32)
    gw_out = jnp.zeros(logits.shape, F32)
    for k in range(TOP_K):
        idx_out = jnp.where(lane == float(k), idxs[k], idx_out)
        gw_out = jnp.where(lane == float(k), es[k] / den, gw_out)
    idx_ref[...] = idx_out.astype(jnp.int32)
    gw_ref[...] = gw_out


def _merge(yf, yd, gate, x2, wbf, wbd, wo, n2, rw, rb, tm):
    t, d = x2.shape
    fw = yf.shape[1]
    dw = yd.shape[1]
    const = dict(pipeline_mode=pl.Buffered(1))
    return pl.pallas_call(
        _merge_kernel,
        grid=(t // tm,),
        in_specs=[
            pl.BlockSpec((tm, fw), lambda i: (i, 0)),
            pl.BlockSpec((tm, dw), lambda i: (i, 0)),
            pl.BlockSpec((tm, d), lambda i: (i, 0)),
            pl.BlockSpec((tm, d), lambda i: (i, 1)),
            pl.BlockSpec((tm, d), lambda i: (i, 0)),
            pl.BlockSpec((fw, d), lambda i: (0, 0), **const),
            pl.BlockSpec((dw, d), lambda i: (0, 0), **const),
            pl.BlockSpec((d, d), lambda i: (0, 0), **const),
            pl.BlockSpec((1, d), lambda i: (0, 0)),
            pl.BlockSpec((d, LANES), lambda i: (0, 0), **const),
            pl.BlockSpec((1, LANES), lambda i: (0, 0)),
        ],
        out_specs=[
            pl.BlockSpec((tm, d), lambda i: (i, 0)),
            pl.BlockSpec((tm, d // 2), lambda i: (i, 0)),
            pl.BlockSpec((tm, LANES), lambda i: (i, 0)),
            pl.BlockSpec((tm, LANES), lambda i: (i, 0)),
        ],
        out_shape=[
            jax.ShapeDtypeStruct((t, d), F32),
            jax.ShapeDtypeStruct((t, d // 2), jnp.uint32),
            jax.ShapeDtypeStruct((t, LANES), jnp.int32),
            jax.ShapeDtypeStruct((t, LANES), F32),
        ],
        compiler_params=_cparams(("parallel",)),
    )(yf, yd, gate, gate, x2, wbf, wbd, wo, n2, rw, rb)


def _dispatch_kernel(cnt_ref, pstart_ref, pend_ref, roa_ref, u_ref, xs_hbm, stage, zeros, sems,
                     *, tb, sub, n_rows):
    s = pl.program_id(0)
    n = pl.num_programs(0)
    slot = s % 2
    n_dma = TOP_K * tb

    def wait_slot(sl):
        pltpu.make_async_copy(stage.at[sl], xs_hbm.at[pl.ds(0, n_dma), :], sems.at[sl]).wait()

    @pl.when(s == 0)
    def _():
        zeros[...] = jnp.zeros(zeros.shape, zeros.dtype)

    @pl.when(s >= 2)
    def _():
        wait_slot(slot)

    blk = u_ref[...]
    for k in range(TOP_K):
        stage[slot, pl.ds(k * tb, tb), :] = blk

    def body(g, carry):
        for q8 in range(ISSUE_UNROLL):
            q = g * ISSUE_UNROLL + q8
            pltpu.make_async_copy(stage.at[slot, pl.ds(q, 1), :],
                                  xs_hbm.at[pl.ds(roa_ref[0, 0, q], 1), :], sems.at[slot]).start()
        return carry
    lax.fori_loop(0, n_dma // ISSUE_UNROLL, body, 0)

    @pl.when(s == n - 1)
    def _():
        @pl.when(n >= 2)
        def _():
            wait_slot(1 - slot)
        wait_slot(slot)

        def zero_row(r):
            return pltpu.make_async_copy(zeros.at[pl.ds(0, 1), :], xs_hbm.at[pl.ds(r, 1), :], sems.at[2])

        def zero_block(r):
            return pltpu.make_async_copy(zeros, xs_hbm.at[pl.ds(pl.multiple_of(r, sub), sub), :], sems.at[2])

        def fill(first, stop, copy, step):
            cnt = (stop - first) // step

            def start(q, carry):
                copy(first + q * step).start()
                return carry

            def wait(q, carry):
                copy(first).wait()
                return carry
            lax.fori_loop(0, cnt, start, 0)
            lax.fori_loop(0, cnt, wait, 0)

        def per_expert(e, carry):
            first = pstart_ref[e] + cnt_ref[e]
            edge = (first + sub - 1) // sub * sub
            fill(first, edge, zero_row, 1)
            fill(edge, pend_ref[e], zero_block, sub)
            return carry
        lax.fori_loop(0, N_EXPERTS, per_expert, 0)
        fill(pend_ref[N_EXPERTS - 1], n_rows, zero_block, sub)


def _dispatch(counts, pad_start, pad_end, roa3, u_packed, n_rows, tb, sub):
    t, w = u_packed.shape
    kern = functools.partial(_dispatch_kernel, tb=tb, sub=sub, n_rows=n_rows)
    return pl.pallas_call(
        kern,
        grid_spec=pltpu.PrefetchScalarGridSpec(
            num_scalar_prefetch=3,
            grid=(t // tb,),
            in_specs=[
                pl.BlockSpec((1, 1, TOP_K * tb), lambda s, c, ps, pe: (s, 0, 0), memory_space=pltpu.SMEM),
                pl.BlockSpec((tb, w), lambda s, c, ps, pe: (s, 0)),
            ],
            out_specs=pl.BlockSpec(memory_space=pl.ANY),
            scratch_shapes=[
                pltpu.VMEM((2, TOP_K * tb, w), jnp.uint32),
                pltpu.VMEM((sub, w), jnp.uint32),
                pltpu.SemaphoreType.DMA((3,)),
            ],
        ),
        out_shape=jax.ShapeDtypeStruct((n_rows, w), jnp.uint32),
        compiler_params=_cparams(("arbitrary",)),
    )(counts, pad_start, pad_end, roa3, u_packed)


def _expert_kernel(te_ref, ts_ref, nsub_ref, xs_ref, wg_ref, wu_ref, bg_ref, bu_ref, wd_ref,
                   bd_ref, out_ref, x_sc, act_sc, *, sub, spt, nf):
    i = pl.program_id(0)
    j = pl.program_id(1)
    nsub = nsub_ref[i]
    half = x_sc.shape[1] // 2

    @pl.when(jnp.logical_and(j == 0, nsub > 0))
    def _():
        lo, hi = _unpack_bf16_pairs(xs_ref[...])
        x_sc[:, :half] = lo
        x_sc[:, half:] = hi

    full = nsub == spt

    def gate_up(rs, wg, wu):
        x = x_sc[rs, :]
        hg = jnp.dot(x, wg, preferred_element_type=F32) + bg_ref[0]
        hu = jnp.dot(x, wu, preferred_element_type=F32) + bu_ref[0]
        hg = jnp.minimum(hg, SWIGLU_LIMIT)
        hl = jnp.clip(hu, -SWIGLU_LIMIT, SWIGLU_LIMIT)
        act = hg * jax.nn.sigmoid(SWIGLU_ALPHA * hg) * (hl + 1.0)
        act_sc[j, rs, :] = act.astype(BF16)

    def down(rs, wd):
        act = jnp.concatenate([act_sc[c, rs, :] for c in range(nf)], axis=1)
        out_ref[rs, :] = jnp.dot(act, wd, preferred_element_type=F32) + bd_ref[0]

    partial_tile = jnp.logical_and(nsub > 0, nsub < spt)

    @pl.when(jnp.logical_and(j < nf, full))
    def _():
        gate_up(slice(None), wg_ref[0].astype(BF16), wu_ref[0].astype(BF16))

    @pl.when(jnp.logical_and(j < nf, partial_tile))
    def _():
        wg = wg_ref[0].astype(BF16)
        wu = wu_ref[0].astype(BF16)
        for s in range(spt - 1):
            pl.when(s < nsub)(functools.partial(gate_up, slice(s * sub, (s + 1) * sub), wg, wu))

    @pl.when(jnp.logical_and(j >= nf, full))
    def _():
        down(slice(None), wd_ref[0].astype(BF16))

    @pl.when(jnp.logical_and(j >= nf, partial_tile))
    def _():
        wd = wd_ref[0].astype(BF16)
        for s in range(spt - 1):
            pl.when(s < nsub)(functools.partial(down, slice(s * sub, (s + 1) * sub), wd))

    @pl.when(jnp.logical_and(j >= nf, nsub < spt))
    def _():
        for s in range(spt):
            @pl.when(s >= nsub)
            def _():
                out_ref[s * sub:(s + 1) * sub, :] = jnp.zeros((sub, out_ref.shape[1]), out_ref.dtype)


def _experts(tile_e, tile_src, tile_nsub, xs, w_gu, b_gu3, w_down, b_down3, tm_e, sub, tf):
    n_rows = xs.shape[0]
    n_tiles = n_rows // tm_e
    d = 2 * xs.shape[1]
    dff = w_down.shape[1]
    nf = dff // tf
    nn = d // tf
    assert nn == nf
    kern = functools.partial(_expert_kernel, sub=sub, spt=tm_e // sub, nf=nf)

    def c1(i, j, ns):
        return jnp.where(ns[i] > 0, jnp.minimum(j, nf - 1), nf - 1)

    def c2(i, j, ns):
        return jnp.where(ns[i] > 0, jnp.maximum(j - nf, 0), nf - 1)

    return pl.pallas_call(
        kern,
        grid_spec=pltpu.PrefetchScalarGridSpec(
            num_scalar_prefetch=3,
            grid=(n_tiles, 2 * nf),
            in_specs=[
                pl.BlockSpec((tm_e, d // 2), lambda i, j, te, ts, ns: (ts[i], 0)),
                pl.BlockSpec((1, d, tf), lambda i, j, te, ts, ns: (te[i], 0, c1(i, j, ns))),
                pl.BlockSpec((1, d, tf), lambda i, j, te, ts, ns: (te[i], 0, nf + c1(i, j, ns))),
                pl.BlockSpec((1, 1, tf), lambda i, j, te, ts, ns: (te[i], 0, c1(i, j, ns))),
                pl.BlockSpec((1, 1, tf), lambda i, j, te, ts, ns: (te[i], 0, nf + c1(i, j, ns))),
                pl.BlockSpec((1, dff, tf), lambda i, j, te, ts, ns: (te[i], 0, c2(i, j, ns))),
                pl.BlockSpec((1, 1, tf), lambda i, j, te, ts, ns: (te[i], 0, c2(i, j, ns))),
            ],
            out_specs=pl.BlockSpec((tm_e, tf), lambda i, j, te, ts, ns: (i, jnp.maximum(j - nf, 0))),
            scratch_shapes=[
                pltpu.VMEM((tm_e, d), BF16),
                pltpu.VMEM((nf, tm_e, tf), BF16),
            ],
        ),
        out_shape=jax.ShapeDtypeStruct((n_rows, d), F32),
        compiler_params=_cparams(("arbitrary", "arbitrary")),
    )(tile_e, tile_src, tile_nsub, xs, w_gu, w_gu, b_gu3, b_gu3, w_down, b_down3)


def _combine_kernel(cur_ref, nxt_ref, yr_hbm, h_ref, gw_ref, gf_ref, out_ref, ysel, sems, *, tmc,
                    final_norm):
    i = pl.program_id(0)
    n = pl.num_programs(0)
    slot = i % 2
    n_dma = TOP_K * tmc

    def row_copy(row, q, sl):
        return pltpu.make_async_copy(yr_hbm.at[pl.ds(row, 1), :], ysel.at[sl, pl.ds(q, 1), :], sems.at[sl])

    def issue(idx_ref, sl):
        def body(g, carry):
            for q8 in range(ISSUE_UNROLL):
                q = g * ISSUE_UNROLL + q8
                row_copy(idx_ref[0, 0, q], q, sl).start()
            return carry
        lax.fori_loop(0, n_dma // ISSUE_UNROLL, body, 0)

    @pl.when(i == 0)
    def _():
        issue(cur_ref, 0)

    @pl.when(i + 1 < n)
    def _():
        issue(nxt_ref, 1 - slot)

    pltpu.make_async_copy(yr_hbm.at[pl.ds(0, n_dma), :], ysel.at[slot], sems.at[slot]).wait()

    gw = gw_ref[...]
    y = h_ref[...]
    for k in range(TOP_K):
        y = y + gw[:, k:k + 1] * ysel[slot, k * tmc:(k + 1) * tmc, :]
    if final_norm:
        ms = jnp.mean(y * y, axis=-1, keepdims=True)
        y = (y * lax.rsqrt(ms + EPS)) * gf_ref[...]
    out_ref[...] = y


def _combine(roa3, y_rows, h, gw, gf, tmc, final_norm):
    t, d = h.shape
    n = t // tmc
    kern = functools.partial(_combine_kernel, tmc=tmc, final_norm=final_norm)
    return pl.pallas_call(
        kern,
        grid=(n,),
        in_specs=[
            pl.BlockSpec((1, 1, TOP_K * tmc), lambda i: (i, 0, 0), memory_space=pltpu.SMEM),
            pl.BlockSpec((1, 1, TOP_K * tmc), lambda i: (jnp.minimum(i + 1, n - 1), 0, 0),
                         memory_space=pltpu.SMEM),
            pl.BlockSpec(memory_space=pl.ANY),
            pl.BlockSpec((tmc, d), lambda i: (i, 0)),
            pl.BlockSpec((tmc, LANES), lambda i: (i, 0)),
            pl.BlockSpec((1, d), lambda i: (0, 0)),
        ],
        out_specs=pl.BlockSpec((tmc, d), lambda i: (i, 0)),
        out_shape=jax.ShapeDtypeStruct((t, d), F32),
        scratch_shapes=[pltpu.VMEM((2, TOP_K * tmc, d), F32), pltpu.SemaphoreType.DMA((2,))],
        compiler_params=_cparams(("arbitrary",)),
    )(roa3, roa3, y_rows, h, gw, gf)


def _routing(top_idx, tm_e, sub):
    t = top_idx.shape[0]
    a = t * TOP_K
    flat_e = top_idx.reshape(a)
    onehot = (flat_e[:, None] == jnp.arange(N_EXPERTS, dtype=jnp.int32)[None, :]).astype(jnp.int32)
    csum = jnp.cumsum(onehot, axis=0)
    counts = csum[-1]
    rank = jnp.sum(onehot * csum, axis=1) - 1
    padded = (counts + tm_e - 1) // tm_e * tm_e
    pad_end = jnp.cumsum(padded)
    pad_start = pad_end - padded
    row_of_assign = (pad_start[flat_e] + rank).astype(jnp.int32)
    n_tiles = -(-a // tm_e) + N_EXPERTS
    tile_start = jnp.arange(n_tiles, dtype=jnp.int32) * tm_e
    n_used = pad_end[-1] // tm_e
    tile_e_raw = jnp.minimum(jnp.searchsorted(pad_end, tile_start, side='right'), N_EXPERTS - 1)
    used = tile_start < pad_end[-1]
    last = jnp.maximum(n_used - 1, 0)
    tile_src = jnp.where(used, jnp.arange(n_tiles, dtype=jnp.int32), last).astype(jnp.int32)
    tile_e = tile_e_raw[tile_src].astype(jnp.int32)
    valid_rows = jnp.clip(counts[tile_e_raw] - (tile_start - pad_start[tile_e_raw]), 0, tm_e)
    tile_nsub = jnp.where(used, (valid_rows + sub - 1) // sub, 0).astype(jnp.int32)
    bounds = (counts.astype(jnp.int32), pad_start.astype(jnp.int32), pad_end.astype(jnp.int32))
    return bounds, row_of_assign, tile_e, tile_src, tile_nsub, n_tiles * tm_e


def kernel(x, positions, norm1_g, w_in, b_fgate, b_gate, lam_q1, lam_k1, lam_q2, lam_k2, subln_g,
           w_branch_fox, w_branch_diff, w_out, norm2_g, router_w, router_b, w_gu, b_gu, w_down,
           b_down, normf_g):
    b, s, d = x.shape
    t = b * s
    depth = norm1_g.shape[0]
    fox_w = FOX_HEADS * HEAD_DIM
    tm_a = _pick(t, 1024)
    tq = _pick(s, 512)
    tm_d = _pick(t, 256)
    tm_e, sub, tf = 1024, 256, 512
    tmc = _pick(t, 128)

    half = ROPE_DIM // 2
    inv_freq = ROPE_THETA ** (-jnp.arange(0, ROPE_DIM, 2, dtype=F32) / ROPE_DIM)
    zeros = jnp.zeros((LANES - ROPE_DIM,), F32)
    rope_tab = jnp.zeros((8, LANES), F32)
    rope_tab = rope_tab.at[0].set(jnp.concatenate([inv_freq, inv_freq, zeros]))
    rope_tab = rope_tab.at[1].set(jnp.concatenate([-jnp.ones((half,), F32), jnp.zeros((half,), F32), zeros]))
    rope_tab = rope_tab.at[2].set(jnp.concatenate([jnp.zeros((half,), F32), jnp.ones((half,), F32), zeros]))
    pos = positions.astype(F32).reshape(t, 1)

    h = x.reshape(t, d)
    for l in range(depth):
        wi = w_in[l]
        w_main = jnp.concatenate([wi[:, :3 * fox_w], wi[:, 3 * fox_w + FOX_HEADS:]], axis=1).astype(BF16)
        w_f = jnp.pad(wi[:, 3 * fox_w:3 * fox_w + FOX_HEADS], ((0, 0), (0, LANES - FOX_HEADS))).astype(BF16)
        qkv, gate, flog = _inproj(h, pos, norm1_g[l][None, :], w_main, w_f, rope_tab,
                                  b_gate[l][None, :], tm_a)

        fl = flog[:, :FOX_HEADS].reshape(b, s, FOX_HEADS).transpose(0, 2, 1).reshape(b * FOX_HEADS, s)
        bf = jnp.tile(b_fgate[l].astype(F32), b).reshape(b * FOX_HEADS, 1)
        c3 = _fgate(fl, bf).reshape(b * FOX_HEADS, 1, s)

        y_fox = _fox_attn(qkv, c3, b, s, tq)
        lam_init = 0.8 - 0.6 * math.exp(-0.3 * l)
        lam_p = jnp.stack([lam_q1[l], lam_k1[l], lam_q2[l], lam_k2[l]]).astype(F32)
        y_diff = _diff_attn(qkv, lam_p, subln_g[l][None, :].astype(F32), b, s, tq, lam_init)

        rw = jnp.pad(router_w[l], ((0, 0), (0, LANES - N_EXPERTS))).astype(BF16)
        rb = jnp.concatenate([router_b[l].astype(F32), jnp.full((LANES - N_EXPERTS,), NEG_BIG, F32)])[None, :]
        h, u2, idx128, gw128 = _merge(y_fox, y_diff, gate, h, w_branch_fox[l].astype(BF16),
                                      w_branch_diff[l].astype(BF16), w_out[l].astype(BF16),
                                      norm2_g[l][None, :], rw, rb, tm_d)

        bounds, roa, tile_e, tile_src, tile_nsub, n_rows = _routing(idx128[:, :TOP_K], tm_e, sub)
        roa3 = roa.reshape(t // tmc, tmc, TOP_K).transpose(0, 2, 1).reshape(t // tmc, 1, TOP_K * tmc)
        xs = _dispatch(*bounds, roa3, u2, n_rows, tmc, sub)
        y_rows = _experts(tile_e, tile_src, tile_nsub, xs, w_gu[l], b_gu[l][:, None, :],
                          w_down[l], b_down[l][:, None, :], tm_e, sub, tf)
        h = _combine(roa3, y_rows, h, gw128, normf_g[None, :], tmc, final_norm=(l == depth - 1))
    return h.reshape(b, s, d)
```

```python
import functools
import math

import jax
import jax.numpy as jnp
from jax import lax
from jax.experimental import pallas as pl
from jax.experimental.pallas import tpu as pltpu

HEAD_DIM = 128
FOX_HEADS = 8
DIFF_HEADS = 4
DIFF_V_DIM = 2 * HEAD_DIM
ROPE_THETA = 500000.0
ROPE_DIM = HEAD_DIM // 4
N_EXPERTS = 32
TOP_K = 4
SWIGLU_ALPHA = 1.702
SWIGLU_LIMIT = 7.0
EPS = 1e-5
LANES = 128

F32 = jnp.float32
BF16 = jnp.bfloat16
NEG_BIG = -1e30
LOG2E = math.log2(math.e)
ISSUE_UNROLL = 8
VMEM_LIMIT = 56 * 1024 * 1024


def _cparams(sem, vmem=VMEM_LIMIT):
    return pltpu.CompilerParams(dimension_semantics=sem, vmem_limit_bytes=vmem)


def _pick(n, pref):
    t = min(n, pref)
    while n % t:
        t //= 2
    return t


QKV_TILE = 1024
QKV_ROTARY_TILES = (3, 4)
QKV_SCALED_TILES = (0, 3)


def _qkv_kernel(x_ref, g1_ref, w_ref, wf_ref, pos_ref, tab_ref,
                qkv_ref, u_ref, flog_ref, cos_sc, sa_sc, sb_sc, *, scale):
    j = pl.program_id(1)

    @pl.when(j == 0)
    def _():
        xf = x_ref[...]
        ms = jnp.mean(xf * xf, axis=-1, keepdims=True)
        u = (xf * lax.rsqrt(ms + EPS)) * g1_ref[...]
        ub = u.astype(BF16)
        u_ref[...] = ub
        flog_ref[...] = jnp.dot(ub, wf_ref[...], preferred_element_type=F32)
        ang = pos_ref[...] * tab_ref[0:1, :]
        sn = jnp.sin(ang)
        cos_sc[...] = jnp.cos(ang)
        sa_sc[...] = sn * tab_ref[1:2, :]
        sb_sc[...] = sn * tab_ref[2:3, :]

    acc = jnp.dot(u_ref[...], w_ref[...], preferred_element_type=F32)

    rot = functools.reduce(jnp.logical_or, [j == q for q in QKV_ROTARY_TILES])
    scaled = functools.reduce(jnp.logical_or, [j == q for q in QKV_SCALED_TILES])
    mult = jnp.where(scaled, scale, 1.0).astype(F32)
    rot_f = jnp.where(rot, mult, 0.0).astype(F32)
    ca = cos_sc[...] * rot_f + (mult - rot_f)
    cb = sa_sc[...] * rot_f
    cc = sb_sc[...] * rot_f
    for c in range(acc.shape[1] // HEAD_DIM):
        t = acc[:, c * HEAD_DIM:(c + 1) * HEAD_DIM]
        r = (t * ca + pltpu.roll(t, HEAD_DIM - ROPE_DIM // 2, 1) * cb
             + pltpu.roll(t, ROPE_DIM // 2, 1) * cc)
        qkv_ref[:, c * HEAD_DIM:(c + 1) * HEAD_DIM] = r.astype(BF16)


def _gates_kernel(u_ref, w_ref, bg_ref, gate_ref):
    acc = jnp.dot(u_ref[...], w_ref[...], preferred_element_type=F32)
    gate_ref[...] = jax.nn.sigmoid(acc + bg_ref[...])


def _inproj(x2, pos, g1, w_main, w_f, rope_tab, b_gate, tm):
    t, d = x2.shape
    tn = QKV_TILE
    n_qkv = 6
    n_gate = (w_main.shape[1] - n_qkv * tn) // tn
    kern = functools.partial(_qkv_kernel, scale=LOG2E / math.sqrt(HEAD_DIM))
    qkv, u, flog = pl.pallas_call(
        kern,
        grid=(t // tm, n_qkv),
        in_specs=[
            pl.BlockSpec((tm, d), lambda i, j: (i, 0)),
            pl.BlockSpec((1, d), lambda i, j: (0, 0)),
            pl.BlockSpec((d, tn), lambda i, j: (0, j)),
            pl.BlockSpec((d, LANES), lambda i, j: (0, 0)),
            pl.BlockSpec((tm, 1), lambda i, j: (i, 0)),
            pl.BlockSpec((8, LANES), lambda i, j: (0, 0)),
        ],
        out_specs=[
            pl.BlockSpec((tm, tn), lambda i, j: (i, j)),
            pl.BlockSpec((tm, d), lambda i, j: (i, 0)),
            pl.BlockSpec((tm, LANES), lambda i, j: (i, 0)),
        ],
        out_shape=[
            jax.ShapeDtypeStruct((t, n_qkv * tn), BF16),
            jax.ShapeDtypeStruct((t, d), BF16),
            jax.ShapeDtypeStruct((t, LANES), F32),
        ],
        scratch_shapes=[
            pltpu.VMEM((tm, LANES), F32),
            pltpu.VMEM((tm, LANES), F32),
            pltpu.VMEM((tm, LANES), F32),
        ],
        compiler_params=_cparams(("parallel", "arbitrary")),
    )(x2, g1, w_main, w_f, pos, rope_tab)
    gate = pl.pallas_call(
        _gates_kernel,
        grid=(t // tm, n_gate),
        in_specs=[
            pl.BlockSpec((tm, d), lambda i, j: (i, 0)),
            pl.BlockSpec((d, tn), lambda i, j: (0, n_qkv + j)),
            pl.BlockSpec((1, tn), lambda i, j: (0, j)),
        ],
        out_specs=pl.BlockSpec((tm, tn), lambda i, j: (i, j)),
        out_shape=jax.ShapeDtypeStruct((t, n_gate * tn), F32),
        compiler_params=_cparams(("parallel", "arbitrary")),
    )(u, w_main, b_gate)
    return qkv, gate, flog


def _fgate_kernel(fl_ref, b_ref, c_ref):
    z = fl_ref[...] + b_ref[...]
    x = jnp.minimum(z, 0.0) - jnp.log(1.0 + jnp.exp(-jnp.abs(z)))
    n = x.shape[1]
    lane = lax.broadcasted_iota(jnp.int32, x.shape, 1)
    sh = 1
    while sh < n:
        x = x + jnp.where(lane >= sh, pltpu.roll(x, sh, 1), 0.0)
        sh *= 2
    c_ref[...] = x


def _fgate(fl, b):
    r, s = fl.shape
    return pl.pallas_call(
        _fgate_kernel,
        grid=(1,),
        in_specs=[pl.BlockSpec((r, s), lambda i: (0, 0)), pl.BlockSpec((r, 1), lambda i: (0, 0))],
        out_specs=pl.BlockSpec((r, s), lambda i: (0, 0)),
        out_shape=jax.ShapeDtypeStruct((r, s), F32),
        compiler_params=_cparams(("arbitrary",)),
    )(fl, b)


def _transpose_bf16(x):
    return x.astype(F32).T.astype(BF16)


def _softmax_step(s, vt, m_sc, l_sc, acc_sc):
    m_old = m_sc[...]
    m_new = jnp.maximum(m_old, jnp.max(s, axis=0, keepdims=True))
    alpha = jnp.exp2(m_old - m_new)
    p = jnp.exp2(s - m_new)
    l_sc[...] = alpha * l_sc[...] + jnp.sum(p, axis=0, keepdims=True)
    acc_sc[...] = alpha * acc_sc[...] + jnp.dot(vt, p.astype(BF16), preferred_element_type=F32)
    m_sc[...] = m_new


def _causal_keep(shape):
    kv = lax.broadcasted_iota(jnp.int32, shape, 0)
    qq = lax.broadcasted_iota(jnp.int32, shape, 1)
    return kv <= qq


FOX_HEADS_PER_STEP = 2


def _fox_kernel(q_ref, k_ref, v_ref, c_ref, o_ref, *scratch, tq, tk, s_len):
    hp = FOX_HEADS_PER_STEP
    qt_sc, kaug_sc, vt_sc, m_sc, l_sc, acc_sc = (scratch[n * hp:(n + 1) * hp] for n in range(6))
    s_sc = scratch[6 * hp:]
    qi = pl.program_id(2)
    head_cols = [slice(h * HEAD_DIM, (h + 1) * HEAD_DIM) for h in range(hp)]

    @pl.when(qi == 0)
    def _():
        sub = lax.broadcasted_iota(jnp.int32, (HEAD_DIM, tk), 0)
        for h in range(hp):
            for ch in range(s_len // tk):
                rows = slice(ch * tk, (ch + 1) * tk)
                vt_sc[h][:, rows] = _transpose_bf16(v_ref[rows, head_cols[h]])
                c = c_ref[h, :, rows] * (-LOG2E)
                hi = c.astype(BF16).astype(F32)
                mid = (c - hi).astype(BF16).astype(F32)
                lo = (c - hi - mid).astype(BF16).astype(F32)
                tab = jnp.where(sub == 0, hi, jnp.where(sub == 1, mid, jnp.where(sub == 2, lo, 0.0)))
                kaug_sc[h][rows, HEAD_DIM:] = tab.T.astype(BF16)
                kaug_sc[h][rows, :HEAD_DIM] = k_ref[rows, head_cols[h]]

    sub_q = lax.broadcasted_iota(jnp.int32, (HEAD_DIM, tq), 0)
    for h in range(hp):
        qt_sc[h][:HEAD_DIM, :] = _transpose_bf16(q_ref[:, head_cols[h]])
        qt_sc[h][HEAD_DIM:, :] = jnp.where(sub_q < 3, 1.0, 0.0).astype(BF16)
        m_sc[h][...] = jnp.full(m_sc[h].shape, -jnp.inf, F32)
        l_sc[h][...] = jnp.zeros(l_sc[h].shape, F32)
        acc_sc[h][...] = jnp.zeros(acc_sc[h].shape, F32)

    def scores(h, chunk):
        start = pl.multiple_of(chunk * tk, tk)
        return jnp.dot(kaug_sc[h][pl.ds(start, tk), :], qt_sc[h][...], preferred_element_type=F32)

    def consume(h, chunk, par, masked):
        s = s_sc[par * hp + h][...]
        if masked:
            s = jnp.where(_causal_keep(s.shape), s, -jnp.inf)
        start = pl.multiple_of(chunk * tk, tk)
        _softmax_step(s, vt_sc[h][:, pl.ds(start, tk)], m_sc[h], l_sc[h], acc_sc[h])

    for h in range(hp):
        s_sc[h][...] = scores(h, 0)

    def body(kc, carry):
        def run(par):
            for h in range(hp):
                s_sc[(1 - par) * hp + h][...] = scores(h, kc + 1)
                consume(h, kc, par, False)
        pl.when(kc % 2 == 0)(functools.partial(run, 0))
        pl.when(kc % 2 == 1)(functools.partial(run, 1))
        return carry

    lax.fori_loop(0, qi, body, 0)

    def diagonal(par):
        for h in range(hp):
            consume(h, qi, par, True)
    pl.when(qi % 2 == 0)(functools.partial(diagonal, 0))
    pl.when(qi % 2 == 1)(functools.partial(diagonal, 1))
    for h in range(hp):
        o_ref[:, head_cols[h]] = (acc_sc[h][...] / l_sc[h][...]).T.astype(o_ref.dtype)


def _fox_attn(qkv, c3, b, s, tq):
    t = b * s
    nq = s // tq
    hp = FOX_HEADS_PER_STEP
    groups = FOX_HEADS // hp
    width = hp * HEAD_DIM
    kern = functools.partial(_fox_kernel, tq=tq, tk=tq, s_len=s)

    def per_head(shape, dtype):
        return [pltpu.VMEM(shape, dtype) for _ in range(hp)]

    return pl.pallas_call(
        kern,
        grid=(b, groups, nq),
        in_specs=[
            pl.BlockSpec((tq, width), lambda bi, g, qi: (bi * nq + qi, g)),
            pl.BlockSpec((s, width), lambda bi, g, qi: (bi, groups + g)),
            pl.BlockSpec((s, width), lambda bi, g, qi: (bi, 2 * groups + g)),
            pl.BlockSpec((hp, 1, s), lambda bi, g, qi: (bi * groups + g, 0, 0)),
        ],
        out_specs=pl.BlockSpec((tq, width), lambda bi, g, qi: (bi * nq + qi, g)),
        out_shape=jax.ShapeDtypeStruct((t, FOX_HEADS * HEAD_DIM), BF16),
        scratch_shapes=(per_head((2 * HEAD_DIM, tq), BF16) + per_head((s, 2 * HEAD_DIM), BF16)
                        + per_head((HEAD_DIM, s), BF16) + per_head((1, tq), F32)
                        + per_head((1, tq), F32) + per_head((HEAD_DIM, tq), F32)
                        + per_head((tq, tq), F32) + per_head((tq, tq), F32)),
        compiler_params=_cparams(("parallel", "parallel", "arbitrary")),
    )(qkv, qkv, qkv, c3)


def _diff_kernel(q1_ref, q2_ref, k1_ref, k2_ref, v_ref, lam_ref, g_ref, o_ref,
                 q1t_sc, q2t_sc, vt_sc, m1_sc, l1_sc, a1_sc, m2_sc, l2_sc, a2_sc, *s_sc,
                 tq, tk, s_len, lam_init):
    qi = pl.program_id(2)

    @pl.when(qi == 0)
    def _():
        for ch in range(s_len // tk):
            rows = slice(ch * tk, (ch + 1) * tk)
            vt_sc[:, rows] = _transpose_bf16(v_ref[rows, :])

    q1t_sc[...] = _transpose_bf16(q1_ref[...])
    q2t_sc[...] = _transpose_bf16(q2_ref[...])
    for r in (m1_sc, m2_sc):
        r[...] = jnp.full(r.shape, -jnp.inf, F32)
    for r in (l1_sc, l2_sc, a1_sc, a2_sc):
        r[...] = jnp.zeros(r.shape, F32)

    chains = ((k1_ref, q1t_sc, m1_sc, l1_sc, a1_sc), (k2_ref, q2t_sc, m2_sc, l2_sc, a2_sc))

    def scores(c, chunk):
        start = pl.multiple_of(chunk * tk, tk)
        return jnp.dot(chains[c][0][pl.ds(start, tk), :], chains[c][1][...], preferred_element_type=F32)

    def consume(c, chunk, par, masked):
        s = s_sc[par * 2 + c][...]
        if masked:
            s = jnp.where(_causal_keep(s.shape), s, -jnp.inf)
        start = pl.multiple_of(chunk * tk, tk)
        _softmax_step(s, vt_sc[:, pl.ds(start, tk)], *chains[c][2:])

    for c in range(2):
        s_sc[c][...] = scores(c, 0)

    def body(kc, carry):
        def run(par):
            for c in range(2):
                s_sc[(1 - par) * 2 + c][...] = scores(c, kc + 1)
                consume(c, kc, par, False)
        pl.when(kc % 2 == 0)(functools.partial(run, 0))
        pl.when(kc % 2 == 1)(functools.partial(run, 1))
        return carry

    lax.fori_loop(0, qi, body, 0)

    def diagonal(par):
        for c in range(2):
            consume(c, qi, par, True)
    pl.when(qi % 2 == 0)(functools.partial(diagonal, 0))
    pl.when(qi % 2 == 1)(functools.partial(diagonal, 1))

    lp = lam_ref[...]
    lam = (jnp.exp(jnp.sum(lp[0:1, :] * lp[1:2, :], axis=-1, keepdims=True))
           - jnp.exp(jnp.sum(lp[2:3, :] * lp[3:4, :], axis=-1, keepdims=True)) + lam_init)
    y = (a1_sc[...] / l1_sc[...]).T - lam * (a2_sc[...] / l2_sc[...]).T
    ms = jnp.mean(y * y, axis=-1, keepdims=True)
    y = (y * lax.rsqrt(ms + EPS)) * g_ref[...]
    o_ref[...] = (y * (1.0 - lam_init)).astype(o_ref.dtype)


def _diff_attn(qkv, lam_p, subln_g, b, s, tq, lam_init):
    t = b * s
    nq = s // tq
    qb = 3 * FOX_HEADS
    kb = qb + 2 * DIFF_HEADS
    vb = (kb + 2 * DIFF_HEADS) // 2
    kern = functools.partial(_diff_kernel, tq=tq, tk=tq, s_len=s, lam_init=lam_init)
    return pl.pallas_call(
        kern,
        grid=(b, DIFF_HEADS, nq),
        in_specs=[
            pl.BlockSpec((tq, HEAD_DIM), lambda bi, h, qi: (bi * nq + qi, qb + 2 * h)),
            pl.BlockSpec((tq, HEAD_DIM), lambda bi, h, qi: (bi * nq + qi, qb + 2 * h + 1)),
            pl.BlockSpec((s, HEAD_DIM), lambda bi, h, qi: (bi, kb + 2 * h)),
            pl.BlockSpec((s, HEAD_DIM), lambda bi, h, qi: (bi, kb + 2 * h + 1)),
            pl.BlockSpec((s, DIFF_V_DIM), lambda bi, h, qi: (bi, vb + h)),
            pl.BlockSpec((4, HEAD_DIM), lambda bi, h, qi: (0, 0)),
            pl.BlockSpec((1, DIFF_V_DIM), lambda bi, h, qi: (0, 0)),
        ],
        out_specs=pl.BlockSpec((tq, DIFF_V_DIM), lambda bi, h, qi: (bi * nq + qi, h)),
        out_shape=jax.ShapeDtypeStruct((t, DIFF_HEADS * DIFF_V_DIM), BF16),
        scratch_shapes=[
            pltpu.VMEM((HEAD_DIM, tq), BF16), pltpu.VMEM((HEAD_DIM, tq), BF16),
            pltpu.VMEM((DIFF_V_DIM, s), BF16),
            pltpu.VMEM((1, tq), F32), pltpu.VMEM((1, tq), F32), pltpu.VMEM((DIFF_V_DIM, tq), F32),
            pltpu.VMEM((1, tq), F32), pltpu.VMEM((1, tq), F32), pltpu.VMEM((DIFF_V_DIM, tq), F32),
        ] + [pltpu.VMEM((tq, tq), F32) for _ in range(4)],
        compiler_params=_cparams(("parallel", "parallel", "arbitrary")),
    )(qkv, qkv, qkv, qkv, qkv, lam_p, subln_g)


def _pack_bf16_pairs(xb):
    w = xb.shape[1] // 2
    lo = lax.bitcast_convert_type(xb[:, :w].astype(F32), jnp.uint32)
    hi = lax.bitcast_convert_type(xb[:, w:].astype(F32), jnp.uint32)
    return (lo >> 16) | (hi & jnp.uint32(0xFFFF0000))


def _unpack_bf16_pairs(p):
    lo = lax.bitcast_convert_type(p << 16, F32).astype(BF16)
    hi = lax.bitcast_convert_type(p & jnp.uint32(0xFFFF0000), F32).astype(BF16)
    return lo, hi


def _merge_kernel(yf_ref, yd_ref, g0_ref, g1_ref, x_ref, wbf_ref, wbd_ref, wo_ref, n2_ref,
                  rw_ref, rb_ref, h_ref, u_ref, idx_ref, gw_ref):
    ya = jnp.dot(yf_ref[...], wbf_ref[...], preferred_element_type=F32)
    yb = jnp.dot(yd_ref[...], wbd_ref[...], preferred_element_type=F32)
    merged = g0_ref[...] * ya + g1_ref[...] * yb
    h = x_ref[...] + jnp.dot(merged.astype(BF16), wo_ref[...], preferred_element_type=F32)
    h_ref[...] = h
    ms = jnp.mean(h * h, axis=-1, keepdims=True)
    u = (h * lax.rsqrt(ms + EPS)) * n2_ref[...]
    ub = u.astype(BF16)
    u_ref[...] = _pack_bf16_pairs(ub)
    logits = jnp.dot(ub, rw_ref[...], preferred_element_type=F32) + rb_ref[...]

    lane = lax.broadcasted_iota(jnp.int32, logits.shape, 1).astype(F32)
    work = logits
    vals, idxs = [], []
    for _ in range(TOP_K):
        m = jnp.max(work, axis=-1, keepdims=True)
        sel = jnp.min(jnp.where(work == m, lane, float(LANES)), axis=-1, keepdims=True)
        vals.append(m)
        idxs.append(sel)
        work = jnp.where(lane == sel, -jnp.inf, work)
    es = [jnp.exp(v - vals[0]) for v in vals]
    den = es[0]
    for e in es[1:]:
        den = den + e
    idx_out = jnp.zeros(logits.shape, F32)
    gw_out = jnp.zeros(logits.shape, F32)
    for k in range(TOP_K):
        idx_out = jnp.where(lane == float(k), idxs[k], idx_out)
        gw_out = jnp.where(lane == float(k), es[k] / den, gw_out)
    idx_ref[...] = idx_out.astype(jnp.int32)
    gw_ref[...] = gw_out


def _merge(yf, yd, gate, x2, wbf, wbd, wo, n2, rw, rb, tm):
    t, d = x2.shape
    fw = yf.shape[1]
    dw = yd.shape[1]
    const = dict(pipeline_mode=pl.Buffered(1))
    return pl.pallas_call(
        _merge_kernel,
        grid=(t // tm,),
        in_specs=[
            pl.BlockSpec((tm, fw), lambda i: (i, 0)),
            pl.BlockSpec((tm, dw), lambda i: (i, 0)),
            pl.BlockSpec((tm, d), lambda i: (i, 0)),
            pl.BlockSpec((tm, d), lambda i: (i, 1)),
            pl.BlockSpec((tm, d), lambda i: (i, 0)),
            pl.BlockSpec((fw, d), lambda i: (0, 0), **const),
            pl.BlockSpec((dw, d), lambda i: (0, 0), **const),
            pl.BlockSpec((d, d), lambda i: (0, 0), **const),
            pl.BlockSpec((1, d), lambda i: (0, 0)),
            pl.BlockSpec((d, LANES), lambda i: (0, 0), **const),
            pl.BlockSpec((1, LANES), lambda i: (0, 0)),
        ],
        out_specs=[
            pl.BlockSpec((tm, d), lambda i: (i, 0)),
            pl.BlockSpec((tm, d // 2), lambda i: (i, 0)),
            pl.BlockSpec((tm, LANES), lambda i: (i, 0)),
            pl.BlockSpec((tm, LANES), lambda i: (i, 0)),
        ],
        out_shape=[
            jax.ShapeDtypeStruct((t, d), F32),
            jax.ShapeDtypeStruct((t, d // 2), jnp.uint32),
            jax.ShapeDtypeStruct((t, LANES), jnp.int32),
            jax.ShapeDtypeStruct((t, LANES), F32),
        ],
        compiler_params=_cparams(("parallel",)),
    )(yf, yd, gate, gate, x2, wbf, wbd, wo, n2, rw, rb)


def _dispatch_kernel(cnt_ref, pstart_ref, pend_ref, roa_ref, u_ref, xs_hbm, stage, zeros, sems,
                     *, tb, sub, n_rows):
    s = pl.program_id(0)
    n = pl.num_programs(0)
    slot = s % 2
    n_dma = TOP_K * tb

    def wait_slot(sl):
        pltpu.make_async_copy(stage.at[sl], xs_hbm.at[pl.ds(0, n_dma), :], sems.at[sl]).wait()

    @pl.when(s == 0)
    def _():
        zeros[...] = jnp.zeros(zeros.shape, zeros.dtype)

    @pl.when(s >= 2)
    def _():
        wait_slot(slot)

    blk = u_ref[...]
    for k in range(TOP_K):
        stage[slot, pl.ds(k * tb, tb), :] = blk

    def body(g, carry):
        for q8 in range(ISSUE_UNROLL):
            q = g * ISSUE_UNROLL + q8
            pltpu.make_async_copy(stage.at[slot, pl.ds(q, 1), :],
                                  xs_hbm.at[pl.ds(roa_ref[0, 0, q], 1), :], sems.at[slot]).start()
        return carry
    lax.fori_loop(0, n_dma // ISSUE_UNROLL, body, 0)

    @pl.when(s == n - 1)
    def _():
        @pl.when(n >= 2)
        def _():
            wait_slot(1 - slot)
        wait_slot(slot)

        def zero_row(r):
            return pltpu.make_async_copy(zeros.at[pl.ds(0, 1), :], xs_hbm.at[pl.ds(r, 1), :], sems.at[2])

        def zero_block(r):
            return pltpu.make_async_copy(zeros, xs_hbm.at[pl.ds(pl.multiple_of(r, sub), sub), :], sems.at[2])

        def fill(first, stop, copy, step):
            cnt = (stop - first) // step

            def start(q, carry):
                copy(first + q * step).start()
                return carry

            def wait(q, carry):
                copy(first).wait()
                return carry
            lax.fori_loop(0, cnt, start, 0)
            lax.fori_loop(0, cnt, wait, 0)

        def per_expert(e, carry):
            first = pstart_ref[e] + cnt_ref[e]
            edge = (first + sub - 1) // sub * sub
            fill(first, edge, zero_row, 1)
            fill(edge, pend_ref[e], zero_block, sub)
            return carry
        lax.fori_loop(0, N_EXPERTS, per_expert, 0)
        fill(pend_ref[N_EXPERTS - 1], n_rows, zero_block, sub)


def _dispatch(counts, pad_start, pad_end, roa3, u_packed, n_rows, tb, sub):
    t, w = u_packed.shape
    kern = functools.partial(_dispatch_kernel, tb=tb, sub=sub, n_rows=n_rows)
    return pl.pallas_call(
        kern,
        grid_spec=pltpu.PrefetchScalarGridSpec(
            num_scalar_prefetch=3,
            grid=(t // tb,),
            in_specs=[
                pl.BlockSpec((1, 1, TOP_K * tb), lambda s, c, ps, pe: (s, 0, 0), memory_space=pltpu.SMEM),
                pl.BlockSpec((tb, w), lambda s, c, ps, pe: (s, 0)),
            ],
            out_specs=pl.BlockSpec(memory_space=pl.ANY),
            scratch_shapes=[
                pltpu.VMEM((2, TOP_K * tb, w), jnp.uint32),
                pltpu.VMEM((sub, w), jnp.uint32),
                pltpu.SemaphoreType.DMA((3,)),
            ],
        ),
        out_shape=jax.ShapeDtypeStruct((n_rows, w), jnp.uint32),
        compiler_params=_cparams(("arbitrary",)),
    )(counts, pad_start, pad_end, roa3, u_packed)


def _expert_kernel(te_ref, ts_ref, nsub_ref, xs_ref, wg_ref, wu_ref, bg_ref, bu_ref, wd_ref,
                   bd_ref, out_ref, x_sc, act_sc, *, sub, spt, nf):
    i = pl.program_id(0)
    j = pl.program_id(1)
    nsub = nsub_ref[i]
    half = x_sc.shape[1] // 2

    @pl.when(jnp.logical_and(j == 0, nsub > 0))
    def _():
        lo, hi = _unpack_bf16_pairs(xs_ref[...])
        x_sc[:, :half] = lo
        x_sc[:, half:] = hi

    full = nsub == spt

    def gate_up(rs, wg, wu):
        x = x_sc[rs, :]
        hg = jnp.dot(x, wg, preferred_element_type=F32) + bg_ref[0]
        hu = jnp.dot(x, wu, preferred_element_type=F32) + bu_ref[0]
        hg = jnp.minimum(hg, SWIGLU_LIMIT)
        hl = jnp.clip(hu, -SWIGLU_LIMIT, SWIGLU_LIMIT)
        act = hg * jax.nn.sigmoid(SWIGLU_ALPHA * hg) * (hl + 1.0)
        act_sc[j, rs, :] = act.astype(BF16)

    def down(rs, wd):
        act = jnp.concatenate([act_sc[c, rs, :] for c in range(nf)], axis=1)
        out_ref[rs, :] = jnp.dot(act, wd, preferred_element_type=F32) + bd_ref[0]

    partial_tile = jnp.logical_and(nsub > 0, nsub < spt)

    @pl.when(jnp.logical_and(j < nf, full))
    def _():
        gate_up(slice(None), wg_ref[0].astype(BF16), wu_ref[0].astype(BF16))

    @pl.when(jnp.logical_and(j < nf, partial_tile))
    def _():
        wg = wg_ref[0].astype(BF16)
        wu = wu_ref[0].astype(BF16)
        for s in range(spt - 1):
            pl.when(s < nsub)(functools.partial(gate_up, slice(s * sub, (s + 1) * sub), wg, wu))

    @pl.when(jnp.logical_and(j >= nf, full))
    def _():
        down(slice(None), wd_ref[0].astype(BF16))

    @pl.when(jnp.logical_and(j >= nf, partial_tile))
    def _():
        wd = wd_ref[0].astype(BF16)
        for s in range(spt - 1):
            pl.when(s < nsub)(functools.partial(down, slice(s * sub, (s + 1) * sub), wd))

    @pl.when(jnp.logical_and(j >= nf, nsub < spt))
    def _():
        for s in range(spt):
            @pl.when(s >= nsub)
            def _():
                out_ref[s * sub:(s + 1) * sub, :] = jnp.zeros((sub, out_ref.shape[1]), out_ref.dtype)


def _experts(tile_e, tile_src, tile_nsub, xs, w_gu, b_gu3, w_down, b_down3, tm_e, sub, tf):
    n_rows = xs.shape[0]
    n_tiles = n_rows // tm_e
    d = 2 * xs.shape[1]
    dff = w_down.shape[1]
    nf = dff // tf
    nn = d // tf
    assert nn == nf
    kern = functools.partial(_expert_kernel, sub=sub, spt=tm_e // sub, nf=nf)

    def c1(i, j, ns):
        return jnp.where(ns[i] > 0, jnp.minimum(j, nf - 1), nf - 1)

    def c2(i, j, ns):
        return jnp.where(ns[i] > 0, jnp.maximum(j - nf, 0), nf - 1)

    return pl.pallas_call(
        kern,
        grid_spec=pltpu.PrefetchScalarGridSpec(
            num_scalar_prefetch=3,
            grid=(n_tiles, 2 * nf),
            in_specs=[
                pl.BlockSpec((tm_e, d // 2), lambda i, j, te, ts, ns: (ts[i], 0)),
                pl.BlockSpec((1, d, tf), lambda i, j, te, ts, ns: (te[i], 0, c1(i, j, ns))),
                pl.BlockSpec((1, d, tf), lambda i, j, te, ts, ns: (te[i], 0, nf + c1(i, j, ns))),
                pl.BlockSpec((1, 1, tf), lambda i, j, te, ts, ns: (te[i], 0, c1(i, j, ns))),
                pl.BlockSpec((1, 1, tf), lambda i, j, te, ts, ns: (te[i], 0, nf + c1(i, j, ns))),
                pl.BlockSpec((1, dff, tf), lambda i, j, te, ts, ns: (te[i], 0, c2(i, j, ns))),
                pl.BlockSpec((1, 1, tf), lambda i, j, te, ts, ns: (te[i], 0, c2(i, j, ns))),
            ],
            out_specs=pl.BlockSpec((tm_e, tf), lambda i, j, te, ts, ns: (i, jnp.maximum(j - nf, 0))),
            scratch_shapes=[
                pltpu.VMEM((tm_e, d), BF16),
                pltpu.VMEM((nf, tm_e, tf), BF16),
            ],
        ),
        out_shape=jax.ShapeDtypeStruct((n_rows, d), F32),
        compiler_params=_cparams(("arbitrary", "arbitrary")),
    )(tile_e, tile_src, tile_nsub, xs, w_gu, w_gu, b_gu3, b_gu3, w_down, b_down3)


def _combine_kernel(cur_ref, nxt_ref, yr_hbm, h_ref, gw_ref, gf_ref, out_ref, ysel, sems, *, tmc,
                    final_norm):
    i = pl.program_id(0)
    n = pl.num_programs(0)
    slot = i % 2
    n_dma = TOP_K * tmc

    def row_copy(row, q, sl):
        return pltpu.make_async_copy(yr_hbm.at[pl.ds(row, 1), :], ysel.at[sl, pl.ds(q, 1), :], sems.at[sl])

    def issue(idx_ref, sl):
        def body(g, carry):
            for q8 in range(ISSUE_UNROLL):
                q = g * ISSUE_UNROLL + q8
                row_copy(idx_ref[0, 0, q], q, sl).start()
            return carry
        lax.fori_loop(0, n_dma // ISSUE_UNROLL, body, 0)

    @pl.when(i == 0)
    def _():
        issue(cur_ref, 0)

    @pl.when(i + 1 < n)
    def _():
        issue(nxt_ref, 1 - slot)

    pltpu.make_async_copy(yr_hbm.at[pl.ds(0, n_dma), :], ysel.at[slot], sems.at[slot]).wait()

    gw = gw_ref[...]
    y = h_ref[...]
    for k in range(TOP_K):
        y = y + gw[:, k:k + 1] * ysel[slot, k * tmc:(k + 1) * tmc, :]
    if final_norm:
        ms = jnp.mean(y * y, axis=-1, keepdims=True)
        y = (y * lax.rsqrt(ms + EPS)) * gf_ref[...]
    out_ref[...] = y


def _combine(roa3, y_rows, h, gw, gf, tmc, final_norm):
    t, d = h.shape
    n = t // tmc
    kern = functools.partial(_combine_kernel, tmc=tmc, final_norm=final_norm)
    return pl.pallas_call(
        kern,
        grid=(n,),
        in_specs=[
            pl.BlockSpec((1, 1, TOP_K * tmc), lambda i: (i, 0, 0), memory_space=pltpu.SMEM),
            pl.BlockSpec((1, 1, TOP_K * tmc), lambda i: (jnp.minimum(i + 1, n - 1), 0, 0),
                         memory_space=pltpu.SMEM),
            pl.BlockSpec(memory_space=pl.ANY),
            pl.BlockSpec((tmc, d), lambda i: (i, 0)),
            pl.BlockSpec((tmc, LANES), lambda i: (i, 0)),
            pl.BlockSpec((1, d), lambda i: (0, 0)),
        ],
        out_specs=pl.BlockSpec((tmc, d), lambda i: (i, 0)),
        out_shape=jax.ShapeDtypeStruct((t, d), F32),
        scratch_shapes=[pltpu.VMEM((2, TOP_K * tmc, d), F32), pltpu.SemaphoreType.DMA((2,))],
        compiler_params=_cparams(("arbitrary",)),
    )(roa3, roa3, y_rows, h, gw, gf)


def _routing(top_idx, tm_e, sub):
    t = top_idx.shape[0]
    a = t * TOP_K
    flat_e = top_idx.reshape(a)
    onehot = (flat_e[:, None] == jnp.arange(N_EXPERTS, dtype=jnp.int32)[None, :]).astype(jnp.int32)
    csum = jnp.cumsum(onehot, axis=0)
    counts = csum[-1]
    rank = jnp.sum(onehot * csum, axis=1) - 1
    padded = (counts + tm_e - 1) // tm_e * tm_e
    pad_end = jnp.cumsum(padded)
    pad_start = pad_end - padded
    row_of_assign = (pad_start[flat_e] + rank).astype(jnp.int32)
    n_tiles = -(-a // tm_e) + N_EXPERTS
    tile_start = jnp.arange(n_tiles, dtype=jnp.int32) * tm_e
    n_used = pad_end[-1] // tm_e
    tile_e_raw = jnp.minimum(jnp.searchsorted(pad_end, tile_start, side='right'), N_EXPERTS - 1)
    used = tile_start < pad_end[-1]
    last = jnp.maximum(n_used - 1, 0)
    tile_src = jnp.where(used, jnp.arange(n_tiles, dtype=jnp.int32), last).astype(jnp.int32)
    tile_e = tile_e_raw[tile_src].astype(jnp.int32)
    valid_rows = jnp.clip(counts[tile_e_raw] - (tile_start - pad_start[tile_e_raw]), 0, tm_e)
    tile_nsub = jnp.where(used, (valid_rows + sub - 1) // sub, 0).astype(jnp.int32)
    bounds = (counts.astype(jnp.int32), pad_start.astype(jnp.int32), pad_end.astype(jnp.int32))
    return bounds, row_of_assign, tile_e, tile_src, tile_nsub, n_tiles * tm_e


def kernel(x, positions, norm1_g, w_in, b_fgate, b_gate, lam_q1, lam_k1, lam_q2, lam_k2, subln_g,
           w_branch_fox, w_branch_diff, w_out, norm2_g, router_w, router_b, w_gu, b_gu, w_down,
           b_down, normf_g):
    b, s, d = x.shape
    t = b * s
    depth = norm1_g.shape[0]
    fox_w = FOX_HEADS * HEAD_DIM
    tm_a = _pick(t, 1024)
    tq = _pick(s, 512)
    tm_d = _pick(t, 256)
    tm_e, sub, tf = 1024, 256, 512
    tmc = _pick(t, 128)

    half = ROPE_DIM // 2
    inv_freq = ROPE_THETA ** (-jnp.arange(0, ROPE_DIM, 2, dtype=F32) / ROPE_DIM)
    zeros = jnp.zeros((LANES - ROPE_DIM,), F32)
    rope_tab = jnp.zeros((8, LANES), F32)
    rope_tab = rope_tab.at[0].set(jnp.concatenate([inv_freq, inv_freq, zeros]))
    rope_tab = rope_tab.at[1].set(jnp.concatenate([-jnp.ones((half,), F32), jnp.zeros((half,), F32), zeros]))
    rope_tab = rope_tab.at[2].set(jnp.concatenate([jnp.zeros((half,), F32), jnp.ones((half,), F32), zeros]))
    pos = positions.astype(F32).reshape(t, 1)

    h = x.reshape(t, d)
    for l in range(depth):
        wi = w_in[l]
        w_main = jnp.concatenate([wi[:, :3 * fox_w], wi[:, 3 * fox_w + FOX_HEADS:]], axis=1).astype(BF16)
        w_f = jnp.pad(wi[:, 3 * fox_w:3 * fox_w + FOX_HEADS], ((0, 0), (0, LANES - FOX_HEADS))).astype(BF16)
        qkv, gate, flog = _inproj(h, pos, norm1_g[l][None, :], w_main, w_f, rope_tab,
                                  b_gate[l][None, :], tm_a)

        fl = flog[:, :FOX_HEADS].reshape(b, s, FOX_HEADS).transpose(0, 2, 1).reshape(b * FOX_HEADS, s)
        bf = jnp.tile(b_fgate[l].astype(F32), b).reshape(b * FOX_HEADS, 1)
        c3 = _fgate(fl, bf).reshape(b * FOX_HEADS, 1, s)

        y_fox = _fox_attn(qkv, c3, b, s, tq)
        lam_init = 0.8 - 0.6 * math.exp(-0.3 * l)
        lam_p = jnp.stack([lam_q1[l], lam_k1[l], lam_q2[l], lam_k2[l]]).astype(F32)
        y_diff = _diff_attn(qkv, lam_p, subln_g[l][None, :].astype(F32), b, s, tq, lam_init)

        rw = jnp.pad(router_w[l], ((0, 0), (0, LANES - N_EXPERTS))).astype(BF16)
        rb = jnp.concatenate([router_b[l].astype(F32), jnp.full((LANES - N_EXPERTS,), NEG_BIG, F32)])[None, :]
        h, u2, idx128, gw128 = _merge(y_fox, y_diff, gate, h, w_branch_fox[l].astype(BF16),
                                      w_branch_diff[l].astype(BF16), w_out[l].astype(BF16),
                                      norm2_g[l][None, :], rw, rb, tm_d)

        bounds, roa, tile_e, tile_src, tile_nsub, n_rows = _routing(idx128[:, :TOP_K], tm_e, sub)
        roa3 = roa.reshape(t // tmc, tmc, TOP_K).transpose(0, 2, 1).reshape(t // tmc, 1, TOP_K * tmc)
        xs = _dispatch(*bounds, roa3, u2, n_rows, tmc, sub)
        y_rows = _experts(tile_e, tile_src, tile_nsub, xs, w_gu[l], b_gu[l][:, None, :],
                          w_down[l], b_down[l][:, None, :], tm_e, sub, tf)
        h = _combine(roa3, y_rows, h, gw128, normf_g[None, :], tmc, final_norm=(l == depth - 1))
    return h.reshape(b, s, d)
```

```python
import functools
import math

import jax
import jax.numpy as jnp
from jax import lax
from jax.experimental import pallas as pl
from jax.experimental.pallas import tpu as pltpu

HEAD_DIM = 128
FOX_HEADS = 8
DIFF_HEADS = 4
DIFF_V_DIM = 2 * HEAD_DIM
ROPE_THETA = 500000.0
ROPE_DIM = HEAD_DIM // 4
N_EXPERTS = 32
TOP_K = 4
SWIGLU_ALPHA = 1.702
SWIGLU_LIMIT = 7.0
EPS = 1e-5
LANES = 128

F32 = jnp.float32
BF16 = jnp.bfloat16
NEG_BIG = -1e30
LOG2E = math.log2(math.e)
ISSUE_UNROLL = 8
VMEM_LIMIT = 56 * 1024 * 1024


def _cparams(sem, vmem=VMEM_LIMIT):
    return pltpu.CompilerParams(dimension_semantics=sem, vmem_limit_bytes=vmem)


def _pick(n, pref):
    t = min(n, pref)
    while n % t:
        t //= 2
    return t


QKV_TILE = 1024
QKV_ROTARY_TILES = (3, 4)
QKV_SCALED_TILES = (0, 3)


def _qkv_kernel(x_ref, g1_ref, w_ref, wf_ref, pos_ref, tab_ref,
                qkv_ref, u_ref, flog_ref, cos_sc, sa_sc, sb_sc, *, scale):
    j = pl.program_id(1)

    @pl.when(j == 0)
    def _():
        xf = x_ref[...]
        ms = jnp.mean(xf * xf, axis=-1, keepdims=True)
        u = (xf * lax.rsqrt(ms + EPS)) * g1_ref[...]
        ub = u.astype(BF16)
        u_ref[...] = ub
        flog_ref[...] = jnp.dot(ub, wf_ref[...], preferred_element_type=F32)
        ang = pos_ref[...] * tab_ref[0:1, :]
        sn = jnp.sin(ang)
        cos_sc[...] = jnp.cos(ang)
        sa_sc[...] = sn * tab_ref[1:2, :]
        sb_sc[...] = sn * tab_ref[2:3, :]

    acc = jnp.dot(u_ref[...], w_ref[...], preferred_element_type=F32)

    rot = functools.reduce(jnp.logical_or, [j == q for q in QKV_ROTARY_TILES])
    scaled = functools.reduce(jnp.logical_or, [j == q for q in QKV_SCALED_TILES])
    mult = jnp.where(scaled, scale, 1.0).astype(F32)
    rot_f = jnp.where(rot, mult, 0.0).astype(F32)
    ca = cos_sc[...] * rot_f + (mult - rot_f)
    cb = sa_sc[...] * rot_f
    cc = sb_sc[...] * rot_f
    for c in range(acc.shape[1] // HEAD_DIM):
        t = acc[:, c * HEAD_DIM:(c + 1) * HEAD_DIM]
        r = (t * ca + pltpu.roll(t, HEAD_DIM - ROPE_DIM // 2, 1) * cb
             + pltpu.roll(t, ROPE_DIM // 2, 1) * cc)
        qkv_ref[:, c * HEAD_DIM:(c + 1) * HEAD_DIM] = r.astype(BF16)


def _gates_kernel(u_ref, w_ref, bg_ref, gate_ref):
    acc = jnp.dot(u_ref[...], w_ref[...], preferred_element_type=F32)
    gate_ref[...] = jax.nn.sigmoid(acc + bg_ref[...])


def _inproj(x2, pos, g1, w_main, w_f, rope_tab, b_gate, tm):
    t, d = x2.shape
    tn = QKV_TILE
    n_qkv = 6
    n_gate = (w_main.shape[1] - n_qkv * tn) // tn
    kern = functools.partial(_qkv_kernel, scale=LOG2E / math.sqrt(HEAD_DIM))
    qkv, u, flog = pl.pallas_call(
        kern,
        grid=(t // tm, n_qkv),
        in_specs=[
            pl.BlockSpec((tm, d), lambda i, j: (i, 0)),
            pl.BlockSpec((1, d), lambda i, j: (0, 0)),
            pl.BlockSpec((d, tn), lambda i, j: (0, j)),
            pl.BlockSpec((d, LANES), lambda i, j: (0, 0)),
            pl.BlockSpec((tm, 1), lambda i, j: (i, 0)),
            pl.BlockSpec((8, LANES), lambda i, j: (0, 0)),
        ],
        out_specs=[
            pl.BlockSpec((tm, tn), lambda i, j: (i, j)),
            pl.BlockSpec((tm, d), lambda i, j: (i, 0)),
            pl.BlockSpec((tm, LANES), lambda i, j: (i, 0)),
        ],
        out_shape=[
            jax.ShapeDtypeStruct((t, n_qkv * tn), BF16),
            jax.ShapeDtypeStruct((t, d), BF16),
            jax.ShapeDtypeStruct((t, LANES), F32),
        ],
        scratch_shapes=[
            pltpu.VMEM((tm, LANES), F32),
            pltpu.VMEM((tm, LANES), F32),
            pltpu.VMEM((tm, LANES), F32),
        ],
        compiler_params=_cparams(("parallel", "arbitrary")),
    )(x2, g1, w_main, w_f, pos, rope_tab)
    gate = pl.pallas_call(
        _gates_kernel,
        grid=(t // tm, n_gate),
        in_specs=[
            pl.BlockSpec((tm, d), lambda i, j: (i, 0)),
            pl.BlockSpec((d, tn), lambda i, j: (0, n_qkv + j)),
            pl.BlockSpec((1, tn), lambda i, j: (0, j)),
        ],
        out_specs=pl.BlockSpec((tm, tn), lambda i, j: (i, j)),
        out_shape=jax.ShapeDtypeStruct((t, n_gate * tn), F32),
        compiler_params=_cparams(("parallel", "arbitrary")),
    )(u, w_main, b_gate)
    return qkv, gate, flog


def _fgate_kernel(fl_ref, b_ref, c_ref):
    z = fl_ref[...] + b_ref[...]
    x = jnp.minimum(z, 0.0) - jnp.log(1.0 + jnp.exp(-jnp.abs(z)))
    n = x.shape[1]
    lane = lax.broadcasted_iota(jnp.int32, x.shape, 1)
    sh = 1
    while sh < n:
        x = x + jnp.where(lane >= sh, pltpu.roll(x, sh, 1), 0.0)
        sh *= 2
    c_ref[...] = x


def _fgate(fl, b):
    r, s = fl.shape
    return pl.pallas_call(
        _fgate_kernel,
        grid=(1,),
        in_specs=[pl.BlockSpec((r, s), lambda i: (0, 0)), pl.BlockSpec((r, 1), lambda i: (0, 0))],
        out_specs=pl.BlockSpec((r, s), lambda i: (0, 0)),
        out_shape=jax.ShapeDtypeStruct((r, s), F32),
        compiler_params=_cparams(("arbitrary",)),
    )(fl, b)


def _transpose_bf16(x):
    return x.astype(F32).T.astype(BF16)


def _softmax_step(s, vt, m_sc, l_sc, acc_sc):
    m_old = m_sc[...]
    m_new = jnp.maximum(m_old, jnp.max(s, axis=0, keepdims=True))
    alpha = jnp.exp2(m_old - m_new)
    p = jnp.exp2(s - m_new)
    l_sc[...] = alpha * l_sc[...] + jnp.sum(p, axis=0, keepdims=True)
    acc_sc[...] = alpha * acc_sc[...] + jnp.dot(vt, p.astype(BF16), preferred_element_type=F32)
    m_sc[...] = m_new


def _causal_keep(shape):
    kv = lax.broadcasted_iota(jnp.int32, shape, 0)
    qq = lax.broadcasted_iota(jnp.int32, shape, 1)
    return kv <= qq


FOX_HEADS_PER_STEP = 2


def _fox_kernel(q_ref, k_ref, v_ref, c_ref, o_ref, *scratch, tq, tk, s_len):
    hp = FOX_HEADS_PER_STEP
    qt_sc, kaug_sc, vt_sc, m_sc, l_sc, acc_sc = (scratch[n * hp:(n + 1) * hp] for n in range(6))
    s_sc = scratch[6 * hp:]
    qi = pl.program_id(2)
    head_cols = [slice(h * HEAD_DIM, (h + 1) * HEAD_DIM) for h in range(hp)]

    @pl.when(qi == 0)
    def _():
        sub = lax.broadcasted_iota(jnp.int32, (HEAD_DIM, tk), 0)
        for h in range(hp):
            for ch in range(s_len // tk):
                rows = slice(ch * tk, (ch + 1) * tk)
                vt_sc[h][:, rows] = _transpose_bf16(v_ref[rows, head_cols[h]])
                c = c_ref[h, :, rows] * (-LOG2E)
                hi = c.astype(BF16).astype(F32)
                mid = (c - hi).astype(BF16).astype(F32)
                lo = (c - hi - mid).astype(BF16).astype(F32)
                tab = jnp.where(sub == 0, hi, jnp.where(sub == 1, mid, jnp.where(sub == 2, lo, 0.0)))
                kaug_sc[h][rows, HEAD_DIM:] = tab.T.astype(BF16)
                kaug_sc[h][rows, :HEAD_DIM] = k_ref[rows, head_cols[h]]

    sub_q = lax.broadcasted_iota(jnp.int32, (HEAD_DIM, tq), 0)
    for h in range(hp):
        qt_sc[h][:HEAD_DIM, :] = _transpose_bf16(q_ref[:, head_cols[h]])
        qt_sc[h][HEAD_DIM:, :] = jnp.where(sub_q < 3, 1.0, 0.0).astype(BF16)
        m_sc[h][...] = jnp.full(m_sc[h].shape, -jnp.inf, F32)
        l_sc[h][...] = jnp.zeros(l_sc[h].shape, F32)
        acc_sc[h][...] = jnp.zeros(acc_sc[h].shape, F32)

    def scores(h, chunk):
        start = pl.multiple_of(chunk * tk, tk)
        return jnp.dot(kaug_sc[h][pl.ds(start, tk), :], qt_sc[h][...], preferred_element_type=F32)

    def consume(h, chunk, par, masked):
        s = s_sc[par * hp + h][...]
        if masked:
            s = jnp.where(_causal_keep(s.shape), s, -jnp.inf)
        start = pl.multiple_of(chunk * tk, tk)
        _softmax_step(s, vt_sc[h][:, pl.ds(start, tk)], m_sc[h], l_sc[h], acc_sc[h])

    for h in range(hp):
        s_sc[h][...] = scores(h, 0)

    def body(kc, carry):
        def run(par):
            for h in range(hp):
                s_sc[(1 - par) * hp + h][...] = scores(h, kc + 1)
                consume(h, kc, par, False)
        pl.when(kc % 2 == 0)(functools.partial(run, 0))
        pl.when(kc % 2 == 1)(functools.partial(run, 1))
        return carry

    lax.fori_loop(0, qi, body, 0)

    def diagonal(par):
        for h in range(hp):
            consume(h, qi, par, True)
    pl.when(qi % 2 == 0)(functools.partial(diagonal, 0))
    pl.when(qi % 2 == 1)(functools.partial(diagonal, 1))
    for h in range(hp):
        o_ref[:, head_cols[h]] = (acc_sc[h][...] / l_sc[h][...]).T.astype(o_ref.dtype)


def _fox_attn(qkv, c3, b, s, tq):
    t = b * s
    nq = s // tq
    hp = FOX_HEADS_PER_STEP
    groups = FOX_HEADS // hp
    width = hp * HEAD_DIM
    kern = functools.partial(_fox_kernel, tq=tq, tk=tq, s_len=s)

    def per_head(shape, dtype):
        return [pltpu.VMEM(shape, dtype) for _ in range(hp)]

    return pl.pallas_call(
        kern,
        grid=(b, groups, nq),
        in_specs=[
            pl.BlockSpec((tq, width), lambda bi, g, qi: (bi * nq + qi, g)),
            pl.BlockSpec((s, width), lambda bi, g, qi: (bi, groups + g)),
            pl.BlockSpec((s, width), lambda bi, g, qi: (bi, 2 * groups + g)),
            pl.BlockSpec((hp, 1, s), lambda bi, g, qi: (bi * groups + g, 0, 0)),
        ],
        out_specs=pl.BlockSpec((tq, width), lambda bi, g, qi: (bi * nq + qi, g)),
        out_shape=jax.ShapeDtypeStruct((t, FOX_HEADS * HEAD_DIM), BF16),
        scratch_shapes=(per_head((2 * HEAD_DIM, tq), BF16) + per_head((s, 2 * HEAD_DIM), BF16)
                        + per_head((HEAD_DIM, s), BF16) + per_head((1, tq), F32)
                        + per_head((1, tq), F32) + per_head((HEAD_DIM, tq), F32)
                        + per_head((tq, tq), F32) + per_head((tq, tq), F32)),
        compiler_params=_cparams(("parallel", "parallel", "arbitrary")),
    )(qkv, qkv, qkv, c3)


def _diff_kernel(q1_ref, q2_ref, k1_ref, k2_ref, v_ref, lam_ref, g_ref, o_ref,
                 q1t_sc, q2t_sc, vt_sc, m1_sc, l1_sc, a1_sc, m2_sc, l2_sc, a2_sc, *s_sc,
                 tq, tk, s_len, lam_init):
    qi = pl.program_id(2)

    @pl.when(qi == 0)
    def _():
        for ch in range(s_len // tk):
            rows = slice(ch * tk, (ch + 1) * tk)
            vt_sc[:, rows] = _transpose_bf16(v_ref[rows, :])

    q1t_sc[...] = _transpose_bf16(q1_ref[...])
    q2t_sc[...] = _transpose_bf16(q2_ref[...])
    for r in (m1_sc, m2_sc):
        r[...] = jnp.full(r.shape, -jnp.inf, F32)
    for r in (l1_sc, l2_sc, a1_sc, a2_sc):
        r[...] = jnp.zeros(r.shape, F32)

    chains = ((k1_ref, q1t_sc, m1_sc, l1_sc, a1_sc), (k2_ref, q2t_sc, m2_sc, l2_sc, a2_sc))

    def scores(c, chunk):
        start = pl.multiple_of(chunk * tk, tk)
        return jnp.dot(chains[c][0][pl.ds(start, tk), :], chains[c][1][...], preferred_element_type=F32)

    def consume(c, chunk, par, masked):
        s = s_sc[par * 2 + c][...]
        if masked:
            s = jnp.where(_causal_keep(s.shape), s, -jnp.inf)
        start = pl.multiple_of(chunk * tk, tk)
        _softmax_step(s, vt_sc[:, pl.ds(start, tk)], *chains[c][2:])

    for c in range(2):
        s_sc[c][...] = scores(c, 0)

    def body(kc, carry):
        def run(par):
            for c in range(2):
                s_sc[(1 - par) * 2 + c][...] = scores(c, kc + 1)
                consume(c, kc, par, False)
        pl.when(kc % 2 == 0)(functools.partial(run, 0))
        pl.when(kc % 2 == 1)(functools.partial(run, 1))
        return carry

    lax.fori_loop(0, qi, body, 0)

    def diagonal(par):
        for c in range(2):
            consume(c, qi, par, True)
    pl.when(qi % 2 == 0)(functools.partial(diagonal, 0))
    pl.when(qi % 2 == 1)(functools.partial(diagonal, 1))

    lp = lam_ref[...]
    lam = (jnp.exp(jnp.sum(lp[0:1, :] * lp[1:2, :], axis=-1, keepdims=True))
           - jnp.exp(jnp.sum(lp[2:3, :] * lp[3:4, :], axis=-1, keepdims=True)) + lam_init)
    y = (a1_sc[...] / l1_sc[...]).T - lam * (a2_sc[...] / l2_sc[...]).T
    ms = jnp.mean(y * y, axis=-1, keepdims=True)
    y = (y * lax.rsqrt(ms + EPS)) * g_ref[...]
    o_ref[...] = (y * (1.0 - lam_init)).astype(o_ref.dtype)


def _diff_attn(qkv, lam_p, subln_g, b, s, tq, lam_init):
    t = b * s
    nq = s // tq
    qb = 3 * FOX_HEADS
    kb = qb + 2 * DIFF_HEADS
    vb = (kb + 2 * DIFF_HEADS) // 2
    kern = functools.partial(_diff_kernel, tq=tq, tk=tq, s_len=s, lam_init=lam_init)
    return pl.pallas_call(
        kern,
        grid=(b, DIFF_HEADS, nq),
        in_specs=[
            pl.BlockSpec((tq, HEAD_DIM), lambda bi, h, qi: (bi * nq + qi, qb + 2 * h)),
            pl.BlockSpec((tq, HEAD_DIM), lambda bi, h, qi: (bi * nq + qi, qb + 2 * h + 1)),
            pl.BlockSpec((s, HEAD_DIM), lambda bi, h, qi: (bi, kb + 2 * h)),
            pl.BlockSpec((s, HEAD_DIM), lambda bi, h, qi: (bi, kb + 2 * h + 1)),
            pl.BlockSpec((s, DIFF_V_DIM), lambda bi, h, qi: (bi, vb + h)),
            pl.BlockSpec((4, HEAD_DIM), lambda bi, h, qi: (0, 0)),
            pl.BlockSpec((1, DIFF_V_DIM), lambda bi, h, qi: (0, 0)),
        ],
        out_specs=pl.BlockSpec((tq, DIFF_V_DIM), lambda bi, h, qi: (bi * nq + qi, h)),
        out_shape=jax.ShapeDtypeStruct((t, DIFF_HEADS * DIFF_V_DIM), BF16),
        scratch_shapes=[
            pltpu.VMEM((HEAD_DIM, tq), BF16), pltpu.VMEM((HEAD_DIM, tq), BF16),
            pltpu.VMEM((DIFF_V_DIM, s), BF16),
            pltpu.VMEM((1, tq), F32), pltpu.VMEM((1, tq), F32), pltpu.VMEM((DIFF_V_DIM, tq), F32),
            pltpu.VMEM((1, tq), F32), pltpu.VMEM((1, tq), F32), pltpu.VMEM((DIFF_V_DIM, tq), F32),
        ] + [pltpu.VMEM((tq, tq), F32) for _ in range(4)],
        compiler_params=_cparams(("parallel", "parallel", "arbitrary")),
    )(qkv, qkv, qkv, qkv, qkv, lam_p, subln_g)


def _pack_bf16_pairs(xb):
    w = xb.shape[1] // 2
    lo = lax.bitcast_convert_type(xb[:, :w].astype(F32), jnp.uint32)
    hi = lax.bitcast_convert_type(xb[:, w:].astype(F32), jnp.uint32)
    return (lo >> 16) | (hi & jnp.uint32(0xFFFF0000))


def _unpack_bf16_pairs(p):
    lo = lax.bitcast_convert_type(p << 16, F32).astype(BF16)
    hi = lax.bitcast_convert_type(p & jnp.uint32(0xFFFF0000), F32).astype(BF16)
    return lo, hi


def _merge_kernel(yf_ref, yd_ref, g0_ref, g1_ref, x_ref, wbf_ref, wbd_ref, wo_ref, n2_ref,
                  rw_ref, rb_ref, h_ref, u_ref, idx_ref, gw_ref):
    ya = jnp.dot(yf_ref[...], wbf_ref[...], preferred_element_type=F32)
    yb = jnp.dot(yd_ref[...], wbd_ref[...], preferred_element_type=F32)
    merged = g0_ref[...] * ya + g1_ref[...] * yb
    h = x_ref[...] + jnp.dot(merged.astype(BF16), wo_ref[...], preferred_element_type=F32)
    h_ref[...] = h
    ms = jnp.mean(h * h, axis=-1, keepdims=True)
    u = (h * lax.rsqrt(ms + EPS)) * n2_ref[...]
    ub = u.astype(BF16)
    u_ref[...] = _pack_bf16_pairs(ub)
    logits = jnp.dot(ub, rw_ref[...], preferred_element_type=F32) + rb_ref[...]

    lane = lax.broadcasted_iota(jnp.int32, logits.shape, 1).astype(F32)
    work = logits
    vals, idxs = [], []
    for _ in range(TOP_K):
        m = jnp.max(work, axis=-1, keepdims=True)
        sel = jnp.min(jnp.where(work == m, lane, float(LANES)), axis=-1, keepdims=True)
        vals.append(m)
        idxs.append(sel)
        work = jnp.where(lane == sel, -jnp.inf, work)
    es = [jnp.exp(v - vals[0]) for v in vals]
    den = es[0]
    for e in es[1:]:
        den = den + e
    idx_out = jnp.zeros(logits.shape, F32)
    gw_out = jnp.zeros(logits.shape, F32)
    for k in range(TOP_K):
        idx_out = jnp.where(lane == float(k), idxs[k], idx_out)
        gw_out = jnp.where(lane == float(k), es[k] / den, gw_out)
    idx_ref[...] = idx_out.astype(jnp.int32)
    gw_ref[...] = gw_out


def _merge(yf, yd, gate, x2, wbf, wbd, wo, n2, rw, rb, tm):
    t, d = x2.shape
    fw = yf.shape[1]
    dw = yd.shape[1]
    const = dict(pipeline_mode=pl.Buffered(1))
    return pl.pallas_call(
        _merge_kernel,
        grid=(t // tm,),
        in_specs=[
            pl.BlockSpec((tm, fw), lambda i: (i, 0)),
            pl.BlockSpec((tm, dw), lambda i: (i, 0)),
            pl.BlockSpec((tm, d), lambda i: (i, 0)),
            pl.BlockSpec((tm, d), lambda i: (i, 1)),
            pl.BlockSpec((tm, d), lambda i: (i, 0)),
            pl.BlockSpec((fw, d), lambda i: (0, 0), **const),
            pl.BlockSpec((dw, d), lambda i: (0, 0), **const),
            pl.BlockSpec((d, d), lambda i: (0, 0), **const),
            pl.BlockSpec((1, d), lambda i: (0, 0)),
            pl.BlockSpec((d, LANES), lambda i: (0, 0), **const),
            pl.BlockSpec((1, LANES), lambda i: (0, 0)),
        ],
        out_specs=[
            pl.BlockSpec((tm, d), lambda i: (i, 0)),
            pl.BlockSpec((tm, d // 2), lambda i: (i, 0)),
            pl.BlockSpec((tm, LANES), lambda i: (i, 0)),
            pl.BlockSpec((tm, LANES), lambda i: (i, 0)),
        ],
        out_shape=[
            jax.ShapeDtypeStruct((t, d), F32),
            jax.ShapeDtypeStruct((t, d // 2), jnp.uint32),
            jax.ShapeDtypeStruct((t, LANES), jnp.int32),
            jax.ShapeDtypeStruct((t, LANES), F32),
        ],
        compiler_params=_cparams(("parallel",)),
    )(yf, yd, gate, gate, x2, wbf, wbd, wo, n2, rw, rb)


def _dispatch_kernel(cnt_ref, pstart_ref, pend_ref, roa_ref, u_ref, xs_hbm, stage, zeros, sems,
                     *, tb, sub, n_rows):
    s = pl.program_id(0)
    n = pl.num_programs(0)
    slot = s % 2
    n_dma = TOP_K * tb

    def wait_slot(sl):
        pltpu.make_async_copy(stage.at[sl], xs_hbm.at[pl.ds(0, n_dma), :], sems.at[sl]).wait()

    @pl.when(s == 0)
    def _():
        zeros[...] = jnp.zeros(zeros.shape, zeros.dtype)

    @pl.when(s >= 2)
    def _():
        wait_slot(slot)

    blk = u_ref[...]
    for k in range(TOP_K):
        stage[slot, pl.ds(k * tb, tb), :] = blk

    def body(g, carry):
        for q8 in range(ISSUE_UNROLL):
            q = g * ISSUE_UNROLL + q8
            pltpu.make_async_copy(stage.at[slot, pl.ds(q, 1), :],
                                  xs_hbm.at[pl.ds(roa_ref[0, 0, q], 1), :], sems.at[slot]).start()
        return carry
    lax.fori_loop(0, n_dma // ISSUE_UNROLL, body, 0)

    @pl.when(s == n - 1)
    def _():
        @pl.when(n >= 2)
        def _():
            wait_slot(1 - slot)
        wait_slot(slot)

        def zero_row(r):
            return pltpu.make_async_copy(zeros.at[pl.ds(0, 1), :], xs_hbm.at[pl.ds(r, 1), :], sems.at[2])

        def zero_block(r):
            return pltpu.make_async_copy(zeros, xs_hbm.at[pl.ds(pl.multiple_of(r, sub), sub), :], sems.at[2])

        def fill(first, stop, copy, step):
            cnt = (stop - first) // step

            def start(q, carry):
                copy(first + q * step).start()
                return carry

            def wait(q, carry):
                copy(first).wait()
                return carry
            lax.fori_loop(0, cnt, start, 0)
            lax.fori_loop(0, cnt, wait, 0)

        def per_expert(e, carry):
            first = pstart_ref[e] + cnt_ref[e]
            edge = (first + sub - 1) // sub * sub
            fill(first, edge, zero_row, 1)
            fill(edge, pend_ref[e], zero_block, sub)
            return carry
        lax.fori_loop(0, N_EXPERTS, per_expert, 0)
        fill(pend_ref[N_EXPERTS - 1], n_rows, zero_block, sub)


def _dispatch(counts, pad_start, pad_end, roa3, u_packed, n_rows, tb, sub):
    t, w = u_packed.shape
    kern = functools.partial(_dispatch_kernel, tb=tb, sub=sub, n_rows=n_rows)
    return pl.pallas_call(
        kern,
        grid_spec=pltpu.PrefetchScalarGridSpec(
            num_scalar_prefetch=3,
            grid=(t // tb,),
            in_specs=[
                pl.BlockSpec((1, 1, TOP_K * tb), lambda s, c, ps, pe: (s, 0, 0), memory_space=pltpu.SMEM),
                pl.BlockSpec((tb, w), lambda s, c, ps, pe: (s, 0)),
            ],
            out_specs=pl.BlockSpec(memory_space=pl.ANY),
            scratch_shapes=[
                pltpu.VMEM((2, TOP_K * tb, w), jnp.uint32),
                pltpu.VMEM((sub, w), jnp.uint32),
                pltpu.SemaphoreType.DMA((3,)),
            ],
        ),
        out_shape=jax.ShapeDtypeStruct((n_rows, w), jnp.uint32),
        compiler_params=_cparams(("arbitrary",)),
    )(counts, pad_start, pad_end, roa3, u_packed)


def _expert_kernel(te_ref, ts_ref, nsub_ref, xs_ref, wg_ref, wu_ref, bg_ref, bu_ref, wd_ref,
                   bd_ref, out_ref, x_sc, act_sc, *, sub, spt, nf):
    i = pl.program_id(0)
    j = pl.program_id(1)
    nsub = nsub_ref[i]
    half = x_sc.shape[1] // 2

    @pl.when(jnp.logical_and(j == 0, nsub > 0))
    def _():
        lo, hi = _unpack_bf16_pairs(xs_ref[...])
        x_sc[:, :half] = lo
        x_sc[:, half:] = hi

    full = nsub == spt

    def gate_up(rs, wg, wu):
        x = x_sc[rs, :]
        hg = jnp.dot(x, wg, preferred_element_type=F32) + bg_ref[0]
        hu = jnp.dot(x, wu, preferred_element_type=F32) + bu_ref[0]
        hg = jnp.minimum(hg, SWIGLU_LIMIT)
        hl = jnp.clip(hu, -SWIGLU_LIMIT, SWIGLU_LIMIT)
        act = hg * jax.nn.sigmoid(SWIGLU_ALPHA * hg) * (hl + 1.0)
        act_sc[j, rs, :] = act.astype(BF16)

    def down(rs, wd):
        act = jnp.concatenate([act_sc[c, rs, :] for c in range(nf)], axis=1)
        out_ref[rs, :] = jnp.dot(act, wd, preferred_element_type=F32) + bd_ref[0]

    partial_tile = jnp.logical_and(nsub > 0, nsub < spt)

    @pl.when(jnp.logical_and(j < nf, full))
    def _():
        gate_up(slice(None), wg_ref[0].astype(BF16), wu_ref[0].astype(BF16))

    @pl.when(jnp.logical_and(j < nf, partial_tile))
    def _():
        wg = wg_ref[0].astype(BF16)
        wu = wu_ref[0].astype(BF16)
        for s in range(spt - 1):
            pl.when(s < nsub)(functools.partial(gate_up, slice(s * sub, (s + 1) * sub), wg, wu))

    @pl.when(jnp.logical_and(j >= nf, full))
    def _():
        down(slice(None), wd_ref[0].astype(BF16))

    @pl.when(jnp.logical_and(j >= nf, partial_tile))
    def _():
        wd = wd_ref[0].astype(BF16)
        for s in range(spt - 1):
            pl.when(s < nsub)(functools.partial(down, slice(s * sub, (s + 1) * sub), wd))

    @pl.when(jnp.logical_and(j >= nf, nsub < spt))
    def _():
        for s in range(spt):
            @pl.when(s >= nsub)
            def _():
                out_ref[s * sub:(s + 1) * sub, :] = jnp.zeros((sub, out_ref.shape[1]), out_ref.dtype)


def _experts(tile_e, tile_src, tile_nsub, xs, w_gu, b_gu3, w_down, b_down3, tm_e, sub, tf):
    n_rows = xs.shape[0]
    n_tiles = n_rows // tm_e
    d = 2 * xs.shape[1]
    dff = w_down.shape[1]
    nf = dff // tf
    nn = d // tf
    assert nn == nf
    kern = functools.partial(_expert_kernel, sub=sub, spt=tm_e // sub, nf=nf)

    def c1(i, j, ns):
        return jnp.where(ns[i] > 0, jnp.minimum(j, nf - 1), nf - 1)

    def c2(i, j, ns):
        return jnp.where(ns[i] > 0, jnp.maximum(j - nf, 0), nf - 1)

    return pl.pallas_call(
        kern,
        grid_spec=pltpu.PrefetchScalarGridSpec(
            num_scalar_prefetch=3,
            grid=(n_tiles, 2 * nf),
            in_specs=[
                pl.BlockSpec((tm_e, d // 2), lambda i, j, te, ts, ns: (ts[i], 0)),
                pl.BlockSpec((1, d, tf), lambda i, j, te, ts, ns: (te[i], 0, c1(i, j, ns))),
                pl.BlockSpec((1, d, tf), lambda i, j, te, ts, ns: (te[i], 0, nf + c1(i, j, ns))),
                pl.BlockSpec((1, 1, tf), lambda i, j, te, ts, ns: (te[i], 0, c1(i, j, ns))),
                pl.BlockSpec((1, 1, tf), lambda i, j, te, ts, ns: (te[i], 0, nf + c1(i, j, ns))),
                pl.BlockSpec((1, dff, tf), lambda i, j, te, ts, ns: (te[i], 0, c2(i, j, ns))),
                pl.BlockSpec((1, 1, tf), lambda i, j, te, ts, ns: (te[i], 0, c2(i, j, ns))),
            ],
            out_specs=pl.BlockSpec((tm_e, tf), lambda i, j, te, ts, ns: (i, jnp.maximum(j - nf, 0))),
            scratch_shapes=[
                pltpu.VMEM((tm_e, d), BF16),
                pltpu.VMEM((nf, tm_e, tf), BF16),
            ],
        ),
        out_shape=jax.ShapeDtypeStruct((n_rows, d), F32),
        compiler_params=_cparams(("arbitrary", "arbitrary")),
    )(tile_e, tile_src, tile_nsub, xs, w_gu, w_gu, b_gu3, b_gu3, w_down, b_down3)


def _combine_kernel(cur_ref, nxt_ref, yr_hbm, h_ref, gw_ref, gf_ref, out_ref, ysel, sems, *, tmc,
                    final_norm):
    i = pl.program_id(0)
    n = pl.num_programs(0)
    slot = i % 2
    n_dma = TOP_K * tmc

    def row_copy(row, q, sl):
        return pltpu.make_async_copy(yr_hbm.at[pl.ds(row, 1), :], ysel.at[sl, pl.ds(q, 1), :], sems.at[sl])

    def issue(idx_ref, sl):
        def body(g, carry):
            for q8 in range(ISSUE_UNROLL):
                q = g * ISSUE_UNROLL + q8
                row_copy(idx_ref[0, 0, q], q, sl).start()
            return carry
        lax.fori_loop(0, n_dma // ISSUE_UNROLL, body, 0)

    @pl.when(i == 0)
    def _():
        issue(cur_ref, 0)

    @pl.when(i + 1 < n)
    def _():
        issue(nxt_ref, 1 - slot)

    pltpu.make_async_copy(yr_hbm.at[pl.ds(0, n_dma), :], ysel.at[slot], sems.at[slot]).wait()

    gw = gw_ref[...]
    y = h_ref[...]
    for k in range(TOP_K):
        y = y + gw[:, k:k + 1] * ysel[slot, k * tmc:(k + 1) * tmc, :]
    if final_norm:
        ms = jnp.mean(y * y, axis=-1, keepdims=True)
        y = (y * lax.rsqrt(ms + EPS)) * gf_ref[...]
    out_ref[...] = y


def _combine(roa3, y_rows, h, gw, gf, tmc, final_norm):
    t, d = h.shape
    n = t // tmc
    kern = functools.partial(_combine_kernel, tmc=tmc, final_norm=final_norm)
    return pl.pallas_call(
        kern,
        grid=(n,),
        in_specs=[
            pl.BlockSpec((1, 1, TOP_K * tmc), lambda i: (i, 0, 0), memory_space=pltpu.SMEM),
            pl.BlockSpec((1, 1, TOP_K * tmc), lambda i: (jnp.minimum(i + 1, n - 1), 0, 0),
                         memory_space=pltpu.SMEM),
            pl.BlockSpec(memory_space=pl.ANY),
            pl.BlockSpec((tmc, d), lambda i: (i, 0)),
            pl.BlockSpec((tmc, LANES), lambda i: (i, 0)),
            pl.BlockSpec((1, d), lambda i: (0, 0)),
        ],
        out_specs=pl.BlockSpec((tmc, d), lambda i: (i, 0)),
        out_shape=jax.ShapeDtypeStruct((t, d), F32),
        scratch_shapes=[pltpu.VMEM((2, TOP_K * tmc, d), F32), pltpu.SemaphoreType.DMA((2,))],
        compiler_params=_cparams(("arbitrary",)),
    )(roa3, roa3, y_rows, h, gw, gf)


def _routing(top_idx, tm_e, sub):
    t = top_idx.shape[0]
    a = t * TOP_K
    flat_e = top_idx.reshape(a)
    onehot = (flat_e[:, None] == jnp.arange(N_EXPERTS, dtype=jnp.int32)[None, :]).astype(jnp.int32)
    csum = jnp.cumsum(onehot, axis=0)
    counts = csum[-1]
    rank = jnp.sum(onehot * csum, axis=1) - 1
    padded = (counts + tm_e - 1) // tm_e * tm_e
    pad_end = jnp.cumsum(padded)
    pad_start = pad_end - padded
    row_of_assign = (pad_start[flat_e] + rank).astype(jnp.int32)
    n_tiles = -(-a // tm_e) + N_EXPERTS
    tile_start = jnp.arange(n_tiles, dtype=jnp.int32) * tm_e
    n_used = pad_end[-1] // tm_e
    tile_e_raw = jnp.minimum(jnp.searchsorted(pad_end, tile_start, side='right'), N_EXPERTS - 1)
    used = tile_start < pad_end[-1]
    last = jnp.maximum(n_used - 1, 0)
    tile_src = jnp.where(used, jnp.arange(n_tiles, dtype=jnp.int32), last).astype(jnp.int32)
    tile_e = tile_e_raw[tile_src].astype(jnp.int32)
    valid_rows = jnp.clip(counts[tile_e_raw] - (tile_start - pad_start[tile_e_raw]), 0, tm_e)
    tile_nsub = jnp.where(used, (valid_rows + sub - 1) // sub, 0).astype(jnp.int32)
    bounds = (counts.astype(jnp.int32), pad_start.astype(jnp.int32), pad_end.astype(jnp.int32))
    return bounds, row_of_assign, tile_e, tile_src, tile_nsub, n_tiles * tm_e


def kernel(x, positions, norm1_g, w_in, b_fgate, b_gate, lam_q1, lam_k1, lam_q2, lam_k2, subln_g,
           w_branch_fox, w_branch_diff, w_out, norm2_g, router_w, router_b, w_gu, b_gu, w_down,
           b_down, normf_g):
    b, s, d = x.shape
    t = b * s
    depth = norm1_g.shape[0]
    fox_w = FOX_HEADS * HEAD_DIM
    tm_a = _pick(t, 1024)
    tq = _pick(s, 512)
    tm_d = _pick(t, 256)
    tm_e, sub, tf = 2048, 256, 256
    tmc = _pick(t, 128)

    half = ROPE_DIM // 2
    inv_freq = ROPE_THETA ** (-jnp.arange(0, ROPE_DIM, 2, dtype=F32) / ROPE_DIM)
    zeros = jnp.zeros((LANES - ROPE_DIM,), F32)
    rope_tab = jnp.zeros((8, LANES), F32)
    rope_tab = rope_tab.at[0].set(jnp.concatenate([inv_freq, inv_freq, zeros]))
    rope_tab = rope_tab.at[1].set(jnp.concatenate([-jnp.ones((half,), F32), jnp.zeros((half,), F32), zeros]))
    rope_tab = rope_tab.at[2].set(jnp.concatenate([jnp.zeros((half,), F32), jnp.ones((half,), F32), zeros]))
    pos = positions.astype(F32).reshape(t, 1)

    h = x.reshape(t, d)
    for l in range(depth):
        wi = w_in[l]
        w_main = jnp.concatenate([wi[:, :3 * fox_w], wi[:, 3 * fox_w + FOX_HEADS:]], axis=1).astype(BF16)
        w_f = jnp.pad(wi[:, 3 * fox_w:3 * fox_w + FOX_HEADS], ((0, 0), (0, LANES - FOX_HEADS))).astype(BF16)
        qkv, gate, flog = _inproj(h, pos, norm1_g[l][None, :], w_main, w_f, rope_tab,
                                  b_gate[l][None, :], tm_a)

        fl = flog[:, :FOX_HEADS].reshape(b, s, FOX_HEADS).transpose(0, 2, 1).reshape(b * FOX_HEADS, s)
        bf = jnp.tile(b_fgate[l].astype(F32), b).reshape(b * FOX_HEADS, 1)
        c3 = _fgate(fl, bf).reshape(b * FOX_HEADS, 1, s)

        y_fox = _fox_attn(qkv, c3, b, s, tq)
        lam_init = 0.8 - 0.6 * math.exp(-0.3 * l)
        lam_p = jnp.stack([lam_q1[l], lam_k1[l], lam_q2[l], lam_k2[l]]).astype(F32)
        y_diff = _diff_attn(qkv, lam_p, subln_g[l][None, :].astype(F32), b, s, tq, lam_init)

        rw = jnp.pad(router_w[l], ((0, 0), (0, LANES - N_EXPERTS))).astype(BF16)
        rb = jnp.concatenate([router_b[l].astype(F32), jnp.full((LANES - N_EXPERTS,), NEG_BIG, F32)])[None, :]
        h, u2, idx128, gw128 = _merge(y_fox, y_diff, gate, h, w_branch_fox[l].astype(BF16),
                                      w_branch_diff[l].astype(BF16), w_out[l].astype(BF16),
                                      norm2_g[l][None, :], rw, rb, tm_d)

        bounds, roa, tile_e, tile_src, tile_nsub, n_rows = _routing(idx128[:, :TOP_K], tm_e, sub)
        roa3 = roa.reshape(t // tmc, tmc, TOP_K).transpose(0, 2, 1).reshape(t // tmc, 1, TOP_K * tmc)
        xs = _dispatch(*bounds, roa3, u2, n_rows, tmc, sub)
        y_rows = _experts(tile_e, tile_src, tile_nsub, xs, w_gu[l], b_gu[l][:, None, :],
                          w_down[l], b_down[l][:, None, :], tm_e, sub, tf)
        h = _combine(roa3, y_rows, h, gw128, normf_g[None, :], tmc, final_norm=(l == depth - 1))
    return h.reshape(b, s, d)
```

```python
import functools
import math

import jax
import jax.numpy as jnp
from jax import lax
from jax.experimental import pallas as pl
from jax.experimental.pallas import tpu as pltpu

HEAD_DIM = 128
FOX_HEADS = 8
DIFF_HEADS = 4
DIFF_V_DIM = 2 * HEAD_DIM
ROPE_THETA = 500000.0
ROPE_DIM = HEAD_DIM // 4
N_EXPERTS = 32
TOP_K = 4
SWIGLU_ALPHA = 1.702
SWIGLU_LIMIT = 7.0
EPS = 1e-5
LANES = 128

F32 = jnp.float32
BF16 = jnp.bfloat16
NEG_BIG = -1e30
LOG2E = math.log2(math.e)
ISSUE_UNROLL = 8
VMEM_LIMIT = 56 * 1024 * 1024


def _cparams(sem, vmem=VMEM_LIMIT):
    return pltpu.CompilerParams(dimension_semantics=sem, vmem_limit_bytes=vmem)


def _pick(n, pref):
    t = min(n, pref)
    while n % t:
        t //= 2
    return t


QKV_TILE = 1024
QKV_LEFT_TILES = 3
QKV_ROTARY_TILES = (3, 4)
QKV_SCALED_TILES = (0, 3)


def _qkv_kernel(x_ref, g1_ref, wa_ref, wb_ref, wf_ref, pos_ref, tab_ref,
                qkv_ref, u_ref, flog_ref, cos_sc, sa_sc, sb_sc, *, scale):
    j = pl.program_id(1)

    @pl.when(j == 0)
    def _():
        xf = x_ref[...]
        ms = jnp.mean(xf * xf, axis=-1, keepdims=True)
        u = (xf * lax.rsqrt(ms + EPS)) * g1_ref[...]
        ub = u.astype(BF16)
        u_ref[...] = ub
        flog_ref[...] = jnp.dot(ub, wf_ref[...], preferred_element_type=F32)
        ang = pos_ref[...] * tab_ref[0:1, :]
        sn = jnp.sin(ang)
        cos_sc[...] = jnp.cos(ang)
        sa_sc[...] = sn * tab_ref[1:2, :]
        sb_sc[...] = sn * tab_ref[2:3, :]

    any_of = lambda tiles: functools.reduce(jnp.logical_or, [j == q for q in tiles])
    mult = jnp.where(any_of(QKV_SCALED_TILES), scale, 1.0).astype(F32)
    rotary = any_of(QKV_ROTARY_TILES)
    left = j < QKV_LEFT_TILES

    def plain(w_ref):
        acc = jnp.dot(u_ref[...], w_ref[...], preferred_element_type=F32)
        qkv_ref[...] = (acc * mult).astype(BF16)

    pl.when(left)(functools.partial(plain, wa_ref))
    pl.when(jnp.logical_not(jnp.logical_or(left, rotary)))(functools.partial(plain, wb_ref))

    @pl.when(rotary)
    def _():
        acc = jnp.dot(u_ref[...], wb_ref[...], preferred_element_type=F32)
        ca = cos_sc[...] * mult
        cb = sa_sc[...] * mult
        cc = sb_sc[...] * mult
        for c in range(acc.shape[1] // HEAD_DIM):
            t = acc[:, c * HEAD_DIM:(c + 1) * HEAD_DIM]
            r = (t * ca + pltpu.roll(t, HEAD_DIM - ROPE_DIM // 2, 1) * cb
                 + pltpu.roll(t, ROPE_DIM // 2, 1) * cc)
            qkv_ref[:, c * HEAD_DIM:(c + 1) * HEAD_DIM] = r.astype(BF16)


def _gates_kernel(u_ref, w_ref, bg_ref, gate_ref):
    acc = jnp.dot(u_ref[...], w_ref[...], preferred_element_type=F32)
    gate_ref[...] = jax.nn.sigmoid(acc + bg_ref[...])


def _inproj(x2, pos, g1, w_left, w_right, w_f, rope_tab, b_gate, tm):
    t, d = x2.shape
    tn = QKV_TILE
    n_qkv = 6
    n_left = QKV_LEFT_TILES
    n_gate = w_right.shape[1] // tn - (n_qkv - n_left)
    kern = functools.partial(_qkv_kernel, scale=LOG2E / math.sqrt(HEAD_DIM))
    qkv, u, flog = pl.pallas_call(
        kern,
        grid=(t // tm, n_qkv),
        in_specs=[
            pl.BlockSpec((tm, d), lambda i, j: (i, 0)),
            pl.BlockSpec((1, d), lambda i, j: (0, 0)),
            pl.BlockSpec((d, tn), lambda i, j: (0, jnp.minimum(j, n_left - 1))),
            pl.BlockSpec((d, tn), lambda i, j: (0, jnp.maximum(j - n_left, 0))),
            pl.BlockSpec((d, LANES), lambda i, j: (0, 0)),
            pl.BlockSpec((tm, 1), lambda i, j: (i, 0)),
            pl.BlockSpec((8, LANES), lambda i, j: (0, 0)),
        ],
        out_specs=[
            pl.BlockSpec((tm, tn), lambda i, j: (i, j)),
            pl.BlockSpec((tm, d), lambda i, j: (i, 0)),
            pl.BlockSpec((tm, LANES), lambda i, j: (i, 0)),
        ],
        out_shape=[
            jax.ShapeDtypeStruct((t, n_qkv * tn), BF16),
            jax.ShapeDtypeStruct((t, d), BF16),
            jax.ShapeDtypeStruct((t, LANES), F32),
        ],
        scratch_shapes=[
            pltpu.VMEM((tm, LANES), F32),
            pltpu.VMEM((tm, LANES), F32),
            pltpu.VMEM((tm, LANES), F32),
        ],
        compiler_params=_cparams(("parallel", "arbitrary")),
    )(x2, g1, w_left, w_right, w_f, pos, rope_tab)
    gate = pl.pallas_call(
        _gates_kernel,
        grid=(t // tm, n_gate),
        in_specs=[
            pl.BlockSpec((tm, d), lambda i, j: (i, 0)),
            pl.BlockSpec((d, tn), lambda i, j: (0, n_qkv - n_left + j)),
            pl.BlockSpec((1, tn), lambda i, j: (0, j)),
        ],
        out_specs=pl.BlockSpec((tm, tn), lambda i, j: (i, j)),
        out_shape=jax.ShapeDtypeStruct((t, n_gate * tn), F32),
        compiler_params=_cparams(("parallel", "arbitrary")),
    )(u, w_right, b_gate)
    return qkv, gate, flog


def _fgate_kernel(fl_ref, b_ref, c_ref):
    z = fl_ref[...] + b_ref[...]
    x = jnp.minimum(z, 0.0) - jnp.log(1.0 + jnp.exp(-jnp.abs(z)))
    n = x.shape[1]
    lane = lax.broadcasted_iota(jnp.int32, x.shape, 1)
    sh = 1
    while sh < n:
        x = x + jnp.where(lane >= sh, pltpu.roll(x, sh, 1), 0.0)
        sh *= 2
    c_ref[...] = x


def _fgate(fl, b):
    r, s = fl.shape
    return pl.pallas_call(
        _fgate_kernel,
        grid=(1,),
        in_specs=[pl.BlockSpec((r, s), lambda i: (0, 0)), pl.BlockSpec((r, 1), lambda i: (0, 0))],
        out_specs=pl.BlockSpec((r, s), lambda i: (0, 0)),
        out_shape=jax.ShapeDtypeStruct((r, s), F32),
        compiler_params=_cparams(("arbitrary",)),
    )(fl, b)


def _transpose_bf16(x):
    return x.astype(F32).T.astype(BF16)


def _softmax_step(s, vt, m_sc, l_sc, acc_sc):
    m_old = m_sc[...]
    m_new = jnp.maximum(m_old, jnp.max(s, axis=0, keepdims=True))
    alpha = jnp.exp2(m_old - m_new)
    p = jnp.exp2(s - m_new)
    l_sc[...] = alpha * l_sc[...] + jnp.sum(p, axis=0, keepdims=True)
    acc_sc[...] = alpha * acc_sc[...] + jnp.dot(vt, p.astype(BF16), preferred_element_type=F32)
    m_sc[...] = m_new


def _causal_keep(shape):
    kv = lax.broadcasted_iota(jnp.int32, shape, 0)
    qq = lax.broadcasted_iota(jnp.int32, shape, 1)
    return kv <= qq


FOX_HEADS_PER_STEP = 2


def _fox_kernel(q_ref, k_ref, v_ref, c_ref, o_ref, *scratch, tq, tk, s_len):
    hp = FOX_HEADS_PER_STEP
    qt_sc, kaug_sc, vt_sc, m_sc, l_sc, acc_sc = (scratch[n * hp:(n + 1) * hp] for n in range(6))
    s_sc = scratch[6 * hp:]
    qi = pl.program_id(2)
    head_cols = [slice(h * HEAD_DIM, (h + 1) * HEAD_DIM) for h in range(hp)]

    @pl.when(qi == 0)
    def _():
        sub = lax.broadcasted_iota(jnp.int32, (HEAD_DIM, tk), 0)
        for h in range(hp):
            for ch in range(s_len // tk):
                rows = slice(ch * tk, (ch + 1) * tk)
                vt_sc[h][:, rows] = _transpose_bf16(v_ref[rows, head_cols[h]])
                c = c_ref[h, :, rows] * (-LOG2E)
                hi = c.astype(BF16).astype(F32)
                mid = (c - hi).astype(BF16).astype(F32)
                lo = (c - hi - mid).astype(BF16).astype(F32)
                tab = jnp.where(sub == 0, hi, jnp.where(sub == 1, mid, jnp.where(sub == 2, lo, 0.0)))
                kaug_sc[h][rows, HEAD_DIM:] = tab.T.astype(BF16)
                kaug_sc[h][rows, :HEAD_DIM] = k_ref[rows, head_cols[h]]

    sub_q = lax.broadcasted_iota(jnp.int32, (HEAD_DIM, tq), 0)
    for h in range(hp):
        qt_sc[h][:HEAD_DIM, :] = _transpose_bf16(q_ref[:, head_cols[h]])
        qt_sc[h][HEAD_DIM:, :] = jnp.where(sub_q < 3, 1.0, 0.0).astype(BF16)
        m_sc[h][...] = jnp.full(m_sc[h].shape, -jnp.inf, F32)
        l_sc[h][...] = jnp.zeros(l_sc[h].shape, F32)
        acc_sc[h][...] = jnp.zeros(acc_sc[h].shape, F32)

    def scores(h, chunk):
        start = pl.multiple_of(chunk * tk, tk)
        return jnp.dot(kaug_sc[h][pl.ds(start, tk), :], qt_sc[h][...], preferred_element_type=F32)

    def consume(h, chunk, par, masked):
        s = s_sc[par * hp + h][...]
        if masked:
            s = jnp.where(_causal_keep(s.shape), s, -jnp.inf)
        start = pl.multiple_of(chunk * tk, tk)
        _softmax_step(s, vt_sc[h][:, pl.ds(start, tk)], m_sc[h], l_sc[h], acc_sc[h])

    for h in range(hp):
        s_sc[h][...] = scores(h, 0)

    def body(kc, carry):
        def run(par):
            for h in range(hp):
                s_sc[(1 - par) * hp + h][...] = scores(h, kc + 1)
                consume(h, kc, par, False)
        pl.when(kc % 2 == 0)(functools.partial(run, 0))
        pl.when(kc % 2 == 1)(functools.partial(run, 1))
        return carry

    lax.fori_loop(0, qi, body, 0)

    def diagonal(par):
        for h in range(hp):
            consume(h, qi, par, True)
    pl.when(qi % 2 == 0)(functools.partial(diagonal, 0))
    pl.when(qi % 2 == 1)(functools.partial(diagonal, 1))
    for h in range(hp):
        o_ref[:, head_cols[h]] = (acc_sc[h][...] / l_sc[h][...]).T.astype(o_ref.dtype)


def _fox_attn(qkv, c3, b, s, tq):
    t = b * s
    nq = s // tq
    hp = FOX_HEADS_PER_STEP
    groups = FOX_HEADS // hp
    width = hp * HEAD_DIM
    kern = functools.partial(_fox_kernel, tq=tq, tk=tq, s_len=s)

    def per_head(shape, dtype):
        return [pltpu.VMEM(shape, dtype) for _ in range(hp)]

    return pl.pallas_call(
        kern,
        grid=(b, groups, nq),
        in_specs=[
            pl.BlockSpec((tq, width), lambda bi, g, qi: (bi * nq + qi, g)),
            pl.BlockSpec((s, width), lambda bi, g, qi: (bi, groups + g)),
            pl.BlockSpec((s, width), lambda bi, g, qi: (bi, 2 * groups + g)),
            pl.BlockSpec((hp, 1, s), lambda bi, g, qi: (bi * groups + g, 0, 0)),
        ],
        out_specs=pl.BlockSpec((tq, width), lambda bi, g, qi: (bi * nq + qi, g)),
        out_shape=jax.ShapeDtypeStruct((t, FOX_HEADS * HEAD_DIM), BF16),
        scratch_shapes=(per_head((2 * HEAD_DIM, tq), BF16) + per_head((s, 2 * HEAD_DIM), BF16)
                        + per_head((HEAD_DIM, s), BF16) + per_head((1, tq), F32)
                        + per_head((1, tq), F32) + per_head((HEAD_DIM, tq), F32)
                        + per_head((tq, tq), F32) + per_head((tq, tq), F32)),
        compiler_params=_cparams(("parallel", "parallel", "arbitrary")),
    )(qkv, qkv, qkv, c3)


def _diff_kernel(q1_ref, q2_ref, k1_ref, k2_ref, v_ref, lam_ref, g_ref, o_ref,
                 q1t_sc, q2t_sc, vt_sc, m1_sc, l1_sc, a1_sc, m2_sc, l2_sc, a2_sc, *s_sc,
                 tq, tk, s_len, lam_init):
    qi = pl.program_id(2)

    @pl.when(qi == 0)
    def _():
        for ch in range(s_len // tk):
            rows = slice(ch * tk, (ch + 1) * tk)
            vt_sc[:, rows] = _transpose_bf16(v_ref[rows, :])

    q1t_sc[...] = _transpose_bf16(q1_ref[...])
    q2t_sc[...] = _transpose_bf16(q2_ref[...])
    for r in (m1_sc, m2_sc):
        r[...] = jnp.full(r.shape, -jnp.inf, F32)
    for r in (l1_sc, l2_sc, a1_sc, a2_sc):
        r[...] = jnp.zeros(r.shape, F32)

    chains = ((k1_ref, q1t_sc, m1_sc, l1_sc, a1_sc), (k2_ref, q2t_sc, m2_sc, l2_sc, a2_sc))

    def scores(c, chunk):
        start = pl.multiple_of(chunk * tk, tk)
        return jnp.dot(chains[c][0][pl.ds(start, tk), :], chains[c][1][...], preferred_element_type=F32)

    def consume(c, chunk, par, masked):
        s = s_sc[par * 2 + c][...]
        if masked:
            s = jnp.where(_causal_keep(s.shape), s, -jnp.inf)
        start = pl.multiple_of(chunk * tk, tk)
        _softmax_step(s, vt_sc[:, pl.ds(start, tk)], *chains[c][2:])

    for c in range(2):
        s_sc[c][...] = scores(c, 0)

    def body(kc, carry):
        def run(par):
            for c in range(2):
                s_sc[(1 - par) * 2 + c][...] = scores(c, kc + 1)
                consume(c, kc, par, False)
        pl.when(kc % 2 == 0)(functools.partial(run, 0))
        pl.when(kc % 2 == 1)(functools.partial(run, 1))
        return carry

    lax.fori_loop(0, qi, body, 0)

    def diagonal(par):
        for c in range(2):
            consume(c, qi, par, True)
    pl.when(qi % 2 == 0)(functools.partial(diagonal, 0))
    pl.when(qi % 2 == 1)(functools.partial(diagonal, 1))

    lp = lam_ref[...]
    lam = (jnp.exp(jnp.sum(lp[0:1, :] * lp[1:2, :], axis=-1, keepdims=True))
           - jnp.exp(jnp.sum(lp[2:3, :] * lp[3:4, :], axis=-1, keepdims=True)) + lam_init)
    y = (a1_sc[...] / l1_sc[...]).T - lam * (a2_sc[...] / l2_sc[...]).T
    ms = jnp.mean(y * y, axis=-1, keepdims=True)
    y = (y * lax.rsqrt(ms + EPS)) * g_ref[...]
    o_ref[...] = (y * (1.0 - lam_init)).astype(o_ref.dtype)


def _diff_attn(qkv, lam_p, subln_g, b, s, tq, lam_init):
    t = b * s
    nq = s // tq
    qb = 3 * FOX_HEADS
    kb = qb + 2 * DIFF_HEADS
    vb = (kb + 2 * DIFF_HEADS) // 2
    kern = functools.partial(_diff_kernel, tq=tq, tk=tq, s_len=s, lam_init=lam_init)
    return pl.pallas_call(
        kern,
        grid=(b, DIFF_HEADS, nq),
        in_specs=[
            pl.BlockSpec((tq, HEAD_DIM), lambda bi, h, qi: (bi * nq + qi, qb + 2 * h)),
            pl.BlockSpec((tq, HEAD_DIM), lambda bi, h, qi: (bi * nq + qi, qb + 2 * h + 1)),
            pl.BlockSpec((s, HEAD_DIM), lambda bi, h, qi: (bi, kb + 2 * h)),
            pl.BlockSpec((s, HEAD_DIM), lambda bi, h, qi: (bi, kb + 2 * h + 1)),
            pl.BlockSpec((s, DIFF_V_DIM), lambda bi, h, qi: (bi, vb + h)),
            pl.BlockSpec((4, HEAD_DIM), lambda bi, h, qi: (0, 0)),
            pl.BlockSpec((1, DIFF_V_DIM), lambda bi, h, qi: (0, 0)),
        ],
        out_specs=pl.BlockSpec((tq, DIFF_V_DIM), lambda bi, h, qi: (bi * nq + qi, h)),
        out_shape=jax.ShapeDtypeStruct((t, DIFF_HEADS * DIFF_V_DIM), BF16),
        scratch_shapes=[
            pltpu.VMEM((HEAD_DIM, tq), BF16), pltpu.VMEM((HEAD_DIM, tq), BF16),
            pltpu.VMEM((DIFF_V_DIM, s), BF16),
            pltpu.VMEM((1, tq), F32), pltpu.VMEM((1, tq), F32), pltpu.VMEM((DIFF_V_DIM, tq), F32),
            pltpu.VMEM((1, tq), F32), pltpu.VMEM((1, tq), F32), pltpu.VMEM((DIFF_V_DIM, tq), F32),
        ] + [pltpu.VMEM((tq, tq), F32) for _ in range(4)],
        compiler_params=_cparams(("parallel", "parallel", "arbitrary")),
    )(qkv, qkv, qkv, qkv, qkv, lam_p, subln_g)


def _pack_bf16_pairs(xb):
    w = xb.shape[1] // 2
    lo = lax.bitcast_convert_type(xb[:, :w].astype(F32), jnp.uint32)
    hi = lax.bitcast_convert_type(xb[:, w:].astype(F32), jnp.uint32)
    return (lo >> 16) | (hi & jnp.uint32(0xFFFF0000))


def _unpack_bf16_pairs(p):
    lo = lax.bitcast_convert_type(p << 16, F32).astype(BF16)
    hi = lax.bitcast_convert_type(p & jnp.uint32(0xFFFF0000), F32).astype(BF16)
    return lo, hi


def _merge_kernel(yf_ref, yd_ref, g0_ref, g1_ref, x_ref, wbf_ref, wbd_ref, wo_ref, n2_ref,
                  rw_ref, rb_ref, h_ref, u_ref, idx_ref, gw_ref):
    ya = jnp.dot(yf_ref[...], wbf_ref[...], preferred_element_type=F32)
    yb = jnp.dot(yd_ref[...], wbd_ref[...], preferred_element_type=F32)
    merged = g0_ref[...] * ya + g1_ref[...] * yb
    h = x_ref[...] + jnp.dot(merged.astype(BF16), wo_ref[...], preferred_element_type=F32)
    h_ref[...] = h
    ms = jnp.mean(h * h, axis=-1, keepdims=True)
    u = (h * lax.rsqrt(ms + EPS)) * n2_ref[...]
    ub = u.astype(BF16)
    u_ref[...] = _pack_bf16_pairs(ub)
    logits = jnp.dot(ub, rw_ref[...], preferred_element_type=F32) + rb_ref[...]

    lane = lax.broadcasted_iota(jnp.int32, logits.shape, 1).astype(F32)
    work = logits
    vals, idxs = [], []
    for _ in range(TOP_K):
        m = jnp.max(work, axis=-1, keepdims=True)
        sel = jnp.min(jnp.where(work == m, lane, float(LANES)), axis=-1, keepdims=True)
        vals.append(m)
        idxs.append(sel)
        work = jnp.where(lane == sel, -jnp.inf, work)
    es = [jnp.exp(v - vals[0]) for v in vals]
    den = es[0]
    for e in es[1:]:
        den = den + e
    idx_out = jnp.zeros(logits.shape, F32)
    gw_out = jnp.zeros(logits.shape, F32)
    for k in range(TOP_K):
        idx_out = jnp.where(lane == float(k), idxs[k], idx_out)
        gw_out = jnp.where(lane == float(k), es[k] / den, gw_out)
    idx_ref[...] = idx_out.astype(jnp.int32)
    gw_ref[...] = gw_out


def _merge(yf, yd, gate, x2, wbf, wbd, wo, n2, rw, rb, tm):
    t, d = x2.shape
    fw = yf.shape[1]
    dw = yd.shape[1]
    const = dict(pipeline_mode=pl.Buffered(1))
    return pl.pallas_call(
        _merge_kernel,
        grid=(t // tm,),
        in_specs=[
            pl.BlockSpec((tm, fw), lambda i: (i, 0)),
            pl.BlockSpec((tm, dw), lambda i: (i, 0)),
            pl.BlockSpec((tm, d), lambda i: (i, 0)),
            pl.BlockSpec((tm, d), lambda i: (i, 1)),
            pl.BlockSpec((tm, d), lambda i: (i, 0)),
            pl.BlockSpec((fw, d), lambda i: (0, 0), **const),
            pl.BlockSpec((dw, d), lambda i: (0, 0), **const),
            pl.BlockSpec((d, d), lambda i: (0, 0), **const),
            pl.BlockSpec((1, d), lambda i: (0, 0)),
            pl.BlockSpec((d, LANES), lambda i: (0, 0), **const),
            pl.BlockSpec((1, LANES), lambda i: (0, 0)),
        ],
        out_specs=[
            pl.BlockSpec((tm, d), lambda i: (i, 0)),
            pl.BlockSpec((tm, d // 2), lambda i: (i, 0)),
            pl.BlockSpec((tm, LANES), lambda i: (i, 0)),
            pl.BlockSpec((tm, LANES), lambda i: (i, 0)),
        ],
        out_shape=[
            jax.ShapeDtypeStruct((t, d), F32),
            jax.ShapeDtypeStruct((t, d // 2), jnp.uint32),
            jax.ShapeDtypeStruct((t, LANES), jnp.int32),
            jax.ShapeDtypeStruct((t, LANES), F32),
        ],
        compiler_params=_cparams(("parallel",)),
    )(yf, yd, gate, gate, x2, wbf, wbd, wo, n2, rw, rb)


def _dispatch_kernel(cnt_ref, pstart_ref, pend_ref, roa_ref, u_ref, xs_hbm, stage, zeros, sems,
                     *, tb, sub, n_rows):
    s = pl.program_id(0)
    n = pl.num_programs(0)
    slot = s % 2
    n_dma = TOP_K * tb

    def wait_slot(sl):
        pltpu.make_async_copy(stage.at[sl], xs_hbm.at[pl.ds(0, n_dma), :], sems.at[sl]).wait()

    @pl.when(s == 0)
    def _():
        zeros[...] = jnp.zeros(zeros.shape, zeros.dtype)

    @pl.when(s >= 2)
    def _():
        wait_slot(slot)

    blk = u_ref[...]
    for k in range(TOP_K):
        stage[slot, pl.ds(k * tb, tb), :] = blk

    def body(g, carry):
        for q8 in range(ISSUE_UNROLL):
            q = g * ISSUE_UNROLL + q8
            pltpu.make_async_copy(stage.at[slot, pl.ds(q, 1), :],
                                  xs_hbm.at[pl.ds(roa_ref[0, 0, q], 1), :], sems.at[slot]).start()
        return carry
    lax.fori_loop(0, n_dma // ISSUE_UNROLL, body, 0)

    @pl.when(s == n - 1)
    def _():
        @pl.when(n >= 2)
        def _():
            wait_slot(1 - slot)
        wait_slot(slot)

        def zero_row(r):
            return pltpu.make_async_copy(zeros.at[pl.ds(0, 1), :], xs_hbm.at[pl.ds(r, 1), :], sems.at[2])

        def zero_block(r):
            return pltpu.make_async_copy(zeros, xs_hbm.at[pl.ds(pl.multiple_of(r, sub), sub), :], sems.at[2])

        def fill(first, stop, copy, step):
            cnt = (stop - first) // step

            def start(q, carry):
                copy(first + q * step).start()
                return carry

            def wait(q, carry):
                copy(first).wait()
                return carry
            lax.fori_loop(0, cnt, start, 0)
            lax.fori_loop(0, cnt, wait, 0)

        def per_expert(e, carry):
            first = pstart_ref[e] + cnt_ref[e]
            edge = (first + sub - 1) // sub * sub
            fill(first, edge, zero_row, 1)
            fill(edge, pend_ref[e], zero_block, sub)
            return carry
        lax.fori_loop(0, N_EXPERTS, per_expert, 0)
        fill(pend_ref[N_EXPERTS - 1], n_rows, zero_block, sub)


def _dispatch(counts, pad_start, pad_end, roa3, u_packed, n_rows, tb, sub):
    t, w = u_packed.shape
    kern = functools.partial(_dispatch_kernel, tb=tb, sub=sub, n_rows=n_rows)
    return pl.pallas_call(
        kern,
        grid_spec=pltpu.PrefetchScalarGridSpec(
            num_scalar_prefetch=3,
            grid=(t // tb,),
            in_specs=[
                pl.BlockSpec((1, 1, TOP_K * tb), lambda s, c, ps, pe: (s, 0, 0), memory_space=pltpu.SMEM),
                pl.BlockSpec((tb, w), lambda s, c, ps, pe: (s, 0)),
            ],
            out_specs=pl.BlockSpec(memory_space=pl.ANY),
            scratch_shapes=[
                pltpu.VMEM((2, TOP_K * tb, w), jnp.uint32),
                pltpu.VMEM((sub, w), jnp.uint32),
                pltpu.SemaphoreType.DMA((3,)),
            ],
        ),
        out_shape=jax.ShapeDtypeStruct((n_rows, w), jnp.uint32),
        compiler_params=_cparams(("arbitrary",)),
    )(counts, pad_start, pad_end, roa3, u_packed)


def _expert_kernel(te_ref, ts_ref, nsub_ref, xs_ref, wg_ref, wu_ref, bg_ref, bu_ref, wd_ref,
                   bd_ref, out_ref, x_sc, act_sc, *, sub, spt, nf):
    i = pl.program_id(0)
    j = pl.program_id(1)
    nsub = nsub_ref[i]
    half = x_sc.shape[1] // 2

    @pl.when(jnp.logical_and(j == 0, nsub > 0))
    def _():
        lo, hi = _unpack_bf16_pairs(xs_ref[...])
        x_sc[:, :half] = lo
        x_sc[:, half:] = hi

    full = nsub == spt

    def gate_up(rs, wg, wu):
        x = x_sc[rs, :]
        hg = jnp.dot(x, wg, preferred_element_type=F32) + bg_ref[0]
        hu = jnp.dot(x, wu, preferred_element_type=F32) + bu_ref[0]
        hg = jnp.minimum(hg, SWIGLU_LIMIT)
        hl = jnp.clip(hu, -SWIGLU_LIMIT, SWIGLU_LIMIT)
        act = hg * jax.nn.sigmoid(SWIGLU_ALPHA * hg) * (hl + 1.0)
        act_sc[j, rs, :] = act.astype(BF16)

    def down(rs, wd):
        act = jnp.concatenate([act_sc[c, rs, :] for c in range(nf)], axis=1)
        out_ref[rs, :] = jnp.dot(act, wd, preferred_element_type=F32) + bd_ref[0]

    partial_tile = jnp.logical_and(nsub > 0, nsub < spt)

    @pl.when(jnp.logical_and(j < nf, full))
    def _():
        gate_up(slice(None), wg_ref[0].astype(BF16), wu_ref[0].astype(BF16))

    @pl.when(jnp.logical_and(j < nf, partial_tile))
    def _():
        wg = wg_ref[0].astype(BF16)
        wu = wu_ref[0].astype(BF16)
        for s in range(spt - 1):
            pl.when(s < nsub)(functools.partial(gate_up, slice(s * sub, (s + 1) * sub), wg, wu))

    @pl.when(jnp.logical_and(j >= nf, full))
    def _():
        down(slice(None), wd_ref[0].astype(BF16))

    @pl.when(jnp.logical_and(j >= nf, partial_tile))
    def _():
        wd = wd_ref[0].astype(BF16)
        for s in range(spt - 1):
            pl.when(s < nsub)(functools.partial(down, slice(s * sub, (s + 1) * sub), wd))

    @pl.when(jnp.logical_and(j >= nf, nsub < spt))
    def _():
        for s in range(spt):
            @pl.when(s >= nsub)
            def _():
                out_ref[s * sub:(s + 1) * sub, :] = jnp.zeros((sub, out_ref.shape[1]), out_ref.dtype)


def _experts(tile_e, tile_src, tile_nsub, xs, w_gu, b_gu3, w_down, b_down3, tm_e, sub, tf):
    n_rows = xs.shape[0]
    n_tiles = n_rows // tm_e
    d = 2 * xs.shape[1]
    dff = w_down.shape[1]
    nf = dff // tf
    nn = d // tf
    assert nn == nf
    kern = functools.partial(_expert_kernel, sub=sub, spt=tm_e // sub, nf=nf)

    def c1(i, j, ns):
        return jnp.where(ns[i] > 0, jnp.minimum(j, nf - 1), nf - 1)

    def c2(i, j, ns):
        return jnp.where(ns[i] > 0, jnp.maximum(j - nf, 0), nf - 1)

    return pl.pallas_call(
        kern,
        grid_spec=pltpu.PrefetchScalarGridSpec(
            num_scalar_prefetch=3,
            grid=(n_tiles, 2 * nf),
            in_specs=[
                pl.BlockSpec((tm_e, d // 2), lambda i, j, te, ts, ns: (ts[i], 0)),
                pl.BlockSpec((1, d, tf), lambda i, j, te, ts, ns: (te[i], 0, c1(i, j, ns))),
                pl.BlockSpec((1, d, tf), lambda i, j, te, ts, ns: (te[i], 0, nf + c1(i, j, ns))),
                pl.BlockSpec((1, 1, tf), lambda i, j, te, ts, ns: (te[i], 0, c1(i, j, ns))),
                pl.BlockSpec((1, 1, tf), lambda i, j, te, ts, ns: (te[i], 0, nf + c1(i, j, ns))),
                pl.BlockSpec((1, dff, tf), lambda i, j, te, ts, ns: (te[i], 0, c2(i, j, ns))),
                pl.BlockSpec((1, 1, tf), lambda i, j, te, ts, ns: (te[i], 0, c2(i, j, ns))),
            ],
            out_specs=pl.BlockSpec((tm_e, tf), lambda i, j, te, ts, ns: (i, jnp.maximum(j - nf, 0))),
            scratch_shapes=[
                pltpu.VMEM((tm_e, d), BF16),
                pltpu.VMEM((nf, tm_e, tf), BF16),
            ],
        ),
        out_shape=jax.ShapeDtypeStruct((n_rows, d), F32),
        compiler_params=_cparams(("arbitrary", "arbitrary")),
    )(tile_e, tile_src, tile_nsub, xs, w_gu, w_gu, b_gu3, b_gu3, w_down, b_down3)


def _combine_kernel(cur_ref, nxt_ref, yr_hbm, h_ref, gw_ref, gf_ref, out_ref, ysel, sems, *, tmc,
                    final_norm):
    i = pl.program_id(0)
    n = pl.num_programs(0)
    slot = i % 2
    n_dma = TOP_K * tmc

    def row_copy(row, q, sl):
        return pltpu.make_async_copy(yr_hbm.at[pl.ds(row, 1), :], ysel.at[sl, pl.ds(q, 1), :], sems.at[sl])

    def issue(idx_ref, sl):
        def body(g, carry):
            for q8 in range(ISSUE_UNROLL):
                q = g * ISSUE_UNROLL + q8
                row_copy(idx_ref[0, 0, q], q, sl).start()
            return carry
        lax.fori_loop(0, n_dma // ISSUE_UNROLL, body, 0)

    @pl.when(i == 0)
    def _():
        issue(cur_ref, 0)

    @pl.when(i + 1 < n)
    def _():
        issue(nxt_ref, 1 - slot)

    pltpu.make_async_copy(yr_hbm.at[pl.ds(0, n_dma), :], ysel.at[slot], sems.at[slot]).wait()

    gw = gw_ref[...]
    y = h_ref[...]
    for k in range(TOP_K):
        y = y + gw[:, k:k + 1] * ysel[slot, k * tmc:(k + 1) * tmc, :]
    if final_norm:
        ms = jnp.mean(y * y, axis=-1, keepdims=True)
        y = (y * lax.rsqrt(ms + EPS)) * gf_ref[...]
    out_ref[...] = y


def _combine(roa3, y_rows, h, gw, gf, tmc, final_norm):
    t, d = h.shape
    n = t // tmc
    kern = functools.partial(_combine_kernel, tmc=tmc, final_norm=final_norm)
    return pl.pallas_call(
        kern,
        grid=(n,),
        in_specs=[
            pl.BlockSpec((1, 1, TOP_K * tmc), lambda i: (i, 0, 0), memory_space=pltpu.SMEM),
            pl.BlockSpec((1, 1, TOP_K * tmc), lambda i: (jnp.minimum(i + 1, n - 1), 0, 0),
                         memory_space=pltpu.SMEM),
            pl.BlockSpec(memory_space=pl.ANY),
            pl.BlockSpec((tmc, d), lambda i: (i, 0)),
            pl.BlockSpec((tmc, LANES), lambda i: (i, 0)),
            pl.BlockSpec((1, d), lambda i: (0, 0)),
        ],
        out_specs=pl.BlockSpec((tmc, d), lambda i: (i, 0)),
        out_shape=jax.ShapeDtypeStruct((t, d), F32),
        scratch_shapes=[pltpu.VMEM((2, TOP_K * tmc, d), F32), pltpu.SemaphoreType.DMA((2,))],
        compiler_params=_cparams(("arbitrary",)),
    )(roa3, roa3, y_rows, h, gw, gf)


def _routing(top_idx, tm_e, sub):
    t = top_idx.shape[0]
    a = t * TOP_K
    flat_e = top_idx.reshape(a)
    onehot = (flat_e[:, None] == jnp.arange(N_EXPERTS, dtype=jnp.int32)[None, :]).astype(jnp.int32)
    csum = jnp.cumsum(onehot, axis=0)
    counts = csum[-1]
    rank = jnp.sum(onehot * csum, axis=1) - 1
    padded = (counts + tm_e - 1) // tm_e * tm_e
    pad_end = jnp.cumsum(padded)
    pad_start = pad_end - padded
    row_of_assign = (pad_start[flat_e] + rank).astype(jnp.int32)
    n_tiles = -(-a // tm_e) + N_EXPERTS
    tile_start = jnp.arange(n_tiles, dtype=jnp.int32) * tm_e
    n_used = pad_end[-1] // tm_e
    tile_e_raw = jnp.minimum(jnp.searchsorted(pad_end, tile_start, side='right'), N_EXPERTS - 1)
    used = tile_start < pad_end[-1]
    last = jnp.maximum(n_used - 1, 0)
    tile_src = jnp.where(used, jnp.arange(n_tiles, dtype=jnp.int32), last).astype(jnp.int32)
    tile_e = tile_e_raw[tile_src].astype(jnp.int32)
    valid_rows = jnp.clip(counts[tile_e_raw] - (tile_start - pad_start[tile_e_raw]), 0, tm_e)
    tile_nsub = jnp.where(used, (valid_rows + sub - 1) // sub, 0).astype(jnp.int32)
    bounds = (counts.astype(jnp.int32), pad_start.astype(jnp.int32), pad_end.astype(jnp.int32))
    return bounds, row_of_assign, tile_e, tile_src, tile_nsub, n_tiles * tm_e


def kernel(x, positions, norm1_g, w_in, b_fgate, b_gate, lam_q1, lam_k1, lam_q2, lam_k2, subln_g,
           w_branch_fox, w_branch_diff, w_out, norm2_g, router_w, router_b, w_gu, b_gu, w_down,
           b_down, normf_g):
    b, s, d = x.shape
    t = b * s
    depth = norm1_g.shape[0]
    fox_w = FOX_HEADS * HEAD_DIM
    tm_a = _pick(t, 1024)
    tq = _pick(s, 512)
    tm_d = _pick(t, 256)
    tm_e, sub, tf = 1024, 256, 512
    tmc = _pick(t, 128)

    half = ROPE_DIM // 2
    inv_freq = ROPE_THETA ** (-jnp.arange(0, ROPE_DIM, 2, dtype=F32) / ROPE_DIM)
    zeros = jnp.zeros((LANES - ROPE_DIM,), F32)
    rope_tab = jnp.zeros((8, LANES), F32)
    rope_tab = rope_tab.at[0].set(jnp.concatenate([inv_freq, inv_freq, zeros]))
    rope_tab = rope_tab.at[1].set(jnp.concatenate([-jnp.ones((half,), F32), jnp.zeros((half,), F32), zeros]))
    rope_tab = rope_tab.at[2].set(jnp.concatenate([jnp.zeros((half,), F32), jnp.ones((half,), F32), zeros]))
    pos = positions.astype(F32).reshape(t, 1)

    h = x.reshape(t, d)
    for l in range(depth):
        wi = w_in[l]
        w_left = wi[:, :3 * fox_w].astype(BF16)
        w_right = wi[:, 3 * fox_w + FOX_HEADS:].astype(BF16)
        w_f = jnp.pad(wi[:, 3 * fox_w:3 * fox_w + FOX_HEADS], ((0, 0), (0, LANES - FOX_HEADS))).astype(BF16)
        qkv, gate, flog = _inproj(h, pos, norm1_g[l][None, :], w_left, w_right, w_f, rope_tab,
                                  b_gate[l][None, :], tm_a)

        fl = flog[:, :FOX_HEADS].reshape(b, s, FOX_HEADS).transpose(0, 2, 1).reshape(b * FOX_HEADS, s)
        bf = jnp.tile(b_fgate[l].astype(F32), b).reshape(b * FOX_HEADS, 1)
        c3 = _fgate(fl, bf).reshape(b * FOX_HEADS, 1, s)

        y_fox = _fox_attn(qkv, c3, b, s, tq)
        lam_init = 0.8 - 0.6 * math.exp(-0.3 * l)
        lam_p = jnp.stack([lam_q1[l], lam_k1[l], lam_q2[l], lam_k2[l]]).astype(F32)
        y_diff = _diff_attn(qkv, lam_p, subln_g[l][None, :].astype(F32), b, s, tq, lam_init)

        rw = jnp.pad(router_w[l], ((0, 0), (0, LANES - N_EXPERTS))).astype(BF16)
        rb = jnp.concatenate([router_b[l].astype(F32), jnp.full((LANES - N_EXPERTS,), NEG_BIG, F32)])[None, :]
        h, u2, idx128, gw128 = _merge(y_fox, y_diff, gate, h, w_branch_fox[l].astype(BF16),
                                      w_branch_diff[l].astype(BF16), w_out[l].astype(BF16),
                                      norm2_g[l][None, :], rw, rb, tm_d)

        bounds, roa, tile_e, tile_src, tile_nsub, n_rows = _routing(idx128[:, :TOP_K], tm_e, sub)
        roa3 = roa.reshape(t // tmc, tmc, TOP_K).transpose(0, 2, 1).reshape(t // tmc, 1, TOP_K * tmc)
        xs = _dispatch(*bounds, roa3, u2, n_rows, tmc, sub)
        y_rows = _experts(tile_e, tile_src, tile_nsub, xs, w_gu[l], b_gu[l][:, None, :],
                          w_down[l], b_down[l][:, None, :], tm_e, sub, tf)
        h = _combine(roa3, y_rows, h, gw128, normf_g[None, :], tmc, final_norm=(l == depth - 1))
    return h.reshape(b, s, d)
```

```python
import functools
import math

import jax
import jax.numpy as jnp
from jax import lax
from jax.experimental import pallas as pl
from jax.experimental.pallas import tpu as pltpu

HEAD_DIM = 128
FOX_HEADS = 8
DIFF_HEADS = 4
DIFF_V_DIM = 2 * HEAD_DIM
ROPE_THETA = 500000.0
ROPE_DIM = HEAD_DIM // 4
N_EXPERTS = 32
TOP_K = 4
SWIGLU_ALPHA = 1.702
SWIGLU_LIMIT = 7.0
EPS = 1e-5
LANES = 128

F32 = jnp.float32
BF16 = jnp.bfloat16
NEG_BIG = -1e30
LOG2E = math.log2(math.e)
ISSUE_UNROLL = 8
VMEM_LIMIT = 56 * 1024 * 1024


def _cparams(sem, vmem=VMEM_LIMIT):
    return pltpu.CompilerParams(dimension_semantics=sem, vmem_limit_bytes=vmem)


def _pick(n, pref):
    t = min(n, pref)
    while n % t:
        t //= 2
    return t


QKV_TILE = 1024
QKV_LEFT_TILES = 3
QKV_ROTARY_TILES = (3, 4)
QKV_SCALED_TILES = (0, 3)


def _qkv_kernel(x_ref, g1_ref, wa_ref, wb_ref, wf_ref, pos_ref, tab_ref,
                qkv_ref, u_ref, flog_ref, cos_sc, sa_sc, sb_sc, *, scale):
    j = pl.program_id(1)

    @pl.when(j == 0)
    def _():
        xf = x_ref[...]
        ms = jnp.mean(xf * xf, axis=-1, keepdims=True)
        u = (xf * lax.rsqrt(ms + EPS)) * g1_ref[...]
        ub = u.astype(BF16)
        u_ref[...] = ub
        flog_ref[...] = jnp.dot(ub, wf_ref[...], preferred_element_type=F32)
        ang = pos_ref[...] * tab_ref[0:1, :]
        sn = jnp.sin(ang)
        cos_sc[...] = jnp.cos(ang)
        sa_sc[...] = sn * tab_ref[1:2, :]
        sb_sc[...] = sn * tab_ref[2:3, :]

    any_of = lambda tiles: functools.reduce(jnp.logical_or, [j == q for q in tiles])
    mult = jnp.where(any_of(QKV_SCALED_TILES), scale, 1.0).astype(F32)
    rotary = any_of(QKV_ROTARY_TILES)
    left = j < QKV_LEFT_TILES

    def plain(w_ref):
        acc = jnp.dot(u_ref[...], w_ref[...], preferred_element_type=F32)
        qkv_ref[...] = (acc * mult).astype(BF16)

    pl.when(left)(functools.partial(plain, wa_ref))
    pl.when(jnp.logical_not(jnp.logical_or(left, rotary)))(functools.partial(plain, wb_ref))

    @pl.when(rotary)
    def _():
        acc = jnp.dot(u_ref[...], wb_ref[...], preferred_element_type=F32)
        ca = cos_sc[...] * mult
        cb = sa_sc[...] * mult
        cc = sb_sc[...] * mult
        for c in range(acc.shape[1] // HEAD_DIM):
            t = acc[:, c * HEAD_DIM:(c + 1) * HEAD_DIM]
            r = (t * ca + pltpu.roll(t, HEAD_DIM - ROPE_DIM // 2, 1) * cb
                 + pltpu.roll(t, ROPE_DIM // 2, 1) * cc)
            qkv_ref[:, c * HEAD_DIM:(c + 1) * HEAD_DIM] = r.astype(BF16)


def _split_w_kernel(w_ref, left_ref, right_ref, f_ref, *, n_left, n_f):
    left_ref[...] = w_ref[:, :n_left].astype(BF16)
    right_ref[...] = w_ref[:, n_left + n_f:].astype(BF16)
    lane = lax.broadcasted_iota(jnp.int32, f_ref.shape, 1)
    f_ref[...] = jnp.where(lane < n_f, w_ref[:, n_left:n_left + LANES], 0.0).astype(BF16)


def _split_w(w, n_left, n_f, tr):
    d, width = w.shape
    n_right = width - n_left - n_f
    kern = functools.partial(_split_w_kernel, n_left=n_left, n_f=n_f)
    return pl.pallas_call(
        kern,
        grid=(d // tr,),
        in_specs=[pl.BlockSpec((tr, width), lambda i: (i, 0))],
        out_specs=[
            pl.BlockSpec((tr, n_left), lambda i: (i, 0)),
            pl.BlockSpec((tr, n_right), lambda i: (i, 0)),
            pl.BlockSpec((tr, LANES), lambda i: (i, 0)),
        ],
        out_shape=[
            jax.ShapeDtypeStruct((d, n_left), BF16),
            jax.ShapeDtypeStruct((d, n_right), BF16),
            jax.ShapeDtypeStruct((d, LANES), BF16),
        ],
        compiler_params=_cparams(("parallel",)),
    )(w)


def _gates_kernel(u_ref, w_ref, bg_ref, gate_ref):
    acc = jnp.dot(u_ref[...], w_ref[...], preferred_element_type=F32)
    gate_ref[...] = jax.nn.sigmoid(acc + bg_ref[...]).astype(gate_ref.dtype)


def _inproj(x2, pos, g1, w_left, w_right, w_f, rope_tab, b_gate, tm):
    t, d = x2.shape
    tn = QKV_TILE
    n_qkv = 6
    n_left = QKV_LEFT_TILES
    n_gate = w_right.shape[1] // tn - (n_qkv - n_left)
    kern = functools.partial(_qkv_kernel, scale=LOG2E / math.sqrt(HEAD_DIM))
    qkv, u, flog = pl.pallas_call(
        kern,
        grid=(t // tm, n_qkv),
        in_specs=[
            pl.BlockSpec((tm, d), lambda i, j: (i, 0)),
            pl.BlockSpec((1, d), lambda i, j: (0, 0)),
            pl.BlockSpec((d, tn), lambda i, j: (0, jnp.minimum(j, n_left - 1))),
            pl.BlockSpec((d, tn), lambda i, j: (0, jnp.maximum(j - n_left, 0))),
            pl.BlockSpec((d, LANES), lambda i, j: (0, 0)),
            pl.BlockSpec((tm, 1), lambda i, j: (i, 0)),
            pl.BlockSpec((8, LANES), lambda i, j: (0, 0)),
        ],
        out_specs=[
            pl.BlockSpec((tm, tn), lambda i, j: (i, j)),
            pl.BlockSpec((tm, d), lambda i, j: (i, 0)),
            pl.BlockSpec((tm, LANES), lambda i, j: (i, 0)),
        ],
        out_shape=[
            jax.ShapeDtypeStruct((t, n_qkv * tn), BF16),
            jax.ShapeDtypeStruct((t, d), BF16),
            jax.ShapeDtypeStruct((t, LANES), F32),
        ],
        scratch_shapes=[
            pltpu.VMEM((tm, LANES), F32),
            pltpu.VMEM((tm, LANES), F32),
            pltpu.VMEM((tm, LANES), F32),
        ],
        compiler_params=_cparams(("parallel", "arbitrary")),
    )(x2, g1, w_left, w_right, w_f, pos, rope_tab)
    gate = pl.pallas_call(
        _gates_kernel,
        grid=(t // tm, n_gate),
        in_specs=[
            pl.BlockSpec((tm, d), lambda i, j: (i, 0)),
            pl.BlockSpec((d, tn), lambda i, j: (0, n_qkv - n_left + j)),
            pl.BlockSpec((1, tn), lambda i, j: (0, j)),
        ],
        out_specs=pl.BlockSpec((tm, tn), lambda i, j: (i, j)),
        out_shape=jax.ShapeDtypeStruct((t, n_gate * tn), BF16),
        compiler_params=_cparams(("parallel", "arbitrary")),
    )(u, w_right, b_gate)
    return qkv, gate, flog


def _fgate_kernel(fl_ref, b_ref, c_ref):
    z = fl_ref[...] + b_ref[...]
    x = jnp.minimum(z, 0.0) - jnp.log(1.0 + jnp.exp(-jnp.abs(z)))
    n = x.shape[1]
    lane = lax.broadcasted_iota(jnp.int32, x.shape, 1)
    sh = 1
    while sh < n:
        x = x + jnp.where(lane >= sh, pltpu.roll(x, sh, 1), 0.0)
        sh *= 2
    c_ref[...] = x


def _fgate(fl, b):
    r, s = fl.shape
    return pl.pallas_call(
        _fgate_kernel,
        grid=(1,),
        in_specs=[pl.BlockSpec((r, s), lambda i: (0, 0)), pl.BlockSpec((r, 1), lambda i: (0, 0))],
        out_specs=pl.BlockSpec((r, s), lambda i: (0, 0)),
        out_shape=jax.ShapeDtypeStruct((r, s), F32),
        compiler_params=_cparams(("arbitrary",)),
    )(fl, b)


def _transpose_bf16(x):
    return x.astype(F32).T.astype(BF16)


def _softmax_step(s, vt, m_sc, l_sc, acc_sc):
    m_old = m_sc[...]
    m_new = jnp.maximum(m_old, jnp.max(s, axis=0, keepdims=True))
    alpha = jnp.exp2(m_old - m_new)
    p = jnp.exp2(s - m_new)
    l_sc[...] = alpha * l_sc[...] + jnp.sum(p, axis=0, keepdims=True)
    acc_sc[...] = alpha * acc_sc[...] + jnp.dot(vt, p.astype(BF16), preferred_element_type=F32)
    m_sc[...] = m_new


def _causal_keep(shape):
    kv = lax.broadcasted_iota(jnp.int32, shape, 0)
    qq = lax.broadcasted_iota(jnp.int32, shape, 1)
    return kv <= qq


FOX_HEADS_PER_STEP = 2


def _fox_kernel(q_ref, k_ref, v_ref, c_ref, o_ref, *scratch, tq, tk, s_len):
    hp = FOX_HEADS_PER_STEP
    qt_sc, kaug_sc, vt_sc, m_sc, l_sc, acc_sc = (scratch[n * hp:(n + 1) * hp] for n in range(6))
    s_sc = scratch[6 * hp:]
    qi = pl.program_id(2)
    head_cols = [slice(h * HEAD_DIM, (h + 1) * HEAD_DIM) for h in range(hp)]

    @pl.when(qi == 0)
    def _():
        sub = lax.broadcasted_iota(jnp.int32, (HEAD_DIM, tk), 0)
        for h in range(hp):
            for ch in range(s_len // tk):
                rows = slice(ch * tk, (ch + 1) * tk)
                vt_sc[h][:, rows] = _transpose_bf16(v_ref[rows, head_cols[h]])
                c = c_ref[h, :, rows] * (-LOG2E)
                hi = c.astype(BF16).astype(F32)
                mid = (c - hi).astype(BF16).astype(F32)
                lo = (c - hi - mid).astype(BF16).astype(F32)
                tab = jnp.where(sub == 0, hi, jnp.where(sub == 1, mid, jnp.where(sub == 2, lo, 0.0)))
                kaug_sc[h][rows, HEAD_DIM:] = tab.T.astype(BF16)
                kaug_sc[h][rows, :HEAD_DIM] = k_ref[rows, head_cols[h]]

    sub_q = lax.broadcasted_iota(jnp.int32, (HEAD_DIM, tq), 0)
    for h in range(hp):
        qt_sc[h][:HEAD_DIM, :] = _transpose_bf16(q_ref[:, head_cols[h]])
        qt_sc[h][HEAD_DIM:, :] = jnp.where(sub_q < 3, 1.0, 0.0).astype(BF16)
        m_sc[h][...] = jnp.full(m_sc[h].shape, -jnp.inf, F32)
        l_sc[h][...] = jnp.zeros(l_sc[h].shape, F32)
        acc_sc[h][...] = jnp.zeros(acc_sc[h].shape, F32)

    def scores(h, chunk):
        start = pl.multiple_of(chunk * tk, tk)
        return jnp.dot(kaug_sc[h][pl.ds(start, tk), :], qt_sc[h][...], preferred_element_type=F32)

    def consume(h, chunk, par, masked):
        s = s_sc[par * hp + h][...]
        if masked:
            s = jnp.where(_causal_keep(s.shape), s, -jnp.inf)
        start = pl.multiple_of(chunk * tk, tk)
        _softmax_step(s, vt_sc[h][:, pl.ds(start, tk)], m_sc[h], l_sc[h], acc_sc[h])

    for h in range(hp):
        s_sc[h][...] = scores(h, 0)

    def body(kc, carry):
        def run(par):
            for h in range(hp):
                s_sc[(1 - par) * hp + h][...] = scores(h, kc + 1)
                consume(h, kc, par, False)
        pl.when(kc % 2 == 0)(functools.partial(run, 0))
        pl.when(kc % 2 == 1)(functools.partial(run, 1))
        return carry

    lax.fori_loop(0, qi, body, 0)

    def diagonal(par):
        for h in range(hp):
            consume(h, qi, par, True)
    pl.when(qi % 2 == 0)(functools.partial(diagonal, 0))
    pl.when(qi % 2 == 1)(functools.partial(diagonal, 1))
    for h in range(hp):
        o_ref[:, head_cols[h]] = (acc_sc[h][...] / l_sc[h][...]).T.astype(o_ref.dtype)


def _fox_attn(qkv, c3, b, s, tq):
    t = b * s
    nq = s // tq
    hp = FOX_HEADS_PER_STEP
    groups = FOX_HEADS // hp
    width = hp * HEAD_DIM
    kern = functools.partial(_fox_kernel, tq=tq, tk=tq, s_len=s)

    def per_head(shape, dtype):
        return [pltpu.VMEM(shape, dtype) for _ in range(hp)]

    return pl.pallas_call(
        kern,
        grid=(b, groups, nq),
        in_specs=[
            pl.BlockSpec((tq, width), lambda bi, g, qi: (bi * nq + qi, g)),
            pl.BlockSpec((s, width), lambda bi, g, qi: (bi, groups + g)),
            pl.BlockSpec((s, width), lambda bi, g, qi: (bi, 2 * groups + g)),
            pl.BlockSpec((hp, 1, s), lambda bi, g, qi: (bi * groups + g, 0, 0)),
        ],
        out_specs=pl.BlockSpec((tq, width), lambda bi, g, qi: (bi * nq + qi, g)),
        out_shape=jax.ShapeDtypeStruct((t, FOX_HEADS * HEAD_DIM), BF16),
        scratch_shapes=(per_head((2 * HEAD_DIM, tq), BF16) + per_head((s, 2 * HEAD_DIM), BF16)
                        + per_head((HEAD_DIM, s), BF16) + per_head((1, tq), F32)
                        + per_head((1, tq), F32) + per_head((HEAD_DIM, tq), F32)
                        + per_head((tq, tq), F32) + per_head((tq, tq), F32)),
        compiler_params=_cparams(("parallel", "parallel", "arbitrary")),
    )(qkv, qkv, qkv, c3)


def _diff_kernel(q1_ref, q2_ref, k1_ref, k2_ref, v_ref, lam_ref, g_ref, o_ref,
                 q1t_sc, q2t_sc, vt_sc, m1_sc, l1_sc, a1_sc, m2_sc, l2_sc, a2_sc, *s_sc,
                 tq, tk, s_len, lam_init):
    qi = pl.program_id(2)

    @pl.when(qi == 0)
    def _():
        for ch in range(s_len // tk):
            rows = slice(ch * tk, (ch + 1) * tk)
            vt_sc[:, rows] = _transpose_bf16(v_ref[rows, :])

    q1t_sc[...] = _transpose_bf16(q1_ref[...])
    q2t_sc[...] = _transpose_bf16(q2_ref[...])
    for r in (m1_sc, m2_sc):
        r[...] = jnp.full(r.shape, -jnp.inf, F32)
    for r in (l1_sc, l2_sc, a1_sc, a2_sc):
        r[...] = jnp.zeros(r.shape, F32)

    chains = ((k1_ref, q1t_sc, m1_sc, l1_sc, a1_sc), (k2_ref, q2t_sc, m2_sc, l2_sc, a2_sc))

    def scores(c, chunk):
        start = pl.multiple_of(chunk * tk, tk)
        return jnp.dot(chains[c][0][pl.ds(start, tk), :], chains[c][1][...], preferred_element_type=F32)

    def consume(c, chunk, par, masked):
        s = s_sc[par * 2 + c][...]
        if masked:
            s = jnp.where(_causal_keep(s.shape), s, -jnp.inf)
        start = pl.multiple_of(chunk * tk, tk)
        _softmax_step(s, vt_sc[:, pl.ds(start, tk)], *chains[c][2:])

    for c in range(2):
        s_sc[c][...] = scores(c, 0)

    def body(kc, carry):
        def run(par):
            for c in range(2):
                s_sc[(1 - par) * 2 + c][...] = scores(c, kc + 1)
                consume(c, kc, par, False)
        pl.when(kc % 2 == 0)(functools.partial(run, 0))
        pl.when(kc % 2 == 1)(functools.partial(run, 1))
        return carry

    lax.fori_loop(0, qi, body, 0)

    def diagonal(par):
        for c in range(2):
            consume(c, qi, par, True)
    pl.when(qi % 2 == 0)(functools.partial(diagonal, 0))
    pl.when(qi % 2 == 1)(functools.partial(diagonal, 1))

    lp = lam_ref[...]
    lam = (jnp.exp(jnp.sum(lp[0:1, :] * lp[1:2, :], axis=-1, keepdims=True))
           - jnp.exp(jnp.sum(lp[2:3, :] * lp[3:4, :], axis=-1, keepdims=True)) + lam_init)
    y = (a1_sc[...] / l1_sc[...]).T - lam * (a2_sc[...] / l2_sc[...]).T
    ms = jnp.mean(y * y, axis=-1, keepdims=True)
    y = (y * lax.rsqrt(ms + EPS)) * g_ref[...]
    o_ref[...] = (y * (1.0 - lam_init)).astype(o_ref.dtype)


def _diff_attn(qkv, lam_p, subln_g, b, s, tq, lam_init):
    t = b * s
    nq = s // tq
    qb = 3 * FOX_HEADS
    kb = qb + 2 * DIFF_HEADS
    vb = (kb + 2 * DIFF_HEADS) // 2
    kern = functools.partial(_diff_kernel, tq=tq, tk=tq, s_len=s, lam_init=lam_init)
    return pl.pallas_call(
        kern,
        grid=(b, DIFF_HEADS, nq),
        in_specs=[
            pl.BlockSpec((tq, HEAD_DIM), lambda bi, h, qi: (bi * nq + qi, qb + 2 * h)),
            pl.BlockSpec((tq, HEAD_DIM), lambda bi, h, qi: (bi * nq + qi, qb + 2 * h + 1)),
            pl.BlockSpec((s, HEAD_DIM), lambda bi, h, qi: (bi, kb + 2 * h)),
            pl.BlockSpec((s, HEAD_DIM), lambda bi, h, qi: (bi, kb + 2 * h + 1)),
            pl.BlockSpec((s, DIFF_V_DIM), lambda bi, h, qi: (bi, vb + h)),
            pl.BlockSpec((4, HEAD_DIM), lambda bi, h, qi: (0, 0)),
            pl.BlockSpec((1, DIFF_V_DIM), lambda bi, h, qi: (0, 0)),
        ],
        out_specs=pl.BlockSpec((tq, DIFF_V_DIM), lambda bi, h, qi: (bi * nq + qi, h)),
        out_shape=jax.ShapeDtypeStruct((t, DIFF_HEADS * DIFF_V_DIM), BF16),
        scratch_shapes=[
            pltpu.VMEM((HEAD_DIM, tq), BF16), pltpu.VMEM((HEAD_DIM, tq), BF16),
            pltpu.VMEM((DIFF_V_DIM, s), BF16),
            pltpu.VMEM((1, tq), F32), pltpu.VMEM((1, tq), F32), pltpu.VMEM((DIFF_V_DIM, tq), F32),
            pltpu.VMEM((1, tq), F32), pltpu.VMEM((1, tq), F32), pltpu.VMEM((DIFF_V_DIM, tq), F32),
        ] + [pltpu.VMEM((tq, tq), F32) for _ in range(4)],
        compiler_params=_cparams(("parallel", "parallel", "arbitrary")),
    )(qkv, qkv, qkv, qkv, qkv, lam_p, subln_g)


def _pack_bf16_pairs(xb):
    w = xb.shape[1] // 2
    lo = lax.bitcast_convert_type(xb[:, :w].astype(F32), jnp.uint32)
    hi = lax.bitcast_convert_type(xb[:, w:].astype(F32), jnp.uint32)
    return (lo >> 16) | (hi & jnp.uint32(0xFFFF0000))


def _unpack_bf16_pairs(p):
    lo = lax.bitcast_convert_type(p << 16, F32).astype(BF16)
    hi = lax.bitcast_convert_type(p & jnp.uint32(0xFFFF0000), F32).astype(BF16)
    return lo, hi


def _merge_kernel(yf_ref, yd_ref, g0_ref, g1_ref, x_ref, wbf_ref, wbd_ref, wo_ref, n2_ref,
                  rw_ref, rb_ref, h_ref, u_ref, idx_ref, gw_ref):
    ya = jnp.dot(yf_ref[...], wbf_ref[...], preferred_element_type=F32)
    yb = jnp.dot(yd_ref[...], wbd_ref[...], preferred_element_type=F32)
    merged = g0_ref[...] * ya + g1_ref[...] * yb
    h = x_ref[...] + jnp.dot(merged.astype(BF16), wo_ref[...], preferred_element_type=F32)
    h_ref[...] = h
    ms = jnp.mean(h * h, axis=-1, keepdims=True)
    u = (h * lax.rsqrt(ms + EPS)) * n2_ref[...]
    ub = u.astype(BF16)
    u_ref[...] = _pack_bf16_pairs(ub)
    logits = jnp.dot(ub, rw_ref[...], preferred_element_type=F32) + rb_ref[...]

    lane = lax.broadcasted_iota(jnp.int32, logits.shape, 1).astype(F32)
    work = logits
    vals, idxs = [], []
    for _ in range(TOP_K):
        m = jnp.max(work, axis=-1, keepdims=True)
        sel = jnp.min(jnp.where(work == m, lane, float(LANES)), axis=-1, keepdims=True)
        vals.append(m)
        idxs.append(sel)
        work = jnp.where(lane == sel, -jnp.inf, work)
    es = [jnp.exp(v - vals[0]) for v in vals]
    den = es[0]
    for e in es[1:]:
        den = den + e
    idx_out = jnp.zeros(logits.shape, F32)
    gw_out = jnp.zeros(logits.shape, F32)
    for k in range(TOP_K):
        idx_out = jnp.where(lane == float(k), idxs[k], idx_out)
        gw_out = jnp.where(lane == float(k), es[k] / den, gw_out)
    idx_ref[...] = idx_out.astype(jnp.int32)
    gw_ref[...] = gw_out


def _merge(yf, yd, gate, x2, wbf, wbd, wo, n2, rw, rb, tm):
    t, d = x2.shape
    fw = yf.shape[1]
    dw = yd.shape[1]
    const = dict(pipeline_mode=pl.Buffered(1))
    return pl.pallas_call(
        _merge_kernel,
        grid=(t // tm,),
        in_specs=[
            pl.BlockSpec((tm, fw), lambda i: (i, 0)),
            pl.BlockSpec((tm, dw), lambda i: (i, 0)),
            pl.BlockSpec((tm, d), lambda i: (i, 0)),
            pl.BlockSpec((tm, d), lambda i: (i, 1)),
            pl.BlockSpec((tm, d), lambda i: (i, 0)),
            pl.BlockSpec((fw, d), lambda i: (0, 0), **const),
            pl.BlockSpec((dw, d), lambda i: (0, 0), **const),
            pl.BlockSpec((d, d), lambda i: (0, 0), **const),
            pl.BlockSpec((1, d), lambda i: (0, 0)),
            pl.BlockSpec((d, LANES), lambda i: (0, 0), **const),
            pl.BlockSpec((1, LANES), lambda i: (0, 0)),
        ],
        out_specs=[
            pl.BlockSpec((tm, d), lambda i: (i, 0)),
            pl.BlockSpec((tm, d // 2), lambda i: (i, 0)),
            pl.BlockSpec((tm, LANES), lambda i: (i, 0)),
            pl.BlockSpec((tm, LANES), lambda i: (i, 0)),
        ],
        out_shape=[
            jax.ShapeDtypeStruct((t, d), F32),
            jax.ShapeDtypeStruct((t, d // 2), jnp.uint32),
            jax.ShapeDtypeStruct((t, LANES), jnp.int32),
            jax.ShapeDtypeStruct((t, LANES), F32),
        ],
        compiler_params=_cparams(("parallel",)),
    )(yf, yd, gate, gate, x2, wbf, wbd, wo, n2, rw, rb)


def _dispatch_kernel(cnt_ref, pstart_ref, pend_ref, roa_ref, u_ref, xs_hbm, stage, zeros, sems,
                     *, tb, sub, n_rows):
    s = pl.program_id(0)
    n = pl.num_programs(0)
    slot = s % 2
    n_dma = TOP_K * tb

    def wait_slot(sl):
        pltpu.make_async_copy(stage.at[sl], xs_hbm.at[pl.ds(0, n_dma), :], sems.at[sl]).wait()

    @pl.when(s == 0)
    def _():
        zeros[...] = jnp.zeros(zeros.shape, zeros.dtype)

    @pl.when(s >= 2)
    def _():
        wait_slot(slot)

    blk = u_ref[...]
    for k in range(TOP_K):
        stage[slot, pl.ds(k * tb, tb), :] = blk

    def body(g, carry):
        for q8 in range(ISSUE_UNROLL):
            q = g * ISSUE_UNROLL + q8
            pltpu.make_async_copy(stage.at[slot, pl.ds(q, 1), :],
                                  xs_hbm.at[pl.ds(roa_ref[0, 0, q], 1), :], sems.at[slot]).start()
        return carry
    lax.fori_loop(0, n_dma // ISSUE_UNROLL, body, 0)

    @pl.when(s == n - 1)
    def _():
        @pl.when(n >= 2)
        def _():
            wait_slot(1 - slot)
        wait_slot(slot)

        def zero_row(r):
            return pltpu.make_async_copy(zeros.at[pl.ds(0, 1), :], xs_hbm.at[pl.ds(r, 1), :], sems.at[2])

        def zero_block(r):
            return pltpu.make_async_copy(zeros, xs_hbm.at[pl.ds(pl.multiple_of(r, sub), sub), :], sems.at[2])

        def fill(first, stop, copy, step):
            cnt = (stop - first) // step

            def start(q, carry):
                copy(first + q * step).start()
                return carry

            def wait(q, carry):
                copy(first).wait()
                return carry
            lax.fori_loop(0, cnt, start, 0)
            lax.fori_loop(0, cnt, wait, 0)

        def per_expert(e, carry):
            first = pstart_ref[e] + cnt_ref[e]
            edge = (first + sub - 1) // sub * sub
            fill(first, edge, zero_row, 1)
            fill(edge, pend_ref[e], zero_block, sub)
            return carry
        lax.fori_loop(0, N_EXPERTS, per_expert, 0)
        fill(pend_ref[N_EXPERTS - 1], n_rows, zero_block, sub)


def _dispatch(counts, pad_start, pad_end, roa3, u_packed, n_rows, tb, sub):
    t, w = u_packed.shape
    kern = functools.partial(_dispatch_kernel, tb=tb, sub=sub, n_rows=n_rows)
    return pl.pallas_call(
        kern,
        grid_spec=pltpu.PrefetchScalarGridSpec(
            num_scalar_prefetch=3,
            grid=(t // tb,),
            in_specs=[
                pl.BlockSpec((1, 1, TOP_K * tb), lambda s, c, ps, pe: (s, 0, 0), memory_space=pltpu.SMEM),
                pl.BlockSpec((tb, w), lambda s, c, ps, pe: (s, 0)),
            ],
            out_specs=pl.BlockSpec(memory_space=pl.ANY),
            scratch_shapes=[
                pltpu.VMEM((2, TOP_K * tb, w), jnp.uint32),
                pltpu.VMEM((sub, w), jnp.uint32),
                pltpu.SemaphoreType.DMA((3,)),
            ],
        ),
        out_shape=jax.ShapeDtypeStruct((n_rows, w), jnp.uint32),
        compiler_params=_cparams(("arbitrary",)),
    )(counts, pad_start, pad_end, roa3, u_packed)


def _expert_kernel(te_ref, ts_ref, nsub_ref, xs_ref, wg_ref, wu_ref, bg_ref, bu_ref, wd_ref,
                   bd_ref, out_ref, x_sc, act_sc, *, sub, spt, nf):
    i = pl.program_id(0)
    j = pl.program_id(1)
    nsub = nsub_ref[i]
    half = x_sc.shape[1] // 2

    @pl.when(jnp.logical_and(j == 0, nsub > 0))
    def _():
        lo, hi = _unpack_bf16_pairs(xs_ref[...])
        x_sc[:, :half] = lo
        x_sc[:, half:] = hi

    full = nsub == spt

    def gate_up(rs, wg, wu):
        x = x_sc[rs, :]
        hg = jnp.dot(x, wg, preferred_element_type=F32) + bg_ref[0]
        hu = jnp.dot(x, wu, preferred_element_type=F32) + bu_ref[0]
        hg = jnp.minimum(hg, SWIGLU_LIMIT)
        hl = jnp.clip(hu, -SWIGLU_LIMIT, SWIGLU_LIMIT)
        act = hg * jax.nn.sigmoid(SWIGLU_ALPHA * hg) * (hl + 1.0)
        act_sc[j, rs, :] = act.astype(BF16)

    def down(rs, wd):
        act = jnp.concatenate([act_sc[c, rs, :] for c in range(nf)], axis=1)
        out_ref[rs, :] = jnp.dot(act, wd, preferred_element_type=F32) + bd_ref[0]

    partial_tile = jnp.logical_and(nsub > 0, nsub < spt)

    @pl.when(jnp.logical_and(j < nf, full))
    def _():
        gate_up(slice(None), wg_ref[0].astype(BF16), wu_ref[0].astype(BF16))

    @pl.when(jnp.logical_and(j < nf, partial_tile))
    def _():
        wg = wg_ref[0].astype(BF16)
        wu = wu_ref[0].astype(BF16)
        for s in range(spt - 1):
            pl.when(s < nsub)(functools.partial(gate_up, slice(s * sub, (s + 1) * sub), wg, wu))

    @pl.when(jnp.logical_and(j >= nf, full))
    def _():
        down(slice(None), wd_ref[0].astype(BF16))

    @pl.when(jnp.logical_and(j >= nf, partial_tile))
    def _():
        wd = wd_ref[0].astype(BF16)
        for s in range(spt - 1):
            pl.when(s < nsub)(functools.partial(down, slice(s * sub, (s + 1) * sub), wd))

    @pl.when(jnp.logical_and(j >= nf, nsub < spt))
    def _():
        for s in range(spt):
            @pl.when(s >= nsub)
            def _():
                out_ref[s * sub:(s + 1) * sub, :] = jnp.zeros((sub, out_ref.shape[1]), out_ref.dtype)


def _experts(tile_e, tile_src, tile_nsub, xs, w_gu, b_gu3, w_down, b_down3, tm_e, sub, tf):
    n_rows = xs.shape[0]
    n_tiles = n_rows // tm_e
    d = 2 * xs.shape[1]
    dff = w_down.shape[1]
    nf = dff // tf
    nn = d // tf
    assert nn == nf
    kern = functools.partial(_expert_kernel, sub=sub, spt=tm_e // sub, nf=nf)

    def c1(i, j, ns):
        return jnp.where(ns[i] > 0, jnp.minimum(j, nf - 1), nf - 1)

    def c2(i, j, ns):
        return jnp.where(ns[i] > 0, jnp.maximum(j - nf, 0), nf - 1)

    return pl.pallas_call(
        kern,
        grid_spec=pltpu.PrefetchScalarGridSpec(
            num_scalar_prefetch=3,
            grid=(n_tiles, 2 * nf),
            in_specs=[
                pl.BlockSpec((tm_e, d // 2), lambda i, j, te, ts, ns: (ts[i], 0)),
                pl.BlockSpec((1, d, tf), lambda i, j, te, ts, ns: (te[i], 0, c1(i, j, ns))),
                pl.BlockSpec((1, d, tf), lambda i, j, te, ts, ns: (te[i], 0, nf + c1(i, j, ns))),
                pl.BlockSpec((1, 1, tf), lambda i, j, te, ts, ns: (te[i], 0, c1(i, j, ns))),
                pl.BlockSpec((1, 1, tf), lambda i, j, te, ts, ns: (te[i], 0, nf + c1(i, j, ns))),
                pl.BlockSpec((1, dff, tf), lambda i, j, te, ts, ns: (te[i], 0, c2(i, j, ns))),
                pl.BlockSpec((1, 1, tf), lambda i, j, te, ts, ns: (te[i], 0, c2(i, j, ns))),
            ],
            out_specs=pl.BlockSpec((tm_e, tf), lambda i, j, te, ts, ns: (i, jnp.maximum(j - nf, 0))),
            scratch_shapes=[
                pltpu.VMEM((tm_e, d), BF16),
                pltpu.VMEM((nf, tm_e, tf), BF16),
            ],
        ),
        out_shape=jax.ShapeDtypeStruct((n_rows, d), F32),
        compiler_params=_cparams(("arbitrary", "arbitrary")),
    )(tile_e, tile_src, tile_nsub, xs, w_gu, w_gu, b_gu3, b_gu3, w_down, b_down3)


def _combine_kernel(cur_ref, nxt_ref, yr_hbm, h_ref, gw_ref, gf_ref, out_ref, ysel, sems, *, tmc,
                    final_norm):
    i = pl.program_id(0)
    n = pl.num_programs(0)
    slot = i % 2
    n_dma = TOP_K * tmc

    def row_copy(row, q, sl):
        return pltpu.make_async_copy(yr_hbm.at[pl.ds(row, 1), :], ysel.at[sl, pl.ds(q, 1), :], sems.at[sl])

    def issue(idx_ref, sl):
        def body(g, carry):
            for q8 in range(ISSUE_UNROLL):
                q = g * ISSUE_UNROLL + q8
                row_copy(idx_ref[0, 0, q], q, sl).start()
            return carry
        lax.fori_loop(0, n_dma // ISSUE_UNROLL, body, 0)

    @pl.when(i == 0)
    def _():
        issue(cur_ref, 0)

    @pl.when(i + 1 < n)
    def _():
        issue(nxt_ref, 1 - slot)

    pltpu.make_async_copy(yr_hbm.at[pl.ds(0, n_dma), :], ysel.at[slot], sems.at[slot]).wait()

    gw = gw_ref[...]
    y = h_ref[...]
    for k in range(TOP_K):
        y = y + gw[:, k:k + 1] * ysel[slot, k * tmc:(k + 1) * tmc, :]
    if final_norm:
        ms = jnp.mean(y * y, axis=-1, keepdims=True)
        y = (y * lax.rsqrt(ms + EPS)) * gf_ref[...]
    out_ref[...] = y


def _combine(roa3, y_rows, h, gw, gf, tmc, final_norm):
    t, d = h.shape
    n = t // tmc
    kern = functools.partial(_combine_kernel, tmc=tmc, final_norm=final_norm)
    return pl.pallas_call(
        kern,
        grid=(n,),
        in_specs=[
            pl.BlockSpec((1, 1, TOP_K * tmc), lambda i: (i, 0, 0), memory_space=pltpu.SMEM),
            pl.BlockSpec((1, 1, TOP_K * tmc), lambda i: (jnp.minimum(i + 1, n - 1), 0, 0),
                         memory_space=pltpu.SMEM),
            pl.BlockSpec(memory_space=pl.ANY),
            pl.BlockSpec((tmc, d), lambda i: (i, 0)),
            pl.BlockSpec((tmc, LANES), lambda i: (i, 0)),
            pl.BlockSpec((1, d), lambda i: (0, 0)),
        ],
        out_specs=pl.BlockSpec((tmc, d), lambda i: (i, 0)),
        out_shape=jax.ShapeDtypeStruct((t, d), F32),
        scratch_shapes=[pltpu.VMEM((2, TOP_K * tmc, d), F32), pltpu.SemaphoreType.DMA((2,))],
        compiler_params=_cparams(("arbitrary",)),
    )(roa3, roa3, y_rows, h, gw, gf)


def _routing(top_idx, tm_e, sub):
    t = top_idx.shape[0]
    a = t * TOP_K
    flat_e = top_idx.reshape(a)
    onehot = (flat_e[:, None] == jnp.arange(N_EXPERTS, dtype=jnp.int32)[None, :]).astype(jnp.int32)
    csum = jnp.cumsum(onehot, axis=0)
    counts = csum[-1]
    rank = jnp.sum(onehot * csum, axis=1) - 1
    padded = (counts + tm_e - 1) // tm_e * tm_e
    pad_end = jnp.cumsum(padded)
    pad_start = pad_end - padded
    row_of_assign = (pad_start[flat_e] + rank).astype(jnp.int32)
    n_tiles = -(-a // tm_e) + N_EXPERTS
    tile_start = jnp.arange(n_tiles, dtype=jnp.int32) * tm_e
    n_used = pad_end[-1] // tm_e
    tile_e_raw = jnp.minimum(jnp.searchsorted(pad_end, tile_start, side='right'), N_EXPERTS - 1)
    used = tile_start < pad_end[-1]
    last = jnp.maximum(n_used - 1, 0)
    tile_src = jnp.where(used, jnp.arange(n_tiles, dtype=jnp.int32), last).astype(jnp.int32)
    tile_e = tile_e_raw[tile_src].astype(jnp.int32)
    valid_rows = jnp.clip(counts[tile_e_raw] - (tile_start - pad_start[tile_e_raw]), 0, tm_e)
    tile_nsub = jnp.where(used, (valid_rows + sub - 1) // sub, 0).astype(jnp.int32)
    bounds = (counts.astype(jnp.int32), pad_start.astype(jnp.int32), pad_end.astype(jnp.int32))
    return bounds, row_of_assign, tile_e, tile_src, tile_nsub, n_tiles * tm_e


def kernel(x, positions, norm1_g, w_in, b_fgate, b_gate, lam_q1, lam_k1, lam_q2, lam_k2, subln_g,
           w_branch_fox, w_branch_diff, w_out, norm2_g, router_w, router_b, w_gu, b_gu, w_down,
           b_down, normf_g):
    b, s, d = x.shape
    t = b * s
    depth = norm1_g.shape[0]
    fox_w = FOX_HEADS * HEAD_DIM
    tm_a = _pick(t, 1024)
    tq = _pick(s, 512)
    tm_d = _pick(t, 256)
    tm_e, sub, tf = 1024, 256, 512
    tmc = _pick(t, 128)

    half = ROPE_DIM // 2
    inv_freq = ROPE_THETA ** (-jnp.arange(0, ROPE_DIM, 2, dtype=F32) / ROPE_DIM)
    zeros = jnp.zeros((LANES - ROPE_DIM,), F32)
    rope_tab = jnp.zeros((8, LANES), F32)
    rope_tab = rope_tab.at[0].set(jnp.concatenate([inv_freq, inv_freq, zeros]))
    rope_tab = rope_tab.at[1].set(jnp.concatenate([-jnp.ones((half,), F32), jnp.zeros((half,), F32), zeros]))
    rope_tab = rope_tab.at[2].set(jnp.concatenate([jnp.zeros((half,), F32), jnp.ones((half,), F32), zeros]))
    pos = positions.astype(F32).reshape(t, 1)

    h = x.reshape(t, d)
    for l in range(depth):
        wi = w_in[l]
        w_left, w_right, w_f = _split_w(wi, 3 * fox_w, FOX_HEADS, _pick(d, 256))
        qkv, gate, flog = _inproj(h, pos, norm1_g[l][None, :], w_left, w_right, w_f, rope_tab,
                                  b_gate[l][None, :], tm_a)

        fl = flog[:, :FOX_HEADS].reshape(b, s, FOX_HEADS).transpose(0, 2, 1).reshape(b * FOX_HEADS, s)
        bf = jnp.tile(b_fgate[l].astype(F32), b).reshape(b * FOX_HEADS, 1)
        c3 = _fgate(fl, bf).reshape(b * FOX_HEADS, 1, s)

        y_fox = _fox_attn(qkv, c3, b, s, tq)
        lam_init = 0.8 - 0.6 * math.exp(-0.3 * l)
        lam_p = jnp.stack([lam_q1[l], lam_k1[l], lam_q2[l], lam_k2[l]]).astype(F32)
        y_diff = _diff_attn(qkv, lam_p, subln_g[l][None, :].astype(F32), b, s, tq, lam_init)

        rw = jnp.pad(router_w[l], ((0, 0), (0, LANES - N_EXPERTS))).astype(BF16)
        rb = jnp.concatenate([router_b[l].astype(F32), jnp.full((LANES - N_EXPERTS,), NEG_BIG, F32)])[None, :]
        h, u2, idx128, gw128 = _merge(y_fox, y_diff, gate, h, w_branch_fox[l].astype(BF16),
                                      w_branch_diff[l].astype(BF16), w_out[l].astype(BF16),
                                      norm2_g[l][None, :], rw, rb, tm_d)

        bounds, roa, tile_e, tile_src, tile_nsub, n_rows = _routing(idx128[:, :TOP_K], tm_e, sub)
        roa3 = roa.reshape(t // tmc, tmc, TOP_K).transpose(0, 2, 1).reshape(t // tmc, 1, TOP_K * tmc)
        xs = _dispatch(*bounds, roa3, u2, n_rows, tmc, sub)
        y_rows = _experts(tile_e, tile_src, tile_nsub, xs, w_gu[l], b_gu[l][:, None, :],
                          w_down[l], b_down[l][:, None, :], tm_e, sub, tf)
        h = _combine(roa3, y_rows, h, gw128, normf_g[None, :], tmc, final_norm=(l == depth - 1))
    return h.reshape(b, s, d)
```

```python
import functools
import math

import jax
import jax.numpy as jnp
from jax import lax
from jax.experimental import pallas as pl
from jax.experimental.pallas import tpu as pltpu

HEAD_DIM = 128
FOX_HEADS = 8
DIFF_HEADS = 4
DIFF_V_DIM = 2 * HEAD_DIM
ROPE_THETA = 500000.0
ROPE_DIM = HEAD_DIM // 4
N_EXPERTS = 32
TOP_K = 4
SWIGLU_ALPHA = 1.702
SWIGLU_LIMIT = 7.0
EPS = 1e-5
LANES = 128

F32 = jnp.float32
BF16 = jnp.bfloat16
NEG_BIG = -1e30
LOG2E = math.log2(math.e)
ISSUE_UNROLL = 8
VMEM_LIMIT = 56 * 1024 * 1024


def _cparams(sem, vmem=VMEM_LIMIT):
    return pltpu.CompilerParams(dimension_semantics=sem, vmem_limit_bytes=vmem)


def _pick(n, pref):
    t = min(n, pref)
    while n % t:
        t //= 2
    return t


QKV_TILE = 1024
QKV_LEFT_TILES = 3
QKV_ROTARY_TILES = (3, 4)
QKV_SCALED_TILES = (0, 3)


def _qkv_kernel(x_ref, g1_ref, wa_ref, wb_ref, wf_ref, pos_ref, tab_ref,
                qkv_ref, u_ref, flog_ref, cos_sc, sa_sc, sb_sc, *, scale):
    j = pl.program_id(1)

    @pl.when(j == 0)
    def _():
        xf = x_ref[...]
        ms = jnp.mean(xf * xf, axis=-1, keepdims=True)
        u = (xf * lax.rsqrt(ms + EPS)) * g1_ref[...]
        ub = u.astype(BF16)
        u_ref[...] = ub
        flog_ref[...] = jnp.dot(ub, wf_ref[...], preferred_element_type=F32)
        ang = pos_ref[...] * tab_ref[0:1, :]
        sn = jnp.sin(ang)
        cos_sc[...] = jnp.cos(ang)
        sa_sc[...] = sn * tab_ref[1:2, :]
        sb_sc[...] = sn * tab_ref[2:3, :]

    any_of = lambda tiles: functools.reduce(jnp.logical_or, [j == q for q in tiles])
    mult = jnp.where(any_of(QKV_SCALED_TILES), scale, 1.0).astype(F32)
    rotary = any_of(QKV_ROTARY_TILES)
    left = j < QKV_LEFT_TILES

    def plain(w_ref):
        acc = jnp.dot(u_ref[...], w_ref[...], preferred_element_type=F32)
        qkv_ref[...] = (acc * mult).astype(BF16)

    pl.when(left)(functools.partial(plain, wa_ref))
    pl.when(jnp.logical_not(jnp.logical_or(left, rotary)))(functools.partial(plain, wb_ref))

    @pl.when(rotary)
    def _():
        acc = jnp.dot(u_ref[...], wb_ref[...], preferred_element_type=F32)
        ca = cos_sc[...] * mult
        cb = sa_sc[...] * mult
        cc = sb_sc[...] * mult
        for c in range(acc.shape[1] // HEAD_DIM):
            t = acc[:, c * HEAD_DIM:(c + 1) * HEAD_DIM]
            r = (t * ca + pltpu.roll(t, HEAD_DIM - ROPE_DIM // 2, 1) * cb
                 + pltpu.roll(t, ROPE_DIM // 2, 1) * cc)
            qkv_ref[:, c * HEAD_DIM:(c + 1) * HEAD_DIM] = r.astype(BF16)


def _gates_kernel(u_ref, w_ref, bg_ref, gate_ref):
    acc = jnp.dot(u_ref[...], w_ref[...], preferred_element_type=F32)
    gate_ref[...] = jax.nn.sigmoid(acc + bg_ref[...])


def _inproj(x2, pos, g1, w_left, w_right, w_f, rope_tab, b_gate, tm):
    t, d = x2.shape
    tn = QKV_TILE
    n_qkv = 6
    n_left = QKV_LEFT_TILES
    n_gate = w_right.shape[1] // tn - (n_qkv - n_left)
    kern = functools.partial(_qkv_kernel, scale=LOG2E / math.sqrt(HEAD_DIM))
    qkv, u, flog = pl.pallas_call(
        kern,
        grid=(t // tm, n_qkv),
        in_specs=[
            pl.BlockSpec((tm, d), lambda i, j: (i, 0)),
            pl.BlockSpec((1, d), lambda i, j: (0, 0)),
            pl.BlockSpec((d, tn), lambda i, j: (0, jnp.minimum(j, n_left - 1))),
            pl.BlockSpec((d, tn), lambda i, j: (0, jnp.maximum(j - n_left, 0))),
            pl.BlockSpec((d, LANES), lambda i, j: (0, 0)),
            pl.BlockSpec((tm, 1), lambda i, j: (i, 0)),
            pl.BlockSpec((8, LANES), lambda i, j: (0, 0)),
        ],
        out_specs=[
            pl.BlockSpec((tm, tn), lambda i, j: (i, j)),
            pl.BlockSpec((tm, d), lambda i, j: (i, 0)),
            pl.BlockSpec((tm, LANES), lambda i, j: (i, 0)),
        ],
        out_shape=[
            jax.ShapeDtypeStruct((t, n_qkv * tn), BF16),
            jax.ShapeDtypeStruct((t, d), BF16),
            jax.ShapeDtypeStruct((t, LANES), F32),
        ],
        scratch_shapes=[
            pltpu.VMEM((tm, LANES), F32),
            pltpu.VMEM((tm, LANES), F32),
            pltpu.VMEM((tm, LANES), F32),
        ],
        compiler_params=_cparams(("parallel", "arbitrary")),
    )(x2, g1, w_left, w_right, w_f, pos, rope_tab)
    gate = pl.pallas_call(
        _gates_kernel,
        grid=(t // tm, n_gate),
        in_specs=[
            pl.BlockSpec((tm, d), lambda i, j: (i, 0)),
            pl.BlockSpec((d, tn), lambda i, j: (0, n_qkv - n_left + j)),
            pl.BlockSpec((1, tn), lambda i, j: (0, j)),
        ],
        out_specs=pl.BlockSpec((tm, tn), lambda i, j: (i, j)),
        out_shape=jax.ShapeDtypeStruct((t, n_gate * tn), F32),
        compiler_params=_cparams(("parallel", "arbitrary")),
    )(u, w_right, b_gate)
    return qkv, gate, flog


def _fgate_kernel(fl_ref, b_ref, c_ref):
    z = fl_ref[...] + b_ref[...]
    x = jnp.minimum(z, 0.0) - jnp.log(1.0 + jnp.exp(-jnp.abs(z)))
    n = x.shape[1]
    lane = lax.broadcasted_iota(jnp.int32, x.shape, 1)
    sh = 1
    while sh < n:
        x = x + jnp.where(lane >= sh, pltpu.roll(x, sh, 1), 0.0)
        sh *= 2
    c_ref[...] = x


def _fgate(fl, b):
    r, s = fl.shape
    return pl.pallas_call(
        _fgate_kernel,
        grid=(1,),
        in_specs=[pl.BlockSpec((r, s), lambda i: (0, 0)), pl.BlockSpec((r, 1), lambda i: (0, 0))],
        out_specs=pl.BlockSpec((r, s), lambda i: (0, 0)),
        out_shape=jax.ShapeDtypeStruct((r, s), F32),
        compiler_params=_cparams(("arbitrary",)),
    )(fl, b)


def _transpose_bf16(x):
    return x.astype(F32).T.astype(BF16)


def _softmax_step(s, vt, m_sc, l_sc, acc_sc):
    m_old = m_sc[...]
    m_new = jnp.maximum(m_old, jnp.max(s, axis=0, keepdims=True))
    alpha = jnp.exp2(m_old - m_new)
    p = jnp.exp2(s - m_new)
    l_sc[...] = alpha * l_sc[...] + jnp.sum(p, axis=0, keepdims=True)
    acc_sc[...] = alpha * acc_sc[...] + jnp.dot(vt, p.astype(BF16), preferred_element_type=F32)
    m_sc[...] = m_new


def _causal_keep(shape):
    kv = lax.broadcasted_iota(jnp.int32, shape, 0)
    qq = lax.broadcasted_iota(jnp.int32, shape, 1)
    return kv <= qq


FOX_HEADS_PER_STEP = 2


def _fox_kernel(q_ref, k_ref, v_ref, c_ref, o_ref, *scratch, tq, tk, s_len):
    hp = FOX_HEADS_PER_STEP
    qt_sc, kaug_sc, vt_sc, m_sc, l_sc, acc_sc = (scratch[n * hp:(n + 1) * hp] for n in range(6))
    s_sc = scratch[6 * hp:]
    qi = pl.program_id(2)
    head_cols = [slice(h * HEAD_DIM, (h + 1) * HEAD_DIM) for h in range(hp)]

    @pl.when(qi == 0)
    def _():
        sub = lax.broadcasted_iota(jnp.int32, (HEAD_DIM, tk), 0)
        for h in range(hp):
            for ch in range(s_len // tk):
                rows = slice(ch * tk, (ch + 1) * tk)
                vt_sc[h][:, rows] = _transpose_bf16(v_ref[rows, head_cols[h]])
                c = c_ref[h, :, rows] * (-LOG2E)
                hi = c.astype(BF16).astype(F32)
                mid = (c - hi).astype(BF16).astype(F32)
                lo = (c - hi - mid).astype(BF16).astype(F32)
                tab = jnp.where(sub == 0, hi, jnp.where(sub == 1, mid, jnp.where(sub == 2, lo, 0.0)))
                kaug_sc[h][rows, HEAD_DIM:] = tab.T.astype(BF16)
                kaug_sc[h][rows, :HEAD_DIM] = k_ref[rows, head_cols[h]]

    sub_q = lax.broadcasted_iota(jnp.int32, (HEAD_DIM, tq), 0)
    for h in range(hp):
        qt_sc[h][:HEAD_DIM, :] = _transpose_bf16(q_ref[:, head_cols[h]])
        qt_sc[h][HEAD_DIM:, :] = jnp.where(sub_q < 3, 1.0, 0.0).astype(BF16)
        m_sc[h][...] = jnp.full(m_sc[h].shape, -jnp.inf, F32)
        l_sc[h][...] = jnp.zeros(l_sc[h].shape, F32)
        acc_sc[h][...] = jnp.zeros(acc_sc[h].shape, F32)

    def scores(h, chunk):
        start = pl.multiple_of(chunk * tk, tk)
        return jnp.dot(kaug_sc[h][pl.ds(start, tk), :], qt_sc[h][...], preferred_element_type=F32)

    def consume(h, chunk, par, masked):
        s = s_sc[par * hp + h][...]
        if masked:
            s = jnp.where(_causal_keep(s.shape), s, -jnp.inf)
        start = pl.multiple_of(chunk * tk, tk)
        _softmax_step(s, vt_sc[h][:, pl.ds(start, tk)], m_sc[h], l_sc[h], acc_sc[h])

    for h in range(hp):
        s_sc[h][...] = scores(h, 0)

    def body(kc, carry):
        def run(par):
            for h in range(hp):
                s_sc[(1 - par) * hp + h][...] = scores(h, kc + 1)
                consume(h, kc, par, False)
        pl.when(kc % 2 == 0)(functools.partial(run, 0))
        pl.when(kc % 2 == 1)(functools.partial(run, 1))
        return carry

    lax.fori_loop(0, qi, body, 0)

    def diagonal(par):
        for h in range(hp):
            consume(h, qi, par, True)
    pl.when(qi % 2 == 0)(functools.partial(diagonal, 0))
    pl.when(qi % 2 == 1)(functools.partial(diagonal, 1))
    for h in range(hp):
        o_ref[:, head_cols[h]] = (acc_sc[h][...] / l_sc[h][...]).T.astype(o_ref.dtype)


def _fox_attn(qkv, c3, b, s, tq):
    t = b * s
    nq = s // tq
    hp = FOX_HEADS_PER_STEP
    groups = FOX_HEADS // hp
    width = hp * HEAD_DIM
    kern = functools.partial(_fox_kernel, tq=tq, tk=tq, s_len=s)

    def per_head(shape, dtype):
        return [pltpu.VMEM(shape, dtype) for _ in range(hp)]

    return pl.pallas_call(
        kern,
        grid=(b, groups, nq),
        in_specs=[
            pl.BlockSpec((tq, width), lambda bi, g, qi: (bi * nq + qi, g)),
            pl.BlockSpec((s, width), lambda bi, g, qi: (bi, groups + g)),
            pl.BlockSpec((s, width), lambda bi, g, qi: (bi, 2 * groups + g)),
            pl.BlockSpec((hp, 1, s), lambda bi, g, qi: (bi * groups + g, 0, 0)),
        ],
        out_specs=pl.BlockSpec((tq, width), lambda bi, g, qi: (bi * nq + qi, g)),
        out_shape=jax.ShapeDtypeStruct((t, FOX_HEADS * HEAD_DIM), BF16),
        scratch_shapes=(per_head((2 * HEAD_DIM, tq), BF16) + per_head((s, 2 * HEAD_DIM), BF16)
                        + per_head((HEAD_DIM, s), BF16) + per_head((1, tq), F32)
                        + per_head((1, tq), F32) + per_head((HEAD_DIM, tq), F32)
                        + per_head((tq, tq), F32) + per_head((tq, tq), F32)),
        compiler_params=_cparams(("parallel", "parallel", "arbitrary")),
    )(qkv, qkv, qkv, c3)


def _diff_kernel(q1_ref, q2_ref, k1_ref, k2_ref, v_ref, lam_ref, g_ref, o_ref,
                 q1t_sc, q2t_sc, vt_sc, m1_sc, l1_sc, a1_sc, m2_sc, l2_sc, a2_sc, *s_sc,
                 tq, tk, s_len, lam_init):
    qi = pl.program_id(2)

    @pl.when(qi == 0)
    def _():
        for ch in range(s_len // tk):
            rows = slice(ch * tk, (ch + 1) * tk)
            vt_sc[:, rows] = _transpose_bf16(v_ref[rows, :])

    q1t_sc[...] = _transpose_bf16(q1_ref[...])
    q2t_sc[...] = _transpose_bf16(q2_ref[...])
    for r in (m1_sc, m2_sc):
        r[...] = jnp.full(r.shape, -jnp.inf, F32)
    for r in (l1_sc, l2_sc, a1_sc, a2_sc):
        r[...] = jnp.zeros(r.shape, F32)

    chains = ((k1_ref, q1t_sc, m1_sc, l1_sc, a1_sc), (k2_ref, q2t_sc, m2_sc, l2_sc, a2_sc))

    def scores(c, chunk):
        start = pl.multiple_of(chunk * tk, tk)
        return jnp.dot(chains[c][0][pl.ds(start, tk), :], chains[c][1][...], preferred_element_type=F32)

    def consume(c, chunk, par, masked):
        s = s_sc[par * 2 + c][...]
        if masked:
            s = jnp.where(_causal_keep(s.shape), s, -jnp.inf)
        start = pl.multiple_of(chunk * tk, tk)
        _softmax_step(s, vt_sc[:, pl.ds(start, tk)], *chains[c][2:])

    for c in range(2):
        s_sc[c][...] = scores(c, 0)

    def body(kc, carry):
        def run(par):
            for c in range(2):
                s_sc[(1 - par) * 2 + c][...] = scores(c, kc + 1)
                consume(c, kc, par, False)
        pl.when(kc % 2 == 0)(functools.partial(run, 0))
        pl.when(kc % 2 == 1)(functools.partial(run, 1))
        return carry

    lax.fori_loop(0, qi, body, 0)

    def diagonal(par):
        for c in range(2):
            consume(c, qi, par, True)
    pl.when(qi % 2 == 0)(functools.partial(diagonal, 0))
    pl.when(qi % 2 == 1)(functools.partial(diagonal, 1))

    lp = lam_ref[...]
    lam = (jnp.exp(jnp.sum(lp[0:1, :] * lp[1:2, :], axis=-1, keepdims=True))
           - jnp.exp(jnp.sum(lp[2:3, :] * lp[3:4, :], axis=-1, keepdims=True)) + lam_init)
    y = (a1_sc[...] / l1_sc[...]).T - lam * (a2_sc[...] / l2_sc[...]).T
    ms = jnp.mean(y * y, axis=-1, keepdims=True)
    y = (y * lax.rsqrt(ms + EPS)) * g_ref[...]
    o_ref[...] = (y * (1.0 - lam_init)).astype(o_ref.dtype)


def _diff_attn(qkv, lam_p, subln_g, b, s, tq, lam_init):
    t = b * s
    nq = s // tq
    qb = 3 * FOX_HEADS
    kb = qb + 2 * DIFF_HEADS
    vb = (kb + 2 * DIFF_HEADS) // 2
    kern = functools.partial(_diff_kernel, tq=tq, tk=tq, s_len=s, lam_init=lam_init)
    return pl.pallas_call(
        kern,
        grid=(b, DIFF_HEADS, nq),
        in_specs=[
            pl.BlockSpec((tq, HEAD_DIM), lambda bi, h, qi: (bi * nq + qi, qb + 2 * h)),
            pl.BlockSpec((tq, HEAD_DIM), lambda bi, h, qi: (bi * nq + qi, qb + 2 * h + 1)),
            pl.BlockSpec((s, HEAD_DIM), lambda bi, h, qi: (bi, kb + 2 * h)),
            pl.BlockSpec((s, HEAD_DIM), lambda bi, h, qi: (bi, kb + 2 * h + 1)),
            pl.BlockSpec((s, DIFF_V_DIM), lambda bi, h, qi: (bi, vb + h)),
            pl.BlockSpec((4, HEAD_DIM), lambda bi, h, qi: (0, 0)),
            pl.BlockSpec((1, DIFF_V_DIM), lambda bi, h, qi: (0, 0)),
        ],
        out_specs=pl.BlockSpec((tq, DIFF_V_DIM), lambda bi, h, qi: (bi * nq + qi, h)),
        out_shape=jax.ShapeDtypeStruct((t, DIFF_HEADS * DIFF_V_DIM), BF16),
        scratch_shapes=[
            pltpu.VMEM((HEAD_DIM, tq), BF16), pltpu.VMEM((HEAD_DIM, tq), BF16),
            pltpu.VMEM((DIFF_V_DIM, s), BF16),
            pltpu.VMEM((1, tq), F32), pltpu.VMEM((1, tq), F32), pltpu.VMEM((DIFF_V_DIM, tq), F32),
            pltpu.VMEM((1, tq), F32), pltpu.VMEM((1, tq), F32), pltpu.VMEM((DIFF_V_DIM, tq), F32),
        ] + [pltpu.VMEM((tq, tq), F32) for _ in range(4)],
        compiler_params=_cparams(("parallel", "parallel", "arbitrary")),
    )(qkv, qkv, qkv, qkv, qkv, lam_p, subln_g)


def _pack_bf16_pairs(xb):
    w = xb.shape[1] // 2
    lo = lax.bitcast_convert_type(xb[:, :w].astype(F32), jnp.uint32)
    hi = lax.bitcast_convert_type(xb[:, w:].astype(F32), jnp.uint32)
    return (lo >> 16) | (hi & jnp.uint32(0xFFFF0000))


def _unpack_bf16_pairs(p):
    lo = lax.bitcast_convert_type(p << 16, F32).astype(BF16)
    hi = lax.bitcast_convert_type(p & jnp.uint32(0xFFFF0000), F32).astype(BF16)
    return lo, hi


def _merge_kernel(yf_ref, yd_ref, g0_ref, g1_ref, x_ref, wbf_ref, wbd_ref, wo_ref, n2_ref,
                  rw_ref, rb_ref, h_ref, u_ref, idx_ref, gw_ref):
    ya = jnp.dot(yf_ref[...], wbf_ref[...], preferred_element_type=F32)
    yb = jnp.dot(yd_ref[...], wbd_ref[...], preferred_element_type=F32)
    merged = g0_ref[...] * ya + g1_ref[...] * yb
    h = x_ref[...] + jnp.dot(merged.astype(BF16), wo_ref[...], preferred_element_type=F32)
    h_ref[...] = h
    ms = jnp.mean(h * h, axis=-1, keepdims=True)
    u = (h * lax.rsqrt(ms + EPS)) * n2_ref[...]
    ub = u.astype(BF16)
    u_ref[...] = _pack_bf16_pairs(ub)
    logits = jnp.dot(ub, rw_ref[...], preferred_element_type=F32) + rb_ref[...]

    lane = lax.broadcasted_iota(jnp.int32, logits.shape, 1).astype(F32)
    work = logits
    vals, idxs = [], []
    for _ in range(TOP_K):
        m = jnp.max(work, axis=-1, keepdims=True)
        sel = jnp.min(jnp.where(work == m, lane, float(LANES)), axis=-1, keepdims=True)
        vals.append(m)
        idxs.append(sel)
        work = jnp.where(lane == sel, -jnp.inf, work)
    es = [jnp.exp(v - vals[0]) for v in vals]
    den = es[0]
    for e in es[1:]:
        den = den + e
    idx_out = jnp.zeros(logits.shape, F32)
    gw_out = jnp.zeros(logits.shape, F32)
    for k in range(TOP_K):
        idx_out = jnp.where(lane == float(k), idxs[k], idx_out)
        gw_out = jnp.where(lane == float(k), es[k] / den, gw_out)
    idx_ref[...] = idx_out.astype(jnp.int32)
    gw_ref[...] = gw_out


def _merge(yf, yd, gate, x2, wbf, wbd, wo, n2, rw, rb, tm):
    t, d = x2.shape
    fw = yf.shape[1]
    dw = yd.shape[1]
    const = dict(pipeline_mode=pl.Buffered(1))
    return pl.pallas_call(
        _merge_kernel,
        grid=(t // tm,),
        in_specs=[
            pl.BlockSpec((tm, fw), lambda i: (i, 0)),
            pl.BlockSpec((tm, dw), lambda i: (i, 0)),
            pl.BlockSpec((tm, d), lambda i: (i, 0)),
            pl.BlockSpec((tm, d), lambda i: (i, 1)),
            pl.BlockSpec((tm, d), lambda i: (i, 0)),
            pl.BlockSpec((fw, d), lambda i: (0, 0), **const),
            pl.BlockSpec((dw, d), lambda i: (0, 0), **const),
            pl.BlockSpec((d, d), lambda i: (0, 0), **const),
            pl.BlockSpec((1, d), lambda i: (0, 0)),
            pl.BlockSpec((d, LANES), lambda i: (0, 0), **const),
            pl.BlockSpec((1, LANES), lambda i: (0, 0)),
        ],
        out_specs=[
            pl.BlockSpec((tm, d), lambda i: (i, 0)),
            pl.BlockSpec((tm, d // 2), lambda i: (i, 0)),
            pl.BlockSpec((tm, LANES), lambda i: (i, 0)),
            pl.BlockSpec((tm, LANES), lambda i: (i, 0)),
        ],
        out_shape=[
            jax.ShapeDtypeStruct((t, d), F32),
            jax.ShapeDtypeStruct((t, d // 2), jnp.uint32),
            jax.ShapeDtypeStruct((t, LANES), jnp.int32),
            jax.ShapeDtypeStruct((t, LANES), F32),
        ],
        compiler_params=_cparams(("parallel",)),
    )(yf, yd, gate, gate, x2, wbf, wbd, wo, n2, rw, rb)


def _dispatch_kernel(cnt_ref, pstart_ref, pend_ref, roa_ref, u_ref, xs_hbm, stage, zeros, sems,
                     *, tb, sub, n_rows):
    s = pl.program_id(0)
    n = pl.num_programs(0)
    slot = s % 2
    n_dma = TOP_K * tb

    def wait_slot(sl):
        pltpu.make_async_copy(stage.at[sl], xs_hbm.at[pl.ds(0, n_dma), :], sems.at[sl]).wait()

    @pl.when(s == 0)
    def _():
        zeros[...] = jnp.zeros(zeros.shape, zeros.dtype)

    @pl.when(s >= 2)
    def _():
        wait_slot(slot)

    blk = u_ref[...]
    for k in range(TOP_K):
        stage[slot, pl.ds(k * tb, tb), :] = blk

    def body(g, carry):
        for q8 in range(ISSUE_UNROLL):
            q = g * ISSUE_UNROLL + q8
            pltpu.make_async_copy(stage.at[slot, pl.ds(q, 1), :],
                                  xs_hbm.at[pl.ds(roa_ref[0, 0, q], 1), :], sems.at[slot]).start()
        return carry
    lax.fori_loop(0, n_dma // ISSUE_UNROLL, body, 0)

    @pl.when(s == n - 1)
    def _():
        @pl.when(n >= 2)
        def _():
            wait_slot(1 - slot)
        wait_slot(slot)

        def zero_row(r):
            return pltpu.make_async_copy(zeros.at[pl.ds(0, 1), :], xs_hbm.at[pl.ds(r, 1), :], sems.at[2])

        def zero_block(r):
            return pltpu.make_async_copy(zeros, xs_hbm.at[pl.ds(pl.multiple_of(r, sub), sub), :], sems.at[2])

        def fill(first, stop, copy, step):
            cnt = (stop - first) // step

            def start(q, carry):
                copy(first + q * step).start()
                return carry

            def wait(q, carry):
                copy(first).wait()
                return carry
            lax.fori_loop(0, cnt, start, 0)
            lax.fori_loop(0, cnt, wait, 0)

        def per_expert(e, carry):
            first = pstart_ref[e] + cnt_ref[e]
            edge = (first + sub - 1) // sub * sub
            fill(first, edge, zero_row, 1)
            fill(edge, pend_ref[e], zero_block, sub)
            return carry
        lax.fori_loop(0, N_EXPERTS, per_expert, 0)
        fill(pend_ref[N_EXPERTS - 1], n_rows, zero_block, sub)


def _dispatch(counts, pad_start, pad_end, roa3, u_packed, n_rows, tb, sub):
    t, w = u_packed.shape
    kern = functools.partial(_dispatch_kernel, tb=tb, sub=sub, n_rows=n_rows)
    return pl.pallas_call(
        kern,
        grid_spec=pltpu.PrefetchScalarGridSpec(
            num_scalar_prefetch=3,
            grid=(t // tb,),
            in_specs=[
                pl.BlockSpec((1, 1, TOP_K * tb), lambda s, c, ps, pe: (s, 0, 0), memory_space=pltpu.SMEM),
                pl.BlockSpec((tb, w), lambda s, c, ps, pe: (s, 0)),
            ],
            out_specs=pl.BlockSpec(memory_space=pl.ANY),
            scratch_shapes=[
                pltpu.VMEM((2, TOP_K * tb, w), jnp.uint32),
                pltpu.VMEM((sub, w), jnp.uint32),
                pltpu.SemaphoreType.DMA((3,)),
            ],
        ),
        out_shape=jax.ShapeDtypeStruct((n_rows, w), jnp.uint32),
        compiler_params=_cparams(("arbitrary",)),
    )(counts, pad_start, pad_end, roa3, u_packed)


def _expert_kernel(te_ref, ts_ref, nch_ref, xs_ref, wg_ref, wu_ref, bg_ref, bu_ref, wd_ref,
                   bd_ref, out_ref, x_sc, act_sc, *, chunks, nf):
    i = pl.program_id(0)
    j = pl.program_id(1)
    nch = nch_ref[i]
    half = x_sc.shape[1] // 2

    @pl.when(jnp.logical_and(j == 0, nch > 0))
    def _():
        lo, hi = _unpack_bf16_pairs(xs_ref[...])
        x_sc[:, :half] = lo
        x_sc[:, half:] = hi

    full = nch == len(chunks)

    def gate_up(rs, wg, wu):
        x = x_sc[rs, :]
        hg = jnp.dot(x, wg, preferred_element_type=F32) + bg_ref[0]
        hu = jnp.dot(x, wu, preferred_element_type=F32) + bu_ref[0]
        hg = jnp.minimum(hg, SWIGLU_LIMIT)
        hl = jnp.clip(hu, -SWIGLU_LIMIT, SWIGLU_LIMIT)
        act = hg * jax.nn.sigmoid(SWIGLU_ALPHA * hg) * (hl + 1.0)
        act_sc[j, rs, :] = act.astype(BF16)

    def down(rs, wd):
        act = jnp.concatenate([act_sc[c, rs, :] for c in range(nf)], axis=1)
        out_ref[rs, :] = jnp.dot(act, wd, preferred_element_type=F32) + bd_ref[0]

    partial_tile = jnp.logical_and(nch > 0, nch < len(chunks))
    row_slices = [slice(a, b) for a, b in chunks]

    @pl.when(jnp.logical_and(j < nf, full))
    def _():
        gate_up(slice(None), wg_ref[0].astype(BF16), wu_ref[0].astype(BF16))

    @pl.when(jnp.logical_and(j < nf, partial_tile))
    def _():
        wg = wg_ref[0].astype(BF16)
        wu = wu_ref[0].astype(BF16)
        for c, rs in enumerate(row_slices[:-1]):
            pl.when(c < nch)(functools.partial(gate_up, rs, wg, wu))

    @pl.when(jnp.logical_and(j >= nf, full))
    def _():
        down(slice(None), wd_ref[0].astype(BF16))

    @pl.when(jnp.logical_and(j >= nf, partial_tile))
    def _():
        wd = wd_ref[0].astype(BF16)
        for c, rs in enumerate(row_slices[:-1]):
            pl.when(c < nch)(functools.partial(down, rs, wd))

    @pl.when(jnp.logical_and(j >= nf, nch < len(chunks)))
    def _():
        for c, (a, b) in enumerate(chunks):
            @pl.when(c >= nch)
            def _():
                out_ref[a:b, :] = jnp.zeros((b - a, out_ref.shape[1]), out_ref.dtype)


def _experts(tile_e, tile_src, tile_nch, xs, w_gu, b_gu3, w_down, b_down3, tm_e, chunks, tf):
    n_rows = xs.shape[0]
    n_tiles = n_rows // tm_e
    d = 2 * xs.shape[1]
    dff = w_down.shape[1]
    nf = dff // tf
    nn = d // tf
    assert nn == nf
    kern = functools.partial(_expert_kernel, chunks=chunks, nf=nf)

    def c1(i, j, ns):
        return jnp.where(ns[i] > 0, jnp.minimum(j, nf - 1), nf - 1)

    def c2(i, j, ns):
        return jnp.where(ns[i] > 0, jnp.maximum(j - nf, 0), nf - 1)

    return pl.pallas_call(
        kern,
        grid_spec=pltpu.PrefetchScalarGridSpec(
            num_scalar_prefetch=3,
            grid=(n_tiles, 2 * nf),
            in_specs=[
                pl.BlockSpec((tm_e, d // 2), lambda i, j, te, ts, ns: (ts[i], 0)),
                pl.BlockSpec((1, d, tf), lambda i, j, te, ts, ns: (te[i], 0, c1(i, j, ns))),
                pl.BlockSpec((1, d, tf), lambda i, j, te, ts, ns: (te[i], 0, nf + c1(i, j, ns))),
                pl.BlockSpec((1, 1, tf), lambda i, j, te, ts, ns: (te[i], 0, c1(i, j, ns))),
                pl.BlockSpec((1, 1, tf), lambda i, j, te, ts, ns: (te[i], 0, nf + c1(i, j, ns))),
                pl.BlockSpec((1, dff, tf), lambda i, j, te, ts, ns: (te[i], 0, c2(i, j, ns))),
                pl.BlockSpec((1, 1, tf), lambda i, j, te, ts, ns: (te[i], 0, c2(i, j, ns))),
            ],
            out_specs=pl.BlockSpec((tm_e, tf), lambda i, j, te, ts, ns: (i, jnp.maximum(j - nf, 0))),
            scratch_shapes=[
                pltpu.VMEM((tm_e, d), BF16),
                pltpu.VMEM((nf, tm_e, tf), BF16),
            ],
        ),
        out_shape=jax.ShapeDtypeStruct((n_rows, d), F32),
        compiler_params=_cparams(("arbitrary", "arbitrary")),
    )(tile_e, tile_src, tile_nch, xs, w_gu, w_gu, b_gu3, b_gu3, w_down, b_down3)


def _combine_kernel(cur_ref, nxt_ref, yr_hbm, h_ref, gw_ref, gf_ref, out_ref, ysel, sems, *, tmc,
                    final_norm):
    i = pl.program_id(0)
    n = pl.num_programs(0)
    slot = i % 2
    n_dma = TOP_K * tmc

    def row_copy(row, q, sl):
        return pltpu.make_async_copy(yr_hbm.at[pl.ds(row, 1), :], ysel.at[sl, pl.ds(q, 1), :], sems.at[sl])

    def issue(idx_ref, sl):
        def body(g, carry):
            for q8 in range(ISSUE_UNROLL):
                q = g * ISSUE_UNROLL + q8
                row_copy(idx_ref[0, 0, q], q, sl).start()
            return carry
        lax.fori_loop(0, n_dma // ISSUE_UNROLL, body, 0)

    @pl.when(i == 0)
    def _():
        issue(cur_ref, 0)

    @pl.when(i + 1 < n)
    def _():
        issue(nxt_ref, 1 - slot)

    pltpu.make_async_copy(yr_hbm.at[pl.ds(0, n_dma), :], ysel.at[slot], sems.at[slot]).wait()

    gw = gw_ref[...]
    y = h_ref[...]
    for k in range(TOP_K):
        y = y + gw[:, k:k + 1] * ysel[slot, k * tmc:(k + 1) * tmc, :]
    if final_norm:
        ms = jnp.mean(y * y, axis=-1, keepdims=True)
        y = (y * lax.rsqrt(ms + EPS)) * gf_ref[...]
    out_ref[...] = y


def _combine(roa3, y_rows, h, gw, gf, tmc, final_norm):
    t, d = h.shape
    n = t // tmc
    kern = functools.partial(_combine_kernel, tmc=tmc, final_norm=final_norm)
    return pl.pallas_call(
        kern,
        grid=(n,),
        in_specs=[
            pl.BlockSpec((1, 1, TOP_K * tmc), lambda i: (i, 0, 0), memory_space=pltpu.SMEM),
            pl.BlockSpec((1, 1, TOP_K * tmc), lambda i: (jnp.minimum(i + 1, n - 1), 0, 0),
                         memory_space=pltpu.SMEM),
            pl.BlockSpec(memory_space=pl.ANY),
            pl.BlockSpec((tmc, d), lambda i: (i, 0)),
            pl.BlockSpec((tmc, LANES), lambda i: (i, 0)),
            pl.BlockSpec((1, d), lambda i: (0, 0)),
        ],
        out_specs=pl.BlockSpec((tmc, d), lambda i: (i, 0)),
        out_shape=jax.ShapeDtypeStruct((t, d), F32),
        scratch_shapes=[pltpu.VMEM((2, TOP_K * tmc, d), F32), pltpu.SemaphoreType.DMA((2,))],
        compiler_params=_cparams(("arbitrary",)),
    )(roa3, roa3, y_rows, h, gw, gf)


def _routing(top_idx, tm_e, chunks):
    t = top_idx.shape[0]
    a = t * TOP_K
    flat_e = top_idx.reshape(a)
    onehot = (flat_e[:, None] == jnp.arange(N_EXPERTS, dtype=jnp.int32)[None, :]).astype(jnp.int32)
    csum = jnp.cumsum(onehot, axis=0)
    counts = csum[-1]
    rank = jnp.sum(onehot * csum, axis=1) - 1
    padded = (counts + tm_e - 1) // tm_e * tm_e
    pad_end = jnp.cumsum(padded)
    pad_start = pad_end - padded
    row_of_assign = (pad_start[flat_e] + rank).astype(jnp.int32)
    n_tiles = -(-a // tm_e) + N_EXPERTS
    tile_start = jnp.arange(n_tiles, dtype=jnp.int32) * tm_e
    n_used = pad_end[-1] // tm_e
    tile_e_raw = jnp.minimum(jnp.searchsorted(pad_end, tile_start, side='right'), N_EXPERTS - 1)
    used = tile_start < pad_end[-1]
    last = jnp.maximum(n_used - 1, 0)
    tile_src = jnp.where(used, jnp.arange(n_tiles, dtype=jnp.int32), last).astype(jnp.int32)
    tile_e = tile_e_raw[tile_src].astype(jnp.int32)
    valid_rows = jnp.clip(counts[tile_e_raw] - (tile_start - pad_start[tile_e_raw]), 0, tm_e)
    starts = jnp.asarray([a for a, _ in chunks], jnp.int32)
    tile_nch = jnp.sum(starts[None, :] < jnp.where(used, valid_rows, 0)[:, None], axis=1).astype(jnp.int32)
    bounds = (counts.astype(jnp.int32), pad_start.astype(jnp.int32), pad_end.astype(jnp.int32))
    return bounds, row_of_assign, tile_e, tile_src, tile_nch, n_tiles * tm_e


def kernel(x, positions, norm1_g, w_in, b_fgate, b_gate, lam_q1, lam_k1, lam_q2, lam_k2, subln_g,
           w_branch_fox, w_branch_diff, w_out, norm2_g, router_w, router_b, w_gu, b_gu, w_down,
           b_down, normf_g):
    b, s, d = x.shape
    t = b * s
    depth = norm1_g.shape[0]
    fox_w = FOX_HEADS * HEAD_DIM
    tm_a = _pick(t, 1024)
    tq = _pick(s, 512)
    tm_d = _pick(t, 256)
    tm_e, chunk, zsub, tf = 1152, 512, 128, 512
    chunks = tuple((a, min(a + chunk, tm_e)) for a in range(0, tm_e, chunk))
    tmc = _pick(t, 128)

    half = ROPE_DIM // 2
    inv_freq = ROPE_THETA ** (-jnp.arange(0, ROPE_DIM, 2, dtype=F32) / ROPE_DIM)
    zeros = jnp.zeros((LANES - ROPE_DIM,), F32)
    rope_tab = jnp.zeros((8, LANES), F32)
    rope_tab = rope_tab.at[0].set(jnp.concatenate([inv_freq, inv_freq, zeros]))
    rope_tab = rope_tab.at[1].set(jnp.concatenate([-jnp.ones((half,), F32), jnp.zeros((half,), F32), zeros]))
    rope_tab = rope_tab.at[2].set(jnp.concatenate([jnp.zeros((half,), F32), jnp.ones((half,), F32), zeros]))
    pos = positions.astype(F32).reshape(t, 1)

    h = x.reshape(t, d)
    for l in range(depth):
        wi = w_in[l]
        w_left = wi[:, :3 * fox_w].astype(BF16)
        w_right = wi[:, 3 * fox_w + FOX_HEADS:].astype(BF16)
        w_f = jnp.pad(wi[:, 3 * fox_w:3 * fox_w + FOX_HEADS], ((0, 0), (0, LANES - FOX_HEADS))).astype(BF16)
        qkv, gate, flog = _inproj(h, pos, norm1_g[l][None, :], w_left, w_right, w_f, rope_tab,
                                  b_gate[l][None, :], tm_a)

        fl = flog[:, :FOX_HEADS].reshape(b, s, FOX_HEADS).transpose(0, 2, 1).reshape(b * FOX_HEADS, s)
        bf = jnp.tile(b_fgate[l].astype(F32), b).reshape(b * FOX_HEADS, 1)
        c3 = _fgate(fl, bf).reshape(b * FOX_HEADS, 1, s)

        y_fox = _fox_attn(qkv, c3, b, s, tq)
        lam_init = 0.8 - 0.6 * math.exp(-0.3 * l)
        lam_p = jnp.stack([lam_q1[l], lam_k1[l], lam_q2[l], lam_k2[l]]).astype(F32)
        y_diff = _diff_attn(qkv, lam_p, subln_g[l][None, :].astype(F32), b, s, tq, lam_init)

        rw = jnp.pad(router_w[l], ((0, 0), (0, LANES - N_EXPERTS))).astype(BF16)
        rb = jnp.concatenate([router_b[l].astype(F32), jnp.full((LANES - N_EXPERTS,), NEG_BIG, F32)])[None, :]
        h, u2, idx128, gw128 = _merge(y_fox, y_diff, gate, h, w_branch_fox[l].astype(BF16),
                                      w_branch_diff[l].astype(BF16), w_out[l].astype(BF16),
                                      norm2_g[l][None, :], rw, rb, tm_d)

        bounds, roa, tile_e, tile_src, tile_nch, n_rows = _routing(idx128[:, :TOP_K], tm_e, chunks)
        roa3 = roa.reshape(t // tmc, tmc, TOP_K).transpose(0, 2, 1).reshape(t // tmc, 1, TOP_K * tmc)
        xs = _dispatch(*bounds, roa3, u2, n_rows, tmc, zsub)
        y_rows = _experts(tile_e, tile_src, tile_nch, xs, w_gu[l], b_gu[l][:, None, :],
                          w_down[l], b_down[l][:, None, :], tm_e, chunks, tf)
        h = _combine(roa3, y_rows, h, gw128, normf_g[None, :], tmc, final_norm=(l == depth - 1))
    return h.reshape(b, s, d)
```

```python
import functools
import math

import jax
import jax.numpy as jnp
from jax import lax
from jax.experimental import pallas as pl
from jax.experimental.pallas import tpu as pltpu

HEAD_DIM = 128
FOX_HEADS = 8
DIFF_HEADS = 4
DIFF_V_DIM = 2 * HEAD_DIM
ROPE_THETA = 500000.0
ROPE_DIM = HEAD_DIM // 4
N_EXPERTS = 32
TOP_K = 4
SWIGLU_ALPHA = 1.702
SWIGLU_LIMIT = 7.0
EPS = 1e-5
LANES = 128

F32 = jnp.float32
BF16 = jnp.bfloat16
NEG_BIG = -1e30
LOG2E = math.log2(math.e)
ISSUE_UNROLL = 8
VMEM_LIMIT = 56 * 1024 * 1024


def _cparams(sem, vmem=VMEM_LIMIT):
    return pltpu.CompilerParams(dimension_semantics=sem, vmem_limit_bytes=vmem)


def _pick(n, pref):
    t = min(n, pref)
    while n % t:
        t //= 2
    return t


QKV_TILE = 1024
QKV_LEFT_TILES = 3
QKV_ROTARY_TILES = (3, 4)
QKV_SCALED_TILES = (0, 3)


def _qkv_kernel(x_ref, g1_ref, wa_ref, wb_ref, wf_ref, pos_ref, tab_ref,
                qkv_ref, u_ref, flog_ref, cos_sc, sa_sc, sb_sc, *, scale):
    j = pl.program_id(1)

    @pl.when(j == 0)
    def _():
        xf = x_ref[...]
        ms = jnp.mean(xf * xf, axis=-1, keepdims=True)
        u = (xf * lax.rsqrt(ms + EPS)) * g1_ref[...]
        ub = u.astype(BF16)
        u_ref[...] = ub
        flog_ref[...] = jnp.dot(ub, wf_ref[...], preferred_element_type=F32)
        ang = pos_ref[...] * tab_ref[0:1, :]
        sn = jnp.sin(ang)
        cos_sc[...] = jnp.cos(ang)
        sa_sc[...] = sn * tab_ref[1:2, :]
        sb_sc[...] = sn * tab_ref[2:3, :]

    any_of = lambda tiles: functools.reduce(jnp.logical_or, [j == q for q in tiles])
    mult = jnp.where(any_of(QKV_SCALED_TILES), scale, 1.0).astype(F32)
    rotary = any_of(QKV_ROTARY_TILES)
    left = j < QKV_LEFT_TILES

    def plain(w_ref):
        acc = jnp.dot(u_ref[...], w_ref[...], preferred_element_type=F32)
        qkv_ref[...] = (acc * mult).astype(BF16)

    pl.when(left)(functools.partial(plain, wa_ref))
    pl.when(jnp.logical_not(jnp.logical_or(left, rotary)))(functools.partial(plain, wb_ref))

    @pl.when(rotary)
    def _():
        acc = jnp.dot(u_ref[...], wb_ref[...], preferred_element_type=F32)
        ca = cos_sc[...] * mult
        cb = sa_sc[...] * mult
        cc = sb_sc[...] * mult
        for c in range(acc.shape[1] // HEAD_DIM):
            t = acc[:, c * HEAD_DIM:(c + 1) * HEAD_DIM]
            r = (t * ca + pltpu.roll(t, HEAD_DIM - ROPE_DIM // 2, 1) * cb
                 + pltpu.roll(t, ROPE_DIM // 2, 1) * cc)
            qkv_ref[:, c * HEAD_DIM:(c + 1) * HEAD_DIM] = r.astype(BF16)


def _gates_kernel(u_ref, w_ref, bg_ref, gate_ref):
    acc = jnp.dot(u_ref[...], w_ref[...], preferred_element_type=F32)
    gate_ref[...] = jax.nn.sigmoid(acc + bg_ref[...])


def _inproj(x2, pos, g1, w_left, w_right, w_f, rope_tab, b_gate, tm):
    t, d = x2.shape
    tn = QKV_TILE
    n_qkv = 6
    n_left = QKV_LEFT_TILES
    n_gate = w_right.shape[1] // tn - (n_qkv - n_left)
    kern = functools.partial(_qkv_kernel, scale=LOG2E / math.sqrt(HEAD_DIM))
    qkv, u, flog = pl.pallas_call(
        kern,
        grid=(t // tm, n_qkv),
        in_specs=[
            pl.BlockSpec((tm, d), lambda i, j: (i, 0)),
            pl.BlockSpec((1, d), lambda i, j: (0, 0)),
            pl.BlockSpec((d, tn), lambda i, j: (0, jnp.minimum(j, n_left - 1))),
            pl.BlockSpec((d, tn), lambda i, j: (0, jnp.maximum(j - n_left, 0))),
            pl.BlockSpec((d, LANES), lambda i, j: (0, 0)),
            pl.BlockSpec((tm, 1), lambda i, j: (i, 0)),
            pl.BlockSpec((8, LANES), lambda i, j: (0, 0)),
        ],
        out_specs=[
            pl.BlockSpec((tm, tn), lambda i, j: (i, j)),
            pl.BlockSpec((tm, d), lambda i, j: (i, 0)),
            pl.BlockSpec((tm, LANES), lambda i, j: (i, 0)),
        ],
        out_shape=[
            jax.ShapeDtypeStruct((t, n_qkv * tn), BF16),
            jax.ShapeDtypeStruct((t, d), BF16),
            jax.ShapeDtypeStruct((t, LANES), F32),
        ],
        scratch_shapes=[
            pltpu.VMEM((tm, LANES), F32),
            pltpu.VMEM((tm, LANES), F32),
            pltpu.VMEM((tm, LANES), F32),
        ],
        compiler_params=_cparams(("parallel", "arbitrary")),
    )(x2, g1, w_left, w_right, w_f, pos, rope_tab)
    gate = pl.pallas_call(
        _gates_kernel,
        grid=(t // tm, n_gate),
        in_specs=[
            pl.BlockSpec((tm, d), lambda i, j: (i, 0)),
            pl.BlockSpec((d, tn), lambda i, j: (0, n_qkv - n_left + j)),
            pl.BlockSpec((1, tn), lambda i, j: (0, j)),
        ],
        out_specs=pl.BlockSpec((tm, tn), lambda i, j: (i, j)),
        out_shape=jax.ShapeDtypeStruct((t, n_gate * tn), F32),
        compiler_params=_cparams(("parallel", "arbitrary")),
    )(u, w_right, b_gate)
    return qkv, gate, flog


def _fgate_kernel(fl_ref, b_ref, c_ref):
    z = fl_ref[...] + b_ref[...]
    x = jnp.minimum(z, 0.0) - jnp.log(1.0 + jnp.exp(-jnp.abs(z)))
    n = x.shape[1]
    lane = lax.broadcasted_iota(jnp.int32, x.shape, 1)
    sh = 1
    while sh < n:
        x = x + jnp.where(lane >= sh, pltpu.roll(x, sh, 1), 0.0)
        sh *= 2
    c_ref[...] = x


def _fgate(fl, b):
    r, s = fl.shape
    return pl.pallas_call(
        _fgate_kernel,
        grid=(1,),
        in_specs=[pl.BlockSpec((r, s), lambda i: (0, 0)), pl.BlockSpec((r, 1), lambda i: (0, 0))],
        out_specs=pl.BlockSpec((r, s), lambda i: (0, 0)),
        out_shape=jax.ShapeDtypeStruct((r, s), F32),
        compiler_params=_cparams(("arbitrary",)),
    )(fl, b)


def _transpose_bf16(x):
    return x.astype(F32).T.astype(BF16)


def _softmax_step(s, vt, m_sc, l_sc, acc_sc):
    m_old = m_sc[...]
    m_new = jnp.maximum(m_old, jnp.max(s, axis=0, keepdims=True))
    alpha = jnp.exp2(m_old - m_new)
    p = jnp.exp2(s - m_new)
    l_sc[...] = alpha * l_sc[...] + jnp.sum(p, axis=0, keepdims=True)
    acc_sc[...] = alpha * acc_sc[...] + jnp.dot(vt, p.astype(BF16), preferred_element_type=F32)
    m_sc[...] = m_new


def _causal_keep(shape):
    kv = lax.broadcasted_iota(jnp.int32, shape, 0)
    qq = lax.broadcasted_iota(jnp.int32, shape, 1)
    return kv <= qq


FOX_HEADS_PER_STEP = 2


def _fox_kernel(q_ref, k_ref, v_ref, c_ref, o_ref, *scratch, tq, tk, s_len):
    hp = FOX_HEADS_PER_STEP
    qt_sc, kaug_sc, vt_sc, m_sc, l_sc, acc_sc = (scratch[n * hp:(n + 1) * hp] for n in range(6))
    s_sc = scratch[6 * hp:]
    qi = pl.program_id(2)
    head_cols = [slice(h * HEAD_DIM, (h + 1) * HEAD_DIM) for h in range(hp)]

    @pl.when(qi == 0)
    def _():
        sub = lax.broadcasted_iota(jnp.int32, (HEAD_DIM, tk), 0)
        for h in range(hp):
            for ch in range(s_len // tk):
                rows = slice(ch * tk, (ch + 1) * tk)
                vt_sc[h][:, rows] = _transpose_bf16(v_ref[rows, head_cols[h]])
                c = c_ref[h, :, rows] * (-LOG2E)
                hi = c.astype(BF16).astype(F32)
                mid = (c - hi).astype(BF16).astype(F32)
                lo = (c - hi - mid).astype(BF16).astype(F32)
                tab = jnp.where(sub == 0, hi, jnp.where(sub == 1, mid, jnp.where(sub == 2, lo, 0.0)))
                kaug_sc[h][rows, HEAD_DIM:] = tab.T.astype(BF16)
                kaug_sc[h][rows, :HEAD_DIM] = k_ref[rows, head_cols[h]]

    sub_q = lax.broadcasted_iota(jnp.int32, (HEAD_DIM, tq), 0)
    for h in range(hp):
        qt_sc[h][:HEAD_DIM, :] = _transpose_bf16(q_ref[:, head_cols[h]])
        qt_sc[h][HEAD_DIM:, :] = jnp.where(sub_q < 3, 1.0, 0.0).astype(BF16)
        m_sc[h][...] = jnp.full(m_sc[h].shape, -jnp.inf, F32)
        l_sc[h][...] = jnp.zeros(l_sc[h].shape, F32)
        acc_sc[h][...] = jnp.zeros(acc_sc[h].shape, F32)

    def scores(h, chunk):
        start = pl.multiple_of(chunk * tk, tk)
        return jnp.dot(kaug_sc[h][pl.ds(start, tk), :], qt_sc[h][...], preferred_element_type=F32)

    def consume(h, chunk, par, masked):
        s = s_sc[par * hp + h][...]
        if masked:
            s = jnp.where(_causal_keep(s.shape), s, -jnp.inf)
        start = pl.multiple_of(chunk * tk, tk)
        _softmax_step(s, vt_sc[h][:, pl.ds(start, tk)], m_sc[h], l_sc[h], acc_sc[h])

    for h in range(hp):
        s_sc[h][...] = scores(h, 0)

    def body(kc, carry):
        def run(par):
            for h in range(hp):
                s_sc[(1 - par) * hp + h][...] = scores(h, kc + 1)
                consume(h, kc, par, False)
        pl.when(kc % 2 == 0)(functools.partial(run, 0))
        pl.when(kc % 2 == 1)(functools.partial(run, 1))
        return carry

    lax.fori_loop(0, qi, body, 0)

    def diagonal(par):
        for h in range(hp):
            consume(h, qi, par, True)
    pl.when(qi % 2 == 0)(functools.partial(diagonal, 0))
    pl.when(qi % 2 == 1)(functools.partial(diagonal, 1))
    for h in range(hp):
        o_ref[:, head_cols[h]] = (acc_sc[h][...] / l_sc[h][...]).T.astype(o_ref.dtype)


def _fox_attn(qkv, c3, b, s, tq):
    t = b * s
    nq = s // tq
    hp = FOX_HEADS_PER_STEP
    groups = FOX_HEADS // hp
    width = hp * HEAD_DIM
    kern = functools.partial(_fox_kernel, tq=tq, tk=tq, s_len=s)

    def per_head(shape, dtype):
        return [pltpu.VMEM(shape, dtype) for _ in range(hp)]

    return pl.pallas_call(
        kern,
        grid=(b, groups, nq),
        in_specs=[
            pl.BlockSpec((tq, width), lambda bi, g, qi: (bi * nq + qi, g)),
            pl.BlockSpec((s, width), lambda bi, g, qi: (bi, groups + g)),
            pl.BlockSpec((s, width), lambda bi, g, qi: (bi, 2 * groups + g)),
            pl.BlockSpec((hp, 1, s), lambda bi, g, qi: (bi * groups + g, 0, 0)),
        ],
        out_specs=pl.BlockSpec((tq, width), lambda bi, g, qi: (bi * nq + qi, g)),
        out_shape=jax.ShapeDtypeStruct((t, FOX_HEADS * HEAD_DIM), BF16),
        scratch_shapes=(per_head((2 * HEAD_DIM, tq), BF16) + per_head((s, 2 * HEAD_DIM), BF16)
                        + per_head((HEAD_DIM, s), BF16) + per_head((1, tq), F32)
                        + per_head((1, tq), F32) + per_head((HEAD_DIM, tq), F32)
                        + per_head((tq, tq), F32) + per_head((tq, tq), F32)),
        compiler_params=_cparams(("parallel", "parallel", "arbitrary")),
    )(qkv, qkv, qkv, c3)


def _diff_kernel(q1_ref, q2_ref, k1_ref, k2_ref, v_ref, lam_ref, g_ref, o_ref,
                 q1t_sc, q2t_sc, vt_sc, m1_sc, l1_sc, a1_sc, m2_sc, l2_sc, a2_sc, *s_sc,
                 tq, tk, s_len, lam_init):
    qi = pl.program_id(2)

    @pl.when(qi == 0)
    def _():
        for ch in range(s_len // tk):
            rows = slice(ch * tk, (ch + 1) * tk)
            vt_sc[:, rows] = _transpose_bf16(v_ref[rows, :])

    q1t_sc[...] = _transpose_bf16(q1_ref[...])
    q2t_sc[...] = _transpose_bf16(q2_ref[...])
    for r in (m1_sc, m2_sc):
        r[...] = jnp.full(r.shape, -jnp.inf, F32)
    for r in (l1_sc, l2_sc, a1_sc, a2_sc):
        r[...] = jnp.zeros(r.shape, F32)

    chains = ((k1_ref, q1t_sc, m1_sc, l1_sc, a1_sc), (k2_ref, q2t_sc, m2_sc, l2_sc, a2_sc))

    def scores(c, chunk):
        start = pl.multiple_of(chunk * tk, tk)
        return jnp.dot(chains[c][0][pl.ds(start, tk), :], chains[c][1][...], preferred_element_type=F32)

    def consume(c, chunk, par, masked):
        s = s_sc[par * 2 + c][...]
        if masked:
            s = jnp.where(_causal_keep(s.shape), s, -jnp.inf)
        start = pl.multiple_of(chunk * tk, tk)
        _softmax_step(s, vt_sc[:, pl.ds(start, tk)], *chains[c][2:])

    for c in range(2):
        s_sc[c][...] = scores(c, 0)

    def body(kc, carry):
        def run(par):
            for c in range(2):
                s_sc[(1 - par) * 2 + c][...] = scores(c, kc + 1)
                consume(c, kc, par, False)
        pl.when(kc % 2 == 0)(functools.partial(run, 0))
        pl.when(kc % 2 == 1)(functools.partial(run, 1))
        return carry

    lax.fori_loop(0, qi, body, 0)

    def diagonal(par):
        for c in range(2):
            consume(c, qi, par, True)
    pl.when(qi % 2 == 0)(functools.partial(diagonal, 0))
    pl.when(qi % 2 == 1)(functools.partial(diagonal, 1))

    lp = lam_ref[...]
    lam = (jnp.exp(jnp.sum(lp[0:1, :] * lp[1:2, :], axis=-1, keepdims=True))
           - jnp.exp(jnp.sum(lp[2:3, :] * lp[3:4, :], axis=-1, keepdims=True)) + lam_init)
    y = (a1_sc[...] / l1_sc[...]).T - lam * (a2_sc[...] / l2_sc[...]).T
    ms = jnp.mean(y * y, axis=-1, keepdims=True)
    y = (y * lax.rsqrt(ms + EPS)) * g_ref[...]
    o_ref[...] = (y * (1.0 - lam_init)).astype(o_ref.dtype)


def _diff_attn(qkv, lam_p, subln_g, b, s, tq, lam_init):
    t = b * s
    nq = s // tq
    qb = 3 * FOX_HEADS
    kb = qb + 2 * DIFF_HEADS
    vb = (kb + 2 * DIFF_HEADS) // 2
    kern = functools.partial(_diff_kernel, tq=tq, tk=tq, s_len=s, lam_init=lam_init)
    return pl.pallas_call(
        kern,
        grid=(b, DIFF_HEADS, nq),
        in_specs=[
            pl.BlockSpec((tq, HEAD_DIM), lambda bi, h, qi: (bi * nq + qi, qb + 2 * h)),
            pl.BlockSpec((tq, HEAD_DIM), lambda bi, h, qi: (bi * nq + qi, qb + 2 * h + 1)),
            pl.BlockSpec((s, HEAD_DIM), lambda bi, h, qi: (bi, kb + 2 * h)),
            pl.BlockSpec((s, HEAD_DIM), lambda bi, h, qi: (bi, kb + 2 * h + 1)),
            pl.BlockSpec((s, DIFF_V_DIM), lambda bi, h, qi: (bi, vb + h)),
            pl.BlockSpec((4, HEAD_DIM), lambda bi, h, qi: (0, 0)),
            pl.BlockSpec((1, DIFF_V_DIM), lambda bi, h, qi: (0, 0)),
        ],
        out_specs=pl.BlockSpec((tq, DIFF_V_DIM), lambda bi, h, qi: (bi * nq + qi, h)),
        out_shape=jax.ShapeDtypeStruct((t, DIFF_HEADS * DIFF_V_DIM), BF16),
        scratch_shapes=[
            pltpu.VMEM((HEAD_DIM, tq), BF16), pltpu.VMEM((HEAD_DIM, tq), BF16),
            pltpu.VMEM((DIFF_V_DIM, s), BF16),
            pltpu.VMEM((1, tq), F32), pltpu.VMEM((1, tq), F32), pltpu.VMEM((DIFF_V_DIM, tq), F32),
            pltpu.VMEM((1, tq), F32), pltpu.VMEM((1, tq), F32), pltpu.VMEM((DIFF_V_DIM, tq), F32),
        ] + [pltpu.VMEM((tq, tq), F32) for _ in range(4)],
        compiler_params=_cparams(("parallel", "parallel", "arbitrary")),
    )(qkv, qkv, qkv, qkv, qkv, lam_p, subln_g)


def _pack_bf16_pairs(xb):
    w = xb.shape[1] // 2
    lo = lax.bitcast_convert_type(xb[:, :w].astype(F32), jnp.uint32)
    hi = lax.bitcast_convert_type(xb[:, w:].astype(F32), jnp.uint32)
    return (lo >> 16) | (hi & jnp.uint32(0xFFFF0000))


def _unpack_bf16_pairs(p):
    lo = lax.bitcast_convert_type(p << 16, F32).astype(BF16)
    hi = lax.bitcast_convert_type(p & jnp.uint32(0xFFFF0000), F32).astype(BF16)
    return lo, hi


def _merge_kernel(yf_ref, yd_ref, g0_ref, g1_ref, x_ref, wbf_ref, wbd_ref, wo_ref, n2_ref,
                  rw_ref, rb_ref, h_ref, u_ref, idx_ref, gw_ref):
    ya = jnp.dot(yf_ref[...], wbf_ref[...], preferred_element_type=F32)
    yb = jnp.dot(yd_ref[...], wbd_ref[...], preferred_element_type=F32)
    merged = g0_ref[...] * ya + g1_ref[...] * yb
    h = x_ref[...] + jnp.dot(merged.astype(BF16), wo_ref[...], preferred_element_type=F32)
    h_ref[...] = h
    ms = jnp.mean(h * h, axis=-1, keepdims=True)
    u = (h * lax.rsqrt(ms + EPS)) * n2_ref[...]
    ub = u.astype(BF16)
    u_ref[...] = _pack_bf16_pairs(ub)
    logits = jnp.dot(ub, rw_ref[...], preferred_element_type=F32) + rb_ref[...]

    lane = lax.broadcasted_iota(jnp.int32, logits.shape, 1).astype(F32)
    work = logits
    vals, idxs = [], []
    for _ in range(TOP_K):
        m = jnp.max(work, axis=-1, keepdims=True)
        sel = jnp.min(jnp.where(work == m, lane, float(LANES)), axis=-1, keepdims=True)
        vals.append(m)
        idxs.append(sel)
        work = jnp.where(lane == sel, -jnp.inf, work)
    es = [jnp.exp(v - vals[0]) for v in vals]
    den = es[0]
    for e in es[1:]:
        den = den + e
    idx_out = jnp.zeros(logits.shape, F32)
    gw_out = jnp.zeros(logits.shape, F32)
    for k in range(TOP_K):
        idx_out = jnp.where(lane == float(k), idxs[k], idx_out)
        gw_out = jnp.where(lane == float(k), es[k] / den, gw_out)
    idx_ref[...] = idx_out.astype(jnp.int32)
    gw_ref[...] = gw_out


def _merge(yf, yd, gate, x2, wbf, wbd, wo, n2, rw, rb, tm):
    t, d = x2.shape
    fw = yf.shape[1]
    dw = yd.shape[1]
    const = dict(pipeline_mode=pl.Buffered(1))
    return pl.pallas_call(
        _merge_kernel,
        grid=(t // tm,),
        in_specs=[
            pl.BlockSpec((tm, fw), lambda i: (i, 0)),
            pl.BlockSpec((tm, dw), lambda i: (i, 0)),
            pl.BlockSpec((tm, d), lambda i: (i, 0)),
            pl.BlockSpec((tm, d), lambda i: (i, 1)),
            pl.BlockSpec((tm, d), lambda i: (i, 0)),
            pl.BlockSpec((fw, d), lambda i: (0, 0), **const),
            pl.BlockSpec((dw, d), lambda i: (0, 0), **const),
            pl.BlockSpec((d, d), lambda i: (0, 0), **const),
            pl.BlockSpec((1, d), lambda i: (0, 0)),
            pl.BlockSpec((d, LANES), lambda i: (0, 0), **const),
            pl.BlockSpec((1, LANES), lambda i: (0, 0)),
        ],
        out_specs=[
            pl.BlockSpec((tm, d), lambda i: (i, 0)),
            pl.BlockSpec((tm, d // 2), lambda i: (i, 0)),
            pl.BlockSpec((tm, LANES), lambda i: (i, 0)),
            pl.BlockSpec((tm, LANES), lambda i: (i, 0)),
        ],
        out_shape=[
            jax.ShapeDtypeStruct((t, d), F32),
            jax.ShapeDtypeStruct((t, d // 2), jnp.uint32),
            jax.ShapeDtypeStruct((t, LANES), jnp.int32),
            jax.ShapeDtypeStruct((t, LANES), F32),
        ],
        compiler_params=_cparams(("parallel",)),
    )(yf, yd, gate, gate, x2, wbf, wbd, wo, n2, rw, rb)


def _dispatch_kernel(cnt_ref, pstart_ref, pend_ref, roa_ref, u_ref, xs_hbm, stage, zeros, sems,
                     *, tb, sub, n_rows):
    s = pl.program_id(0)
    n = pl.num_programs(0)
    slot = s % 2
    n_dma = TOP_K * tb

    def wait_slot(sl):
        pltpu.make_async_copy(stage.at[sl], xs_hbm.at[pl.ds(0, n_dma), :], sems.at[sl]).wait()

    @pl.when(s == 0)
    def _():
        zeros[...] = jnp.zeros(zeros.shape, zeros.dtype)

    @pl.when(s >= 2)
    def _():
        wait_slot(slot)

    blk = u_ref[...]
    for k in range(TOP_K):
        stage[slot, pl.ds(k * tb, tb), :] = blk

    def body(g, carry):
        for q8 in range(ISSUE_UNROLL):
            q = g * ISSUE_UNROLL + q8
            pltpu.make_async_copy(stage.at[slot, pl.ds(q, 1), :],
                                  xs_hbm.at[pl.ds(roa_ref[0, 0, q], 1), :], sems.at[slot]).start()
        return carry
    lax.fori_loop(0, n_dma // ISSUE_UNROLL, body, 0)

    @pl.when(s == n - 1)
    def _():
        @pl.when(n >= 2)
        def _():
            wait_slot(1 - slot)
        wait_slot(slot)

        def zero_row(r):
            return pltpu.make_async_copy(zeros.at[pl.ds(0, 1), :], xs_hbm.at[pl.ds(r, 1), :], sems.at[2])

        def zero_block(r):
            return pltpu.make_async_copy(zeros, xs_hbm.at[pl.ds(pl.multiple_of(r, sub), sub), :], sems.at[2])

        def fill(first, stop, copy, step):
            cnt = (stop - first) // step

            def start(q, carry):
                copy(first + q * step).start()
                return carry

            def wait(q, carry):
                copy(first).wait()
                return carry
            lax.fori_loop(0, cnt, start, 0)
            lax.fori_loop(0, cnt, wait, 0)

        def per_expert(e, carry):
            first = pstart_ref[e] + cnt_ref[e]
            edge = (first + sub - 1) // sub * sub
            fill(first, edge, zero_row, 1)
            fill(edge, pend_ref[e], zero_block, sub)
            return carry
        lax.fori_loop(0, N_EXPERTS, per_expert, 0)
        fill(pend_ref[N_EXPERTS - 1], n_rows, zero_block, sub)


def _dispatch(counts, pad_start, pad_end, roa3, u_packed, n_rows, tb, sub):
    t, w = u_packed.shape
    kern = functools.partial(_dispatch_kernel, tb=tb, sub=sub, n_rows=n_rows)
    return pl.pallas_call(
        kern,
        grid_spec=pltpu.PrefetchScalarGridSpec(
            num_scalar_prefetch=3,
            grid=(t // tb,),
            in_specs=[
                pl.BlockSpec((1, 1, TOP_K * tb), lambda s, c, ps, pe: (s, 0, 0), memory_space=pltpu.SMEM),
                pl.BlockSpec((tb, w), lambda s, c, ps, pe: (s, 0)),
            ],
            out_specs=pl.BlockSpec(memory_space=pl.ANY),
            scratch_shapes=[
                pltpu.VMEM((2, TOP_K * tb, w), jnp.uint32),
                pltpu.VMEM((sub, w), jnp.uint32),
                pltpu.SemaphoreType.DMA((3,)),
            ],
        ),
        out_shape=jax.ShapeDtypeStruct((n_rows, w), jnp.uint32),
        compiler_params=_cparams(("arbitrary",)),
    )(counts, pad_start, pad_end, roa3, u_packed)


def _expert_kernel(te_ref, ts_ref, nch_ref, xs_ref, wg_ref, wu_ref, bg_ref, bu_ref, wd_ref,
                   bd_ref, out_ref, x_sc, act_sc, *, chunks, nf):
    i = pl.program_id(0)
    j = pl.program_id(1)
    nch = nch_ref[i]
    half = x_sc.shape[1] // 2

    @pl.when(jnp.logical_and(j == 0, nch > 0))
    def _():
        lo, hi = _unpack_bf16_pairs(xs_ref[...])
        x_sc[:, :half] = lo
        x_sc[:, half:] = hi

    full = nch == len(chunks)

    def gate_up(rs, wg, wu):
        x = x_sc[rs, :]
        hg = jnp.dot(x, wg, preferred_element_type=F32) + bg_ref[0]
        hu = jnp.dot(x, wu, preferred_element_type=F32) + bu_ref[0]
        hg = jnp.minimum(hg, SWIGLU_LIMIT)
        hl = jnp.clip(hu, -SWIGLU_LIMIT, SWIGLU_LIMIT)
        act = hg * jax.nn.sigmoid(SWIGLU_ALPHA * hg) * (hl + 1.0)
        act_sc[j, rs, :] = act.astype(BF16)

    def down(rs, wd):
        act = jnp.concatenate([act_sc[c, rs, :] for c in range(nf)], axis=1)
        out_ref[rs, :] = jnp.dot(act, wd, preferred_element_type=F32) + bd_ref[0]

    partial_tile = jnp.logical_and(nch > 0, nch < len(chunks))
    row_slices = [slice(a, b) for a, b in chunks]

    @pl.when(jnp.logical_and(j < nf, full))
    def _():
        gate_up(slice(None), wg_ref[0].astype(BF16), wu_ref[0].astype(BF16))

    @pl.when(jnp.logical_and(j < nf, partial_tile))
    def _():
        wg = wg_ref[0].astype(BF16)
        wu = wu_ref[0].astype(BF16)
        for c, rs in enumerate(row_slices[:-1]):
            pl.when(c < nch)(functools.partial(gate_up, rs, wg, wu))

    @pl.when(jnp.logical_and(j >= nf, full))
    def _():
        down(slice(None), wd_ref[0].astype(BF16))

    @pl.when(jnp.logical_and(j >= nf, partial_tile))
    def _():
        wd = wd_ref[0].astype(BF16)
        for c, rs in enumerate(row_slices[:-1]):
            pl.when(c < nch)(functools.partial(down, rs, wd))

    @pl.when(jnp.logical_and(j >= nf, nch < len(chunks)))
    def _():
        for c, (a, b) in enumerate(chunks):
            @pl.when(c >= nch)
            def _():
                out_ref[a:b, :] = jnp.zeros((b - a, out_ref.shape[1]), out_ref.dtype)


def _experts(tile_e, tile_src, tile_nch, xs, w_gu, b_gu3, w_down, b_down3, tm_e, chunks, tf):
    n_rows = xs.shape[0]
    n_tiles = n_rows // tm_e
    d = 2 * xs.shape[1]
    dff = w_down.shape[1]
    nf = dff // tf
    nn = d // tf
    assert nn == nf
    kern = functools.partial(_expert_kernel, chunks=chunks, nf=nf)

    def c1(i, j, ns):
        return jnp.where(ns[i] > 0, jnp.minimum(j, nf - 1), nf - 1)

    def c2(i, j, ns):
        return jnp.where(ns[i] > 0, jnp.maximum(j - nf, 0), nf - 1)

    return pl.pallas_call(
        kern,
        grid_spec=pltpu.PrefetchScalarGridSpec(
            num_scalar_prefetch=3,
            grid=(n_tiles, 2 * nf),
            in_specs=[
                pl.BlockSpec((tm_e, d // 2), lambda i, j, te, ts, ns: (ts[i], 0)),
                pl.BlockSpec((1, d, tf), lambda i, j, te, ts, ns: (te[i], 0, c1(i, j, ns))),
                pl.BlockSpec((1, d, tf), lambda i, j, te, ts, ns: (te[i], 0, nf + c1(i, j, ns))),
                pl.BlockSpec((1, 1, tf), lambda i, j, te, ts, ns: (te[i], 0, c1(i, j, ns))),
                pl.BlockSpec((1, 1, tf), lambda i, j, te, ts, ns: (te[i], 0, nf + c1(i, j, ns))),
                pl.BlockSpec((1, dff, tf), lambda i, j, te, ts, ns: (te[i], 0, c2(i, j, ns))),
                pl.BlockSpec((1, 1, tf), lambda i, j, te, ts, ns: (te[i], 0, c2(i, j, ns))),
            ],
            out_specs=pl.BlockSpec((tm_e, tf), lambda i, j, te, ts, ns: (i, jnp.maximum(j - nf, 0))),
            scratch_shapes=[
                pltpu.VMEM((tm_e, d), BF16),
                pltpu.VMEM((nf, tm_e, tf), BF16),
            ],
        ),
        out_shape=jax.ShapeDtypeStruct((n_rows, d), F32),
        compiler_params=_cparams(("arbitrary", "arbitrary")),
    )(tile_e, tile_src, tile_nch, xs, w_gu, w_gu, b_gu3, b_gu3, w_down, b_down3)


def _combine_kernel(cur_ref, nxt_ref, yr_hbm, h_ref, gw_ref, gf_ref, out_ref, ysel, sems, *, tmc,
                    final_norm):
    i = pl.program_id(0)
    n = pl.num_programs(0)
    slot = i % 2
    n_dma = TOP_K * tmc

    def row_copy(row, q, sl):
        return pltpu.make_async_copy(yr_hbm.at[pl.ds(row, 1), :], ysel.at[sl, pl.ds(q, 1), :], sems.at[sl])

    def issue(idx_ref, sl):
        def body(g, carry):
            for q8 in range(ISSUE_UNROLL):
                q = g * ISSUE_UNROLL + q8
                row_copy(idx_ref[0, 0, q], q, sl).start()
            return carry
        lax.fori_loop(0, n_dma // ISSUE_UNROLL, body, 0)

    @pl.when(i == 0)
    def _():
        issue(cur_ref, 0)

    @pl.when(i + 1 < n)
    def _():
        issue(nxt_ref, 1 - slot)

    pltpu.make_async_copy(yr_hbm.at[pl.ds(0, n_dma), :], ysel.at[slot], sems.at[slot]).wait()

    gw = gw_ref[...]
    y = h_ref[...]
    for k in range(TOP_K):
        y = y + gw[:, k:k + 1] * ysel[slot, k * tmc:(k + 1) * tmc, :]
    if final_norm:
        ms = jnp.mean(y * y, axis=-1, keepdims=True)
        y = (y * lax.rsqrt(ms + EPS)) * gf_ref[...]
    out_ref[...] = y


def _combine(roa3, y_rows, h, gw, gf, tmc, final_norm):
    t, d = h.shape
    n = t // tmc
    kern = functools.partial(_combine_kernel, tmc=tmc, final_norm=final_norm)
    return pl.pallas_call(
        kern,
        grid=(n,),
        in_specs=[
            pl.BlockSpec((1, 1, TOP_K * tmc), lambda i: (i, 0, 0), memory_space=pltpu.SMEM),
            pl.BlockSpec((1, 1, TOP_K * tmc), lambda i: (jnp.minimum(i + 1, n - 1), 0, 0),
                         memory_space=pltpu.SMEM),
            pl.BlockSpec(memory_space=pl.ANY),
            pl.BlockSpec((tmc, d), lambda i: (i, 0)),
            pl.BlockSpec((tmc, LANES), lambda i: (i, 0)),
            pl.BlockSpec((1, d), lambda i: (0, 0)),
        ],
        out_specs=pl.BlockSpec((tmc, d), lambda i: (i, 0)),
        out_shape=jax.ShapeDtypeStruct((t, d), F32),
        scratch_shapes=[pltpu.VMEM((2, TOP_K * tmc, d), F32), pltpu.SemaphoreType.DMA((2,))],
        compiler_params=_cparams(("arbitrary",)),
    )(roa3, roa3, y_rows, h, gw, gf)


def _routing(top_idx, tm_e, chunks):
    t = top_idx.shape[0]
    a = t * TOP_K
    flat_e = top_idx.reshape(a)
    onehot = (flat_e[:, None] == jnp.arange(N_EXPERTS, dtype=jnp.int32)[None, :]).astype(jnp.int32)
    csum = jnp.cumsum(onehot, axis=0)
    counts = csum[-1]
    rank = jnp.sum(onehot * csum, axis=1) - 1
    padded = (counts + tm_e - 1) // tm_e * tm_e
    pad_end = jnp.cumsum(padded)
    pad_start = pad_end - padded
    row_of_assign = (pad_start[flat_e] + rank).astype(jnp.int32)
    n_tiles = -(-a // tm_e) + N_EXPERTS
    tile_start = jnp.arange(n_tiles, dtype=jnp.int32) * tm_e
    n_used = pad_end[-1] // tm_e
    tile_e_raw = jnp.minimum(jnp.searchsorted(pad_end, tile_start, side='right'), N_EXPERTS - 1)
    used = tile_start < pad_end[-1]
    last = jnp.maximum(n_used - 1, 0)
    tile_src = jnp.where(used, jnp.arange(n_tiles, dtype=jnp.int32), last).astype(jnp.int32)
    tile_e = tile_e_raw[tile_src].astype(jnp.int32)
    valid_rows = jnp.clip(counts[tile_e_raw] - (tile_start - pad_start[tile_e_raw]), 0, tm_e)
    starts = jnp.asarray([a for a, _ in chunks], jnp.int32)
    tile_nch = jnp.sum(starts[None, :] < jnp.where(used, valid_rows, 0)[:, None], axis=1).astype(jnp.int32)
    bounds = (counts.astype(jnp.int32), pad_start.astype(jnp.int32), pad_end.astype(jnp.int32))
    return bounds, row_of_assign, tile_e, tile_src, tile_nch, n_tiles * tm_e


def kernel(x, positions, norm1_g, w_in, b_fgate, b_gate, lam_q1, lam_k1, lam_q2, lam_k2, subln_g,
           w_branch_fox, w_branch_diff, w_out, norm2_g, router_w, router_b, w_gu, b_gu, w_down,
           b_down, normf_g):
    b, s, d = x.shape
    t = b * s
    depth = norm1_g.shape[0]
    fox_w = FOX_HEADS * HEAD_DIM
    tm_a = _pick(t, 1024)
    tq = _pick(s, 512)
    tm_d = _pick(t, 256)
    tm_e, chunk, zsub, tf = 1152, 512, 128, 512
    chunks = tuple((a, min(a + chunk, tm_e)) for a in range(0, tm_e, chunk))
    tmc = _pick(t, 256)

    half = ROPE_DIM // 2
    inv_freq = ROPE_THETA ** (-jnp.arange(0, ROPE_DIM, 2, dtype=F32) / ROPE_DIM)
    zeros = jnp.zeros((LANES - ROPE_DIM,), F32)
    rope_tab = jnp.zeros((8, LANES), F32)
    rope_tab = rope_tab.at[0].set(jnp.concatenate([inv_freq, inv_freq, zeros]))
    rope_tab = rope_tab.at[1].set(jnp.concatenate([-jnp.ones((half,), F32), jnp.zeros((half,), F32), zeros]))
    rope_tab = rope_tab.at[2].set(jnp.concatenate([jnp.zeros((half,), F32), jnp.ones((half,), F32), zeros]))
    pos = positions.astype(F32).reshape(t, 1)

    h = x.reshape(t, d)
    for l in range(depth):
        wi = w_in[l]
        w_left = wi[:, :3 * fox_w].astype(BF16)
        w_right = wi[:, 3 * fox_w + FOX_HEADS:].astype(BF16)
        w_f = jnp.pad(wi[:, 3 * fox_w:3 * fox_w + FOX_HEADS], ((0, 0), (0, LANES - FOX_HEADS))).astype(BF16)
        qkv, gate, flog = _inproj(h, pos, norm1_g[l][None, :], w_left, w_right, w_f, rope_tab,
                                  b_gate[l][None, :], tm_a)

        fl = flog[:, :FOX_HEADS].reshape(b, s, FOX_HEADS).transpose(0, 2, 1).reshape(b * FOX_HEADS, s)
        bf = jnp.tile(b_fgate[l].astype(F32), b).reshape(b * FOX_HEADS, 1)
        c3 = _fgate(fl, bf).reshape(b * FOX_HEADS, 1, s)

        y_fox = _fox_attn(qkv, c3, b, s, tq)
        lam_init = 0.8 - 0.6 * math.exp(-0.3 * l)
        lam_p = jnp.stack([lam_q1[l], lam_k1[l], lam_q2[l], lam_k2[l]]).astype(F32)
        y_diff = _diff_attn(qkv, lam_p, subln_g[l][None, :].astype(F32), b, s, tq, lam_init)

        rw = jnp.pad(router_w[l], ((0, 0), (0, LANES - N_EXPERTS))).astype(BF16)
        rb = jnp.concatenate([router_b[l].astype(F32), jnp.full((LANES - N_EXPERTS,), NEG_BIG, F32)])[None, :]
        h, u2, idx128, gw128 = _merge(y_fox, y_diff, gate, h, w_branch_fox[l].astype(BF16),
                                      w_branch_diff[l].astype(BF16), w_out[l].astype(BF16),
                                      norm2_g[l][None, :], rw, rb, tm_d)

        bounds, roa, tile_e, tile_src, tile_nch, n_rows = _routing(idx128[:, :TOP_K], tm_e, chunks)
        roa3 = roa.reshape(t // tmc, tmc, TOP_K).transpose(0, 2, 1).reshape(t // tmc, 1, TOP_K * tmc)
        xs = _dispatch(*bounds, roa3, u2, n_rows, tmc, zsub)
        y_rows = _experts(tile_e, tile_src, tile_nch, xs, w_gu[l], b_gu[l][:, None, :],
                          w_down[l], b_down[l][:, None, :], tm_e, chunks, tf)
        h = _combine(roa3, y_rows, h, gw128, normf_g[None, :], tmc, final_norm=(l == depth - 1))
    return h.reshape(b, s, d)
```

```python
import functools
import math

import jax
import jax.numpy as jnp
from jax import lax
from jax.experimental import pallas as pl
from jax.experimental.pallas import tpu as pltpu

HEAD_DIM = 128
FOX_HEADS = 8
DIFF_HEADS = 4
DIFF_V_DIM = 2 * HEAD_DIM
ROPE_THETA = 500000.0
ROPE_DIM = HEAD_DIM // 4
N_EXPERTS = 32
TOP_K = 4
SWIGLU_ALPHA = 1.702
SWIGLU_LIMIT = 7.0
EPS = 1e-5
LANES = 128

F32 = jnp.float32
BF16 = jnp.bfloat16
NEG_BIG = -1e30
LOG2E = math.log2(math.e)
ISSUE_UNROLL = 8
DMA_QUEUES = 2
VMEM_LIMIT = 56 * 1024 * 1024


def _cparams(sem, vmem=VMEM_LIMIT):
    return pltpu.CompilerParams(dimension_semantics=sem, vmem_limit_bytes=vmem)


def _pick(n, pref):
    t = min(n, pref)
    while n % t:
        t //= 2
    return t


QKV_TILE = 1024
QKV_LEFT_TILES = 3
QKV_ROTARY_TILES = (3, 4)
QKV_SCALED_TILES = (0, 3)


def _qkv_kernel(x_ref, g1_ref, wa_ref, wb_ref, wf_ref, pos_ref, tab_ref,
                qkv_ref, u_ref, flog_ref, cos_sc, sa_sc, sb_sc, *, scale):
    j = pl.program_id(1)

    @pl.when(j == 0)
    def _():
        xf = x_ref[...]
        ms = jnp.mean(xf * xf, axis=-1, keepdims=True)
        u = (xf * lax.rsqrt(ms + EPS)) * g1_ref[...]
        ub = u.astype(BF16)
        u_ref[...] = ub
        flog_ref[...] = jnp.dot(ub, wf_ref[...], preferred_element_type=F32)
        ang = pos_ref[...] * tab_ref[0:1, :]
        sn = jnp.sin(ang)
        cos_sc[...] = jnp.cos(ang)
        sa_sc[...] = sn * tab_ref[1:2, :]
        sb_sc[...] = sn * tab_ref[2:3, :]

    any_of = lambda tiles: functools.reduce(jnp.logical_or, [j == q for q in tiles])
    mult = jnp.where(any_of(QKV_SCALED_TILES), scale, 1.0).astype(F32)
    rotary = any_of(QKV_ROTARY_TILES)
    left = j < QKV_LEFT_TILES

    def plain(w_ref):
        acc = jnp.dot(u_ref[...], w_ref[...], preferred_element_type=F32)
        qkv_ref[...] = (acc * mult).astype(BF16)

    pl.when(left)(functools.partial(plain, wa_ref))
    pl.when(jnp.logical_not(jnp.logical_or(left, rotary)))(functools.partial(plain, wb_ref))

    @pl.when(rotary)
    def _():
        acc = jnp.dot(u_ref[...], wb_ref[...], preferred_element_type=F32)
        ca = cos_sc[...] * mult
        cb = sa_sc[...] * mult
        cc = sb_sc[...] * mult
        for c in range(acc.shape[1] // HEAD_DIM):
            t = acc[:, c * HEAD_DIM:(c + 1) * HEAD_DIM]
            r = (t * ca + pltpu.roll(t, HEAD_DIM - ROPE_DIM // 2, 1) * cb
                 + pltpu.roll(t, ROPE_DIM // 2, 1) * cc)
            qkv_ref[:, c * HEAD_DIM:(c + 1) * HEAD_DIM] = r.astype(BF16)


def _gates_kernel(u_ref, w_ref, bg_ref, gate_ref):
    acc = jnp.dot(u_ref[...], w_ref[...], preferred_element_type=F32)
    gate_ref[...] = jax.nn.sigmoid(acc + bg_ref[...])


def _inproj(x2, pos, g1, w_left, w_right, w_f, rope_tab, b_gate, tm):
    t, d = x2.shape
    tn = QKV_TILE
    n_qkv = 6
    n_left = QKV_LEFT_TILES
    n_gate = w_right.shape[1] // tn - (n_qkv - n_left)
    kern = functools.partial(_qkv_kernel, scale=LOG2E / math.sqrt(HEAD_DIM))
    qkv, u, flog = pl.pallas_call(
        kern,
        grid=(t // tm, n_qkv),
        in_specs=[
            pl.BlockSpec((tm, d), lambda i, j: (i, 0)),
            pl.BlockSpec((1, d), lambda i, j: (0, 0)),
            pl.BlockSpec((d, tn), lambda i, j: (0, jnp.minimum(j, n_left - 1))),
            pl.BlockSpec((d, tn), lambda i, j: (0, jnp.maximum(j - n_left, 0))),
            pl.BlockSpec((d, LANES), lambda i, j: (0, 0)),
            pl.BlockSpec((tm, 1), lambda i, j: (i, 0)),
            pl.BlockSpec((8, LANES), lambda i, j: (0, 0)),
        ],
        out_specs=[
            pl.BlockSpec((tm, tn), lambda i, j: (i, j)),
            pl.BlockSpec((tm, d), lambda i, j: (i, 0)),
            pl.BlockSpec((tm, LANES), lambda i, j: (i, 0)),
        ],
        out_shape=[
            jax.ShapeDtypeStruct((t, n_qkv * tn), BF16),
            jax.ShapeDtypeStruct((t, d), BF16),
            jax.ShapeDtypeStruct((t, LANES), F32),
        ],
        scratch_shapes=[
            pltpu.VMEM((tm, LANES), F32),
            pltpu.VMEM((tm, LANES), F32),
            pltpu.VMEM((tm, LANES), F32),
        ],
        compiler_params=_cparams(("parallel", "arbitrary")),
    )(x2, g1, w_left, w_right, w_f, pos, rope_tab)
    gate = pl.pallas_call(
        _gates_kernel,
        grid=(t // tm, n_gate),
        in_specs=[
            pl.BlockSpec((tm, d), lambda i, j: (i, 0)),
            pl.BlockSpec((d, tn), lambda i, j: (0, n_qkv - n_left + j)),
            pl.BlockSpec((1, tn), lambda i, j: (0, j)),
        ],
        out_specs=pl.BlockSpec((tm, tn), lambda i, j: (i, j)),
        out_shape=jax.ShapeDtypeStruct((t, n_gate * tn), F32),
        compiler_params=_cparams(("parallel", "arbitrary")),
    )(u, w_right, b_gate)
    return qkv, gate, flog


def _fgate_kernel(fl_ref, b_ref, c_ref):
    z = fl_ref[...] + b_ref[...]
    x = jnp.minimum(z, 0.0) - jnp.log(1.0 + jnp.exp(-jnp.abs(z)))
    n = x.shape[1]
    lane = lax.broadcasted_iota(jnp.int32, x.shape, 1)
    sh = 1
    while sh < n:
        x = x + jnp.where(lane >= sh, pltpu.roll(x, sh, 1), 0.0)
        sh *= 2
    c_ref[...] = x


def _fgate(fl, b):
    r, s = fl.shape
    return pl.pallas_call(
        _fgate_kernel,
        grid=(1,),
        in_specs=[pl.BlockSpec((r, s), lambda i: (0, 0)), pl.BlockSpec((r, 1), lambda i: (0, 0))],
        out_specs=pl.BlockSpec((r, s), lambda i: (0, 0)),
        out_shape=jax.ShapeDtypeStruct((r, s), F32),
        compiler_params=_cparams(("arbitrary",)),
    )(fl, b)


def _transpose_bf16(x):
    return x.astype(F32).T.astype(BF16)


def _softmax_step(s, vt, m_sc, l_sc, acc_sc):
    m_old = m_sc[...]
    m_new = jnp.maximum(m_old, jnp.max(s, axis=0, keepdims=True))
    alpha = jnp.exp2(m_old - m_new)
    p = jnp.exp2(s - m_new)
    l_sc[...] = alpha * l_sc[...] + jnp.sum(p, axis=0, keepdims=True)
    acc_sc[...] = alpha * acc_sc[...] + jnp.dot(vt, p.astype(BF16), preferred_element_type=F32)
    m_sc[...] = m_new


def _causal_keep(shape):
    kv = lax.broadcasted_iota(jnp.int32, shape, 0)
    qq = lax.broadcasted_iota(jnp.int32, shape, 1)
    return kv <= qq


FOX_HEADS_PER_STEP = 2


def _fox_kernel(q_ref, k_ref, v_ref, c_ref, o_ref, *scratch, tq, tk, s_len):
    hp = FOX_HEADS_PER_STEP
    qt_sc, kaug_sc, vt_sc, m_sc, l_sc, acc_sc = (scratch[n * hp:(n + 1) * hp] for n in range(6))
    s_sc = scratch[6 * hp:]
    qi = pl.program_id(2)
    head_cols = [slice(h * HEAD_DIM, (h + 1) * HEAD_DIM) for h in range(hp)]

    @pl.when(qi == 0)
    def _():
        sub = lax.broadcasted_iota(jnp.int32, (HEAD_DIM, tk), 0)
        for h in range(hp):
            for ch in range(s_len // tk):
                rows = slice(ch * tk, (ch + 1) * tk)
                vt_sc[h][:, rows] = _transpose_bf16(v_ref[rows, head_cols[h]])
                c = c_ref[h, :, rows] * (-LOG2E)
                hi = c.astype(BF16).astype(F32)
                mid = (c - hi).astype(BF16).astype(F32)
                lo = (c - hi - mid).astype(BF16).astype(F32)
                tab = jnp.where(sub == 0, hi, jnp.where(sub == 1, mid, jnp.where(sub == 2, lo, 0.0)))
                kaug_sc[h][rows, HEAD_DIM:] = tab.T.astype(BF16)
                kaug_sc[h][rows, :HEAD_DIM] = k_ref[rows, head_cols[h]]

    sub_q = lax.broadcasted_iota(jnp.int32, (HEAD_DIM, tq), 0)
    for h in range(hp):
        qt_sc[h][:HEAD_DIM, :] = _transpose_bf16(q_ref[:, head_cols[h]])
        qt_sc[h][HEAD_DIM:, :] = jnp.where(sub_q < 3, 1.0, 0.0).astype(BF16)
        m_sc[h][...] = jnp.full(m_sc[h].shape, -jnp.inf, F32)
        l_sc[h][...] = jnp.zeros(l_sc[h].shape, F32)
        acc_sc[h][...] = jnp.zeros(acc_sc[h].shape, F32)

    def scores(h, chunk):
        start = pl.multiple_of(chunk * tk, tk)
        return jnp.dot(kaug_sc[h][pl.ds(start, tk), :], qt_sc[h][...], preferred_element_type=F32)

    def consume(h, chunk, par, masked):
        s = s_sc[par * hp + h][...]
        if masked:
            s = jnp.where(_causal_keep(s.shape), s, -jnp.inf)
        start = pl.multiple_of(chunk * tk, tk)
        _softmax_step(s, vt_sc[h][:, pl.ds(start, tk)], m_sc[h], l_sc[h], acc_sc[h])

    for h in range(hp):
        s_sc[h][...] = scores(h, 0)

    def body(kc, carry):
        def run(par):
            for h in range(hp):
                s_sc[(1 - par) * hp + h][...] = scores(h, kc + 1)
                consume(h, kc, par, False)
        pl.when(kc % 2 == 0)(functools.partial(run, 0))
        pl.when(kc % 2 == 1)(functools.partial(run, 1))
        return carry

    lax.fori_loop(0, qi, body, 0)

    def diagonal(par):
        for h in range(hp):
            consume(h, qi, par, True)
    pl.when(qi % 2 == 0)(functools.partial(diagonal, 0))
    pl.when(qi % 2 == 1)(functools.partial(diagonal, 1))
    for h in range(hp):
        o_ref[:, head_cols[h]] = (acc_sc[h][...] / l_sc[h][...]).T.astype(o_ref.dtype)


def _fox_attn(qkv, c3, b, s, tq):
    t = b * s
    nq = s // tq
    hp = FOX_HEADS_PER_STEP
    groups = FOX_HEADS // hp
    width = hp * HEAD_DIM
    kern = functools.partial(_fox_kernel, tq=tq, tk=tq, s_len=s)

    def per_head(shape, dtype):
        return [pltpu.VMEM(shape, dtype) for _ in range(hp)]

    return pl.pallas_call(
        kern,
        grid=(b, groups, nq),
        in_specs=[
            pl.BlockSpec((tq, width), lambda bi, g, qi: (bi * nq + qi, g)),
            pl.BlockSpec((s, width), lambda bi, g, qi: (bi, groups + g)),
            pl.BlockSpec((s, width), lambda bi, g, qi: (bi, 2 * groups + g)),
            pl.BlockSpec((hp, 1, s), lambda bi, g, qi: (bi * groups + g, 0, 0)),
        ],
        out_specs=pl.BlockSpec((tq, width), lambda bi, g, qi: (bi * nq + qi, g)),
        out_shape=jax.ShapeDtypeStruct((t, FOX_HEADS * HEAD_DIM), BF16),
        scratch_shapes=(per_head((2 * HEAD_DIM, tq), BF16) + per_head((s, 2 * HEAD_DIM), BF16)
                        + per_head((HEAD_DIM, s), BF16) + per_head((1, tq), F32)
                        + per_head((1, tq), F32) + per_head((HEAD_DIM, tq), F32)
                        + per_head((tq, tq), F32) + per_head((tq, tq), F32)),
        compiler_params=_cparams(("parallel", "parallel", "arbitrary")),
    )(qkv, qkv, qkv, c3)


def _diff_kernel(q1_ref, q2_ref, k1_ref, k2_ref, v_ref, lam_ref, g_ref, o_ref,
                 q1t_sc, q2t_sc, vt_sc, m1_sc, l1_sc, a1_sc, m2_sc, l2_sc, a2_sc, *s_sc,
                 tq, tk, s_len, lam_init):
    qi = pl.program_id(2)

    @pl.when(qi == 0)
    def _():
        for ch in range(s_len // tk):
            rows = slice(ch * tk, (ch + 1) * tk)
            vt_sc[:, rows] = _transpose_bf16(v_ref[rows, :])

    q1t_sc[...] = _transpose_bf16(q1_ref[...])
    q2t_sc[...] = _transpose_bf16(q2_ref[...])
    for r in (m1_sc, m2_sc):
        r[...] = jnp.full(r.shape, -jnp.inf, F32)
    for r in (l1_sc, l2_sc, a1_sc, a2_sc):
        r[...] = jnp.zeros(r.shape, F32)

    chains = ((k1_ref, q1t_sc, m1_sc, l1_sc, a1_sc), (k2_ref, q2t_sc, m2_sc, l2_sc, a2_sc))

    def scores(c, chunk):
        start = pl.multiple_of(chunk * tk, tk)
        return jnp.dot(chains[c][0][pl.ds(start, tk), :], chains[c][1][...], preferred_element_type=F32)

    def consume(c, chunk, par, masked):
        s = s_sc[par * 2 + c][...]
        if masked:
            s = jnp.where(_causal_keep(s.shape), s, -jnp.inf)
        start = pl.multiple_of(chunk * tk, tk)
        _softmax_step(s, vt_sc[:, pl.ds(start, tk)], *chains[c][2:])

    for c in range(2):
        s_sc[c][...] = scores(c, 0)

    def body(kc, carry):
        def run(par):
            for c in range(2):
                s_sc[(1 - par) * 2 + c][...] = scores(c, kc + 1)
                consume(c, kc, par, False)
        pl.when(kc % 2 == 0)(functools.partial(run, 0))
        pl.when(kc % 2 == 1)(functools.partial(run, 1))
        return carry

    lax.fori_loop(0, qi, body, 0)

    def diagonal(par):
        for c in range(2):
            consume(c, qi, par, True)
    pl.when(qi % 2 == 0)(functools.partial(diagonal, 0))
    pl.when(qi % 2 == 1)(functools.partial(diagonal, 1))

    lp = lam_ref[...]
    lam = (jnp.exp(jnp.sum(lp[0:1, :] * lp[1:2, :], axis=-1, keepdims=True))
           - jnp.exp(jnp.sum(lp[2:3, :] * lp[3:4, :], axis=-1, keepdims=True)) + lam_init)
    y = (a1_sc[...] / l1_sc[...]).T - lam * (a2_sc[...] / l2_sc[...]).T
    ms = jnp.mean(y * y, axis=-1, keepdims=True)
    y = (y * lax.rsqrt(ms + EPS)) * g_ref[...]
    o_ref[...] = (y * (1.0 - lam_init)).astype(o_ref.dtype)


def _diff_attn(qkv, lam_p, subln_g, b, s, tq, lam_init):
    t = b * s
    nq = s // tq
    qb = 3 * FOX_HEADS
    kb = qb + 2 * DIFF_HEADS
    vb = (kb + 2 * DIFF_HEADS) // 2
    kern = functools.partial(_diff_kernel, tq=tq, tk=tq, s_len=s, lam_init=lam_init)
    return pl.pallas_call(
        kern,
        grid=(b, DIFF_HEADS, nq),
        in_specs=[
            pl.BlockSpec((tq, HEAD_DIM), lambda bi, h, qi: (bi * nq + qi, qb + 2 * h)),
            pl.BlockSpec((tq, HEAD_DIM), lambda bi, h, qi: (bi * nq + qi, qb + 2 * h + 1)),
            pl.BlockSpec((s, HEAD_DIM), lambda bi, h, qi: (bi, kb + 2 * h)),
            pl.BlockSpec((s, HEAD_DIM), lambda bi, h, qi: (bi, kb + 2 * h + 1)),
            pl.BlockSpec((s, DIFF_V_DIM), lambda bi, h, qi: (bi, vb + h)),
            pl.BlockSpec((4, HEAD_DIM), lambda bi, h, qi: (0, 0)),
            pl.BlockSpec((1, DIFF_V_DIM), lambda bi, h, qi: (0, 0)),
        ],
        out_specs=pl.BlockSpec((tq, DIFF_V_DIM), lambda bi, h, qi: (bi * nq + qi, h)),
        out_shape=jax.ShapeDtypeStruct((t, DIFF_HEADS * DIFF_V_DIM), BF16),
        scratch_shapes=[
            pltpu.VMEM((HEAD_DIM, tq), BF16), pltpu.VMEM((HEAD_DIM, tq), BF16),
            pltpu.VMEM((DIFF_V_DIM, s), BF16),
            pltpu.VMEM((1, tq), F32), pltpu.VMEM((1, tq), F32), pltpu.VMEM((DIFF_V_DIM, tq), F32),
            pltpu.VMEM((1, tq), F32), pltpu.VMEM((1, tq), F32), pltpu.VMEM((DIFF_V_DIM, tq), F32),
        ] + [pltpu.VMEM((tq, tq), F32) for _ in range(4)],
        compiler_params=_cparams(("parallel", "parallel", "arbitrary")),
    )(qkv, qkv, qkv, qkv, qkv, lam_p, subln_g)


def _pack_bf16_pairs(xb):
    w = xb.shape[1] // 2
    lo = lax.bitcast_convert_type(xb[:, :w].astype(F32), jnp.uint32)
    hi = lax.bitcast_convert_type(xb[:, w:].astype(F32), jnp.uint32)
    return (lo >> 16) | (hi & jnp.uint32(0xFFFF0000))


def _unpack_bf16_pairs(p):
    lo = lax.bitcast_convert_type(p << 16, F32).astype(BF16)
    hi = lax.bitcast_convert_type(p & jnp.uint32(0xFFFF0000), F32).astype(BF16)
    return lo, hi


def _merge_kernel(yf_ref, yd_ref, g0_ref, g1_ref, x_ref, wbf_ref, wbd_ref, wo_ref, n2_ref,
                  rw_ref, rb_ref, h_ref, u_ref, idx_ref, gw_ref):
    ya = jnp.dot(yf_ref[...], wbf_ref[...], preferred_element_type=F32)
    yb = jnp.dot(yd_ref[...], wbd_ref[...], preferred_element_type=F32)
    merged = g0_ref[...] * ya + g1_ref[...] * yb
    h = x_ref[...] + jnp.dot(merged.astype(BF16), wo_ref[...], preferred_element_type=F32)
    h_ref[...] = h
    ms = jnp.mean(h * h, axis=-1, keepdims=True)
    u = (h * lax.rsqrt(ms + EPS)) * n2_ref[...]
    ub = u.astype(BF16)
    u_ref[...] = _pack_bf16_pairs(ub)
    logits = jnp.dot(ub, rw_ref[...], preferred_element_type=F32) + rb_ref[...]

    lane = lax.broadcasted_iota(jnp.int32, logits.shape, 1).astype(F32)
    work = logits
    vals, idxs = [], []
    for _ in range(TOP_K):
        m = jnp.max(work, axis=-1, keepdims=True)
        sel = jnp.min(jnp.where(work == m, lane, float(LANES)), axis=-1, keepdims=True)
        vals.append(m)
        idxs.append(sel)
        work = jnp.where(lane == sel, -jnp.inf, work)
    es = [jnp.exp(v - vals[0]) for v in vals]
    den = es[0]
    for e in es[1:]:
        den = den + e
    idx_out = jnp.zeros(logits.shape, F32)
    gw_out = jnp.zeros(logits.shape, F32)
    for k in range(TOP_K):
        idx_out = jnp.where(lane == float(k), idxs[k], idx_out)
        gw_out = jnp.where(lane == float(k), es[k] / den, gw_out)
    idx_ref[...] = idx_out.astype(jnp.int32)
    gw_ref[...] = gw_out


def _merge(yf, yd, gate, x2, wbf, wbd, wo, n2, rw, rb, tm):
    t, d = x2.shape
    fw = yf.shape[1]
    dw = yd.shape[1]
    const = dict(pipeline_mode=pl.Buffered(1))
    return pl.pallas_call(
        _merge_kernel,
        grid=(t // tm,),
        in_specs=[
            pl.BlockSpec((tm, fw), lambda i: (i, 0)),
            pl.BlockSpec((tm, dw), lambda i: (i, 0)),
            pl.BlockSpec((tm, d), lambda i: (i, 0)),
            pl.BlockSpec((tm, d), lambda i: (i, 1)),
            pl.BlockSpec((tm, d), lambda i: (i, 0)),
            pl.BlockSpec((fw, d), lambda i: (0, 0), **const),
            pl.BlockSpec((dw, d), lambda i: (0, 0), **const),
            pl.BlockSpec((d, d), lambda i: (0, 0), **const),
            pl.BlockSpec((1, d), lambda i: (0, 0)),
            pl.BlockSpec((d, LANES), lambda i: (0, 0), **const),
            pl.BlockSpec((1, LANES), lambda i: (0, 0)),
        ],
        out_specs=[
            pl.BlockSpec((tm, d), lambda i: (i, 0)),
            pl.BlockSpec((tm, d // 2), lambda i: (i, 0)),
            pl.BlockSpec((tm, LANES), lambda i: (i, 0)),
            pl.BlockSpec((tm, LANES), lambda i: (i, 0)),
        ],
        out_shape=[
            jax.ShapeDtypeStruct((t, d), F32),
            jax.ShapeDtypeStruct((t, d // 2), jnp.uint32),
            jax.ShapeDtypeStruct((t, LANES), jnp.int32),
            jax.ShapeDtypeStruct((t, LANES), F32),
        ],
        compiler_params=_cparams(("parallel",)),
    )(yf, yd, gate, gate, x2, wbf, wbd, wo, n2, rw, rb)


def _dispatch_kernel(cnt_ref, pstart_ref, pend_ref, roa_ref, u_ref, xs_hbm, stage, zeros, sems,
                     *, tb, sub, n_rows):
    s = pl.program_id(0)
    n = pl.num_programs(0)
    slot = s % 2
    n_dma = TOP_K * tb

    def wait_slot(sl):
        pltpu.make_async_copy(stage.at[sl], xs_hbm.at[pl.ds(0, n_dma), :], sems.at[sl]).wait()

    @pl.when(s == 0)
    def _():
        zeros[...] = jnp.zeros(zeros.shape, zeros.dtype)

    @pl.when(s >= 2)
    def _():
        wait_slot(slot)

    blk = u_ref[...]
    for k in range(TOP_K):
        stage[slot, pl.ds(k * tb, tb), :] = blk

    def body(g, carry):
        for q8 in range(ISSUE_UNROLL):
            q = g * ISSUE_UNROLL + q8
            pltpu.async_copy(stage.at[slot, pl.ds(q, 1), :], xs_hbm.at[pl.ds(roa_ref[0, 0, q], 1), :],
                             sems.at[slot], priority=q8 % DMA_QUEUES)
        return carry
    lax.fori_loop(0, n_dma // ISSUE_UNROLL, body, 0)

    @pl.when(s == n - 1)
    def _():
        @pl.when(n >= 2)
        def _():
            wait_slot(1 - slot)
        wait_slot(slot)

        def zero_row(r):
            return pltpu.make_async_copy(zeros.at[pl.ds(0, 1), :], xs_hbm.at[pl.ds(r, 1), :], sems.at[2])

        def zero_block(r):
            return pltpu.make_async_copy(zeros, xs_hbm.at[pl.ds(pl.multiple_of(r, sub), sub), :], sems.at[2])

        def fill(first, stop, copy, step):
            cnt = (stop - first) // step

            def start(q, carry):
                copy(first + q * step).start()
                return carry

            def wait(q, carry):
                copy(first).wait()
                return carry
            lax.fori_loop(0, cnt, start, 0)
            lax.fori_loop(0, cnt, wait, 0)

        def per_expert(e, carry):
            first = pstart_ref[e] + cnt_ref[e]
            edge = (first + sub - 1) // sub * sub
            fill(first, edge, zero_row, 1)
            fill(edge, pend_ref[e], zero_block, sub)
            return carry
        lax.fori_loop(0, N_EXPERTS, per_expert, 0)
        fill(pend_ref[N_EXPERTS - 1], n_rows, zero_block, sub)


def _dispatch(counts, pad_start, pad_end, roa3, u_packed, n_rows, tb, sub):
    t, w = u_packed.shape
    kern = functools.partial(_dispatch_kernel, tb=tb, sub=sub, n_rows=n_rows)
    return pl.pallas_call(
        kern,
        grid_spec=pltpu.PrefetchScalarGridSpec(
            num_scalar_prefetch=3,
            grid=(t // tb,),
            in_specs=[
                pl.BlockSpec((1, 1, TOP_K * tb), lambda s, c, ps, pe: (s, 0, 0), memory_space=pltpu.SMEM),
                pl.BlockSpec((tb, w), lambda s, c, ps, pe: (s, 0)),
            ],
            out_specs=pl.BlockSpec(memory_space=pl.ANY),
            scratch_shapes=[
                pltpu.VMEM((2, TOP_K * tb, w), jnp.uint32),
                pltpu.VMEM((sub, w), jnp.uint32),
                pltpu.SemaphoreType.DMA((3,)),
            ],
        ),
        out_shape=jax.ShapeDtypeStruct((n_rows, w), jnp.uint32),
        compiler_params=_cparams(("arbitrary",)),
    )(counts, pad_start, pad_end, roa3, u_packed)


def _expert_kernel(te_ref, ts_ref, nch_ref, xs_ref, wg_ref, wu_ref, bg_ref, bu_ref, wd_ref,
                   bd_ref, out_ref, x_sc, act_sc, *, chunks, nf):
    i = pl.program_id(0)
    j = pl.program_id(1)
    nch = nch_ref[i]
    half = x_sc.shape[1] // 2

    @pl.when(jnp.logical_and(j == 0, nch > 0))
    def _():
        lo, hi = _unpack_bf16_pairs(xs_ref[...])
        x_sc[:, :half] = lo
        x_sc[:, half:] = hi

    full = nch == len(chunks)

    def gate_up(rs, wg, wu):
        x = x_sc[rs, :]
        hg = jnp.dot(x, wg, preferred_element_type=F32) + bg_ref[0]
        hu = jnp.dot(x, wu, preferred_element_type=F32) + bu_ref[0]
        hg = jnp.minimum(hg, SWIGLU_LIMIT)
        hl = jnp.clip(hu, -SWIGLU_LIMIT, SWIGLU_LIMIT)
        act = hg * jax.nn.sigmoid(SWIGLU_ALPHA * hg) * (hl + 1.0)
        act_sc[j, rs, :] = act.astype(BF16)

    def down(rs, wd):
        act = jnp.concatenate([act_sc[c, rs, :] for c in range(nf)], axis=1)
        out_ref[rs, :] = jnp.dot(act, wd, preferred_element_type=F32) + bd_ref[0]

    partial_tile = jnp.logical_and(nch > 0, nch < len(chunks))
    row_slices = [slice(a, b) for a, b in chunks]

    @pl.when(jnp.logical_and(j < nf, full))
    def _():
        gate_up(slice(None), wg_ref[0].astype(BF16), wu_ref[0].astype(BF16))

    @pl.when(jnp.logical_and(j < nf, partial_tile))
    def _():
        wg = wg_ref[0].astype(BF16)
        wu = wu_ref[0].astype(BF16)
        for c, rs in enumerate(row_slices[:-1]):
            pl.when(c < nch)(functools.partial(gate_up, rs, wg, wu))

    @pl.when(jnp.logical_and(j >= nf, full))
    def _():
        down(slice(None), wd_ref[0].astype(BF16))

    @pl.when(jnp.logical_and(j >= nf, partial_tile))
    def _():
        wd = wd_ref[0].astype(BF16)
        for c, rs in enumerate(row_slices[:-1]):
            pl.when(c < nch)(functools.partial(down, rs, wd))

    @pl.when(jnp.logical_and(j >= nf, nch < len(chunks)))
    def _():
        for c, (a, b) in enumerate(chunks):
            @pl.when(c >= nch)
            def _():
                out_ref[a:b, :] = jnp.zeros((b - a, out_ref.shape[1]), out_ref.dtype)


def _experts(tile_e, tile_src, tile_nch, xs, w_gu, b_gu3, w_down, b_down3, tm_e, chunks, tf):
    n_rows = xs.shape[0]
    n_tiles = n_rows // tm_e
    d = 2 * xs.shape[1]
    dff = w_down.shape[1]
    nf = dff // tf
    nn = d // tf
    assert nn == nf
    kern = functools.partial(_expert_kernel, chunks=chunks, nf=nf)

    def c1(i, j, ns):
        return jnp.where(ns[i] > 0, jnp.minimum(j, nf - 1), nf - 1)

    def c2(i, j, ns):
        return jnp.where(ns[i] > 0, jnp.maximum(j - nf, 0), nf - 1)

    return pl.pallas_call(
        kern,
        grid_spec=pltpu.PrefetchScalarGridSpec(
            num_scalar_prefetch=3,
            grid=(n_tiles, 2 * nf),
            in_specs=[
                pl.BlockSpec((tm_e, d // 2), lambda i, j, te, ts, ns: (ts[i], 0)),
                pl.BlockSpec((1, d, tf), lambda i, j, te, ts, ns: (te[i], 0, c1(i, j, ns))),
                pl.BlockSpec((1, d, tf), lambda i, j, te, ts, ns: (te[i], 0, nf + c1(i, j, ns))),
                pl.BlockSpec((1, 1, tf), lambda i, j, te, ts, ns: (te[i], 0, c1(i, j, ns))),
                pl.BlockSpec((1, 1, tf), lambda i, j, te, ts, ns: (te[i], 0, nf + c1(i, j, ns))),
                pl.BlockSpec((1, dff, tf), lambda i, j, te, ts, ns: (te[i], 0, c2(i, j, ns))),
                pl.BlockSpec((1, 1, tf), lambda i, j, te, ts, ns: (te[i], 0, c2(i, j, ns))),
            ],
            out_specs=pl.BlockSpec((tm_e, tf), lambda i, j, te, ts, ns: (i, jnp.maximum(j - nf, 0))),
            scratch_shapes=[
                pltpu.VMEM((tm_e, d), BF16),
                pltpu.VMEM((nf, tm_e, tf), BF16),
            ],
        ),
        out_shape=jax.ShapeDtypeStruct((n_rows, d), F32),
        compiler_params=_cparams(("arbitrary", "arbitrary")),
    )(tile_e, tile_src, tile_nch, xs, w_gu, w_gu, b_gu3, b_gu3, w_down, b_down3)


def _combine_kernel(cur_ref, nxt_ref, yr_hbm, h_ref, gw_ref, gf_ref, out_ref, ysel, sems, *, tmc,
                    final_norm):
    i = pl.program_id(0)
    n = pl.num_programs(0)
    slot = i % 2
    n_dma = TOP_K * tmc

    def issue(idx_ref, sl):
        def body(g, carry):
            for q8 in range(ISSUE_UNROLL):
                q = g * ISSUE_UNROLL + q8
                pltpu.async_copy(yr_hbm.at[pl.ds(idx_ref[0, 0, q], 1), :], ysel.at[sl, pl.ds(q, 1), :],
                                 sems.at[sl], priority=q8 % DMA_QUEUES)
            return carry
        lax.fori_loop(0, n_dma // ISSUE_UNROLL, body, 0)

    @pl.when(i == 0)
    def _():
        issue(cur_ref, 0)

    @pl.when(i + 1 < n)
    def _():
        issue(nxt_ref, 1 - slot)

    pltpu.make_async_copy(yr_hbm.at[pl.ds(0, n_dma), :], ysel.at[slot], sems.at[slot]).wait()

    gw = gw_ref[...]
    y = h_ref[...]
    for k in range(TOP_K):
        y = y + gw[:, k:k + 1] * ysel[slot, k * tmc:(k + 1) * tmc, :]
    if final_norm:
        ms = jnp.mean(y * y, axis=-1, keepdims=True)
        y = (y * lax.rsqrt(ms + EPS)) * gf_ref[...]
    out_ref[...] = y


def _combine(roa3, y_rows, h, gw, gf, tmc, final_norm):
    t, d = h.shape
    n = t // tmc
    kern = functools.partial(_combine_kernel, tmc=tmc, final_norm=final_norm)
    return pl.pallas_call(
        kern,
        grid=(n,),
        in_specs=[
            pl.BlockSpec((1, 1, TOP_K * tmc), lambda i: (i, 0, 0), memory_space=pltpu.SMEM),
            pl.BlockSpec((1, 1, TOP_K * tmc), lambda i: (jnp.minimum(i + 1, n - 1), 0, 0),
                         memory_space=pltpu.SMEM),
            pl.BlockSpec(memory_space=pl.ANY),
            pl.BlockSpec((tmc, d), lambda i: (i, 0)),
            pl.BlockSpec((tmc, LANES), lambda i: (i, 0)),
            pl.BlockSpec((1, d), lambda i: (0, 0)),
        ],
        out_specs=pl.BlockSpec((tmc, d), lambda i: (i, 0)),
        out_shape=jax.ShapeDtypeStruct((t, d), F32),
        scratch_shapes=[pltpu.VMEM((2, TOP_K * tmc, d), F32), pltpu.SemaphoreType.DMA((2,))],
        compiler_params=_cparams(("arbitrary",)),
    )(roa3, roa3, y_rows, h, gw, gf)


def _routing(top_idx, tm_e, chunks):
    t = top_idx.shape[0]
    a = t * TOP_K
    flat_e = top_idx.reshape(a)
    onehot = (flat_e[:, None] == jnp.arange(N_EXPERTS, dtype=jnp.int32)[None, :]).astype(jnp.int32)
    csum = jnp.cumsum(onehot, axis=0)
    counts = csum[-1]
    rank = jnp.sum(onehot * csum, axis=1) - 1
    padded = (counts + tm_e - 1) // tm_e * tm_e
    pad_end = jnp.cumsum(padded)
    pad_start = pad_end - padded
    row_of_assign = (pad_start[flat_e] + rank).astype(jnp.int32)
    n_tiles = -(-a // tm_e) + N_EXPERTS
    tile_start = jnp.arange(n_tiles, dtype=jnp.int32) * tm_e
    n_used = pad_end[-1] // tm_e
    tile_e_raw = jnp.minimum(jnp.searchsorted(pad_end, tile_start, side='right'), N_EXPERTS - 1)
    used = tile_start < pad_end[-1]
    last = jnp.maximum(n_used - 1, 0)
    tile_src = jnp.where(used, jnp.arange(n_tiles, dtype=jnp.int32), last).astype(jnp.int32)
    tile_e = tile_e_raw[tile_src].astype(jnp.int32)
    valid_rows = jnp.clip(counts[tile_e_raw] - (tile_start - pad_start[tile_e_raw]), 0, tm_e)
    starts = jnp.asarray([a for a, _ in chunks], jnp.int32)
    tile_nch = jnp.sum(starts[None, :] < jnp.where(used, valid_rows, 0)[:, None], axis=1).astype(jnp.int32)
    bounds = (counts.astype(jnp.int32), pad_start.astype(jnp.int32), pad_end.astype(jnp.int32))
    return bounds, row_of_assign, tile_e, tile_src, tile_nch, n_tiles * tm_e


def kernel(x, positions, norm1_g, w_in, b_fgate, b_gate, lam_q1, lam_k1, lam_q2, lam_k2, subln_g,
           w_branch_fox, w_branch_diff, w_out, norm2_g, router_w, router_b, w_gu, b_gu, w_down,
           b_down, normf_g):
    b, s, d = x.shape
    t = b * s
    depth = norm1_g.shape[0]
    fox_w = FOX_HEADS * HEAD_DIM
    tm_a = _pick(t, 1024)
    tq = _pick(s, 512)
    tm_d = _pick(t, 256)
    tm_e, chunk, zsub, tf = 1152, 512, 128, 512
    chunks = tuple((a, min(a + chunk, tm_e)) for a in range(0, tm_e, chunk))
    tmc = _pick(t, 256)

    half = ROPE_DIM // 2
    inv_freq = ROPE_THETA ** (-jnp.arange(0, ROPE_DIM, 2, dtype=F32) / ROPE_DIM)
    zeros = jnp.zeros((LANES - ROPE_DIM,), F32)
    rope_tab = jnp.zeros((8, LANES), F32)
    rope_tab = rope_tab.at[0].set(jnp.concatenate([inv_freq, inv_freq, zeros]))
    rope_tab = rope_tab.at[1].set(jnp.concatenate([-jnp.ones((half,), F32), jnp.zeros((half,), F32), zeros]))
    rope_tab = rope_tab.at[2].set(jnp.concatenate([jnp.zeros((half,), F32), jnp.ones((half,), F32), zeros]))
    pos = positions.astype(F32).reshape(t, 1)

    h = x.reshape(t, d)
    for l in range(depth):
        wi = w_in[l]
        w_left = wi[:, :3 * fox_w].astype(BF16)
        w_right = wi[:, 3 * fox_w + FOX_HEADS:].astype(BF16)
        w_f = jnp.pad(wi[:, 3 * fox_w:3 * fox_w + FOX_HEADS], ((0, 0), (0, LANES - FOX_HEADS))).astype(BF16)
        qkv, gate, flog = _inproj(h, pos, norm1_g[l][None, :], w_left, w_right, w_f, rope_tab,
                                  b_gate[l][None, :], tm_a)

        fl = flog[:, :FOX_HEADS].reshape(b, s, FOX_HEADS).transpose(0, 2, 1).reshape(b * FOX_HEADS, s)
        bf = jnp.tile(b_fgate[l].astype(F32), b).reshape(b * FOX_HEADS, 1)
        c3 = _fgate(fl, bf).reshape(b * FOX_HEADS, 1, s)

        y_fox = _fox_attn(qkv, c3, b, s, tq)
        lam_init = 0.8 - 0.6 * math.exp(-0.3 * l)
        lam_p = jnp.stack([lam_q1[l], lam_k1[l], lam_q2[l], lam_k2[l]]).astype(F32)
        y_diff = _diff_attn(qkv, lam_p, subln_g[l][None, :].astype(F32), b, s, tq, lam_init)

        rw = jnp.pad(router_w[l], ((0, 0), (0, LANES - N_EXPERTS))).astype(BF16)
        rb = jnp.concatenate([router_b[l].astype(F32), jnp.full((LANES - N_EXPERTS,), NEG_BIG, F32)])[None, :]
        h, u2, idx128, gw128 = _merge(y_fox, y_diff, gate, h, w_branch_fox[l].astype(BF16),
                                      w_branch_diff[l].astype(BF16), w_out[l].astype(BF16),
                                      norm2_g[l][None, :], rw, rb, tm_d)

        bounds, roa, tile_e, tile_src, tile_nch, n_rows = _routing(idx128[:, :TOP_K], tm_e, chunks)
        roa3 = roa.reshape(t // tmc, tmc, TOP_K).transpose(0, 2, 1).reshape(t // tmc, 1, TOP_K * tmc)
        xs = _dispatch(*bounds, roa3, u2, n_rows, tmc, zsub)
        y_rows = _experts(tile_e, tile_src, tile_nch, xs, w_gu[l], b_gu[l][:, None, :],
                          w_down[l], b_down[l][:, None, :], tm_e, chunks, tf)
        h = _combine(roa3, y_rows, h, gw128, normf_g[None, :], tmc, final_norm=(l == depth - 1))
    return h.reshape(b, s, d)
```

```python
import functools
import math

import jax
import jax.numpy as jnp
from jax import lax
from jax.experimental import pallas as pl
from jax.experimental.pallas import tpu as pltpu

HEAD_DIM = 128
FOX_HEADS = 8
DIFF_HEADS = 4
DIFF_V_DIM = 2 * HEAD_DIM
ROPE_THETA = 500000.0
ROPE_DIM = HEAD_DIM // 4
N_EXPERTS = 32
TOP_K = 4
SWIGLU_ALPHA = 1.702
SWIGLU_LIMIT = 7.0
EPS = 1e-5
LANES = 128

F32 = jnp.float32
BF16 = jnp.bfloat16
NEG_BIG = -1e30
LOG2E = math.log2(math.e)
ISSUE_UNROLL = 8
VMEM_LIMIT = 56 * 1024 * 1024


def _cparams(sem, vmem=VMEM_LIMIT):
    return pltpu.CompilerParams(dimension_semantics=sem, vmem_limit_bytes=vmem)


def _pick(n, pref):
    t = min(n, pref)
    while n % t:
        t //= 2
    return t


QKV_TILE = 1024
QKV_LEFT_TILES = 3
QKV_ROTARY_TILES = (3, 4)
QKV_SCALED_TILES = (0, 3)


def _qkv_kernel(x_ref, g1_ref, wa_ref, wb_ref, wf_ref, pos_ref, tab_ref,
                qkv_ref, u_ref, flog_ref, cos_sc, sa_sc, sb_sc, *, scale):
    j = pl.program_id(1)

    @pl.when(j == 0)
    def _():
        xf = x_ref[...]
        ms = jnp.mean(xf * xf, axis=-1, keepdims=True)
        u = (xf * lax.rsqrt(ms + EPS)) * g1_ref[...]
        ub = u.astype(BF16)
        u_ref[...] = ub
        flog_ref[...] = jnp.dot(ub, wf_ref[...], preferred_element_type=F32)
        ang = pos_ref[...] * tab_ref[0:1, :]
        sn = jnp.sin(ang)
        cos_sc[...] = jnp.cos(ang)
        sa_sc[...] = sn * tab_ref[1:2, :]
        sb_sc[...] = sn * tab_ref[2:3, :]

    any_of = lambda tiles: functools.reduce(jnp.logical_or, [j == q for q in tiles])
    mult = jnp.where(any_of(QKV_SCALED_TILES), scale, 1.0).astype(F32)
    rotary = any_of(QKV_ROTARY_TILES)
    left = j < QKV_LEFT_TILES

    def plain(w_ref):
        acc = jnp.dot(u_ref[...], w_ref[...], preferred_element_type=F32)
        qkv_ref[...] = (acc * mult).astype(BF16)

    pl.when(left)(functools.partial(plain, wa_ref))
    pl.when(jnp.logical_not(jnp.logical_or(left, rotary)))(functools.partial(plain, wb_ref))

    @pl.when(rotary)
    def _():
        acc = jnp.dot(u_ref[...], wb_ref[...], preferred_element_type=F32)
        ca = cos_sc[...] * mult
        cb = sa_sc[...] * mult
        cc = sb_sc[...] * mult
        for c in range(acc.shape[1] // HEAD_DIM):
            t = acc[:, c * HEAD_DIM:(c + 1) * HEAD_DIM]
            r = (t * ca + pltpu.roll(t, HEAD_DIM - ROPE_DIM // 2, 1) * cb
                 + pltpu.roll(t, ROPE_DIM // 2, 1) * cc)
            qkv_ref[:, c * HEAD_DIM:(c + 1) * HEAD_DIM] = r.astype(BF16)


def _gates_kernel(u_ref, w_ref, bg_ref, gate_ref):
    acc = jnp.dot(u_ref[...], w_ref[...], preferred_element_type=F32)
    gate_ref[...] = jax.nn.sigmoid(acc + bg_ref[...])


def _inproj(x2, pos, g1, w_left, w_right, w_f, rope_tab, b_gate, tm):
    t, d = x2.shape
    tn = QKV_TILE
    n_qkv = 6
    n_left = QKV_LEFT_TILES
    n_gate = w_right.shape[1] // tn - (n_qkv - n_left)
    kern = functools.partial(_qkv_kernel, scale=LOG2E / math.sqrt(HEAD_DIM))
    qkv, u, flog = pl.pallas_call(
        kern,
        grid=(t // tm, n_qkv),
        in_specs=[
            pl.BlockSpec((tm, d), lambda i, j: (i, 0)),
            pl.BlockSpec((1, d), lambda i, j: (0, 0)),
            pl.BlockSpec((d, tn), lambda i, j: (0, jnp.minimum(j, n_left - 1))),
            pl.BlockSpec((d, tn), lambda i, j: (0, jnp.maximum(j - n_left, 0))),
            pl.BlockSpec((d, LANES), lambda i, j: (0, 0)),
            pl.BlockSpec((tm, 1), lambda i, j: (i, 0)),
            pl.BlockSpec((8, LANES), lambda i, j: (0, 0)),
        ],
        out_specs=[
            pl.BlockSpec((tm, tn), lambda i, j: (i, j)),
            pl.BlockSpec((tm, d), lambda i, j: (i, 0)),
            pl.BlockSpec((tm, LANES), lambda i, j: (i, 0)),
        ],
        out_shape=[
            jax.ShapeDtypeStruct((t, n_qkv * tn), BF16),
            jax.ShapeDtypeStruct((t, d), BF16),
            jax.ShapeDtypeStruct((t, LANES), F32),
        ],
        scratch_shapes=[
            pltpu.VMEM((tm, LANES), F32),
            pltpu.VMEM((tm, LANES), F32),
            pltpu.VMEM((tm, LANES), F32),
        ],
        compiler_params=_cparams(("parallel", "arbitrary")),
    )(x2, g1, w_left, w_right, w_f, pos, rope_tab)
    gate = pl.pallas_call(
        _gates_kernel,
        grid=(t // tm, n_gate),
        in_specs=[
            pl.BlockSpec((tm, d), lambda i, j: (i, 0)),
            pl.BlockSpec((d, tn), lambda i, j: (0, n_qkv - n_left + j)),
            pl.BlockSpec((1, tn), lambda i, j: (0, j)),
        ],
        out_specs=pl.BlockSpec((tm, tn), lambda i, j: (i, j)),
        out_shape=jax.ShapeDtypeStruct((t, n_gate * tn), F32),
        compiler_params=_cparams(("parallel", "arbitrary")),
    )(u, w_right, b_gate)
    return qkv, gate, flog


def _fgate_kernel(fl_ref, b_ref, c_ref):
    z = fl_ref[...] + b_ref[...]
    x = jnp.minimum(z, 0.0) - jnp.log(1.0 + jnp.exp(-jnp.abs(z)))
    n = x.shape[1]
    lane = lax.broadcasted_iota(jnp.int32, x.shape, 1)
    sh = 1
    while sh < n:
        x = x + jnp.where(lane >= sh, pltpu.roll(x, sh, 1), 0.0)
        sh *= 2
    c_ref[...] = x


def _fgate(fl, b):
    r, s = fl.shape
    return pl.pallas_call(
        _fgate_kernel,
        grid=(1,),
        in_specs=[pl.BlockSpec((r, s), lambda i: (0, 0)), pl.BlockSpec((r, 1), lambda i: (0, 0))],
        out_specs=pl.BlockSpec((r, s), lambda i: (0, 0)),
        out_shape=jax.ShapeDtypeStruct((r, s), F32),
        compiler_params=_cparams(("arbitrary",)),
    )(fl, b)


def _transpose_bf16(x):
    return x.astype(F32).T.astype(BF16)


def _softmax_step(s, vt, m_sc, l_sc, acc_sc):
    m_old = m_sc[...]
    m_new = jnp.maximum(m_old, jnp.max(s, axis=0, keepdims=True))
    alpha = jnp.exp2(m_old - m_new)
    p = jnp.exp2(s - m_new)
    l_sc[...] = alpha * l_sc[...] + jnp.sum(p, axis=0, keepdims=True)
    acc_sc[...] = alpha * acc_sc[...] + jnp.dot(vt, p.astype(BF16), preferred_element_type=F32)
    m_sc[...] = m_new


def _causal_keep(shape):
    kv = lax.broadcasted_iota(jnp.int32, shape, 0)
    qq = lax.broadcasted_iota(jnp.int32, shape, 1)
    return kv <= qq


FOX_HEADS_PER_STEP = 2


def _fox_kernel(q_ref, k_ref, v_ref, c_ref, o_ref, *scratch, tq, tk, s_len):
    hp = FOX_HEADS_PER_STEP
    qt_sc, kaug_sc, vt_sc, m_sc, l_sc, acc_sc = (scratch[n * hp:(n + 1) * hp] for n in range(6))
    s_sc = scratch[6 * hp:]
    qi = pl.program_id(2)
    head_cols = [slice(h * HEAD_DIM, (h + 1) * HEAD_DIM) for h in range(hp)]

    @pl.when(qi == 0)
    def _():
        sub = lax.broadcasted_iota(jnp.int32, (HEAD_DIM, tk), 0)
        for h in range(hp):
            for ch in range(s_len // tk):
                rows = slice(ch * tk, (ch + 1) * tk)
                vt_sc[h][:, rows] = _transpose_bf16(v_ref[rows, head_cols[h]])
                c = c_ref[h, :, rows] * (-LOG2E)
                hi = c.astype(BF16).astype(F32)
                mid = (c - hi).astype(BF16).astype(F32)
                lo = (c - hi - mid).astype(BF16).astype(F32)
                tab = jnp.where(sub == 0, hi, jnp.where(sub == 1, mid, jnp.where(sub == 2, lo, 0.0)))
                kaug_sc[h][rows, HEAD_DIM:] = tab.T.astype(BF16)
                kaug_sc[h][rows, :HEAD_DIM] = k_ref[rows, head_cols[h]]

    sub_q = lax.broadcasted_iota(jnp.int32, (HEAD_DIM, tq), 0)
    for h in range(hp):
        qt_sc[h][:HEAD_DIM, :] = _transpose_bf16(q_ref[:, head_cols[h]])
        qt_sc[h][HEAD_DIM:, :] = jnp.where(sub_q < 3, 1.0, 0.0).astype(BF16)
        m_sc[h][...] = jnp.full(m_sc[h].shape, -jnp.inf, F32)
        l_sc[h][...] = jnp.zeros(l_sc[h].shape, F32)
        acc_sc[h][...] = jnp.zeros(acc_sc[h].shape, F32)

    def scores(h, chunk):
        start = pl.multiple_of(chunk * tk, tk)
        return jnp.dot(kaug_sc[h][pl.ds(start, tk), :], qt_sc[h][...], preferred_element_type=F32)

    def consume(h, chunk, par, masked):
        s = s_sc[par * hp + h][...]
        if masked:
            s = jnp.where(_causal_keep(s.shape), s, -jnp.inf)
        start = pl.multiple_of(chunk * tk, tk)
        _softmax_step(s, vt_sc[h][:, pl.ds(start, tk)], m_sc[h], l_sc[h], acc_sc[h])

    for h in range(hp):
        s_sc[h][...] = scores(h, 0)

    def body(kc, carry):
        def run(par):
            for h in range(hp):
                s_sc[(1 - par) * hp + h][...] = scores(h, kc + 1)
                consume(h, kc, par, False)
        pl.when(kc % 2 == 0)(functools.partial(run, 0))
        pl.when(kc % 2 == 1)(functools.partial(run, 1))
        return carry

    lax.fori_loop(0, qi, body, 0)

    def diagonal(par):
        for h in range(hp):
            consume(h, qi, par, True)
    pl.when(qi % 2 == 0)(functools.partial(diagonal, 0))
    pl.when(qi % 2 == 1)(functools.partial(diagonal, 1))
    for h in range(hp):
        o_ref[:, head_cols[h]] = (acc_sc[h][...] / l_sc[h][...]).T.astype(o_ref.dtype)


def _fox_attn(qkv, c3, b, s, tq):
    t = b * s
    nq = s // tq
    hp = FOX_HEADS_PER_STEP
    groups = FOX_HEADS // hp
    width = hp * HEAD_DIM
    kern = functools.partial(_fox_kernel, tq=tq, tk=tq, s_len=s)

    def per_head(shape, dtype):
        return [pltpu.VMEM(shape, dtype) for _ in range(hp)]

    return pl.pallas_call(
        kern,
        grid=(b, groups, nq),
        in_specs=[
            pl.BlockSpec((tq, width), lambda bi, g, qi: (bi * nq + qi, g)),
            pl.BlockSpec((s, width), lambda bi, g, qi: (bi, groups + g)),
            pl.BlockSpec((s, width), lambda bi, g, qi: (bi, 2 * groups + g)),
            pl.BlockSpec((hp, 1, s), lambda bi, g, qi: (bi * groups + g, 0, 0)),
        ],
        out_specs=pl.BlockSpec((tq, width), lambda bi, g, qi: (bi * nq + qi, g)),
        out_shape=jax.ShapeDtypeStruct((t, FOX_HEADS * HEAD_DIM), BF16),
        scratch_shapes=(per_head((2 * HEAD_DIM, tq), BF16) + per_head((s, 2 * HEAD_DIM), BF16)
                        + per_head((HEAD_DIM, s), BF16) + per_head((1, tq), F32)
                        + per_head((1, tq), F32) + per_head((HEAD_DIM, tq), F32)
                        + per_head((tq, tq), F32) + per_head((tq, tq), F32)),
        compiler_params=_cparams(("parallel", "parallel", "arbitrary")),
    )(qkv, qkv, qkv, c3)


def _diff_kernel(q1_ref, q2_ref, k1_ref, k2_ref, v_ref, lam_ref, g_ref, o_ref,
                 q1t_sc, q2t_sc, vt_sc, m1_sc, l1_sc, a1_sc, m2_sc, l2_sc, a2_sc, *s_sc,
                 tq, tk, s_len, lam_init):
    qi = pl.program_id(2)

    @pl.when(qi == 0)
    def _():
        for ch in range(s_len // tk):
            rows = slice(ch * tk, (ch + 1) * tk)
            vt_sc[:, rows] = _transpose_bf16(v_ref[rows, :])

    q1t_sc[...] = _transpose_bf16(q1_ref[...])
    q2t_sc[...] = _transpose_bf16(q2_ref[...])
    for r in (m1_sc, m2_sc):
        r[...] = jnp.full(r.shape, -jnp.inf, F32)
    for r in (l1_sc, l2_sc, a1_sc, a2_sc):
        r[...] = jnp.zeros(r.shape, F32)

    chains = ((k1_ref, q1t_sc, m1_sc, l1_sc, a1_sc), (k2_ref, q2t_sc, m2_sc, l2_sc, a2_sc))

    def scores(c, chunk):
        start = pl.multiple_of(chunk * tk, tk)
        return jnp.dot(chains[c][0][pl.ds(start, tk), :], chains[c][1][...], preferred_element_type=F32)

    def consume(c, chunk, par, masked):
        s = s_sc[par * 2 + c][...]
        if masked:
            s = jnp.where(_causal_keep(s.shape), s, -jnp.inf)
        start = pl.multiple_of(chunk * tk, tk)
        _softmax_step(s, vt_sc[:, pl.ds(start, tk)], *chains[c][2:])

    for c in range(2):
        s_sc[c][...] = scores(c, 0)

    def body(kc, carry):
        def run(par):
            for c in range(2):
                s_sc[(1 - par) * 2 + c][...] = scores(c, kc + 1)
                consume(c, kc, par, False)
        pl.when(kc % 2 == 0)(functools.partial(run, 0))
        pl.when(kc % 2 == 1)(functools.partial(run, 1))
        return carry

    lax.fori_loop(0, qi, body, 0)

    def diagonal(par):
        for c in range(2):
            consume(c, qi, par, True)
    pl.when(qi % 2 == 0)(functools.partial(diagonal, 0))
    pl.when(qi % 2 == 1)(functools.partial(diagonal, 1))

    lp = lam_ref[...]
    lam = (jnp.exp(jnp.sum(lp[0:1, :] * lp[1:2, :], axis=-1, keepdims=True))
           - jnp.exp(jnp.sum(lp[2:3, :] * lp[3:4, :], axis=-1, keepdims=True)) + lam_init)
    y = (a1_sc[...] / l1_sc[...]).T - lam * (a2_sc[...] / l2_sc[...]).T
    ms = jnp.mean(y * y, axis=-1, keepdims=True)
    y = (y * lax.rsqrt(ms + EPS)) * g_ref[...]
    o_ref[...] = (y * (1.0 - lam_init)).astype(o_ref.dtype)


def _diff_attn(qkv, lam_p, subln_g, b, s, tq, lam_init):
    t = b * s
    nq = s // tq
    qb = 3 * FOX_HEADS
    kb = qb + 2 * DIFF_HEADS
    vb = (kb + 2 * DIFF_HEADS) // 2
    kern = functools.partial(_diff_kernel, tq=tq, tk=tq, s_len=s, lam_init=lam_init)
    return pl.pallas_call(
        kern,
        grid=(b, DIFF_HEADS, nq),
        in_specs=[
            pl.BlockSpec((tq, HEAD_DIM), lambda bi, h, qi: (bi * nq + qi, qb + 2 * h)),
            pl.BlockSpec((tq, HEAD_DIM), lambda bi, h, qi: (bi * nq + qi, qb + 2 * h + 1)),
            pl.BlockSpec((s, HEAD_DIM), lambda bi, h, qi: (bi, kb + 2 * h)),
            pl.BlockSpec((s, HEAD_DIM), lambda bi, h, qi: (bi, kb + 2 * h + 1)),
            pl.BlockSpec((s, DIFF_V_DIM), lambda bi, h, qi: (bi, vb + h)),
            pl.BlockSpec((4, HEAD_DIM), lambda bi, h, qi: (0, 0)),
            pl.BlockSpec((1, DIFF_V_DIM), lambda bi, h, qi: (0, 0)),
        ],
        out_specs=pl.BlockSpec((tq, DIFF_V_DIM), lambda bi, h, qi: (bi * nq + qi, h)),
        out_shape=jax.ShapeDtypeStruct((t, DIFF_HEADS * DIFF_V_DIM), BF16),
        scratch_shapes=[
            pltpu.VMEM((HEAD_DIM, tq), BF16), pltpu.VMEM((HEAD_DIM, tq), BF16),
            pltpu.VMEM((DIFF_V_DIM, s), BF16),
            pltpu.VMEM((1, tq), F32), pltpu.VMEM((1, tq), F32), pltpu.VMEM((DIFF_V_DIM, tq), F32),
            pltpu.VMEM((1, tq), F32), pltpu.VMEM((1, tq), F32), pltpu.VMEM((DIFF_V_DIM, tq), F32),
        ] + [pltpu.VMEM((tq, tq), F32) for _ in range(4)],
        compiler_params=_cparams(("parallel", "parallel", "arbitrary")),
    )(qkv, qkv, qkv, qkv, qkv, lam_p, subln_g)


def _pack_bf16_pairs(xb):
    w = xb.shape[1] // 2
    lo = lax.bitcast_convert_type(xb[:, :w].astype(F32), jnp.uint32)
    hi = lax.bitcast_convert_type(xb[:, w:].astype(F32), jnp.uint32)
    return (lo >> 16) | (hi & jnp.uint32(0xFFFF0000))


def _unpack_bf16_pairs(p):
    lo = lax.bitcast_convert_type(p << 16, F32).astype(BF16)
    hi = lax.bitcast_convert_type(p & jnp.uint32(0xFFFF0000), F32).astype(BF16)
    return lo, hi


def _merge_kernel(yf_ref, yd_ref, g0_ref, g1_ref, x_ref, wbf_ref, wbd_ref, wo_ref, n2_ref,
                  rw_ref, rb_ref, h_ref, u_ref, idx_ref, gw_ref):
    ya = jnp.dot(yf_ref[...], wbf_ref[...], preferred_element_type=F32)
    yb = jnp.dot(yd_ref[...], wbd_ref[...], preferred_element_type=F32)
    merged = g0_ref[...] * ya + g1_ref[...] * yb
    h = x_ref[...] + jnp.dot(merged.astype(BF16), wo_ref[...], preferred_element_type=F32)
    h_ref[...] = h
    ms = jnp.mean(h * h, axis=-1, keepdims=True)
    u = (h * lax.rsqrt(ms + EPS)) * n2_ref[...]
    ub = u.astype(BF16)
    u_ref[...] = _pack_bf16_pairs(ub)
    logits = jnp.dot(ub, rw_ref[...], preferred_element_type=F32) + rb_ref[...]

    lane = lax.broadcasted_iota(jnp.int32, logits.shape, 1).astype(F32)
    work = logits
    vals, idxs = [], []
    for _ in range(TOP_K):
        m = jnp.max(work, axis=-1, keepdims=True)
        sel = jnp.min(jnp.where(work == m, lane, float(LANES)), axis=-1, keepdims=True)
        vals.append(m)
        idxs.append(sel)
        work = jnp.where(lane == sel, -jnp.inf, work)
    es = [jnp.exp(v - vals[0]) for v in vals]
    den = es[0]
    for e in es[1:]:
        den = den + e
    idx_out = jnp.zeros(logits.shape, F32)
    gw_out = jnp.zeros(logits.shape, F32)
    for k in range(TOP_K):
        idx_out = jnp.where(lane == float(k), idxs[k], idx_out)
        gw_out = jnp.where(lane == float(k), es[k] / den, gw_out)
    idx_ref[...] = idx_out.astype(jnp.int32)
    gw_ref[...] = gw_out


def _merge(yf, yd, gate, x2, wbf, wbd, wo, n2, rw, rb, tm):
    t, d = x2.shape
    fw = yf.shape[1]
    dw = yd.shape[1]
    const = dict(pipeline_mode=pl.Buffered(1))
    return pl.pallas_call(
        _merge_kernel,
        grid=(t // tm,),
        in_specs=[
            pl.BlockSpec((tm, fw), lambda i: (i, 0)),
            pl.BlockSpec((tm, dw), lambda i: (i, 0)),
            pl.BlockSpec((tm, d), lambda i: (i, 0)),
            pl.BlockSpec((tm, d), lambda i: (i, 1)),
            pl.BlockSpec((tm, d), lambda i: (i, 0)),
            pl.BlockSpec((fw, d), lambda i: (0, 0), **const),
            pl.BlockSpec((dw, d), lambda i: (0, 0), **const),
            pl.BlockSpec((d, d), lambda i: (0, 0), **const),
            pl.BlockSpec((1, d), lambda i: (0, 0)),
            pl.BlockSpec((d, LANES), lambda i: (0, 0), **const),
            pl.BlockSpec((1, LANES), lambda i: (0, 0)),
        ],
        out_specs=[
            pl.BlockSpec((tm, d), lambda i: (i, 0)),
            pl.BlockSpec((tm, d // 2), lambda i: (i, 0)),
            pl.BlockSpec((tm, LANES), lambda i: (i, 0)),
            pl.BlockSpec((tm, LANES), lambda i: (i, 0)),
        ],
        out_shape=[
            jax.ShapeDtypeStruct((t, d), F32),
            jax.ShapeDtypeStruct((t, d // 2), jnp.uint32),
            jax.ShapeDtypeStruct((t, LANES), jnp.int32),
            jax.ShapeDtypeStruct((t, LANES), F32),
        ],
        compiler_params=_cparams(("parallel",)),
    )(yf, yd, gate, gate, x2, wbf, wbd, wo, n2, rw, rb)


def _dispatch_kernel(cnt_ref, pstart_ref, pend_ref, roa_ref, u_ref, xs_hbm, stage, zeros, sems,
                     *, tb, sub, n_rows):
    s = pl.program_id(0)
    n = pl.num_programs(0)
    slot = s % 2
    n_dma = TOP_K * tb

    def wait_slot(sl):
        pltpu.make_async_copy(stage.at[sl], xs_hbm.at[pl.ds(0, n_dma), :], sems.at[sl]).wait()

    @pl.when(s == 0)
    def _():
        zeros[...] = jnp.zeros(zeros.shape, zeros.dtype)

    @pl.when(s >= 2)
    def _():
        wait_slot(slot)

    blk = u_ref[...]
    for k in range(TOP_K):
        stage[slot, pl.ds(k * tb, tb), :] = blk

    def body(g, carry):
        for q8 in range(ISSUE_UNROLL):
            q = g * ISSUE_UNROLL + q8
            pltpu.make_async_copy(stage.at[slot, pl.ds(q, 1), :],
                                  xs_hbm.at[pl.ds(roa_ref[0, 0, q], 1), :], sems.at[slot]).start()
        return carry
    lax.fori_loop(0, n_dma // ISSUE_UNROLL, body, 0)

    @pl.when(s == n - 1)
    def _():
        @pl.when(n >= 2)
        def _():
            wait_slot(1 - slot)
        wait_slot(slot)

        def zero_row(r):
            return pltpu.make_async_copy(zeros.at[pl.ds(0, 1), :], xs_hbm.at[pl.ds(r, 1), :], sems.at[2])

        def zero_block(r):
            return pltpu.make_async_copy(zeros, xs_hbm.at[pl.ds(pl.multiple_of(r, sub), sub), :], sems.at[2])

        def fill(first, stop, copy, step):
            cnt = (stop - first) // step

            def start(q, carry):
                copy(first + q * step).start()
                return carry

            def wait(q, carry):
                copy(first).wait()
                return carry
            lax.fori_loop(0, cnt, start, 0)
            lax.fori_loop(0, cnt, wait, 0)

        def per_expert(e, carry):
            first = pstart_ref[e] + cnt_ref[e]
            edge = (first + sub - 1) // sub * sub
            fill(first, edge, zero_row, 1)
            fill(edge, pend_ref[e], zero_block, sub)
            return carry
        lax.fori_loop(0, N_EXPERTS, per_expert, 0)
        fill(pend_ref[N_EXPERTS - 1], n_rows, zero_block, sub)


def _dispatch(counts, pad_start, pad_end, roa3, u_packed, n_rows, tb, sub):
    t, w = u_packed.shape
    kern = functools.partial(_dispatch_kernel, tb=tb, sub=sub, n_rows=n_rows)
    return pl.pallas_call(
        kern,
        grid_spec=pltpu.PrefetchScalarGridSpec(
            num_scalar_prefetch=3,
            grid=(t // tb,),
            in_specs=[
                pl.BlockSpec((1, 1, TOP_K * tb), lambda s, c, ps, pe: (s, 0, 0), memory_space=pltpu.SMEM),
                pl.BlockSpec((tb, w), lambda s, c, ps, pe: (s, 0)),
            ],
            out_specs=pl.BlockSpec(memory_space=pl.ANY),
            scratch_shapes=[
                pltpu.VMEM((2, TOP_K * tb, w), jnp.uint32),
                pltpu.VMEM((sub, w), jnp.uint32),
                pltpu.SemaphoreType.DMA((3,)),
            ],
        ),
        out_shape=jax.ShapeDtypeStruct((n_rows, w), jnp.uint32),
        compiler_params=_cparams(("arbitrary",)),
    )(counts, pad_start, pad_end, roa3, u_packed)


def _expert_kernel(te_ref, ts_ref, nch_ref, xs_ref, wg_ref, wu_ref, bg_ref, bu_ref, wd_ref,
                   bd_ref, out_ref, x_sc, act_sc, *, chunks, nf):
    i = pl.program_id(0)
    j = pl.program_id(1)
    nch = nch_ref[i]
    half = x_sc.shape[1] // 2

    @pl.when(jnp.logical_and(j == 0, nch > 0))
    def _():
        lo, hi = _unpack_bf16_pairs(xs_ref[...])
        x_sc[:, :half] = lo
        x_sc[:, half:] = hi

    full = nch == len(chunks)

    def gate_up(rs, wg, wu):
        x = x_sc[rs, :]
        hg = jnp.dot(x, wg, preferred_element_type=F32) + bg_ref[0]
        hu = jnp.dot(x, wu, preferred_element_type=F32) + bu_ref[0]
        hg = jnp.minimum(hg, SWIGLU_LIMIT)
        hl = jnp.clip(hu, -SWIGLU_LIMIT, SWIGLU_LIMIT)
        act = hg * jax.nn.sigmoid(SWIGLU_ALPHA * hg) * (hl + 1.0)
        act_sc[j, rs, :] = act.astype(BF16)

    def down(rs, wd):
        act = jnp.concatenate([act_sc[c, rs, :] for c in range(nf)], axis=1)
        y = jnp.dot(act, wd, preferred_element_type=F32) + bd_ref[0]
        out_ref[rs, :] = _pack_bf16_pairs(y.astype(BF16))

    partial_tile = jnp.logical_and(nch > 0, nch < len(chunks))
    row_slices = [slice(a, b) for a, b in chunks]

    @pl.when(jnp.logical_and(j < nf, full))
    def _():
        gate_up(slice(None), wg_ref[0].astype(BF16), wu_ref[0].astype(BF16))

    @pl.when(jnp.logical_and(j < nf, partial_tile))
    def _():
        wg = wg_ref[0].astype(BF16)
        wu = wu_ref[0].astype(BF16)
        for c, rs in enumerate(row_slices[:-1]):
            pl.when(c < nch)(functools.partial(gate_up, rs, wg, wu))

    @pl.when(jnp.logical_and(j >= nf, full))
    def _():
        down(slice(None), wd_ref[0].astype(BF16))

    @pl.when(jnp.logical_and(j >= nf, partial_tile))
    def _():
        wd = wd_ref[0].astype(BF16)
        for c, rs in enumerate(row_slices[:-1]):
            pl.when(c < nch)(functools.partial(down, rs, wd))

    @pl.when(jnp.logical_and(j >= nf, nch < len(chunks)))
    def _():
        for c, (a, b) in enumerate(chunks):
            @pl.when(c >= nch)
            def _():
                out_ref[a:b, :] = jnp.zeros((b - a, out_ref.shape[1]), out_ref.dtype)


def _experts(tile_e, tile_src, tile_nch, xs, w_gu, b_gu3, w_down, b_down3, tm_e, chunks, tf):
    n_rows = xs.shape[0]
    n_tiles = n_rows // tm_e
    d = 2 * xs.shape[1]
    dff = w_down.shape[1]
    nf = dff // tf
    nn = d // tf
    assert nn == nf
    kern = functools.partial(_expert_kernel, chunks=chunks, nf=nf)

    def c1(i, j, ns):
        return jnp.where(ns[i] > 0, jnp.minimum(j, nf - 1), nf - 1)

    def c2(i, j, ns):
        return jnp.where(ns[i] > 0, jnp.maximum(j - nf, 0), nf - 1)

    return pl.pallas_call(
        kern,
        grid_spec=pltpu.PrefetchScalarGridSpec(
            num_scalar_prefetch=3,
            grid=(n_tiles, 2 * nf),
            in_specs=[
                pl.BlockSpec((tm_e, d // 2), lambda i, j, te, ts, ns: (ts[i], 0)),
                pl.BlockSpec((1, d, tf), lambda i, j, te, ts, ns: (te[i], 0, c1(i, j, ns))),
                pl.BlockSpec((1, d, tf), lambda i, j, te, ts, ns: (te[i], 0, nf + c1(i, j, ns))),
                pl.BlockSpec((1, 1, tf), lambda i, j, te, ts, ns: (te[i], 0, c1(i, j, ns))),
                pl.BlockSpec((1, 1, tf), lambda i, j, te, ts, ns: (te[i], 0, nf + c1(i, j, ns))),
                pl.BlockSpec((1, dff, tf), lambda i, j, te, ts, ns: (te[i], 0, c2(i, j, ns))),
                pl.BlockSpec((1, 1, tf), lambda i, j, te, ts, ns: (te[i], 0, c2(i, j, ns))),
            ],
            out_specs=pl.BlockSpec((tm_e, tf // 2), lambda i, j, te, ts, ns: (i, jnp.maximum(j - nf, 0))),
            scratch_shapes=[
                pltpu.VMEM((tm_e, d), BF16),
                pltpu.VMEM((nf, tm_e, tf), BF16),
            ],
        ),
        out_shape=jax.ShapeDtypeStruct((n_rows, d // 2), jnp.uint32),
        compiler_params=_cparams(("arbitrary", "arbitrary")),
    )(tile_e, tile_src, tile_nch, xs, w_gu, w_gu, b_gu3, b_gu3, w_down, b_down3)


def _combine_kernel(cur_ref, nxt_ref, yr_hbm, h_ref, gw_ref, gf_ref, out_ref, ysel, sems, *, tmc,
                    chunk, final_norm):
    i = pl.program_id(0)
    n = pl.num_programs(0)
    slot = i % 2
    n_dma = TOP_K * tmc

    def row_copy(row, q, sl):
        return pltpu.make_async_copy(yr_hbm.at[pl.ds(row, 1), :], ysel.at[sl, pl.ds(q, 1), :], sems.at[sl])

    def issue(idx_ref, sl):
        def body(g, carry):
            for q8 in range(ISSUE_UNROLL):
                q = g * ISSUE_UNROLL + q8
                row_copy(idx_ref[0, 0, q], q, sl).start()
            return carry
        lax.fori_loop(0, n_dma // ISSUE_UNROLL, body, 0)

    @pl.when(i == 0)
    def _():
        issue(cur_ref, 0)

    @pl.when(i + 1 < n)
    def _():
        issue(nxt_ref, 1 - slot)

    pltpu.make_async_copy(yr_hbm.at[pl.ds(0, n_dma), :], ysel.at[slot], sems.at[slot]).wait()

    gw = gw_ref[...]
    y = h_ref[...]
    half = chunk // 2
    for k in range(TOP_K):
        lo, hi = _unpack_bf16_pairs(ysel[slot, k * tmc:(k + 1) * tmc, :])
        pieces = []
        for c in range(lo.shape[1] // half):
            pieces += [lo[:, c * half:(c + 1) * half], hi[:, c * half:(c + 1) * half]]
        y = y + gw[:, k:k + 1] * jnp.concatenate(pieces, axis=1).astype(F32)
    if final_norm:
        ms = jnp.mean(y * y, axis=-1, keepdims=True)
        y = (y * lax.rsqrt(ms + EPS)) * gf_ref[...]
    out_ref[...] = y


def _combine(roa3, y_rows, h, gw, gf, tmc, chunk, final_norm):
    t, d = h.shape
    n = t // tmc
    kern = functools.partial(_combine_kernel, tmc=tmc, chunk=chunk, final_norm=final_norm)
    return pl.pallas_call(
        kern,
        grid=(n,),
        in_specs=[
            pl.BlockSpec((1, 1, TOP_K * tmc), lambda i: (i, 0, 0), memory_space=pltpu.SMEM),
            pl.BlockSpec((1, 1, TOP_K * tmc), lambda i: (jnp.minimum(i + 1, n - 1), 0, 0),
                         memory_space=pltpu.SMEM),
            pl.BlockSpec(memory_space=pl.ANY),
            pl.BlockSpec((tmc, d), lambda i: (i, 0)),
            pl.BlockSpec((tmc, LANES), lambda i: (i, 0)),
            pl.BlockSpec((1, d), lambda i: (0, 0)),
        ],
        out_specs=pl.BlockSpec((tmc, d), lambda i: (i, 0)),
        out_shape=jax.ShapeDtypeStruct((t, d), F32),
        scratch_shapes=[pltpu.VMEM((2, TOP_K * tmc, d // 2), jnp.uint32), pltpu.SemaphoreType.DMA((2,))],
        compiler_params=_cparams(("arbitrary",)),
    )(roa3, roa3, y_rows, h, gw, gf)


def _routing(top_idx, tm_e, chunks):
    t = top_idx.shape[0]
    a = t * TOP_K
    flat_e = top_idx.reshape(a)
    onehot = (flat_e[:, None] == jnp.arange(N_EXPERTS, dtype=jnp.int32)[None, :]).astype(jnp.int32)
    csum = jnp.cumsum(onehot, axis=0)
    counts = csum[-1]
    rank = jnp.sum(onehot * csum, axis=1) - 1
    padded = (counts + tm_e - 1) // tm_e * tm_e
    pad_end = jnp.cumsum(padded)
    pad_start = pad_end - padded
    row_of_assign = (pad_start[flat_e] + rank).astype(jnp.int32)
    n_tiles = -(-a // tm_e) + N_EXPERTS
    tile_start = jnp.arange(n_tiles, dtype=jnp.int32) * tm_e
    n_used = pad_end[-1] // tm_e
    tile_e_raw = jnp.minimum(jnp.searchsorted(pad_end, tile_start, side='right'), N_EXPERTS - 1)
    used = tile_start < pad_end[-1]
    last = jnp.maximum(n_used - 1, 0)
    tile_src = jnp.where(used, jnp.arange(n_tiles, dtype=jnp.int32), last).astype(jnp.int32)
    tile_e = tile_e_raw[tile_src].astype(jnp.int32)
    valid_rows = jnp.clip(counts[tile_e_raw] - (tile_start - pad_start[tile_e_raw]), 0, tm_e)
    starts = jnp.asarray([a for a, _ in chunks], jnp.int32)
    tile_nch = jnp.sum(starts[None, :] < jnp.where(used, valid_rows, 0)[:, None], axis=1).astype(jnp.int32)
    bounds = (counts.astype(jnp.int32), pad_start.astype(jnp.int32), pad_end.astype(jnp.int32))
    return bounds, row_of_assign, tile_e, tile_src, tile_nch, n_tiles * tm_e


def kernel(x, positions, norm1_g, w_in, b_fgate, b_gate, lam_q1, lam_k1, lam_q2, lam_k2, subln_g,
           w_branch_fox, w_branch_diff, w_out, norm2_g, router_w, router_b, w_gu, b_gu, w_down,
           b_down, normf_g):
    b, s, d = x.shape
    t = b * s
    depth = norm1_g.shape[0]
    fox_w = FOX_HEADS * HEAD_DIM
    tm_a = _pick(t, 1024)
    tq = _pick(s, 512)
    tm_d = _pick(t, 256)
    tm_e, chunk, zsub, tf = 1152, 512, 128, 512
    chunks = tuple((a, min(a + chunk, tm_e)) for a in range(0, tm_e, chunk))
    tmc = _pick(t, 256)

    half = ROPE_DIM // 2
    inv_freq = ROPE_THETA ** (-jnp.arange(0, ROPE_DIM, 2, dtype=F32) / ROPE_DIM)
    zeros = jnp.zeros((LANES - ROPE_DIM,), F32)
    rope_tab = jnp.zeros((8, LANES), F32)
    rope_tab = rope_tab.at[0].set(jnp.concatenate([inv_freq, inv_freq, zeros]))
    rope_tab = rope_tab.at[1].set(jnp.concatenate([-jnp.ones((half,), F32), jnp.zeros((half,), F32), zeros]))
    rope_tab = rope_tab.at[2].set(jnp.concatenate([jnp.zeros((half,), F32), jnp.ones((half,), F32), zeros]))
    pos = positions.astype(F32).reshape(t, 1)

    h = x.reshape(t, d)
    for l in range(depth):
        wi = w_in[l]
        w_left = wi[:, :3 * fox_w].astype(BF16)
        w_right = wi[:, 3 * fox_w + FOX_HEADS:].astype(BF16)
        w_f = jnp.pad(wi[:, 3 * fox_w:3 * fox_w + FOX_HEADS], ((0, 0), (0, LANES - FOX_HEADS))).astype(BF16)
        qkv, gate, flog = _inproj(h, pos, norm1_g[l][None, :], w_left, w_right, w_f, rope_tab,
                                  b_gate[l][None, :], tm_a)

        fl = flog[:, :FOX_HEADS].reshape(b, s, FOX_HEADS).transpose(0, 2, 1).reshape(b * FOX_HEADS, s)
        bf = jnp.tile(b_fgate[l].astype(F32), b).reshape(b * FOX_HEADS, 1)
        c3 = _fgate(fl, bf).reshape(b * FOX_HEADS, 1, s)

        y_fox = _fox_attn(qkv, c3, b, s, tq)
        lam_init = 0.8 - 0.6 * math.exp(-0.3 * l)
        lam_p = jnp.stack([lam_q1[l], lam_k1[l], lam_q2[l], lam_k2[l]]).astype(F32)
        y_diff = _diff_attn(qkv, lam_p, subln_g[l][None, :].astype(F32), b, s, tq, lam_init)

        rw = jnp.pad(router_w[l], ((0, 0), (0, LANES - N_EXPERTS))).astype(BF16)
        rb = jnp.concatenate([router_b[l].astype(F32), jnp.full((LANES - N_EXPERTS,), NEG_BIG, F32)])[None, :]
        h, u2, idx128, gw128 = _merge(y_fox, y_diff, gate, h, w_branch_fox[l].astype(BF16),
                                      w_branch_diff[l].astype(BF16), w_out[l].astype(BF16),
                                      norm2_g[l][None, :], rw, rb, tm_d)

        bounds, roa, tile_e, tile_src, tile_nch, n_rows = _routing(idx128[:, :TOP_K], tm_e, chunks)
        roa3 = roa.reshape(t // tmc, tmc, TOP_K).transpose(0, 2, 1).reshape(t // tmc, 1, TOP_K * tmc)
        xs = _dispatch(*bounds, roa3, u2, n_rows, tmc, zsub)
        y_rows = _experts(tile_e, tile_src, tile_nch, xs, w_gu[l], b_gu[l][:, None, :],
                          w_down[l], b_down[l][:, None, :], tm_e, chunks, tf)
        h = _combine(roa3, y_rows, h, gw128, normf_g[None, :], tmc, tf, final_norm=(l == depth - 1))
    return h.reshape(b, s, d)
```

```python
import functools
import math

import jax
import jax.numpy as jnp
from jax import lax
from jax.experimental import pallas as pl
from jax.experimental.pallas import tpu as pltpu

HEAD_DIM = 128
FOX_HEADS = 8
DIFF_HEADS = 4
DIFF_V_DIM = 2 * HEAD_DIM
ROPE_THETA = 500000.0
ROPE_DIM = HEAD_DIM // 4
N_EXPERTS = 32
TOP_K = 4
SWIGLU_ALPHA = 1.702
SWIGLU_LIMIT = 7.0
EPS = 1e-5
LANES = 128

F32 = jnp.float32
BF16 = jnp.bfloat16
NEG_BIG = -1e30
LOG2E = math.log2(math.e)
ISSUE_UNROLL = 8
VMEM_LIMIT = 56 * 1024 * 1024
EXPERT_VMEM_LIMIT = 60 * 1024 * 1024


def _cparams(sem, vmem=VMEM_LIMIT):
    return pltpu.CompilerParams(dimension_semantics=sem, vmem_limit_bytes=vmem)


def _pick(n, pref):
    t = min(n, pref)
    while n % t:
        t //= 2
    return t


QKV_TILE = 1024
QKV_LEFT_TILES = 3
QKV_ROTARY_TILES = (3, 4)
QKV_SCALED_TILES = (0, 3)


def _qkv_kernel(x_ref, g1_ref, wa_ref, wb_ref, wf_ref, pos_ref, tab_ref,
                qkv_ref, u_ref, flog_ref, cos_sc, sa_sc, sb_sc, *, scale):
    j = pl.program_id(1)

    @pl.when(j == 0)
    def _():
        xf = x_ref[...]
        ms = jnp.mean(xf * xf, axis=-1, keepdims=True)
        u = (xf * lax.rsqrt(ms + EPS)) * g1_ref[...]
        ub = u.astype(BF16)
        u_ref[...] = ub
        flog_ref[...] = jnp.dot(ub, wf_ref[...], preferred_element_type=F32)
        ang = pos_ref[...] * tab_ref[0:1, :]
        sn = jnp.sin(ang)
        cos_sc[...] = jnp.cos(ang)
        sa_sc[...] = sn * tab_ref[1:2, :]
        sb_sc[...] = sn * tab_ref[2:3, :]

    any_of = lambda tiles: functools.reduce(jnp.logical_or, [j == q for q in tiles])
    mult = jnp.where(any_of(QKV_SCALED_TILES), scale, 1.0).astype(F32)
    rotary = any_of(QKV_ROTARY_TILES)
    left = j < QKV_LEFT_TILES

    def plain(w_ref):
        acc = jnp.dot(u_ref[...], w_ref[...], preferred_element_type=F32)
        qkv_ref[...] = (acc * mult).astype(BF16)

    pl.when(left)(functools.partial(plain, wa_ref))
    pl.when(jnp.logical_not(jnp.logical_or(left, rotary)))(functools.partial(plain, wb_ref))

    @pl.when(rotary)
    def _():
        acc = jnp.dot(u_ref[...], wb_ref[...], preferred_element_type=F32)
        ca = cos_sc[...] * mult
        cb = sa_sc[...] * mult
        cc = sb_sc[...] * mult
        for c in range(acc.shape[1] // HEAD_DIM):
            t = acc[:, c * HEAD_DIM:(c + 1) * HEAD_DIM]
            r = (t * ca + pltpu.roll(t, HEAD_DIM - ROPE_DIM // 2, 1) * cb
                 + pltpu.roll(t, ROPE_DIM // 2, 1) * cc)
            qkv_ref[:, c * HEAD_DIM:(c + 1) * HEAD_DIM] = r.astype(BF16)


def _gates_kernel(u_ref, w_ref, bg_ref, gate_ref):
    acc = jnp.dot(u_ref[...], w_ref[...], preferred_element_type=F32)
    gate_ref[...] = jax.nn.sigmoid(acc + bg_ref[...])


def _inproj(x2, pos, g1, w_left, w_right, w_f, rope_tab, b_gate, tm):
    t, d = x2.shape
    tn = QKV_TILE
    n_qkv = 6
    n_left = QKV_LEFT_TILES
    n_gate = w_right.shape[1] // tn - (n_qkv - n_left)
    kern = functools.partial(_qkv_kernel, scale=LOG2E / math.sqrt(HEAD_DIM))
    qkv, u, flog = pl.pallas_call(
        kern,
        grid=(t // tm, n_qkv),
        in_specs=[
            pl.BlockSpec((tm, d), lambda i, j: (i, 0)),
            pl.BlockSpec((1, d), lambda i, j: (0, 0)),
            pl.BlockSpec((d, tn), lambda i, j: (0, jnp.minimum(j, n_left - 1))),
            pl.BlockSpec((d, tn), lambda i, j: (0, jnp.maximum(j - n_left, 0))),
            pl.BlockSpec((d, LANES), lambda i, j: (0, 0)),
            pl.BlockSpec((tm, 1), lambda i, j: (i, 0)),
            pl.BlockSpec((8, LANES), lambda i, j: (0, 0)),
        ],
        out_specs=[
            pl.BlockSpec((tm, tn), lambda i, j: (i, j)),
            pl.BlockSpec((tm, d), lambda i, j: (i, 0)),
            pl.BlockSpec((tm, LANES), lambda i, j: (i, 0)),
        ],
        out_shape=[
            jax.ShapeDtypeStruct((t, n_qkv * tn), BF16),
            jax.ShapeDtypeStruct((t, d), BF16),
            jax.ShapeDtypeStruct((t, LANES), F32),
        ],
        scratch_shapes=[
            pltpu.VMEM((tm, LANES), F32),
            pltpu.VMEM((tm, LANES), F32),
            pltpu.VMEM((tm, LANES), F32),
        ],
        compiler_params=_cparams(("parallel", "arbitrary")),
    )(x2, g1, w_left, w_right, w_f, pos, rope_tab)
    gate = pl.pallas_call(
        _gates_kernel,
        grid=(t // tm, n_gate),
        in_specs=[
            pl.BlockSpec((tm, d), lambda i, j: (i, 0)),
            pl.BlockSpec((d, tn), lambda i, j: (0, n_qkv - n_left + j)),
            pl.BlockSpec((1, tn), lambda i, j: (0, j)),
        ],
        out_specs=pl.BlockSpec((tm, tn), lambda i, j: (i, j)),
        out_shape=jax.ShapeDtypeStruct((t, n_gate * tn), F32),
        compiler_params=_cparams(("parallel", "arbitrary")),
    )(u, w_right, b_gate)
    return qkv, gate, flog


def _fgate_kernel(fl_ref, b_ref, c_ref):
    z = fl_ref[...] + b_ref[...]
    x = jnp.minimum(z, 0.0) - jnp.log(1.0 + jnp.exp(-jnp.abs(z)))
    n = x.shape[1]
    lane = lax.broadcasted_iota(jnp.int32, x.shape, 1)
    sh = 1
    while sh < n:
        x = x + jnp.where(lane >= sh, pltpu.roll(x, sh, 1), 0.0)
        sh *= 2
    c_ref[...] = x


def _fgate(fl, b):
    r, s = fl.shape
    return pl.pallas_call(
        _fgate_kernel,
        grid=(1,),
        in_specs=[pl.BlockSpec((r, s), lambda i: (0, 0)), pl.BlockSpec((r, 1), lambda i: (0, 0))],
        out_specs=pl.BlockSpec((r, s), lambda i: (0, 0)),
        out_shape=jax.ShapeDtypeStruct((r, s), F32),
        compiler_params=_cparams(("arbitrary",)),
    )(fl, b)


def _transpose_bf16(x):
    return x.astype(F32).T.astype(BF16)


def _softmax_step(s, vt, m_sc, l_sc, acc_sc):
    m_old = m_sc[...]
    m_new = jnp.maximum(m_old, jnp.max(s, axis=0, keepdims=True))
    alpha = jnp.exp2(m_old - m_new)
    p = jnp.exp2(s - m_new)
    l_sc[...] = alpha * l_sc[...] + jnp.sum(p, axis=0, keepdims=True)
    acc_sc[...] = alpha * acc_sc[...] + jnp.dot(vt, p.astype(BF16), preferred_element_type=F32)
    m_sc[...] = m_new


def _causal_keep(shape):
    kv = lax.broadcasted_iota(jnp.int32, shape, 0)
    qq = lax.broadcasted_iota(jnp.int32, shape, 1)
    return kv <= qq


FOX_HEADS_PER_STEP = 2


def _fox_kernel(q_ref, k_ref, v_ref, c_ref, o_ref, *scratch, tq, tk, s_len):
    hp = FOX_HEADS_PER_STEP
    qt_sc, kaug_sc, vt_sc, m_sc, l_sc, acc_sc = (scratch[n * hp:(n + 1) * hp] for n in range(6))
    s_sc = scratch[6 * hp:]
    qi = pl.program_id(2)
    head_cols = [slice(h * HEAD_DIM, (h + 1) * HEAD_DIM) for h in range(hp)]

    @pl.when(qi == 0)
    def _():
        sub = lax.broadcasted_iota(jnp.int32, (HEAD_DIM, tk), 0)
        for h in range(hp):
            for ch in range(s_len // tk):
                rows = slice(ch * tk, (ch + 1) * tk)
                vt_sc[h][:, rows] = _transpose_bf16(v_ref[rows, head_cols[h]])
                c = c_ref[h, :, rows] * (-LOG2E)
                hi = c.astype(BF16).astype(F32)
                mid = (c - hi).astype(BF16).astype(F32)
                lo = (c - hi - mid).astype(BF16).astype(F32)
                tab = jnp.where(sub == 0, hi, jnp.where(sub == 1, mid, jnp.where(sub == 2, lo, 0.0)))
                kaug_sc[h][rows, HEAD_DIM:] = tab.T.astype(BF16)
                kaug_sc[h][rows, :HEAD_DIM] = k_ref[rows, head_cols[h]]

    sub_q = lax.broadcasted_iota(jnp.int32, (HEAD_DIM, tq), 0)
    for h in range(hp):
        qt_sc[h][:HEAD_DIM, :] = _transpose_bf16(q_ref[:, head_cols[h]])
        qt_sc[h][HEAD_DIM:, :] = jnp.where(sub_q < 3, 1.0, 0.0).astype(BF16)
        m_sc[h][...] = jnp.full(m_sc[h].shape, -jnp.inf, F32)
        l_sc[h][...] = jnp.zeros(l_sc[h].shape, F32)
        acc_sc[h][...] = jnp.zeros(acc_sc[h].shape, F32)

    def scores(h, chunk):
        start = pl.multiple_of(chunk * tk, tk)
        return jnp.dot(kaug_sc[h][pl.ds(start, tk), :], qt_sc[h][...], preferred_element_type=F32)

    def consume(h, chunk, par, masked):
        s = s_sc[par * hp + h][...]
        if masked:
            s = jnp.where(_causal_keep(s.shape), s, -jnp.inf)
        start = pl.multiple_of(chunk * tk, tk)
        _softmax_step(s, vt_sc[h][:, pl.ds(start, tk)], m_sc[h], l_sc[h], acc_sc[h])

    for h in range(hp):
        s_sc[h][...] = scores(h, 0)

    def body(kc, carry):
        def run(par):
            for h in range(hp):
                s_sc[(1 - par) * hp + h][...] = scores(h, kc + 1)
                consume(h, kc, par, False)
        pl.when(kc % 2 == 0)(functools.partial(run, 0))
        pl.when(kc % 2 == 1)(functools.partial(run, 1))
        return carry

    lax.fori_loop(0, qi, body, 0)

    def diagonal(par):
        for h in range(hp):
            consume(h, qi, par, True)
    pl.when(qi % 2 == 0)(functools.partial(diagonal, 0))
    pl.when(qi % 2 == 1)(functools.partial(diagonal, 1))
    for h in range(hp):
        o_ref[:, head_cols[h]] = (acc_sc[h][...] / l_sc[h][...]).T.astype(o_ref.dtype)


def _fox_attn(qkv, c3, b, s, tq):
    t = b * s
    nq = s // tq
    hp = FOX_HEADS_PER_STEP
    groups = FOX_HEADS // hp
    width = hp * HEAD_DIM
    kern = functools.partial(_fox_kernel, tq=tq, tk=tq, s_len=s)

    def per_head(shape, dtype):
        return [pltpu.VMEM(shape, dtype) for _ in range(hp)]

    return pl.pallas_call(
        kern,
        grid=(b, groups, nq),
        in_specs=[
            pl.BlockSpec((tq, width), lambda bi, g, qi: (bi * nq + qi, g)),
            pl.BlockSpec((s, width), lambda bi, g, qi: (bi, groups + g)),
            pl.BlockSpec((s, width), lambda bi, g, qi: (bi, 2 * groups + g)),
            pl.BlockSpec((hp, 1, s), lambda bi, g, qi: (bi * groups + g, 0, 0)),
        ],
        out_specs=pl.BlockSpec((tq, width), lambda bi, g, qi: (bi * nq + qi, g)),
        out_shape=jax.ShapeDtypeStruct((t, FOX_HEADS * HEAD_DIM), BF16),
        scratch_shapes=(per_head((2 * HEAD_DIM, tq), BF16) + per_head((s, 2 * HEAD_DIM), BF16)
                        + per_head((HEAD_DIM, s), BF16) + per_head((1, tq), F32)
                        + per_head((1, tq), F32) + per_head((HEAD_DIM, tq), F32)
                        + per_head((tq, tq), F32) + per_head((tq, tq), F32)),
        compiler_params=_cparams(("parallel", "parallel", "arbitrary")),
    )(qkv, qkv, qkv, c3)


def _diff_kernel(q1_ref, q2_ref, k1_ref, k2_ref, v_ref, lam_ref, g_ref, o_ref,
                 q1t_sc, q2t_sc, vt_sc, m1_sc, l1_sc, a1_sc, m2_sc, l2_sc, a2_sc, *s_sc,
                 tq, tk, s_len, lam_init):
    qi = pl.program_id(2)

    @pl.when(qi == 0)
    def _():
        for ch in range(s_len // tk):
            rows = slice(ch * tk, (ch + 1) * tk)
            vt_sc[:, rows] = _transpose_bf16(v_ref[rows, :])

    q1t_sc[...] = _transpose_bf16(q1_ref[...])
    q2t_sc[...] = _transpose_bf16(q2_ref[...])
    for r in (m1_sc, m2_sc):
        r[...] = jnp.full(r.shape, -jnp.inf, F32)
    for r in (l1_sc, l2_sc, a1_sc, a2_sc):
        r[...] = jnp.zeros(r.shape, F32)

    chains = ((k1_ref, q1t_sc, m1_sc, l1_sc, a1_sc), (k2_ref, q2t_sc, m2_sc, l2_sc, a2_sc))

    def scores(c, chunk):
        start = pl.multiple_of(chunk * tk, tk)
        return jnp.dot(chains[c][0][pl.ds(start, tk), :], chains[c][1][...], preferred_element_type=F32)

    def consume(c, chunk, par, masked):
        s = s_sc[par * 2 + c][...]
        if masked:
            s = jnp.where(_causal_keep(s.shape), s, -jnp.inf)
        start = pl.multiple_of(chunk * tk, tk)
        _softmax_step(s, vt_sc[:, pl.ds(start, tk)], *chains[c][2:])

    for c in range(2):
        s_sc[c][...] = scores(c, 0)

    def body(kc, carry):
        def run(par):
            for c in range(2):
                s_sc[(1 - par) * 2 + c][...] = scores(c, kc + 1)
                consume(c, kc, par, False)
        pl.when(kc % 2 == 0)(functools.partial(run, 0))
        pl.when(kc % 2 == 1)(functools.partial(run, 1))
        return carry

    lax.fori_loop(0, qi, body, 0)

    def diagonal(par):
        for c in range(2):
            consume(c, qi, par, True)
    pl.when(qi % 2 == 0)(functools.partial(diagonal, 0))
    pl.when(qi % 2 == 1)(functools.partial(diagonal, 1))

    lp = lam_ref[...]
    lam = (jnp.exp(jnp.sum(lp[0:1, :] * lp[1:2, :], axis=-1, keepdims=True))
           - jnp.exp(jnp.sum(lp[2:3, :] * lp[3:4, :], axis=-1, keepdims=True)) + lam_init)
    y = (a1_sc[...] / l1_sc[...]).T - lam * (a2_sc[...] / l2_sc[...]).T
    ms = jnp.mean(y * y, axis=-1, keepdims=True)
    y = (y * lax.rsqrt(ms + EPS)) * g_ref[...]
    o_ref[...] = (y * (1.0 - lam_init)).astype(o_ref.dtype)


def _diff_attn(qkv, lam_p, subln_g, b, s, tq, lam_init):
    t = b * s
    nq = s // tq
    qb = 3 * FOX_HEADS
    kb = qb + 2 * DIFF_HEADS
    vb = (kb + 2 * DIFF_HEADS) // 2
    kern = functools.partial(_diff_kernel, tq=tq, tk=tq, s_len=s, lam_init=lam_init)
    return pl.pallas_call(
        kern,
        grid=(b, DIFF_HEADS, nq),
        in_specs=[
            pl.BlockSpec((tq, HEAD_DIM), lambda bi, h, qi: (bi * nq + qi, qb + 2 * h)),
            pl.BlockSpec((tq, HEAD_DIM), lambda bi, h, qi: (bi * nq + qi, qb + 2 * h + 1)),
            pl.BlockSpec((s, HEAD_DIM), lambda bi, h, qi: (bi, kb + 2 * h)),
            pl.BlockSpec((s, HEAD_DIM), lambda bi, h, qi: (bi, kb + 2 * h + 1)),
            pl.BlockSpec((s, DIFF_V_DIM), lambda bi, h, qi: (bi, vb + h)),
            pl.BlockSpec((4, HEAD_DIM), lambda bi, h, qi: (0, 0)),
            pl.BlockSpec((1, DIFF_V_DIM), lambda bi, h, qi: (0, 0)),
        ],
        out_specs=pl.BlockSpec((tq, DIFF_V_DIM), lambda bi, h, qi: (bi * nq + qi, h)),
        out_shape=jax.ShapeDtypeStruct((t, DIFF_HEADS * DIFF_V_DIM), BF16),
        scratch_shapes=[
            pltpu.VMEM((HEAD_DIM, tq), BF16), pltpu.VMEM((HEAD_DIM, tq), BF16),
            pltpu.VMEM((DIFF_V_DIM, s), BF16),
            pltpu.VMEM((1, tq), F32), pltpu.VMEM((1, tq), F32), pltpu.VMEM((DIFF_V_DIM, tq), F32),
            pltpu.VMEM((1, tq), F32), pltpu.VMEM((1, tq), F32), pltpu.VMEM((DIFF_V_DIM, tq), F32),
        ] + [pltpu.VMEM((tq, tq), F32) for _ in range(4)],
        compiler_params=_cparams(("parallel", "parallel", "arbitrary")),
    )(qkv, qkv, qkv, qkv, qkv, lam_p, subln_g)


def _pack_bf16_pairs(xb):
    w = xb.shape[1] // 2
    lo = lax.bitcast_convert_type(xb[:, :w].astype(F32), jnp.uint32)
    hi = lax.bitcast_convert_type(xb[:, w:].astype(F32), jnp.uint32)
    return (lo >> 16) | (hi & jnp.uint32(0xFFFF0000))


def _unpack_bf16_pairs(p):
    lo = lax.bitcast_convert_type(p << 16, F32).astype(BF16)
    hi = lax.bitcast_convert_type(p & jnp.uint32(0xFFFF0000), F32).astype(BF16)
    return lo, hi


def _merge_kernel(yf_ref, yd_ref, g0_ref, g1_ref, x_ref, wbf_ref, wbd_ref, wo_ref, n2_ref,
                  rw_ref, rb_ref, h_ref, u_ref, idx_ref, gw_ref):
    ya = jnp.dot(yf_ref[...], wbf_ref[...], preferred_element_type=F32)
    yb = jnp.dot(yd_ref[...], wbd_ref[...], preferred_element_type=F32)
    merged = g0_ref[...] * ya + g1_ref[...] * yb
    h = x_ref[...] + jnp.dot(merged.astype(BF16), wo_ref[...], preferred_element_type=F32)
    h_ref[...] = h
    ms = jnp.mean(h * h, axis=-1, keepdims=True)
    u = (h * lax.rsqrt(ms + EPS)) * n2_ref[...]
    ub = u.astype(BF16)
    u_ref[...] = _pack_bf16_pairs(ub)
    logits = jnp.dot(ub, rw_ref[...], preferred_element_type=F32) + rb_ref[...]

    lane = lax.broadcasted_iota(jnp.int32, logits.shape, 1).astype(F32)
    work = logits
    vals, idxs = [], []
    for _ in range(TOP_K):
        m = jnp.max(work, axis=-1, keepdims=True)
        sel = jnp.min(jnp.where(work == m, lane, float(LANES)), axis=-1, keepdims=True)
        vals.append(m)
        idxs.append(sel)
        work = jnp.where(lane == sel, -jnp.inf, work)
    es = [jnp.exp(v - vals[0]) for v in vals]
    den = es[0]
    for e in es[1:]:
        den = den + e
    idx_out = jnp.zeros(logits.shape, F32)
    gw_out = jnp.zeros(logits.shape, F32)
    for k in range(TOP_K):
        idx_out = jnp.where(lane == float(k), idxs[k], idx_out)
        gw_out = jnp.where(lane == float(k), es[k] / den, gw_out)
    idx_ref[...] = idx_out.astype(jnp.int32)
    gw_ref[...] = gw_out


def _merge(yf, yd, gate, x2, wbf, wbd, wo, n2, rw, rb, tm):
    t, d = x2.shape
    fw = yf.shape[1]
    dw = yd.shape[1]
    const = dict(pipeline_mode=pl.Buffered(1))
    return pl.pallas_call(
        _merge_kernel,
        grid=(t // tm,),
        in_specs=[
            pl.BlockSpec((tm, fw), lambda i: (i, 0)),
            pl.BlockSpec((tm, dw), lambda i: (i, 0)),
            pl.BlockSpec((tm, d), lambda i: (i, 0)),
            pl.BlockSpec((tm, d), lambda i: (i, 1)),
            pl.BlockSpec((tm, d), lambda i: (i, 0)),
            pl.BlockSpec((fw, d), lambda i: (0, 0), **const),
            pl.BlockSpec((dw, d), lambda i: (0, 0), **const),
            pl.BlockSpec((d, d), lambda i: (0, 0), **const),
            pl.BlockSpec((1, d), lambda i: (0, 0)),
            pl.BlockSpec((d, LANES), lambda i: (0, 0), **const),
            pl.BlockSpec((1, LANES), lambda i: (0, 0)),
        ],
        out_specs=[
            pl.BlockSpec((tm, d), lambda i: (i, 0)),
            pl.BlockSpec((tm, d // 2), lambda i: (i, 0)),
            pl.BlockSpec((tm, LANES), lambda i: (i, 0)),
            pl.BlockSpec((tm, LANES), lambda i: (i, 0)),
        ],
        out_shape=[
            jax.ShapeDtypeStruct((t, d), F32),
            jax.ShapeDtypeStruct((t, d // 2), jnp.uint32),
            jax.ShapeDtypeStruct((t, LANES), jnp.int32),
            jax.ShapeDtypeStruct((t, LANES), F32),
        ],
        compiler_params=_cparams(("parallel",)),
    )(yf, yd, gate, gate, x2, wbf, wbd, wo, n2, rw, rb)


def _dispatch_kernel(cnt_ref, pstart_ref, pend_ref, roa_ref, u_ref, xs_hbm, stage, zeros, sems,
                     *, tb, sub, n_rows):
    s = pl.program_id(0)
    n = pl.num_programs(0)
    slot = s % 2
    n_dma = TOP_K * tb

    def wait_slot(sl):
        pltpu.make_async_copy(stage.at[sl], xs_hbm.at[pl.ds(0, n_dma), :], sems.at[sl]).wait()

    @pl.when(s == 0)
    def _():
        zeros[...] = jnp.zeros(zeros.shape, zeros.dtype)

    @pl.when(s >= 2)
    def _():
        wait_slot(slot)

    blk = u_ref[...]
    for k in range(TOP_K):
        stage[slot, pl.ds(k * tb, tb), :] = blk

    def body(g, carry):
        for q8 in range(ISSUE_UNROLL):
            q = g * ISSUE_UNROLL + q8
            pltpu.make_async_copy(stage.at[slot, pl.ds(q, 1), :],
                                  xs_hbm.at[pl.ds(roa_ref[0, 0, q], 1), :], sems.at[slot]).start()
        return carry
    lax.fori_loop(0, n_dma // ISSUE_UNROLL, body, 0)

    @pl.when(s == n - 1)
    def _():
        @pl.when(n >= 2)
        def _():
            wait_slot(1 - slot)
        wait_slot(slot)

        def zero_row(r):
            return pltpu.make_async_copy(zeros.at[pl.ds(0, 1), :], xs_hbm.at[pl.ds(r, 1), :], sems.at[2])

        def zero_block(r):
            return pltpu.make_async_copy(zeros, xs_hbm.at[pl.ds(pl.multiple_of(r, sub), sub), :], sems.at[2])

        def fill(first, stop, copy, step):
            cnt = (stop - first) // step

            def start(q, carry):
                copy(first + q * step).start()
                return carry

            def wait(q, carry):
                copy(first).wait()
                return carry
            lax.fori_loop(0, cnt, start, 0)
            lax.fori_loop(0, cnt, wait, 0)

        def per_expert(e, carry):
            first = pstart_ref[e] + cnt_ref[e]
            edge = (first + sub - 1) // sub * sub
            fill(first, edge, zero_row, 1)
            fill(edge, pend_ref[e], zero_block, sub)
            return carry
        lax.fori_loop(0, N_EXPERTS, per_expert, 0)
        fill(pend_ref[N_EXPERTS - 1], n_rows, zero_block, sub)


def _dispatch(counts, pad_start, pad_end, roa3, u_packed, n_rows, tb, sub):
    t, w = u_packed.shape
    kern = functools.partial(_dispatch_kernel, tb=tb, sub=sub, n_rows=n_rows)
    return pl.pallas_call(
        kern,
        grid_spec=pltpu.PrefetchScalarGridSpec(
            num_scalar_prefetch=3,
            grid=(t // tb,),
            in_specs=[
                pl.BlockSpec((1, 1, TOP_K * tb), lambda s, c, ps, pe: (s, 0, 0), memory_space=pltpu.SMEM),
                pl.BlockSpec((tb, w), lambda s, c, ps, pe: (s, 0)),
            ],
            out_specs=pl.BlockSpec(memory_space=pl.ANY),
            scratch_shapes=[
                pltpu.VMEM((2, TOP_K * tb, w), jnp.uint32),
                pltpu.VMEM((sub, w), jnp.uint32),
                pltpu.SemaphoreType.DMA((3,)),
            ],
        ),
        out_shape=jax.ShapeDtypeStruct((n_rows, w), jnp.uint32),
        compiler_params=_cparams(("arbitrary",)),
    )(counts, pad_start, pad_end, roa3, u_packed)


def _expert_kernel(te_ref, ts_ref, nch_ref, xs_ref, wg_ref, wu_ref, bg_ref, bu_ref, wd_ref,
                   bd_ref, out_ref, x_sc, act_sc, *, chunks, nf):
    i = pl.program_id(0)
    j = pl.program_id(1)
    nch = nch_ref[i]
    half = x_sc.shape[1] // 2

    @pl.when(jnp.logical_and(j == 0, nch > 0))
    def _():
        lo, hi = _unpack_bf16_pairs(xs_ref[...])
        x_sc[:, :half] = lo
        x_sc[:, half:] = hi

    full = nch == len(chunks)

    def gate_up(rs, wg, wu):
        x = x_sc[rs, :]
        hg = jnp.dot(x, wg, preferred_element_type=F32) + bg_ref[0]
        hu = jnp.dot(x, wu, preferred_element_type=F32) + bu_ref[0]
        hg = jnp.minimum(hg, SWIGLU_LIMIT)
        hl = jnp.clip(hu, -SWIGLU_LIMIT, SWIGLU_LIMIT)
        act = hg * jax.nn.sigmoid(SWIGLU_ALPHA * hg) * (hl + 1.0)
        act_sc[j, rs, :] = act.astype(BF16)

    def down(rs, wd):
        act = jnp.concatenate([act_sc[c, rs, :] for c in range(nf)], axis=1)
        y = jnp.dot(act, wd, preferred_element_type=F32) + bd_ref[0]
        out_ref[rs, :] = _pack_bf16_pairs(y.astype(BF16))

    partial_tile = jnp.logical_and(nch > 0, nch < len(chunks))
    row_slices = [slice(a, b) for a, b in chunks]

    @pl.when(jnp.logical_and(j < nf, full))
    def _():
        gate_up(slice(None), wg_ref[0].astype(BF16), wu_ref[0].astype(BF16))

    @pl.when(jnp.logical_and(j < nf, partial_tile))
    def _():
        wg = wg_ref[0].astype(BF16)
        wu = wu_ref[0].astype(BF16)
        for c, rs in enumerate(row_slices[:-1]):
            pl.when(c < nch)(functools.partial(gate_up, rs, wg, wu))

    @pl.when(jnp.logical_and(j >= nf, full))
    def _():
        down(slice(None), wd_ref[0].astype(BF16))

    @pl.when(jnp.logical_and(j >= nf, partial_tile))
    def _():
        wd = wd_ref[0].astype(BF16)
        for c, rs in enumerate(row_slices[:-1]):
            pl.when(c < nch)(functools.partial(down, rs, wd))

    @pl.when(jnp.logical_and(j >= nf, nch < len(chunks)))
    def _():
        for c, (a, b) in enumerate(chunks):
            @pl.when(c >= nch)
            def _():
                out_ref[a:b, :] = jnp.zeros((b - a, out_ref.shape[1]), out_ref.dtype)


def _experts(tile_e, tile_src, tile_nch, xs, w_gu, b_gu3, w_down, b_down3, tm_e, chunks, tf):
    n_rows = xs.shape[0]
    n_tiles = n_rows // tm_e
    d = 2 * xs.shape[1]
    dff = w_down.shape[1]
    nf = dff // tf
    nn = d // tf
    assert nn == nf
    kern = functools.partial(_expert_kernel, chunks=chunks, nf=nf)

    def c1(i, j, ns):
        return jnp.where(ns[i] > 0, jnp.minimum(j, nf - 1), nf - 1)

    def c2(i, j, ns):
        return jnp.where(ns[i] > 0, jnp.maximum(j - nf, 0), nf - 1)

    return pl.pallas_call(
        kern,
        grid_spec=pltpu.PrefetchScalarGridSpec(
            num_scalar_prefetch=3,
            grid=(n_tiles, 2 * nf),
            in_specs=[
                pl.BlockSpec((tm_e, d // 2), lambda i, j, te, ts, ns: (ts[i], 0)),
                pl.BlockSpec((1, d, tf), lambda i, j, te, ts, ns: (te[i], 0, c1(i, j, ns))),
                pl.BlockSpec((1, d, tf), lambda i, j, te, ts, ns: (te[i], 0, nf + c1(i, j, ns))),
                pl.BlockSpec((1, 1, tf), lambda i, j, te, ts, ns: (te[i], 0, c1(i, j, ns))),
                pl.BlockSpec((1, 1, tf), lambda i, j, te, ts, ns: (te[i], 0, nf + c1(i, j, ns))),
                pl.BlockSpec((1, dff, tf), lambda i, j, te, ts, ns: (te[i], 0, c2(i, j, ns))),
                pl.BlockSpec((1, 1, tf), lambda i, j, te, ts, ns: (te[i], 0, c2(i, j, ns))),
            ],
            out_specs=pl.BlockSpec((tm_e, tf // 2), lambda i, j, te, ts, ns: (i, jnp.maximum(j - nf, 0))),
            scratch_shapes=[
                pltpu.VMEM((tm_e, d), BF16),
                pltpu.VMEM((nf, tm_e, tf), BF16),
            ],
        ),
        out_shape=jax.ShapeDtypeStruct((n_rows, d // 2), jnp.uint32),
        compiler_params=_cparams(("arbitrary", "arbitrary"), EXPERT_VMEM_LIMIT),
    )(tile_e, tile_src, tile_nch, xs, w_gu, w_gu, b_gu3, b_gu3, w_down, b_down3)


def _combine_kernel(cur_ref, nxt_ref, yr_hbm, h_ref, gw_ref, gf_ref, out_ref, ysel, sems, *, tmc,
                    chunk, final_norm):
    i = pl.program_id(0)
    n = pl.num_programs(0)
    slot = i % 2
    n_dma = TOP_K * tmc

    def row_copy(row, q, sl):
        return pltpu.make_async_copy(yr_hbm.at[pl.ds(row, 1), :], ysel.at[sl, pl.ds(q, 1), :], sems.at[sl])

    def issue(idx_ref, sl):
        def body(g, carry):
            for q8 in range(ISSUE_UNROLL):
                q = g * ISSUE_UNROLL + q8
                row_copy(idx_ref[0, 0, q], q, sl).start()
            return carry
        lax.fori_loop(0, n_dma // ISSUE_UNROLL, body, 0)

    @pl.when(i == 0)
    def _():
        issue(cur_ref, 0)

    @pl.when(i + 1 < n)
    def _():
        issue(nxt_ref, 1 - slot)

    pltpu.make_async_copy(yr_hbm.at[pl.ds(0, n_dma), :], ysel.at[slot], sems.at[slot]).wait()

    gw = gw_ref[...]
    y = h_ref[...]
    half = chunk // 2
    for k in range(TOP_K):
        lo, hi = _unpack_bf16_pairs(ysel[slot, k * tmc:(k + 1) * tmc, :])
        pieces = []
        for c in range(lo.shape[1] // half):
            pieces += [lo[:, c * half:(c + 1) * half], hi[:, c * half:(c + 1) * half]]
        y = y + gw[:, k:k + 1] * jnp.concatenate(pieces, axis=1).astype(F32)
    if final_norm:
        ms = jnp.mean(y * y, axis=-1, keepdims=True)
        y = (y * lax.rsqrt(ms + EPS)) * gf_ref[...]
    out_ref[...] = y


def _combine(roa3, y_rows, h, gw, gf, tmc, chunk, final_norm):
    t, d = h.shape
    n = t // tmc
    kern = functools.partial(_combine_kernel, tmc=tmc, chunk=chunk, final_norm=final_norm)
    return pl.pallas_call(
        kern,
        grid=(n,),
        in_specs=[
            pl.BlockSpec((1, 1, TOP_K * tmc), lambda i: (i, 0, 0), memory_space=pltpu.SMEM),
            pl.BlockSpec((1, 1, TOP_K * tmc), lambda i: (jnp.minimum(i + 1, n - 1), 0, 0),
                         memory_space=pltpu.SMEM),
            pl.BlockSpec(memory_space=pl.ANY),
            pl.BlockSpec((tmc, d), lambda i: (i, 0)),
            pl.BlockSpec((tmc, LANES), lambda i: (i, 0)),
            pl.BlockSpec((1, d), lambda i: (0, 0)),
        ],
        out_specs=pl.BlockSpec((tmc, d), lambda i: (i, 0)),
        out_shape=jax.ShapeDtypeStruct((t, d), F32),
        scratch_shapes=[pltpu.VMEM((2, TOP_K * tmc, d // 2), jnp.uint32), pltpu.SemaphoreType.DMA((2,))],
        compiler_params=_cparams(("arbitrary",)),
    )(roa3, roa3, y_rows, h, gw, gf)


def _routing(top_idx, tm_e, chunks):
    t = top_idx.shape[0]
    a = t * TOP_K
    flat_e = top_idx.reshape(a)
    onehot = (flat_e[:, None] == jnp.arange(N_EXPERTS, dtype=jnp.int32)[None, :]).astype(jnp.int32)
    csum = jnp.cumsum(onehot, axis=0)
    counts = csum[-1]
    rank = jnp.sum(onehot * csum, axis=1) - 1
    padded = (counts + tm_e - 1) // tm_e * tm_e
    pad_end = jnp.cumsum(padded)
    pad_start = pad_end - padded
    row_of_assign = (pad_start[flat_e] + rank).astype(jnp.int32)
    n_tiles = -(-a // tm_e) + N_EXPERTS
    tile_start = jnp.arange(n_tiles, dtype=jnp.int32) * tm_e
    n_used = pad_end[-1] // tm_e
    tile_e_raw = jnp.minimum(jnp.searchsorted(pad_end, tile_start, side='right'), N_EXPERTS - 1)
    used = tile_start < pad_end[-1]
    last = jnp.maximum(n_used - 1, 0)
    tile_src = jnp.where(used, jnp.arange(n_tiles, dtype=jnp.int32), last).astype(jnp.int32)
    tile_e = tile_e_raw[tile_src].astype(jnp.int32)
    valid_rows = jnp.clip(counts[tile_e_raw] - (tile_start - pad_start[tile_e_raw]), 0, tm_e)
    starts = jnp.asarray([a for a, _ in chunks], jnp.int32)
    tile_nch = jnp.sum(starts[None, :] < jnp.where(used, valid_rows, 0)[:, None], axis=1).astype(jnp.int32)
    bounds = (counts.astype(jnp.int32), pad_start.astype(jnp.int32), pad_end.astype(jnp.int32))
    return bounds, row_of_assign, tile_e, tile_src, tile_nch, n_tiles * tm_e


def kernel(x, positions, norm1_g, w_in, b_fgate, b_gate, lam_q1, lam_k1, lam_q2, lam_k2, subln_g,
           w_branch_fox, w_branch_diff, w_out, norm2_g, router_w, router_b, w_gu, b_gu, w_down,
           b_down, normf_g):
    b, s, d = x.shape
    t = b * s
    depth = norm1_g.shape[0]
    fox_w = FOX_HEADS * HEAD_DIM
    tm_a = _pick(t, 1024)
    tq = _pick(s, 512)
    tm_d = _pick(t, 256)
    tm_e, chunk, zsub, tf = 2304, 512, 128, 256
    chunks = tuple((a, min(a + chunk, tm_e)) for a in range(0, tm_e, chunk))
    tmc = _pick(t, 256)

    half = ROPE_DIM // 2
    inv_freq = ROPE_THETA ** (-jnp.arange(0, ROPE_DIM, 2, dtype=F32) / ROPE_DIM)
    zeros = jnp.zeros((LANES - ROPE_DIM,), F32)
    rope_tab = jnp.zeros((8, LANES), F32)
    rope_tab = rope_tab.at[0].set(jnp.concatenate([inv_freq, inv_freq, zeros]))
    rope_tab = rope_tab.at[1].set(jnp.concatenate([-jnp.ones((half,), F32), jnp.zeros((half,), F32), zeros]))
    rope_tab = rope_tab.at[2].set(jnp.concatenate([jnp.zeros((half,), F32), jnp.ones((half,), F32), zeros]))
    pos = positions.astype(F32).reshape(t, 1)

    h = x.reshape(t, d)
    for l in range(depth):
        wi = w_in[l]
        w_left = wi[:, :3 * fox_w].astype(BF16)
        w_right = wi[:, 3 * fox_w + FOX_HEADS:].astype(BF16)
        w_f = jnp.pad(wi[:, 3 * fox_w:3 * fox_w + FOX_HEADS], ((0, 0), (0, LANES - FOX_HEADS))).astype(BF16)
        qkv, gate, flog = _inproj(h, pos, norm1_g[l][None, :], w_left, w_right, w_f, rope_tab,
                                  b_gate[l][None, :], tm_a)

        fl = flog[:, :FOX_HEADS].reshape(b, s, FOX_HEADS).transpose(0, 2, 1).reshape(b * FOX_HEADS, s)
        bf = jnp.tile(b_fgate[l].astype(F32), b).reshape(b * FOX_HEADS, 1)
        c3 = _fgate(fl, bf).reshape(b * FOX_HEADS, 1, s)

        y_fox = _fox_attn(qkv, c3, b, s, tq)
        lam_init = 0.8 - 0.6 * math.exp(-0.3 * l)
        lam_p = jnp.stack([lam_q1[l], lam_k1[l], lam_q2[l], lam_k2[l]]).astype(F32)
        y_diff = _diff_attn(qkv, lam_p, subln_g[l][None, :].astype(F32), b, s, tq, lam_init)

        rw = jnp.pad(router_w[l], ((0, 0), (0, LANES - N_EXPERTS))).astype(BF16)
        rb = jnp.concatenate([router_b[l].astype(F32), jnp.full((LANES - N_EXPERTS,), NEG_BIG, F32)])[None, :]
        h, u2, idx128, gw128 = _merge(y_fox, y_diff, gate, h, w_branch_fox[l].astype(BF16),
                                      w_branch_diff[l].astype(BF16), w_out[l].astype(BF16),
                                      norm2_g[l][None, :], rw, rb, tm_d)

        bounds, roa, tile_e, tile_src, tile_nch, n_rows = _routing(idx128[:, :TOP_K], tm_e, chunks)
        roa3 = roa.reshape(t // tmc, tmc, TOP_K).transpose(0, 2, 1).reshape(t // tmc, 1, TOP_K * tmc)
        xs = _dispatch(*bounds, roa3, u2, n_rows, tmc, zsub)
        y_rows = _experts(tile_e, tile_src, tile_nch, xs, w_gu[l], b_gu[l][:, None, :],
                          w_down[l], b_down[l][:, None, :], tm_e, chunks, tf)
        h = _combine(roa3, y_rows, h, gw128, normf_g[None, :], tmc, tf, final_norm=(l == depth - 1))
    return h.reshape(b, s, d)
```

```python
import functools
import math

import jax
import jax.numpy as jnp
from jax import lax
from jax.experimental import pallas as pl
from jax.experimental.pallas import tpu as pltpu

HEAD_DIM = 128
FOX_HEADS = 8
DIFF_HEADS = 4
DIFF_V_DIM = 2 * HEAD_DIM
ROPE_THETA = 500000.0
ROPE_DIM = HEAD_DIM // 4
N_EXPERTS = 32
TOP_K = 4
SWIGLU_ALPHA = 1.702
SWIGLU_LIMIT = 7.0
EPS = 1e-5
LANES = 128

F32 = jnp.float32
BF16 = jnp.bfloat16
NEG_BIG = -1e30
LOG2E = math.log2(math.e)
SUBLANES = 8
VMEM_LIMIT = 56 * 1024 * 1024


def _cparams(sem, vmem=VMEM_LIMIT):
    return pltpu.CompilerParams(dimension_semantics=sem, vmem_limit_bytes=vmem)


def _pick(n, pref):
    t = min(n, pref)
    while n % t:
        t //= 2
    return t


QKV_TILE = 1024
QKV_LEFT_TILES = 3
QKV_ROTARY_TILES = (3, 4)
QKV_SCALED_TILES = (0, 3)


def _qkv_kernel(x_ref, g1_ref, wa_ref, wb_ref, wf_ref, pos_ref, tab_ref,
                qkv_ref, u_ref, flog_ref, cos_sc, sa_sc, sb_sc, *, scale):
    j = pl.program_id(1)

    @pl.when(j == 0)
    def _():
        xf = x_ref[...]
        ms = jnp.mean(xf * xf, axis=-1, keepdims=True)
        u = (xf * lax.rsqrt(ms + EPS)) * g1_ref[...]
        ub = u.astype(BF16)
        u_ref[...] = ub
        flog_ref[...] = jnp.dot(ub, wf_ref[...], preferred_element_type=F32)
        ang = pos_ref[...] * tab_ref[0:1, :]
        sn = jnp.sin(ang)
        cos_sc[...] = jnp.cos(ang)
        sa_sc[...] = sn * tab_ref[1:2, :]
        sb_sc[...] = sn * tab_ref[2:3, :]

    any_of = lambda tiles: functools.reduce(jnp.logical_or, [j == q for q in tiles])
    mult = jnp.where(any_of(QKV_SCALED_TILES), scale, 1.0).astype(F32)
    rotary = any_of(QKV_ROTARY_TILES)
    left = j < QKV_LEFT_TILES

    def plain(w_ref):
        acc = jnp.dot(u_ref[...], w_ref[...], preferred_element_type=F32)
        qkv_ref[...] = (acc * mult).astype(BF16)

    pl.when(left)(functools.partial(plain, wa_ref))
    pl.when(jnp.logical_not(jnp.logical_or(left, rotary)))(functools.partial(plain, wb_ref))

    @pl.when(rotary)
    def _():
        acc = jnp.dot(u_ref[...], wb_ref[...], preferred_element_type=F32)
        ca = cos_sc[...] * mult
        cb = sa_sc[...] * mult
        cc = sb_sc[...] * mult
        for c in range(acc.shape[1] // HEAD_DIM):
            t = acc[:, c * HEAD_DIM:(c + 1) * HEAD_DIM]
            r = (t * ca + pltpu.roll(t, HEAD_DIM - ROPE_DIM // 2, 1) * cb
                 + pltpu.roll(t, ROPE_DIM // 2, 1) * cc)
            qkv_ref[:, c * HEAD_DIM:(c + 1) * HEAD_DIM] = r.astype(BF16)


def _gates_kernel(u_ref, w_ref, bg_ref, gate_ref):
    acc = jnp.dot(u_ref[...], w_ref[...], preferred_element_type=F32)
    gate_ref[...] = jax.nn.sigmoid(acc + bg_ref[...])


def _inproj(x2, pos, g1, w_left, w_right, w_f, rope_tab, b_gate, tm):
    t, d = x2.shape
    tn = QKV_TILE
    n_qkv = 6
    n_left = QKV_LEFT_TILES
    n_gate = w_right.shape[1] // tn - (n_qkv - n_left)
    kern = functools.partial(_qkv_kernel, scale=LOG2E / math.sqrt(HEAD_DIM))
    qkv, u, flog = pl.pallas_call(
        kern,
        grid=(t // tm, n_qkv),
        in_specs=[
            pl.BlockSpec((tm, d), lambda i, j: (i, 0)),
            pl.BlockSpec((1, d), lambda i, j: (0, 0)),
            pl.BlockSpec((d, tn), lambda i, j: (0, jnp.minimum(j, n_left - 1))),
            pl.BlockSpec((d, tn), lambda i, j: (0, jnp.maximum(j - n_left, 0))),
            pl.BlockSpec((d, LANES), lambda i, j: (0, 0)),
            pl.BlockSpec((tm, 1), lambda i, j: (i, 0)),
            pl.BlockSpec((8, LANES), lambda i, j: (0, 0)),
        ],
        out_specs=[
            pl.BlockSpec((tm, tn), lambda i, j: (i, j)),
            pl.BlockSpec((tm, d), lambda i, j: (i, 0)),
            pl.BlockSpec((tm, LANES), lambda i, j: (i, 0)),
        ],
        out_shape=[
            jax.ShapeDtypeStruct((t, n_qkv * tn), BF16),
            jax.ShapeDtypeStruct((t, d), BF16),
            jax.ShapeDtypeStruct((t, LANES), F32),
        ],
        scratch_shapes=[
            pltpu.VMEM((tm, LANES), F32),
            pltpu.VMEM((tm, LANES), F32),
            pltpu.VMEM((tm, LANES), F32),
        ],
        compiler_params=_cparams(("parallel", "arbitrary")),
    )(x2, g1, w_left, w_right, w_f, pos, rope_tab)
    gate = pl.pallas_call(
        _gates_kernel,
        grid=(t // tm, n_gate),
        in_specs=[
            pl.BlockSpec((tm, d), lambda i, j: (i, 0)),
            pl.BlockSpec((d, tn), lambda i, j: (0, n_qkv - n_left + j)),
            pl.BlockSpec((1, tn), lambda i, j: (0, j)),
        ],
        out_specs=pl.BlockSpec((tm, tn), lambda i, j: (i, j)),
        out_shape=jax.ShapeDtypeStruct((t, n_gate * tn), F32),
        compiler_params=_cparams(("parallel", "arbitrary")),
    )(u, w_right, b_gate)
    return qkv, gate, flog


def _fgate_kernel(fl_ref, b_ref, c_ref):
    z = fl_ref[...] + b_ref[...]
    x = jnp.minimum(z, 0.0) - jnp.log(1.0 + jnp.exp(-jnp.abs(z)))
    n = x.shape[1]
    lane = lax.broadcasted_iota(jnp.int32, x.shape, 1)
    sh = 1
    while sh < n:
        x = x + jnp.where(lane >= sh, pltpu.roll(x, sh, 1), 0.0)
        sh *= 2
    c_ref[...] = x


def _fgate(fl, b):
    r, s = fl.shape
    return pl.pallas_call(
        _fgate_kernel,
        grid=(1,),
        in_specs=[pl.BlockSpec((r, s), lambda i: (0, 0)), pl.BlockSpec((r, 1), lambda i: (0, 0))],
        out_specs=pl.BlockSpec((r, s), lambda i: (0, 0)),
        out_shape=jax.ShapeDtypeStruct((r, s), F32),
        compiler_params=_cparams(("arbitrary",)),
    )(fl, b)


def _transpose_bf16(x):
    return x.astype(F32).T.astype(BF16)


def _softmax_step(s, vt, m_sc, l_sc, acc_sc):
    m_old = m_sc[...]
    m_new = jnp.maximum(m_old, jnp.max(s, axis=0, keepdims=True))
    alpha = jnp.exp2(m_old - m_new)
    p = jnp.exp2(s - m_new)
    l_sc[...] = alpha * l_sc[...] + jnp.sum(p, axis=0, keepdims=True)
    acc_sc[...] = alpha * acc_sc[...] + jnp.dot(vt, p.astype(BF16), preferred_element_type=F32)
    m_sc[...] = m_new


def _causal_keep(shape):
    kv = lax.broadcasted_iota(jnp.int32, shape, 0)
    qq = lax.broadcasted_iota(jnp.int32, shape, 1)
    return kv <= qq


FOX_HEADS_PER_STEP = 2


def _fox_kernel(q_ref, k_ref, v_ref, c_ref, o_ref, *scratch, tq, tk, s_len):
    hp = FOX_HEADS_PER_STEP
    qt_sc, kaug_sc, vt_sc, m_sc, l_sc, acc_sc = (scratch[n * hp:(n + 1) * hp] for n in range(6))
    s_sc = scratch[6 * hp:]
    qi = pl.program_id(2)
    head_cols = [slice(h * HEAD_DIM, (h + 1) * HEAD_DIM) for h in range(hp)]

    @pl.when(qi == 0)
    def _():
        sub = lax.broadcasted_iota(jnp.int32, (HEAD_DIM, tk), 0)
        for h in range(hp):
            for ch in range(s_len // tk):
                rows = slice(ch * tk, (ch + 1) * tk)
                vt_sc[h][:, rows] = _transpose_bf16(v_ref[rows, head_cols[h]])
                c = c_ref[h, :, rows] * (-LOG2E)
                hi = c.astype(BF16).astype(F32)
                mid = (c - hi).astype(BF16).astype(F32)
                lo = (c - hi - mid).astype(BF16).astype(F32)
                tab = jnp.where(sub == 0, hi, jnp.where(sub == 1, mid, jnp.where(sub == 2, lo, 0.0)))
                kaug_sc[h][rows, HEAD_DIM:] = tab.T.astype(BF16)
                kaug_sc[h][rows, :HEAD_DIM] = k_ref[rows, head_cols[h]]

    sub_q = lax.broadcasted_iota(jnp.int32, (HEAD_DIM, tq), 0)
    for h in range(hp):
        qt_sc[h][:HEAD_DIM, :] = _transpose_bf16(q_ref[:, head_cols[h]])
        qt_sc[h][HEAD_DIM:, :] = jnp.where(sub_q < 3, 1.0, 0.0).astype(BF16)
        m_sc[h][...] = jnp.full(m_sc[h].shape, -jnp.inf, F32)
        l_sc[h][...] = jnp.zeros(l_sc[h].shape, F32)
        acc_sc[h][...] = jnp.zeros(acc_sc[h].shape, F32)

    def scores(h, chunk):
        start = pl.multiple_of(chunk * tk, tk)
        return jnp.dot(kaug_sc[h][pl.ds(start, tk), :], qt_sc[h][...], preferred_element_type=F32)

    def consume(h, chunk, par, masked):
        s = s_sc[par * hp + h][...]
        if masked:
            s = jnp.where(_causal_keep(s.shape), s, -jnp.inf)
        start = pl.multiple_of(chunk * tk, tk)
        _softmax_step(s, vt_sc[h][:, pl.ds(start, tk)], m_sc[h], l_sc[h], acc_sc[h])

    for h in range(hp):
        s_sc[h][...] = scores(h, 0)

    def body(kc, carry):
        def run(par):
            for h in range(hp):
                s_sc[(1 - par) * hp + h][...] = scores(h, kc + 1)
                consume(h, kc, par, False)
        pl.when(kc % 2 == 0)(functools.partial(run, 0))
        pl.when(kc % 2 == 1)(functools.partial(run, 1))
        return carry

    lax.fori_loop(0, qi, body, 0)

    def diagonal(par):
        for h in range(hp):
            consume(h, qi, par, True)
    pl.when(qi % 2 == 0)(functools.partial(diagonal, 0))
    pl.when(qi % 2 == 1)(functools.partial(diagonal, 1))
    for h in range(hp):
        o_ref[:, head_cols[h]] = (acc_sc[h][...] / l_sc[h][...]).T.astype(o_ref.dtype)


def _fox_attn(qkv, c3, b, s, tq):
    t = b * s
    nq = s // tq
    hp = FOX_HEADS_PER_STEP
    groups = FOX_HEADS // hp
    width = hp * HEAD_DIM
    kern = functools.partial(_fox_kernel, tq=tq, tk=tq, s_len=s)

    def per_head(shape, dtype):
        return [pltpu.VMEM(shape, dtype) for _ in range(hp)]

    return pl.pallas_call(
        kern,
        grid=(b, groups, nq),
        in_specs=[
            pl.BlockSpec((tq, width), lambda bi, g, qi: (bi * nq + qi, g)),
            pl.BlockSpec((s, width), lambda bi, g, qi: (bi, groups + g)),
            pl.BlockSpec((s, width), lambda bi, g, qi: (bi, 2 * groups + g)),
            pl.BlockSpec((hp, 1, s), lambda bi, g, qi: (bi * groups + g, 0, 0)),
        ],
        out_specs=pl.BlockSpec((tq, width), lambda bi, g, qi: (bi * nq + qi, g)),
        out_shape=jax.ShapeDtypeStruct((t, FOX_HEADS * HEAD_DIM), BF16),
        scratch_shapes=(per_head((2 * HEAD_DIM, tq), BF16) + per_head((s, 2 * HEAD_DIM), BF16)
                        + per_head((HEAD_DIM, s), BF16) + per_head((1, tq), F32)
                        + per_head((1, tq), F32) + per_head((HEAD_DIM, tq), F32)
                        + per_head((tq, tq), F32) + per_head((tq, tq), F32)),
        compiler_params=_cparams(("parallel", "parallel", "arbitrary")),
    )(qkv, qkv, qkv, c3)


def _diff_kernel(q1_ref, q2_ref, k1_ref, k2_ref, v_ref, lam_ref, g_ref, o_ref,
                 q1t_sc, q2t_sc, vt_sc, m1_sc, l1_sc, a1_sc, m2_sc, l2_sc, a2_sc, *s_sc,
                 tq, tk, s_len, lam_init):
    qi = pl.program_id(2)

    @pl.when(qi == 0)
    def _():
        for ch in range(s_len // tk):
            rows = slice(ch * tk, (ch + 1) * tk)
            vt_sc[:, rows] = _transpose_bf16(v_ref[rows, :])

    q1t_sc[...] = _transpose_bf16(q1_ref[...])
    q2t_sc[...] = _transpose_bf16(q2_ref[...])
    for r in (m1_sc, m2_sc):
        r[...] = jnp.full(r.shape, -jnp.inf, F32)
    for r in (l1_sc, l2_sc, a1_sc, a2_sc):
        r[...] = jnp.zeros(r.shape, F32)

    chains = ((k1_ref, q1t_sc, m1_sc, l1_sc, a1_sc), (k2_ref, q2t_sc, m2_sc, l2_sc, a2_sc))

    def scores(c, chunk):
        start = pl.multiple_of(chunk * tk, tk)
        return jnp.dot(chains[c][0][pl.ds(start, tk), :], chains[c][1][...], preferred_element_type=F32)

    def consume(c, chunk, par, masked):
        s = s_sc[par * 2 + c][...]
        if masked:
            s = jnp.where(_causal_keep(s.shape), s, -jnp.inf)
        start = pl.multiple_of(chunk * tk, tk)
        _softmax_step(s, vt_sc[:, pl.ds(start, tk)], *chains[c][2:])

    for c in range(2):
        s_sc[c][...] = scores(c, 0)

    def body(kc, carry):
        def run(par):
            for c in range(2):
                s_sc[(1 - par) * 2 + c][...] = scores(c, kc + 1)
                consume(c, kc, par, False)
        pl.when(kc % 2 == 0)(functools.partial(run, 0))
        pl.when(kc % 2 == 1)(functools.partial(run, 1))
        return carry

    lax.fori_loop(0, qi, body, 0)

    def diagonal(par):
        for c in range(2):
            consume(c, qi, par, True)
    pl.when(qi % 2 == 0)(functools.partial(diagonal, 0))
    pl.when(qi % 2 == 1)(functools.partial(diagonal, 1))

    lp = lam_ref[...]
    lam = (jnp.exp(jnp.sum(lp[0:1, :] * lp[1:2, :], axis=-1, keepdims=True))
           - jnp.exp(jnp.sum(lp[2:3, :] * lp[3:4, :], axis=-1, keepdims=True)) + lam_init)
    y = (a1_sc[...] / l1_sc[...]).T - lam * (a2_sc[...] / l2_sc[...]).T
    ms = jnp.mean(y * y, axis=-1, keepdims=True)
    y = (y * lax.rsqrt(ms + EPS)) * g_ref[...]
    o_ref[...] = (y * (1.0 - lam_init)).astype(o_ref.dtype)


def _diff_attn(qkv, lam_p, subln_g, b, s, tq, lam_init):
    t = b * s
    nq = s // tq
    qb = 3 * FOX_HEADS
    kb = qb + 2 * DIFF_HEADS
    vb = (kb + 2 * DIFF_HEADS) // 2
    kern = functools.partial(_diff_kernel, tq=tq, tk=tq, s_len=s, lam_init=lam_init)
    return pl.pallas_call(
        kern,
        grid=(b, DIFF_HEADS, nq),
        in_specs=[
            pl.BlockSpec((tq, HEAD_DIM), lambda bi, h, qi: (bi * nq + qi, qb + 2 * h)),
            pl.BlockSpec((tq, HEAD_DIM), lambda bi, h, qi: (bi * nq + qi, qb + 2 * h + 1)),
            pl.BlockSpec((s, HEAD_DIM), lambda bi, h, qi: (bi, kb + 2 * h)),
            pl.BlockSpec((s, HEAD_DIM), lambda bi, h, qi: (bi, kb + 2 * h + 1)),
            pl.BlockSpec((s, DIFF_V_DIM), lambda bi, h, qi: (bi, vb + h)),
            pl.BlockSpec((4, HEAD_DIM), lambda bi, h, qi: (0, 0)),
            pl.BlockSpec((1, DIFF_V_DIM), lambda bi, h, qi: (0, 0)),
        ],
        out_specs=pl.BlockSpec((tq, DIFF_V_DIM), lambda bi, h, qi: (bi * nq + qi, h)),
        out_shape=jax.ShapeDtypeStruct((t, DIFF_HEADS * DIFF_V_DIM), BF16),
        scratch_shapes=[
            pltpu.VMEM((HEAD_DIM, tq), BF16), pltpu.VMEM((HEAD_DIM, tq), BF16),
            pltpu.VMEM((DIFF_V_DIM, s), BF16),
            pltpu.VMEM((1, tq), F32), pltpu.VMEM((1, tq), F32), pltpu.VMEM((DIFF_V_DIM, tq), F32),
            pltpu.VMEM((1, tq), F32), pltpu.VMEM((1, tq), F32), pltpu.VMEM((DIFF_V_DIM, tq), F32),
        ] + [pltpu.VMEM((tq, tq), F32) for _ in range(4)],
        compiler_params=_cparams(("parallel", "parallel", "arbitrary")),
    )(qkv, qkv, qkv, qkv, qkv, lam_p, subln_g)


def _pack_bf16_pairs(xb):
    w = xb.shape[1] // 2
    lo = lax.bitcast_convert_type(xb[:, :w].astype(F32), jnp.uint32)
    hi = lax.bitcast_convert_type(xb[:, w:].astype(F32), jnp.uint32)
    return (lo >> 16) | (hi & jnp.uint32(0xFFFF0000))


def _unpack_bf16_pairs(p):
    lo = lax.bitcast_convert_type(p << 16, F32).astype(BF16)
    hi = lax.bitcast_convert_type(p & jnp.uint32(0xFFFF0000), F32).astype(BF16)
    return lo, hi


def _merge_kernel(yf_ref, yd_ref, g0_ref, g1_ref, x_ref, wbf_ref, wbd_ref, wo_ref, n2_ref,
                  rw_ref, rb_ref, h_ref, u_ref, idx_ref, gw_ref):
    ya = jnp.dot(yf_ref[...], wbf_ref[...], preferred_element_type=F32)
    yb = jnp.dot(yd_ref[...], wbd_ref[...], preferred_element_type=F32)
    merged = g0_ref[...] * ya + g1_ref[...] * yb
    h = x_ref[...] + jnp.dot(merged.astype(BF16), wo_ref[...], preferred_element_type=F32)
    h_ref[...] = h
    ms = jnp.mean(h * h, axis=-1, keepdims=True)
    u = (h * lax.rsqrt(ms + EPS)) * n2_ref[...]
    ub = u.astype(BF16)
    u_ref[...] = _pack_bf16_pairs(ub)
    logits = jnp.dot(ub, rw_ref[...], preferred_element_type=F32) + rb_ref[...]

    lane = lax.broadcasted_iota(jnp.int32, logits.shape, 1).astype(F32)
    work = logits
    vals, idxs = [], []
    for _ in range(TOP_K):
        m = jnp.max(work, axis=-1, keepdims=True)
        sel = jnp.min(jnp.where(work == m, lane, float(LANES)), axis=-1, keepdims=True)
        vals.append(m)
        idxs.append(sel)
        work = jnp.where(lane == sel, -jnp.inf, work)
    es = [jnp.exp(v - vals[0]) for v in vals]
    den = es[0]
    for e in es[1:]:
        den = den + e
    idx_out = jnp.zeros(logits.shape, F32)
    gw_out = jnp.zeros(logits.shape, F32)
    for k in range(TOP_K):
        idx_out = jnp.where(lane == float(k), idxs[k], idx_out)
        gw_out = jnp.where(lane == float(k), es[k] / den, gw_out)
    idx_ref[...] = idx_out.astype(jnp.int32)
    gw_ref[...] = gw_out


def _merge(yf, yd, gate, x2, wbf, wbd, wo, n2, rw, rb, tm):
    t, d = x2.shape
    fw = yf.shape[1]
    dw = yd.shape[1]
    const = dict(pipeline_mode=pl.Buffered(1))
    return pl.pallas_call(
        _merge_kernel,
        grid=(t // tm,),
        in_specs=[
            pl.BlockSpec((tm, fw), lambda i: (i, 0)),
            pl.BlockSpec((tm, dw), lambda i: (i, 0)),
            pl.BlockSpec((tm, d), lambda i: (i, 0)),
            pl.BlockSpec((tm, d), lambda i: (i, 1)),
            pl.BlockSpec((tm, d), lambda i: (i, 0)),
            pl.BlockSpec((fw, d), lambda i: (0, 0), **const),
            pl.BlockSpec((dw, d), lambda i: (0, 0), **const),
            pl.BlockSpec((d, d), lambda i: (0, 0), **const),
            pl.BlockSpec((1, d), lambda i: (0, 0)),
            pl.BlockSpec((d, LANES), lambda i: (0, 0), **const),
            pl.BlockSpec((1, LANES), lambda i: (0, 0)),
        ],
        out_specs=[
            pl.BlockSpec((tm, d), lambda i: (i, 0)),
            pl.BlockSpec((tm, d // 2), lambda i: (i, 0)),
            pl.BlockSpec((tm, LANES), lambda i: (i, 0)),
            pl.BlockSpec((tm, LANES), lambda i: (i, 0)),
        ],
        out_shape=[
            jax.ShapeDtypeStruct((t, d), F32),
            jax.ShapeDtypeStruct((t, d // 2), jnp.uint32),
            jax.ShapeDtypeStruct((t, LANES), jnp.int32),
            jax.ShapeDtypeStruct((t, LANES), F32),
        ],
        compiler_params=_cparams(("parallel",)),
    )(yf, yd, gate, gate, x2, wbf, wbd, wo, n2, rw, rb)


def _dispatch_kernel(cnt_ref, pstart_ref, pend_ref, roa_ref, u_ref, xs_hbm, stage, zeros, sems,
                     *, tb, sub, n_rows):
    s = pl.program_id(0)
    n = pl.num_programs(0)
    slot = s % 2
    n_dma = TOP_K * tb

    groups = n_dma // SUBLANES

    def wait_slot(sl):
        def body(g, carry):
            pltpu.make_async_copy(stage.at[sl, g], xs_hbm.at[pl.ds(0, SUBLANES), :], sems.at[sl]).wait()
            return carry
        lax.fori_loop(0, groups, body, 0)

    @pl.when(s == 0)
    def _():
        zeros[...] = jnp.zeros(zeros.shape, zeros.dtype)

    @pl.when(s >= 2)
    def _():
        wait_slot(slot)

    blk = u_ref[...].reshape(tb // SUBLANES, SUBLANES, u_ref.shape[1])
    for k in range(TOP_K):
        stage[slot, pl.ds(k * (tb // SUBLANES), tb // SUBLANES)] = blk

    def body(g, carry):
        for r in range(SUBLANES):
            q = g * SUBLANES + r
            pltpu.make_async_copy(stage.at[slot, g, pl.ds(r, 1), :],
                                  xs_hbm.at[pl.ds(roa_ref[0, 0, q], 1), :], sems.at[slot]).start()
        return carry
    lax.fori_loop(0, groups, body, 0)

    @pl.when(s == n - 1)
    def _():
        @pl.when(n >= 2)
        def _():
            wait_slot(1 - slot)
        wait_slot(slot)

        def zero_row(r):
            return pltpu.make_async_copy(zeros.at[pl.ds(0, 1), :], xs_hbm.at[pl.ds(r, 1), :], sems.at[2])

        def zero_block(r):
            return pltpu.make_async_copy(zeros, xs_hbm.at[pl.ds(pl.multiple_of(r, sub), sub), :], sems.at[2])

        def fill(first, stop, copy, step):
            cnt = (stop - first) // step

            def start(q, carry):
                copy(first + q * step).start()
                return carry

            def wait(q, carry):
                copy(first).wait()
                return carry
            lax.fori_loop(0, cnt, start, 0)
            lax.fori_loop(0, cnt, wait, 0)

        def per_expert(e, carry):
            first = pstart_ref[e] + cnt_ref[e]
            edge = (first + sub - 1) // sub * sub
            fill(first, edge, zero_row, 1)
            fill(edge, pend_ref[e], zero_block, sub)
            return carry
        lax.fori_loop(0, N_EXPERTS, per_expert, 0)
        fill(pend_ref[N_EXPERTS - 1], n_rows, zero_block, sub)


def _dispatch(counts, pad_start, pad_end, roa3, u_packed, n_rows, tb, sub):
    t, w = u_packed.shape
    kern = functools.partial(_dispatch_kernel, tb=tb, sub=sub, n_rows=n_rows)
    return pl.pallas_call(
        kern,
        grid_spec=pltpu.PrefetchScalarGridSpec(
            num_scalar_prefetch=3,
            grid=(t // tb,),
            in_specs=[
                pl.BlockSpec((1, 1, TOP_K * tb), lambda s, c, ps, pe: (s, 0, 0), memory_space=pltpu.SMEM),
                pl.BlockSpec((tb, w), lambda s, c, ps, pe: (s, 0)),
            ],
            out_specs=pl.BlockSpec(memory_space=pl.ANY),
            scratch_shapes=[
                pltpu.VMEM((2, TOP_K * tb // SUBLANES, SUBLANES, w), jnp.uint32),
                pltpu.VMEM((sub, w), jnp.uint32),
                pltpu.SemaphoreType.DMA((3,)),
            ],
        ),
        out_shape=jax.ShapeDtypeStruct((n_rows, w), jnp.uint32),
        compiler_params=_cparams(("arbitrary",)),
    )(counts, pad_start, pad_end, roa3, u_packed)


def _expert_kernel(te_ref, ts_ref, nch_ref, xs_ref, wg_ref, wu_ref, bg_ref, bu_ref, wd_ref,
                   bd_ref, out_ref, x_sc, act_sc, *, chunks, nf):
    i = pl.program_id(0)
    j = pl.program_id(1)
    nch = nch_ref[i]
    half = x_sc.shape[1] // 2

    @pl.when(jnp.logical_and(j == 0, nch > 0))
    def _():
        lo, hi = _unpack_bf16_pairs(xs_ref[...])
        x_sc[:, :half] = lo
        x_sc[:, half:] = hi

    full = nch == len(chunks)

    def gate_up(rs, wg, wu):
        x = x_sc[rs, :]
        hg = jnp.dot(x, wg, preferred_element_type=F32) + bg_ref[0]
        hu = jnp.dot(x, wu, preferred_element_type=F32) + bu_ref[0]
        hg = jnp.minimum(hg, SWIGLU_LIMIT)
        hl = jnp.clip(hu, -SWIGLU_LIMIT, SWIGLU_LIMIT)
        act = hg * jax.nn.sigmoid(SWIGLU_ALPHA * hg) * (hl + 1.0)
        act_sc[j, rs, :] = act.astype(BF16)

    def down(rs, wd):
        act = jnp.concatenate([act_sc[c, rs, :] for c in range(nf)], axis=1)
        y = jnp.dot(act, wd, preferred_element_type=F32) + bd_ref[0]
        out_ref[rs, :] = _pack_bf16_pairs(y.astype(BF16))

    partial_tile = jnp.logical_and(nch > 0, nch < len(chunks))
    row_slices = [slice(a, b) for a, b in chunks]

    @pl.when(jnp.logical_and(j < nf, full))
    def _():
        gate_up(slice(None), wg_ref[0].astype(BF16), wu_ref[0].astype(BF16))

    @pl.when(jnp.logical_and(j < nf, partial_tile))
    def _():
        wg = wg_ref[0].astype(BF16)
        wu = wu_ref[0].astype(BF16)
        for c, rs in enumerate(row_slices[:-1]):
            pl.when(c < nch)(functools.partial(gate_up, rs, wg, wu))

    @pl.when(jnp.logical_and(j >= nf, full))
    def _():
        down(slice(None), wd_ref[0].astype(BF16))

    @pl.when(jnp.logical_and(j >= nf, partial_tile))
    def _():
        wd = wd_ref[0].astype(BF16)
        for c, rs in enumerate(row_slices[:-1]):
            pl.when(c < nch)(functools.partial(down, rs, wd))

    @pl.when(jnp.logical_and(j >= nf, nch < len(chunks)))
    def _():
        for c, (a, b) in enumerate(chunks):
            @pl.when(c >= nch)
            def _():
                out_ref[a:b, :] = jnp.zeros((b - a, out_ref.shape[1]), out_ref.dtype)


def _experts(tile_e, tile_src, tile_nch, xs, w_gu, b_gu3, w_down, b_down3, tm_e, chunks, tf):
    n_rows = xs.shape[0]
    n_tiles = n_rows // tm_e
    d = 2 * xs.shape[1]
    dff = w_down.shape[1]
    nf = dff // tf
    nn = d // tf
    assert nn == nf
    kern = functools.partial(_expert_kernel, chunks=chunks, nf=nf)

    def c1(i, j, ns):
        return jnp.where(ns[i] > 0, jnp.minimum(j, nf - 1), nf - 1)

    def c2(i, j, ns):
        return jnp.where(ns[i] > 0, jnp.maximum(j - nf, 0), nf - 1)

    return pl.pallas_call(
        kern,
        grid_spec=pltpu.PrefetchScalarGridSpec(
            num_scalar_prefetch=3,
            grid=(n_tiles, 2 * nf),
            in_specs=[
                pl.BlockSpec((tm_e, d // 2), lambda i, j, te, ts, ns: (ts[i], 0)),
                pl.BlockSpec((1, d, tf), lambda i, j, te, ts, ns: (te[i], 0, c1(i, j, ns))),
                pl.BlockSpec((1, d, tf), lambda i, j, te, ts, ns: (te[i], 0, nf + c1(i, j, ns))),
                pl.BlockSpec((1, 1, tf), lambda i, j, te, ts, ns: (te[i], 0, c1(i, j, ns))),
                pl.BlockSpec((1, 1, tf), lambda i, j, te, ts, ns: (te[i], 0, nf + c1(i, j, ns))),
                pl.BlockSpec((1, dff, tf), lambda i, j, te, ts, ns: (te[i], 0, c2(i, j, ns))),
                pl.BlockSpec((1, 1, tf), lambda i, j, te, ts, ns: (te[i], 0, c2(i, j, ns))),
            ],
            out_specs=pl.BlockSpec((tm_e, tf // 2), lambda i, j, te, ts, ns: (i, jnp.maximum(j - nf, 0))),
            scratch_shapes=[
                pltpu.VMEM((tm_e, d), BF16),
                pltpu.VMEM((nf, tm_e, tf), BF16),
            ],
        ),
        out_shape=jax.ShapeDtypeStruct((n_rows, d // 2), jnp.uint32),
        compiler_params=_cparams(("arbitrary", "arbitrary")),
    )(tile_e, tile_src, tile_nch, xs, w_gu, w_gu, b_gu3, b_gu3, w_down, b_down3)


def _combine_kernel(cur_ref, nxt_ref, yr_hbm, h_ref, gw_ref, gf_ref, out_ref, ysel, sems, *, tmc,
                    chunk, final_norm):
    i = pl.program_id(0)
    n = pl.num_programs(0)
    slot = i % 2
    groups = TOP_K * tmc // SUBLANES
    per_k = tmc // SUBLANES

    def issue(idx_ref, sl):
        def body(g, carry):
            for r in range(SUBLANES):
                pltpu.make_async_copy(yr_hbm.at[pl.ds(idx_ref[0, 0, g * SUBLANES + r], 1), :],
                                      ysel.at[sl, g, pl.ds(r, 1), :], sems.at[sl]).start()
            return carry
        lax.fori_loop(0, groups, body, 0)

    @pl.when(i == 0)
    def _():
        issue(cur_ref, 0)

    @pl.when(i + 1 < n)
    def _():
        issue(nxt_ref, 1 - slot)

    def wait_group(g, carry):
        pltpu.make_async_copy(yr_hbm.at[pl.ds(0, SUBLANES), :], ysel.at[slot, g], sems.at[slot]).wait()
        return carry
    lax.fori_loop(0, groups, wait_group, 0)

    gw = gw_ref[...]
    y = h_ref[...]
    half = chunk // 2
    for k in range(TOP_K):
        rows_k = ysel[slot, k * per_k:(k + 1) * per_k].reshape(tmc, ysel.shape[-1])
        lo, hi = _unpack_bf16_pairs(rows_k)
        pieces = []
        for c in range(lo.shape[1] // half):
            pieces += [lo[:, c * half:(c + 1) * half], hi[:, c * half:(c + 1) * half]]
        y = y + gw[:, k:k + 1] * jnp.concatenate(pieces, axis=1).astype(F32)
    if final_norm:
        ms = jnp.mean(y * y, axis=-1, keepdims=True)
        y = (y * lax.rsqrt(ms + EPS)) * gf_ref[...]
    out_ref[...] = y


def _combine(roa3, y_rows, h, gw, gf, tmc, chunk, final_norm):
    t, d = h.shape
    n = t // tmc
    kern = functools.partial(_combine_kernel, tmc=tmc, chunk=chunk, final_norm=final_norm)
    return pl.pallas_call(
        kern,
        grid=(n,),
        in_specs=[
            pl.BlockSpec((1, 1, TOP_K * tmc), lambda i: (i, 0, 0), memory_space=pltpu.SMEM),
            pl.BlockSpec((1, 1, TOP_K * tmc), lambda i: (jnp.minimum(i + 1, n - 1), 0, 0),
                         memory_space=pltpu.SMEM),
            pl.BlockSpec(memory_space=pl.ANY),
            pl.BlockSpec((tmc, d), lambda i: (i, 0)),
            pl.BlockSpec((tmc, LANES), lambda i: (i, 0)),
            pl.BlockSpec((1, d), lambda i: (0, 0)),
        ],
        out_specs=pl.BlockSpec((tmc, d), lambda i: (i, 0)),
        out_shape=jax.ShapeDtypeStruct((t, d), F32),
        scratch_shapes=[pltpu.VMEM((2, TOP_K * tmc // SUBLANES, SUBLANES, d // 2), jnp.uint32),
                        pltpu.SemaphoreType.DMA((2,))],
        compiler_params=_cparams(("arbitrary",)),
    )(roa3, roa3, y_rows, h, gw, gf)


def _routing(top_idx, tm_e, chunks):
    t = top_idx.shape[0]
    a = t * TOP_K
    flat_e = top_idx.reshape(a)
    onehot = (flat_e[:, None] == jnp.arange(N_EXPERTS, dtype=jnp.int32)[None, :]).astype(jnp.int32)
    csum = jnp.cumsum(onehot, axis=0)
    counts = csum[-1]
    rank = jnp.sum(onehot * csum, axis=1) - 1
    padded = (counts + tm_e - 1) // tm_e * tm_e
    pad_end = jnp.cumsum(padded)
    pad_start = pad_end - padded
    row_of_assign = (pad_start[flat_e] + rank).astype(jnp.int32)
    n_tiles = -(-a // tm_e) + N_EXPERTS
    tile_start = jnp.arange(n_tiles, dtype=jnp.int32) * tm_e
    n_used = pad_end[-1] // tm_e
    tile_e_raw = jnp.minimum(jnp.searchsorted(pad_end, tile_start, side='right'), N_EXPERTS - 1)
    used = tile_start < pad_end[-1]
    last = jnp.maximum(n_used - 1, 0)
    tile_src = jnp.where(used, jnp.arange(n_tiles, dtype=jnp.int32), last).astype(jnp.int32)
    tile_e = tile_e_raw[tile_src].astype(jnp.int32)
    valid_rows = jnp.clip(counts[tile_e_raw] - (tile_start - pad_start[tile_e_raw]), 0, tm_e)
    starts = jnp.asarray([a for a, _ in chunks], jnp.int32)
    tile_nch = jnp.sum(starts[None, :] < jnp.where(used, valid_rows, 0)[:, None], axis=1).astype(jnp.int32)
    bounds = (counts.astype(jnp.int32), pad_start.astype(jnp.int32), pad_end.astype(jnp.int32))
    return bounds, row_of_assign, tile_e, tile_src, tile_nch, n_tiles * tm_e


def kernel(x, positions, norm1_g, w_in, b_fgate, b_gate, lam_q1, lam_k1, lam_q2, lam_k2, subln_g,
           w_branch_fox, w_branch_diff, w_out, norm2_g, router_w, router_b, w_gu, b_gu, w_down,
           b_down, normf_g):
    b, s, d = x.shape
    t = b * s
    depth = norm1_g.shape[0]
    fox_w = FOX_HEADS * HEAD_DIM
    tm_a = _pick(t, 1024)
    tq = _pick(s, 512)
    tm_d = _pick(t, 256)
    tm_e, chunk, zsub, tf = 1152, 512, 128, 512
    chunks = tuple((a, min(a + chunk, tm_e)) for a in range(0, tm_e, chunk))
    tmc = _pick(t, 256)

    half = ROPE_DIM // 2
    inv_freq = ROPE_THETA ** (-jnp.arange(0, ROPE_DIM, 2, dtype=F32) / ROPE_DIM)
    zeros = jnp.zeros((LANES - ROPE_DIM,), F32)
    rope_tab = jnp.zeros((8, LANES), F32)
    rope_tab = rope_tab.at[0].set(jnp.concatenate([inv_freq, inv_freq, zeros]))
    rope_tab = rope_tab.at[1].set(jnp.concatenate([-jnp.ones((half,), F32), jnp.zeros((half,), F32), zeros]))
    rope_tab = rope_tab.at[2].set(jnp.concatenate([jnp.zeros((half,), F32), jnp.ones((half,), F32), zeros]))
    pos = positions.astype(F32).reshape(t, 1)

    h = x.reshape(t, d)
    for l in range(depth):
        wi = w_in[l]
        w_left = wi[:, :3 * fox_w].astype(BF16)
        w_right = wi[:, 3 * fox_w + FOX_HEADS:].astype(BF16)
        w_f = jnp.pad(wi[:, 3 * fox_w:3 * fox_w + FOX_HEADS], ((0, 0), (0, LANES - FOX_HEADS))).astype(BF16)
        qkv, gate, flog = _inproj(h, pos, norm1_g[l][None, :], w_left, w_right, w_f, rope_tab,
                                  b_gate[l][None, :], tm_a)

        fl = flog[:, :FOX_HEADS].reshape(b, s, FOX_HEADS).transpose(0, 2, 1).reshape(b * FOX_HEADS, s)
        bf = jnp.tile(b_fgate[l].astype(F32), b).reshape(b * FOX_HEADS, 1)
        c3 = _fgate(fl, bf).reshape(b * FOX_HEADS, 1, s)

        y_fox = _fox_attn(qkv, c3, b, s, tq)
        lam_init = 0.8 - 0.6 * math.exp(-0.3 * l)
        lam_p = jnp.stack([lam_q1[l], lam_k1[l], lam_q2[l], lam_k2[l]]).astype(F32)
        y_diff = _diff_attn(qkv, lam_p, subln_g[l][None, :].astype(F32), b, s, tq, lam_init)

        rw = jnp.pad(router_w[l], ((0, 0), (0, LANES - N_EXPERTS))).astype(BF16)
        rb = jnp.concatenate([router_b[l].astype(F32), jnp.full((LANES - N_EXPERTS,), NEG_BIG, F32)])[None, :]
        h, u2, idx128, gw128 = _merge(y_fox, y_diff, gate, h, w_branch_fox[l].astype(BF16),
                                      w_branch_diff[l].astype(BF16), w_out[l].astype(BF16),
                                      norm2_g[l][None, :], rw, rb, tm_d)

        bounds, roa, tile_e, tile_src, tile_nch, n_rows = _routing(idx128[:, :TOP_K], tm_e, chunks)
        roa3 = roa.reshape(t // tmc, tmc, TOP_K).transpose(0, 2, 1).reshape(t // tmc, 1, TOP_K * tmc)
        xs = _dispatch(*bounds, roa3, u2, n_rows, tmc, zsub)
        y_rows = _experts(tile_e, tile_src, tile_nch, xs, w_gu[l], b_gu[l][:, None, :],
                          w_down[l], b_down[l][:, None, :], tm_e, chunks, tf)
        h = _combine(roa3, y_rows, h, gw128, normf_g[None, :], tmc, tf, final_norm=(l == depth - 1))
    return h.reshape(b, s, d)
```

```python
import functools
import math

import jax
import jax.numpy as jnp
from jax import lax
from jax.experimental import pallas as pl
from jax.experimental.pallas import tpu as pltpu

HEAD_DIM = 128
FOX_HEADS = 8
DIFF_HEADS = 4
DIFF_V_DIM = 2 * HEAD_DIM
ROPE_THETA = 500000.0
ROPE_DIM = HEAD_DIM // 4
N_EXPERTS = 32
TOP_K = 4
SWIGLU_ALPHA = 1.702
SWIGLU_LIMIT = 7.0
EPS = 1e-5
LANES = 128

F32 = jnp.float32
BF16 = jnp.bfloat16
NEG_BIG = -1e30
LOG2E = math.log2(math.e)
SUBLANES = 8
VMEM_LIMIT = 56 * 1024 * 1024


def _cparams(sem, vmem=VMEM_LIMIT):
    return pltpu.CompilerParams(dimension_semantics=sem, vmem_limit_bytes=vmem)


def _pick(n, pref):
    t = min(n, pref)
    while n % t:
        t //= 2
    return t


QKV_TILE = 1024
QKV_LEFT_TILES = 3
QKV_ROTARY_TILES = (3, 4)
QKV_SCALED_TILES = (0, 3)


def _qkv_kernel(x_ref, g1_ref, wa_ref, wb_ref, wf_ref, pos_ref, tab_ref,
                qkv_ref, u_ref, flog_ref, cos_sc, sa_sc, sb_sc, *, scale):
    j = pl.program_id(1)

    @pl.when(j == 0)
    def _():
        xf = x_ref[...]
        ms = jnp.mean(xf * xf, axis=-1, keepdims=True)
        u = (xf * lax.rsqrt(ms + EPS)) * g1_ref[...]
        ub = u.astype(BF16)
        u_ref[...] = ub
        flog_ref[...] = jnp.dot(ub, wf_ref[...], preferred_element_type=F32)
        ang = pos_ref[...] * tab_ref[0:1, :]
        sn = jnp.sin(ang)
        cos_sc[...] = jnp.cos(ang)
        sa_sc[...] = sn * tab_ref[1:2, :]
        sb_sc[...] = sn * tab_ref[2:3, :]

    any_of = lambda tiles: functools.reduce(jnp.logical_or, [j == q for q in tiles])
    mult = jnp.where(any_of(QKV_SCALED_TILES), scale, 1.0).astype(F32)
    rotary = any_of(QKV_ROTARY_TILES)
    left = j < QKV_LEFT_TILES

    def plain(w_ref):
        acc = jnp.dot(u_ref[...], w_ref[...], preferred_element_type=F32)
        qkv_ref[...] = (acc * mult).astype(BF16)

    pl.when(left)(functools.partial(plain, wa_ref))
    pl.when(jnp.logical_not(jnp.logical_or(left, rotary)))(functools.partial(plain, wb_ref))

    @pl.when(rotary)
    def _():
        acc = jnp.dot(u_ref[...], wb_ref[...], preferred_element_type=F32)
        ca = cos_sc[...] * mult
        cb = sa_sc[...] * mult
        cc = sb_sc[...] * mult
        for c in range(acc.shape[1] // HEAD_DIM):
            t = acc[:, c * HEAD_DIM:(c + 1) * HEAD_DIM]
            r = (t * ca + pltpu.roll(t, HEAD_DIM - ROPE_DIM // 2, 1) * cb
                 + pltpu.roll(t, ROPE_DIM // 2, 1) * cc)
            qkv_ref[:, c * HEAD_DIM:(c + 1) * HEAD_DIM] = r.astype(BF16)


def _gates_kernel(u_ref, w_ref, bg_ref, gate_ref):
    acc = jnp.dot(u_ref[...], w_ref[...], preferred_element_type=F32)
    gate_ref[...] = jax.nn.sigmoid(acc + bg_ref[...])


def _inproj(x2, pos, g1, w_left, w_right, w_f, rope_tab, b_gate, tm):
    t, d = x2.shape
    tn = QKV_TILE
    n_qkv = 6
    n_left = QKV_LEFT_TILES
    n_gate = w_right.shape[1] // tn - (n_qkv - n_left)
    kern = functools.partial(_qkv_kernel, scale=LOG2E / math.sqrt(HEAD_DIM))
    qkv, u, flog = pl.pallas_call(
        kern,
        grid=(t // tm, n_qkv),
        in_specs=[
            pl.BlockSpec((tm, d), lambda i, j: (i, 0)),
            pl.BlockSpec((1, d), lambda i, j: (0, 0)),
            pl.BlockSpec((d, tn), lambda i, j: (0, jnp.minimum(j, n_left - 1))),
            pl.BlockSpec((d, tn), lambda i, j: (0, jnp.maximum(j - n_left, 0))),
            pl.BlockSpec((d, LANES), lambda i, j: (0, 0)),
            pl.BlockSpec((tm, 1), lambda i, j: (i, 0)),
            pl.BlockSpec((8, LANES), lambda i, j: (0, 0)),
        ],
        out_specs=[
            pl.BlockSpec((tm, tn), lambda i, j: (i, j)),
            pl.BlockSpec((tm, d), lambda i, j: (i, 0)),
            pl.BlockSpec((tm, LANES), lambda i, j: (i, 0)),
        ],
        out_shape=[
            jax.ShapeDtypeStruct((t, n_qkv * tn), BF16),
            jax.ShapeDtypeStruct((t, d), BF16),
            jax.ShapeDtypeStruct((t, LANES), F32),
        ],
        scratch_shapes=[
            pltpu.VMEM((tm, LANES), F32),
            pltpu.VMEM((tm, LANES), F32),
            pltpu.VMEM((tm, LANES), F32),
        ],
        compiler_params=_cparams(("parallel", "arbitrary")),
    )(x2, g1, w_left, w_right, w_f, pos, rope_tab)
    gate = pl.pallas_call(
        _gates_kernel,
        grid=(t // tm, n_gate),
        in_specs=[
            pl.BlockSpec((tm, d), lambda i, j: (i, 0)),
            pl.BlockSpec((d, tn), lambda i, j: (0, n_qkv - n_left + j)),
            pl.BlockSpec((1, tn), lambda i, j: (0, j)),
        ],
        out_specs=pl.BlockSpec((tm, tn), lambda i, j: (i, j)),
        out_shape=jax.ShapeDtypeStruct((t, n_gate * tn), F32),
        compiler_params=_cparams(("parallel", "arbitrary")),
    )(u, w_right, b_gate)
    return qkv, gate, flog


def _fgate_kernel(fl_ref, b_ref, c_ref):
    z = fl_ref[...] + b_ref[...]
    x = jnp.minimum(z, 0.0) - jnp.log(1.0 + jnp.exp(-jnp.abs(z)))
    n = x.shape[1]
    lane = lax.broadcasted_iota(jnp.int32, x.shape, 1)
    sh = 1
    while sh < n:
        x = x + jnp.where(lane >= sh, pltpu.roll(x, sh, 1), 0.0)
        sh *= 2
    c_ref[...] = x


def _fgate(fl, b):
    r, s = fl.shape
    return pl.pallas_call(
        _fgate_kernel,
        grid=(1,),
        in_specs=[pl.BlockSpec((r, s), lambda i: (0, 0)), pl.BlockSpec((r, 1), lambda i: (0, 0))],
        out_specs=pl.BlockSpec((r, s), lambda i: (0, 0)),
        out_shape=jax.ShapeDtypeStruct((r, s), F32),
        compiler_params=_cparams(("arbitrary",)),
    )(fl, b)


def _transpose_bf16(x):
    return x.astype(F32).T.astype(BF16)


def _softmax_step(s, vt, m_sc, l_sc, acc_sc):
    m_old = m_sc[...]
    m_new = jnp.maximum(m_old, jnp.max(s, axis=0, keepdims=True))
    alpha = jnp.exp2(m_old - m_new)
    p = jnp.exp2(s - m_new)
    l_sc[...] = alpha * l_sc[...] + jnp.sum(p, axis=0, keepdims=True)
    acc_sc[...] = alpha * acc_sc[...] + jnp.dot(vt, p.astype(BF16), preferred_element_type=F32)
    m_sc[...] = m_new


def _causal_keep(shape):
    kv = lax.broadcasted_iota(jnp.int32, shape, 0)
    qq = lax.broadcasted_iota(jnp.int32, shape, 1)
    return kv <= qq


FOX_HEADS_PER_STEP = 2


def _fox_kernel(q_ref, k_ref, v_ref, c_ref, o_ref, *scratch, tq, tk, s_len):
    hp = FOX_HEADS_PER_STEP
    qt_sc, kaug_sc, vt_sc, m_sc, l_sc, acc_sc = (scratch[n * hp:(n + 1) * hp] for n in range(6))
    s_sc = scratch[6 * hp:]
    qi = pl.program_id(2)
    head_cols = [slice(h * HEAD_DIM, (h + 1) * HEAD_DIM) for h in range(hp)]

    @pl.when(qi == 0)
    def _():
        sub = lax.broadcasted_iota(jnp.int32, (HEAD_DIM, tk), 0)
        for h in range(hp):
            for ch in range(s_len // tk):
                rows = slice(ch * tk, (ch + 1) * tk)
                vt_sc[h][:, rows] = _transpose_bf16(v_ref[rows, head_cols[h]])
                c = c_ref[h, :, rows] * (-LOG2E)
                hi = c.astype(BF16).astype(F32)
                mid = (c - hi).astype(BF16).astype(F32)
                lo = (c - hi - mid).astype(BF16).astype(F32)
                tab = jnp.where(sub == 0, hi, jnp.where(sub == 1, mid, jnp.where(sub == 2, lo, 0.0)))
                kaug_sc[h][rows, HEAD_DIM:] = tab.T.astype(BF16)
                kaug_sc[h][rows, :HEAD_DIM] = k_ref[rows, head_cols[h]]

    sub_q = lax.broadcasted_iota(jnp.int32, (HEAD_DIM, tq), 0)
    for h in range(hp):
        qt_sc[h][:HEAD_DIM, :] = _transpose_bf16(q_ref[:, head_cols[h]])
        qt_sc[h][HEAD_DIM:, :] = jnp.where(sub_q < 3, 1.0, 0.0).astype(BF16)
        m_sc[h][...] = jnp.full(m_sc[h].shape, -jnp.inf, F32)
        l_sc[h][...] = jnp.zeros(l_sc[h].shape, F32)
        acc_sc[h][...] = jnp.zeros(acc_sc[h].shape, F32)

    def scores(h, chunk):
        start = pl.multiple_of(chunk * tk, tk)
        return jnp.dot(kaug_sc[h][pl.ds(start, tk), :], qt_sc[h][...], preferred_element_type=F32)

    def consume(h, chunk, par, masked):
        s = s_sc[par * hp + h][...]
        if masked:
            s = jnp.where(_causal_keep(s.shape), s, -jnp.inf)
        start = pl.multiple_of(chunk * tk, tk)
        _softmax_step(s, vt_sc[h][:, pl.ds(start, tk)], m_sc[h], l_sc[h], acc_sc[h])

    for h in range(hp):
        s_sc[h][...] = scores(h, 0)

    def body(kc, carry):
        def run(par):
            for h in range(hp):
                s_sc[(1 - par) * hp + h][...] = scores(h, kc + 1)
                consume(h, kc, par, False)
        pl.when(kc % 2 == 0)(functools.partial(run, 0))
        pl.when(kc % 2 == 1)(functools.partial(run, 1))
        return carry

    lax.fori_loop(0, qi, body, 0)

    def diagonal(par):
        for h in range(hp):
            consume(h, qi, par, True)
    pl.when(qi % 2 == 0)(functools.partial(diagonal, 0))
    pl.when(qi % 2 == 1)(functools.partial(diagonal, 1))
    for h in range(hp):
        o_ref[:, head_cols[h]] = (acc_sc[h][...] / l_sc[h][...]).T.astype(o_ref.dtype)


def _fox_attn(qkv, c3, b, s, tq):
    t = b * s
    nq = s // tq
    hp = FOX_HEADS_PER_STEP
    groups = FOX_HEADS // hp
    width = hp * HEAD_DIM
    kern = functools.partial(_fox_kernel, tq=tq, tk=tq, s_len=s)

    def per_head(shape, dtype):
        return [pltpu.VMEM(shape, dtype) for _ in range(hp)]

    return pl.pallas_call(
        kern,
        grid=(b, groups, nq),
        in_specs=[
            pl.BlockSpec((tq, width), lambda bi, g, qi: (bi * nq + qi, g)),
            pl.BlockSpec((s, width), lambda bi, g, qi: (bi, groups + g)),
            pl.BlockSpec((s, width), lambda bi, g, qi: (bi, 2 * groups + g)),
            pl.BlockSpec((hp, 1, s), lambda bi, g, qi: (bi * groups + g, 0, 0)),
        ],
        out_specs=pl.BlockSpec((tq, width), lambda bi, g, qi: (bi * nq + qi, g)),
        out_shape=jax.ShapeDtypeStruct((t, FOX_HEADS * HEAD_DIM), BF16),
        scratch_shapes=(per_head((2 * HEAD_DIM, tq), BF16) + per_head((s, 2 * HEAD_DIM), BF16)
                        + per_head((HEAD_DIM, s), BF16) + per_head((1, tq), F32)
                        + per_head((1, tq), F32) + per_head((HEAD_DIM, tq), F32)
                        + per_head((tq, tq), F32) + per_head((tq, tq), F32)),
        compiler_params=_cparams(("parallel", "parallel", "arbitrary")),
    )(qkv, qkv, qkv, c3)


def _diff_kernel(q1_ref, q2_ref, k1_ref, k2_ref, v_ref, lam_ref, g_ref, o_ref,
                 q1t_sc, q2t_sc, vt_sc, m1_sc, l1_sc, a1_sc, m2_sc, l2_sc, a2_sc, *s_sc,
                 tq, tk, s_len, lam_init):
    qi = pl.program_id(2)

    @pl.when(qi == 0)
    def _():
        for ch in range(s_len // tk):
            rows = slice(ch * tk, (ch + 1) * tk)
            vt_sc[:, rows] = _transpose_bf16(v_ref[rows, :])

    q1t_sc[...] = _transpose_bf16(q1_ref[...])
    q2t_sc[...] = _transpose_bf16(q2_ref[...])
    for r in (m1_sc, m2_sc):
        r[...] = jnp.full(r.shape, -jnp.inf, F32)
    for r in (l1_sc, l2_sc, a1_sc, a2_sc):
        r[...] = jnp.zeros(r.shape, F32)

    chains = ((k1_ref, q1t_sc, m1_sc, l1_sc, a1_sc), (k2_ref, q2t_sc, m2_sc, l2_sc, a2_sc))

    def scores(c, chunk):
        start = pl.multiple_of(chunk * tk, tk)
        return jnp.dot(chains[c][0][pl.ds(start, tk), :], chains[c][1][...], preferred_element_type=F32)

    def consume(c, chunk, par, masked):
        s = s_sc[par * 2 + c][...]
        if masked:
            s = jnp.where(_causal_keep(s.shape), s, -jnp.inf)
        start = pl.multiple_of(chunk * tk, tk)
        _softmax_step(s, vt_sc[:, pl.ds(start, tk)], *chains[c][2:])

    for c in range(2):
        s_sc[c][...] = scores(c, 0)

    def body(kc, carry):
        def run(par):
            for c in range(2):
                s_sc[(1 - par) * 2 + c][...] = scores(c, kc + 1)
                consume(c, kc, par, False)
        pl.when(kc % 2 == 0)(functools.partial(run, 0))
        pl.when(kc % 2 == 1)(functools.partial(run, 1))
        return carry

    lax.fori_loop(0, qi, body, 0)

    def diagonal(par):
        for c in range(2):
            consume(c, qi, par, True)
    pl.when(qi % 2 == 0)(functools.partial(diagonal, 0))
    pl.when(qi % 2 == 1)(functools.partial(diagonal, 1))

    lp = lam_ref[...]
    lam = (jnp.exp(jnp.sum(lp[0:1, :] * lp[1:2, :], axis=-1, keepdims=True))
           - jnp.exp(jnp.sum(lp[2:3, :] * lp[3:4, :], axis=-1, keepdims=True)) + lam_init)
    y = (a1_sc[...] / l1_sc[...]).T - lam * (a2_sc[...] / l2_sc[...]).T
    ms = jnp.mean(y * y, axis=-1, keepdims=True)
    y = (y * lax.rsqrt(ms + EPS)) * g_ref[...]
    o_ref[...] = (y * (1.0 - lam_init)).astype(o_ref.dtype)


def _diff_attn(qkv, lam_p, subln_g, b, s, tq, lam_init):
    t = b * s
    nq = s // tq
    qb = 3 * FOX_HEADS
    kb = qb + 2 * DIFF_HEADS
    vb = (kb + 2 * DIFF_HEADS) // 2
    kern = functools.partial(_diff_kernel, tq=tq, tk=tq, s_len=s, lam_init=lam_init)
    return pl.pallas_call(
        kern,
        grid=(b, DIFF_HEADS, nq),
        in_specs=[
            pl.BlockSpec((tq, HEAD_DIM), lambda bi, h, qi: (bi * nq + qi, qb + 2 * h)),
            pl.BlockSpec((tq, HEAD_DIM), lambda bi, h, qi: (bi * nq + qi, qb + 2 * h + 1)),
            pl.BlockSpec((s, HEAD_DIM), lambda bi, h, qi: (bi, kb + 2 * h)),
            pl.BlockSpec((s, HEAD_DIM), lambda bi, h, qi: (bi, kb + 2 * h + 1)),
            pl.BlockSpec((s, DIFF_V_DIM), lambda bi, h, qi: (bi, vb + h)),
            pl.BlockSpec((4, HEAD_DIM), lambda bi, h, qi: (0, 0)),
            pl.BlockSpec((1, DIFF_V_DIM), lambda bi, h, qi: (0, 0)),
        ],
        out_specs=pl.BlockSpec((tq, DIFF_V_DIM), lambda bi, h, qi: (bi * nq + qi, h)),
        out_shape=jax.ShapeDtypeStruct((t, DIFF_HEADS * DIFF_V_DIM), BF16),
        scratch_shapes=[
            pltpu.VMEM((HEAD_DIM, tq), BF16), pltpu.VMEM((HEAD_DIM, tq), BF16),
            pltpu.VMEM((DIFF_V_DIM, s), BF16),
            pltpu.VMEM((1, tq), F32), pltpu.VMEM((1, tq), F32), pltpu.VMEM((DIFF_V_DIM, tq), F32),
            pltpu.VMEM((1, tq), F32), pltpu.VMEM((1, tq), F32), pltpu.VMEM((DIFF_V_DIM, tq), F32),
        ] + [pltpu.VMEM((tq, tq), F32) for _ in range(4)],
        compiler_params=_cparams(("parallel", "parallel", "arbitrary")),
    )(qkv, qkv, qkv, qkv, qkv, lam_p, subln_g)


def _pack_bf16_pairs(xb):
    w = xb.shape[1] // 2
    lo = lax.bitcast_convert_type(xb[:, :w].astype(F32), jnp.uint32)
    hi = lax.bitcast_convert_type(xb[:, w:].astype(F32), jnp.uint32)
    return (lo >> 16) | (hi & jnp.uint32(0xFFFF0000))


def _unpack_bf16_pairs(p):
    lo = lax.bitcast_convert_type(p << 16, F32).astype(BF16)
    hi = lax.bitcast_convert_type(p & jnp.uint32(0xFFFF0000), F32).astype(BF16)
    return lo, hi


def _merge_kernel(yf_ref, yd_ref, g0_ref, g1_ref, x_ref, wbf_ref, wbd_ref, wo_ref, n2_ref,
                  rw_ref, rb_ref, h_ref, u_ref, idx_ref, gw_ref):
    ya = jnp.dot(yf_ref[...], wbf_ref[...], preferred_element_type=F32)
    yb = jnp.dot(yd_ref[...], wbd_ref[...], preferred_element_type=F32)
    merged = g0_ref[...] * ya + g1_ref[...] * yb
    h = x_ref[...] + jnp.dot(merged.astype(BF16), wo_ref[...], preferred_element_type=F32)
    h_ref[...] = h
    ms = jnp.mean(h * h, axis=-1, keepdims=True)
    u = (h * lax.rsqrt(ms + EPS)) * n2_ref[...]
    ub = u.astype(BF16)
    u_ref[...] = _pack_bf16_pairs(ub)
    logits = jnp.dot(ub, rw_ref[...], preferred_element_type=F32) + rb_ref[...]

    lane = lax.broadcasted_iota(jnp.int32, logits.shape, 1).astype(F32)
    work = logits
    vals, idxs = [], []
    for _ in range(TOP_K):
        m = jnp.max(work, axis=-1, keepdims=True)
        sel = jnp.min(jnp.where(work == m, lane, float(LANES)), axis=-1, keepdims=True)
        vals.append(m)
        idxs.append(sel)
        work = jnp.where(lane == sel, -jnp.inf, work)
    es = [jnp.exp(v - vals[0]) for v in vals]
    den = es[0]
    for e in es[1:]:
        den = den + e
    idx_out = jnp.zeros(logits.shape, F32)
    gw_out = jnp.zeros(logits.shape, F32)
    for k in range(TOP_K):
        idx_out = jnp.where(lane == float(k), idxs[k], idx_out)
        gw_out = jnp.where(lane == float(k), es[k] / den, gw_out)
    idx_ref[...] = idx_out.astype(jnp.int32)
    gw_ref[...] = gw_out


def _merge(yf, yd, gate, x2, wbf, wbd, wo, n2, rw, rb, tm):
    t, d = x2.shape
    fw = yf.shape[1]
    dw = yd.shape[1]
    const = dict(pipeline_mode=pl.Buffered(1))
    return pl.pallas_call(
        _merge_kernel,
        grid=(t // tm,),
        in_specs=[
            pl.BlockSpec((tm, fw), lambda i: (i, 0)),
            pl.BlockSpec((tm, dw), lambda i: (i, 0)),
            pl.BlockSpec((tm, d), lambda i: (i, 0)),
            pl.BlockSpec((tm, d), lambda i: (i, 1)),
            pl.BlockSpec((tm, d), lambda i: (i, 0)),
            pl.BlockSpec((fw, d), lambda i: (0, 0), **const),
            pl.BlockSpec((dw, d), lambda i: (0, 0), **const),
            pl.BlockSpec((d, d), lambda i: (0, 0), **const),
            pl.BlockSpec((1, d), lambda i: (0, 0)),
            pl.BlockSpec((d, LANES), lambda i: (0, 0), **const),
            pl.BlockSpec((1, LANES), lambda i: (0, 0)),
        ],
        out_specs=[
            pl.BlockSpec((tm, d), lambda i: (i, 0)),
            pl.BlockSpec((tm, d // 2), lambda i: (i, 0)),
            pl.BlockSpec((tm, LANES), lambda i: (i, 0)),
            pl.BlockSpec((tm, LANES), lambda i: (i, 0)),
        ],
        out_shape=[
            jax.ShapeDtypeStruct((t, d), F32),
            jax.ShapeDtypeStruct((t, d // 2), jnp.uint32),
            jax.ShapeDtypeStruct((t, LANES), jnp.int32),
            jax.ShapeDtypeStruct((t, LANES), F32),
        ],
        compiler_params=_cparams(("parallel",)),
    )(yf, yd, gate, gate, x2, wbf, wbd, wo, n2, rw, rb)


def _dispatch_kernel(cnt_ref, pstart_ref, pend_ref, roa_ref, u_ref, xs_hbm, stage, zeros, sems,
                     *, tb, sub, n_rows):
    s = pl.program_id(0)
    n = pl.num_programs(0)
    slot = s % 2
    n_dma = TOP_K * tb

    groups = n_dma // SUBLANES

    def wait_slot(sl):
        def body(g, carry):
            pltpu.make_async_copy(stage.at[sl, g], xs_hbm.at[pl.ds(0, SUBLANES), :], sems.at[sl]).wait()
            return carry
        lax.fori_loop(0, groups, body, 0)

    def zero_row(r):
        return pltpu.make_async_copy(zeros.at[pl.ds(0, 1), :], xs_hbm.at[pl.ds(r, 1), :], sems.at[2])

    def zero_block(r):
        return pltpu.make_async_copy(zeros, xs_hbm.at[pl.ds(pl.multiple_of(r, sub), sub), :], sems.at[2])

    def zero_fill(start):
        def fill(first, stop, copy, step):
            def body(q, carry):
                if start:
                    copy(first + q * step).start()
                else:
                    copy(first).wait()
                return carry
            lax.fori_loop(0, (stop - first) // step, body, 0)

        def per_expert(e, carry):
            first = pstart_ref[e] + cnt_ref[e]
            edge = (first + sub - 1) // sub * sub
            fill(first, edge, zero_row, 1)
            fill(edge, pend_ref[e], zero_block, sub)
            return carry
        lax.fori_loop(0, N_EXPERTS, per_expert, 0)
        fill(pend_ref[N_EXPERTS - 1], n_rows, zero_block, sub)

    @pl.when(s == 0)
    def _():
        zeros[...] = jnp.zeros(zeros.shape, zeros.dtype)

    @pl.when(s == jnp.minimum(1, n - 1))
    def _():
        zero_fill(start=True)

    @pl.when(s >= 2)
    def _():
        wait_slot(slot)

    blk = u_ref[...].reshape(tb // SUBLANES, SUBLANES, u_ref.shape[1])
    for k in range(TOP_K):
        stage[slot, pl.ds(k * (tb // SUBLANES), tb // SUBLANES)] = blk

    def body(g, carry):
        for r in range(SUBLANES):
            q = g * SUBLANES + r
            pltpu.make_async_copy(stage.at[slot, g, pl.ds(r, 1), :],
                                  xs_hbm.at[pl.ds(roa_ref[0, 0, q], 1), :], sems.at[slot]).start()
        return carry
    lax.fori_loop(0, groups, body, 0)

    @pl.when(s == n - 1)
    def _():
        @pl.when(n >= 2)
        def _():
            wait_slot(1 - slot)
        wait_slot(slot)
        zero_fill(start=False)


def _dispatch(counts, pad_start, pad_end, roa3, u_packed, n_rows, tb, sub):
    t, w = u_packed.shape
    kern = functools.partial(_dispatch_kernel, tb=tb, sub=sub, n_rows=n_rows)
    return pl.pallas_call(
        kern,
        grid_spec=pltpu.PrefetchScalarGridSpec(
            num_scalar_prefetch=3,
            grid=(t // tb,),
            in_specs=[
                pl.BlockSpec((1, 1, TOP_K * tb), lambda s, c, ps, pe: (s, 0, 0), memory_space=pltpu.SMEM),
                pl.BlockSpec((tb, w), lambda s, c, ps, pe: (s, 0)),
            ],
            out_specs=pl.BlockSpec(memory_space=pl.ANY),
            scratch_shapes=[
                pltpu.VMEM((2, TOP_K * tb // SUBLANES, SUBLANES, w), jnp.uint32),
                pltpu.VMEM((sub, w), jnp.uint32),
                pltpu.SemaphoreType.DMA((3,)),
            ],
        ),
        out_shape=jax.ShapeDtypeStruct((n_rows, w), jnp.uint32),
        compiler_params=_cparams(("arbitrary",)),
    )(counts, pad_start, pad_end, roa3, u_packed)


def _expert_kernel(te_ref, ts_ref, nch_ref, xs_ref, wg_ref, wu_ref, bg_ref, bu_ref, wd_ref,
                   bd_ref, out_ref, x_sc, act_sc, *, chunks, nf):
    i = pl.program_id(0)
    j = pl.program_id(1)
    nch = nch_ref[i]
    half = x_sc.shape[1] // 2

    @pl.when(jnp.logical_and(j == 0, nch > 0))
    def _():
        lo, hi = _unpack_bf16_pairs(xs_ref[...])
        x_sc[:, :half] = lo
        x_sc[:, half:] = hi

    full = nch == len(chunks)

    def gate_up(rs, wg, wu):
        x = x_sc[rs, :]
        hg = jnp.dot(x, wg, preferred_element_type=F32) + bg_ref[0]
        hu = jnp.dot(x, wu, preferred_element_type=F32) + bu_ref[0]
        hg = jnp.minimum(hg, SWIGLU_LIMIT)
        hl = jnp.clip(hu, -SWIGLU_LIMIT, SWIGLU_LIMIT)
        act = hg * jax.nn.sigmoid(SWIGLU_ALPHA * hg) * (hl + 1.0)
        act_sc[j, rs, :] = act.astype(BF16)

    def down(rs, wd):
        act = jnp.concatenate([act_sc[c, rs, :] for c in range(nf)], axis=1)
        y = jnp.dot(act, wd, preferred_element_type=F32) + bd_ref[0]
        out_ref[rs, :] = _pack_bf16_pairs(y.astype(BF16))

    partial_tile = jnp.logical_and(nch > 0, nch < len(chunks))
    row_slices = [slice(a, b) for a, b in chunks]

    @pl.when(jnp.logical_and(j < nf, full))
    def _():
        gate_up(slice(None), wg_ref[0].astype(BF16), wu_ref[0].astype(BF16))

    @pl.when(jnp.logical_and(j < nf, partial_tile))
    def _():
        wg = wg_ref[0].astype(BF16)
        wu = wu_ref[0].astype(BF16)
        for c, rs in enumerate(row_slices[:-1]):
            pl.when(c < nch)(functools.partial(gate_up, rs, wg, wu))

    @pl.when(jnp.logical_and(j >= nf, full))
    def _():
        down(slice(None), wd_ref[0].astype(BF16))

    @pl.when(jnp.logical_and(j >= nf, partial_tile))
    def _():
        wd = wd_ref[0].astype(BF16)
        for c, rs in enumerate(row_slices[:-1]):
            pl.when(c < nch)(functools.partial(down, rs, wd))

    @pl.when(jnp.logical_and(j >= nf, nch < len(chunks)))
    def _():
        for c, (a, b) in enumerate(chunks):
            @pl.when(c >= nch)
            def _():
                out_ref[a:b, :] = jnp.zeros((b - a, out_ref.shape[1]), out_ref.dtype)


def _experts(tile_e, tile_src, tile_nch, xs, w_gu, b_gu3, w_down, b_down3, tm_e, chunks, tf):
    n_rows = xs.shape[0]
    n_tiles = n_rows // tm_e
    d = 2 * xs.shape[1]
    dff = w_down.shape[1]
    nf = dff // tf
    nn = d // tf
    assert nn == nf
    kern = functools.partial(_expert_kernel, chunks=chunks, nf=nf)

    def c1(i, j, ns):
        return jnp.where(ns[i] > 0, jnp.minimum(j, nf - 1), nf - 1)

    def c2(i, j, ns):
        return jnp.where(ns[i] > 0, jnp.maximum(j - nf, 0), nf - 1)

    return pl.pallas_call(
        kern,
        grid_spec=pltpu.PrefetchScalarGridSpec(
            num_scalar_prefetch=3,
            grid=(n_tiles, 2 * nf),
            in_specs=[
                pl.BlockSpec((tm_e, d // 2), lambda i, j, te, ts, ns: (ts[i], 0)),
                pl.BlockSpec((1, d, tf), lambda i, j, te, ts, ns: (te[i], 0, c1(i, j, ns))),
                pl.BlockSpec((1, d, tf), lambda i, j, te, ts, ns: (te[i], 0, nf + c1(i, j, ns))),
                pl.BlockSpec((1, 1, tf), lambda i, j, te, ts, ns: (te[i], 0, c1(i, j, ns))),
                pl.BlockSpec((1, 1, tf), lambda i, j, te, ts, ns: (te[i], 0, nf + c1(i, j, ns))),
                pl.BlockSpec((1, dff, tf), lambda i, j, te, ts, ns: (te[i], 0, c2(i, j, ns))),
                pl.BlockSpec((1, 1, tf), lambda i, j, te, ts, ns: (te[i], 0, c2(i, j, ns))),
            ],
            out_specs=pl.BlockSpec((tm_e, tf // 2), lambda i, j, te, ts, ns: (i, jnp.maximum(j - nf, 0))),
            scratch_shapes=[
                pltpu.VMEM((tm_e, d), BF16),
                pltpu.VMEM((nf, tm_e, tf), BF16),
            ],
        ),
        out_shape=jax.ShapeDtypeStruct((n_rows, d // 2), jnp.uint32),
        compiler_params=_cparams(("arbitrary", "arbitrary")),
    )(tile_e, tile_src, tile_nch, xs, w_gu, w_gu, b_gu3, b_gu3, w_down, b_down3)


def _combine_kernel(cur_ref, nxt_ref, yr_hbm, h_ref, gw_ref, gf_ref, out_ref, ysel, sems, *, tmc,
                    chunk, final_norm):
    i = pl.program_id(0)
    n = pl.num_programs(0)
    slot = i % 2
    groups = TOP_K * tmc // SUBLANES
    per_k = tmc // SUBLANES

    def issue(idx_ref, sl):
        def body(g, carry):
            for r in range(SUBLANES):
                pltpu.make_async_copy(yr_hbm.at[pl.ds(idx_ref[0, 0, g * SUBLANES + r], 1), :],
                                      ysel.at[sl, g, pl.ds(r, 1), :], sems.at[sl]).start()
            return carry
        lax.fori_loop(0, groups, body, 0)

    @pl.when(i == 0)
    def _():
        issue(cur_ref, 0)

    @pl.when(i + 1 < n)
    def _():
        issue(nxt_ref, 1 - slot)

    def wait_group(g, carry):
        pltpu.make_async_copy(yr_hbm.at[pl.ds(0, SUBLANES), :], ysel.at[slot, g], sems.at[slot]).wait()
        return carry
    lax.fori_loop(0, groups, wait_group, 0)

    gw = gw_ref[...]
    y = h_ref[...]
    half = chunk // 2
    for k in range(TOP_K):
        rows_k = ysel[slot, k * per_k:(k + 1) * per_k].reshape(tmc, ysel.shape[-1])
        lo, hi = _unpack_bf16_pairs(rows_k)
        pieces = []
        for c in range(lo.shape[1] // half):
            pieces += [lo[:, c * half:(c + 1) * half], hi[:, c * half:(c + 1) * half]]
        y = y + gw[:, k:k + 1] * jnp.concatenate(pieces, axis=1).astype(F32)
    if final_norm:
        ms = jnp.mean(y * y, axis=-1, keepdims=True)
        y = (y * lax.rsqrt(ms + EPS)) * gf_ref[...]
    out_ref[...] = y


def _combine(roa3, y_rows, h, gw, gf, tmc, chunk, final_norm):
    t, d = h.shape
    n = t // tmc
    kern = functools.partial(_combine_kernel, tmc=tmc, chunk=chunk, final_norm=final_norm)
    return pl.pallas_call(
        kern,
        grid=(n,),
        in_specs=[
            pl.BlockSpec((1, 1, TOP_K * tmc), lambda i: (i, 0, 0), memory_space=pltpu.SMEM),
            pl.BlockSpec((1, 1, TOP_K * tmc), lambda i: (jnp.minimum(i + 1, n - 1), 0, 0),
                         memory_space=pltpu.SMEM),
            pl.BlockSpec(memory_space=pl.ANY),
            pl.BlockSpec((tmc, d), lambda i: (i, 0)),
            pl.BlockSpec((tmc, LANES), lambda i: (i, 0)),
            pl.BlockSpec((1, d), lambda i: (0, 0)),
        ],
        out_specs=pl.BlockSpec((tmc, d), lambda i: (i, 0)),
        out_shape=jax.ShapeDtypeStruct((t, d), F32),
        scratch_shapes=[pltpu.VMEM((2, TOP_K * tmc // SUBLANES, SUBLANES, d // 2), jnp.uint32),
                        pltpu.SemaphoreType.DMA((2,))],
        compiler_params=_cparams(("arbitrary",)),
    )(roa3, roa3, y_rows, h, gw, gf)


def _routing(top_idx, tm_e, chunks):
    t = top_idx.shape[0]
    a = t * TOP_K
    flat_e = top_idx.reshape(a)
    onehot = (flat_e[:, None] == jnp.arange(N_EXPERTS, dtype=jnp.int32)[None, :]).astype(jnp.int32)
    csum = jnp.cumsum(onehot, axis=0)
    counts = csum[-1]
    rank = jnp.sum(onehot * csum, axis=1) - 1
    padded = (counts + tm_e - 1) // tm_e * tm_e
    pad_end = jnp.cumsum(padded)
    pad_start = pad_end - padded
    row_of_assign = (pad_start[flat_e] + rank).astype(jnp.int32)
    n_tiles = -(-a // tm_e) + N_EXPERTS
    tile_start = jnp.arange(n_tiles, dtype=jnp.int32) * tm_e
    n_used = pad_end[-1] // tm_e
    tile_e_raw = jnp.minimum(jnp.searchsorted(pad_end, tile_start, side='right'), N_EXPERTS - 1)
    used = tile_start < pad_end[-1]
    last = jnp.maximum(n_used - 1, 0)
    tile_src = jnp.where(used, jnp.arange(n_tiles, dtype=jnp.int32), last).astype(jnp.int32)
    tile_e = tile_e_raw[tile_src].astype(jnp.int32)
    valid_rows = jnp.clip(counts[tile_e_raw] - (tile_start - pad_start[tile_e_raw]), 0, tm_e)
    starts = jnp.asarray([a for a, _ in chunks], jnp.int32)
    tile_nch = jnp.sum(starts[None, :] < jnp.where(used, valid_rows, 0)[:, None], axis=1).astype(jnp.int32)
    bounds = (counts.astype(jnp.int32), pad_start.astype(jnp.int32), pad_end.astype(jnp.int32))
    return bounds, row_of_assign, tile_e, tile_src, tile_nch, n_tiles * tm_e


def kernel(x, positions, norm1_g, w_in, b_fgate, b_gate, lam_q1, lam_k1, lam_q2, lam_k2, subln_g,
           w_branch_fox, w_branch_diff, w_out, norm2_g, router_w, router_b, w_gu, b_gu, w_down,
           b_down, normf_g):
    b, s, d = x.shape
    t = b * s
    depth = norm1_g.shape[0]
    fox_w = FOX_HEADS * HEAD_DIM
    tm_a = _pick(t, 1024)
    tq = _pick(s, 512)
    tm_d = _pick(t, 256)
    tm_e, chunk, zsub, tf = 1152, 512, 128, 512
    chunks = tuple((a, min(a + chunk, tm_e)) for a in range(0, tm_e, chunk))
    tmc = _pick(t, 256)

    half = ROPE_DIM // 2
    inv_freq = ROPE_THETA ** (-jnp.arange(0, ROPE_DIM, 2, dtype=F32) / ROPE_DIM)
    zeros = jnp.zeros((LANES - ROPE_DIM,), F32)
    rope_tab = jnp.zeros((8, LANES), F32)
    rope_tab = rope_tab.at[0].set(jnp.concatenate([inv_freq, inv_freq, zeros]))
    rope_tab = rope_tab.at[1].set(jnp.concatenate([-jnp.ones((half,), F32), jnp.zeros((half,), F32), zeros]))
    rope_tab = rope_tab.at[2].set(jnp.concatenate([jnp.zeros((half,), F32), jnp.ones((half,), F32), zeros]))
    pos = positions.astype(F32).reshape(t, 1)

    h = x.reshape(t, d)
    for l in range(depth):
        wi = w_in[l]
        w_left = wi[:, :3 * fox_w].astype(BF16)
        w_right = wi[:, 3 * fox_w + FOX_HEADS:].astype(BF16)
        w_f = jnp.pad(wi[:, 3 * fox_w:3 * fox_w + FOX_HEADS], ((0, 0), (0, LANES - FOX_HEADS))).astype(BF16)
        qkv, gate, flog = _inproj(h, pos, norm1_g[l][None, :], w_left, w_right, w_f, rope_tab,
                                  b_gate[l][None, :], tm_a)

        fl = flog[:, :FOX_HEADS].reshape(b, s, FOX_HEADS).transpose(0, 2, 1).reshape(b * FOX_HEADS, s)
        bf = jnp.tile(b_fgate[l].astype(F32), b).reshape(b * FOX_HEADS, 1)
        c3 = _fgate(fl, bf).reshape(b * FOX_HEADS, 1, s)

        y_fox = _fox_attn(qkv, c3, b, s, tq)
        lam_init = 0.8 - 0.6 * math.exp(-0.3 * l)
        lam_p = jnp.stack([lam_q1[l], lam_k1[l], lam_q2[l], lam_k2[l]]).astype(F32)
        y_diff = _diff_attn(qkv, lam_p, subln_g[l][None, :].astype(F32), b, s, tq, lam_init)

        rw = jnp.pad(router_w[l], ((0, 0), (0, LANES - N_EXPERTS))).astype(BF16)
        rb = jnp.concatenate([router_b[l].astype(F32), jnp.full((LANES - N_EXPERTS,), NEG_BIG, F32)])[None, :]
        h, u2, idx128, gw128 = _merge(y_fox, y_diff, gate, h, w_branch_fox[l].astype(BF16),
                                      w_branch_diff[l].astype(BF16), w_out[l].astype(BF16),
                                      norm2_g[l][None, :], rw, rb, tm_d)

        bounds, roa, tile_e, tile_src, tile_nch, n_rows = _routing(idx128[:, :TOP_K], tm_e, chunks)
        roa3 = roa.reshape(t // tmc, tmc, TOP_K).transpose(0, 2, 1).reshape(t // tmc, 1, TOP_K * tmc)
        xs = _dispatch(*bounds, roa3, u2, n_rows, tmc, zsub)
        y_rows = _experts(tile_e, tile_src, tile_nch, xs, w_gu[l], b_gu[l][:, None, :],
                          w_down[l], b_down[l][:, None, :], tm_e, chunks, tf)
        h = _combine(roa3, y_rows, h, gw128, normf_g[None, :], tmc, tf, final_norm=(l == depth - 1))
    return h.reshape(b, s, d)
```

```python
import functools
import math

import jax
import jax.numpy as jnp
from jax import lax
from jax.experimental import pallas as pl
from jax.experimental.pallas import tpu as pltpu

HEAD_DIM = 128
FOX_HEADS = 8
DIFF_HEADS = 4
DIFF_V_DIM = 2 * HEAD_DIM
ROPE_THETA = 500000.0
ROPE_DIM = HEAD_DIM // 4
N_EXPERTS = 32
TOP_K = 4
SWIGLU_ALPHA = 1.702
SWIGLU_LIMIT = 7.0
EPS = 1e-5
LANES = 128

F32 = jnp.float32
BF16 = jnp.bfloat16
NEG_BIG = -1e30
LOG2E = math.log2(math.e)
SUBLANES = 8
VMEM_LIMIT = 56 * 1024 * 1024


def _cparams(sem, vmem=VMEM_LIMIT):
    return pltpu.CompilerParams(dimension_semantics=sem, vmem_limit_bytes=vmem)


def _pick(n, pref):
    t = min(n, pref)
    while n % t:
        t //= 2
    return t


QKV_TILE = 1024
QKV_LEFT_TILES = 3
QKV_ROTARY_TILES = (3, 4)
QKV_SCALED_TILES = (0, 3)


def _qkv_kernel(x_ref, g1_ref, wa_ref, wb_ref, wf_ref, pos_ref, tab_ref,
                qkv_ref, u_ref, flog_ref, cos_sc, sa_sc, sb_sc, *, scale):
    j = pl.program_id(1)

    @pl.when(j == 0)
    def _():
        xf = x_ref[...]
        ms = jnp.mean(xf * xf, axis=-1, keepdims=True)
        u = (xf * lax.rsqrt(ms + EPS)) * g1_ref[...]
        ub = u.astype(BF16)
        u_ref[...] = ub
        flog_ref[...] = jnp.dot(ub, wf_ref[...], preferred_element_type=F32)
        ang = pos_ref[...] * tab_ref[0:1, :]
        sn = jnp.sin(ang)
        cos_sc[...] = jnp.cos(ang)
        sa_sc[...] = sn * tab_ref[1:2, :]
        sb_sc[...] = sn * tab_ref[2:3, :]

    any_of = lambda tiles: functools.reduce(jnp.logical_or, [j == q for q in tiles])
    mult = jnp.where(any_of(QKV_SCALED_TILES), scale, 1.0).astype(F32)
    rotary = any_of(QKV_ROTARY_TILES)
    left = j < QKV_LEFT_TILES

    def plain(w_ref):
        acc = jnp.dot(u_ref[...], w_ref[...], preferred_element_type=F32)
        qkv_ref[...] = (acc * mult).astype(BF16)

    pl.when(left)(functools.partial(plain, wa_ref))
    pl.when(jnp.logical_not(jnp.logical_or(left, rotary)))(functools.partial(plain, wb_ref))

    @pl.when(rotary)
    def _():
        acc = jnp.dot(u_ref[...], wb_ref[...], preferred_element_type=F32)
        ca = cos_sc[...] * mult
        cb = sa_sc[...] * mult
        cc = sb_sc[...] * mult
        for c in range(acc.shape[1] // HEAD_DIM):
            t = acc[:, c * HEAD_DIM:(c + 1) * HEAD_DIM]
            r = (t * ca + pltpu.roll(t, HEAD_DIM - ROPE_DIM // 2, 1) * cb
                 + pltpu.roll(t, ROPE_DIM // 2, 1) * cc)
            qkv_ref[:, c * HEAD_DIM:(c + 1) * HEAD_DIM] = r.astype(BF16)


def _gates_kernel(u_ref, w_ref, bg_ref, gate_ref):
    acc = jnp.dot(u_ref[...], w_ref[...], preferred_element_type=F32)
    gate_ref[...] = jax.nn.sigmoid(acc + bg_ref[...])


def _inproj(x2, pos, g1, w_left, w_right, w_f, rope_tab, b_gate, tm):
    t, d = x2.shape
    tn = QKV_TILE
    n_qkv = 6
    n_left = QKV_LEFT_TILES
    n_gate = w_right.shape[1] // tn - (n_qkv - n_left)
    kern = functools.partial(_qkv_kernel, scale=LOG2E / math.sqrt(HEAD_DIM))
    qkv, u, flog = pl.pallas_call(
        kern,
        grid=(t // tm, n_qkv),
        in_specs=[
            pl.BlockSpec((tm, d), lambda i, j: (i, 0)),
            pl.BlockSpec((1, d), lambda i, j: (0, 0)),
            pl.BlockSpec((d, tn), lambda i, j: (0, jnp.minimum(j, n_left - 1))),
            pl.BlockSpec((d, tn), lambda i, j: (0, jnp.maximum(j - n_left, 0))),
            pl.BlockSpec((d, LANES), lambda i, j: (0, 0)),
            pl.BlockSpec((tm, 1), lambda i, j: (i, 0)),
            pl.BlockSpec((8, LANES), lambda i, j: (0, 0)),
        ],
        out_specs=[
            pl.BlockSpec((tm, tn), lambda i, j: (i, j)),
            pl.BlockSpec((tm, d), lambda i, j: (i, 0)),
            pl.BlockSpec((tm, LANES), lambda i, j: (i, 0)),
        ],
        out_shape=[
            jax.ShapeDtypeStruct((t, n_qkv * tn), BF16),
            jax.ShapeDtypeStruct((t, d), BF16),
            jax.ShapeDtypeStruct((t, LANES), F32),
        ],
        scratch_shapes=[
            pltpu.VMEM((tm, LANES), F32),
            pltpu.VMEM((tm, LANES), F32),
            pltpu.VMEM((tm, LANES), F32),
        ],
        compiler_params=_cparams(("parallel", "arbitrary")),
    )(x2, g1, w_left, w_right, w_f, pos, rope_tab)
    gate = pl.pallas_call(
        _gates_kernel,
        grid=(t // tm, n_gate),
        in_specs=[
            pl.BlockSpec((tm, d), lambda i, j: (i, 0)),
            pl.BlockSpec((d, tn), lambda i, j: (0, n_qkv - n_left + j)),
            pl.BlockSpec((1, tn), lambda i, j: (0, j)),
        ],
        out_specs=pl.BlockSpec((tm, tn), lambda i, j: (i, j)),
        out_shape=jax.ShapeDtypeStruct((t, n_gate * tn), F32),
        compiler_params=_cparams(("parallel", "arbitrary")),
    )(u, w_right, b_gate)
    return qkv, gate, flog


def _fgate_kernel(fl_ref, b_ref, c_ref):
    z = fl_ref[...] + b_ref[...]
    x = jnp.minimum(z, 0.0) - jnp.log(1.0 + jnp.exp(-jnp.abs(z)))
    n = x.shape[1]
    lane = lax.broadcasted_iota(jnp.int32, x.shape, 1)
    sh = 1
    while sh < n:
        x = x + jnp.where(lane >= sh, pltpu.roll(x, sh, 1), 0.0)
        sh *= 2
    c_ref[...] = x


def _fgate(fl, b):
    r, s = fl.shape
    return pl.pallas_call(
        _fgate_kernel,
        grid=(1,),
        in_specs=[pl.BlockSpec((r, s), lambda i: (0, 0)), pl.BlockSpec((r, 1), lambda i: (0, 0))],
        out_specs=pl.BlockSpec((r, s), lambda i: (0, 0)),
        out_shape=jax.ShapeDtypeStruct((r, s), F32),
        compiler_params=_cparams(("arbitrary",)),
    )(fl, b)


def _transpose_bf16(x):
    return x.astype(F32).T.astype(BF16)


def _softmax_step(s, vt, m_sc, l_sc, acc_sc):
    m_old = m_sc[...]
    m_new = jnp.maximum(m_old, jnp.max(s, axis=0, keepdims=True))
    alpha = jnp.exp2(m_old - m_new)
    p = jnp.exp2(s - m_new)
    l_sc[...] = alpha * l_sc[...] + jnp.sum(p, axis=0, keepdims=True)
    acc_sc[...] = alpha * acc_sc[...] + jnp.dot(vt, p.astype(BF16), preferred_element_type=F32)
    m_sc[...] = m_new


def _causal_keep(shape):
    kv = lax.broadcasted_iota(jnp.int32, shape, 0)
    qq = lax.broadcasted_iota(jnp.int32, shape, 1)
    return kv <= qq


FOX_HEADS_PER_STEP = 4


def _fox_kernel(q_ref, k_ref, v_ref, c_ref, o_ref, *scratch, tq, tk, s_len):
    hp = FOX_HEADS_PER_STEP
    qt_sc, kaug_sc, vt_sc, m_sc, l_sc, acc_sc = (scratch[n * hp:(n + 1) * hp] for n in range(6))
    s_sc = scratch[6 * hp:]
    qi = pl.program_id(2)
    head_cols = [slice(h * HEAD_DIM, (h + 1) * HEAD_DIM) for h in range(hp)]

    @pl.when(qi == 0)
    def _():
        sub = lax.broadcasted_iota(jnp.int32, (HEAD_DIM, tk), 0)
        for h in range(hp):
            for ch in range(s_len // tk):
                rows = slice(ch * tk, (ch + 1) * tk)
                vt_sc[h][:, rows] = _transpose_bf16(v_ref[rows, head_cols[h]])
                c = c_ref[h, :, rows] * (-LOG2E)
                hi = c.astype(BF16).astype(F32)
                mid = (c - hi).astype(BF16).astype(F32)
                lo = (c - hi - mid).astype(BF16).astype(F32)
                tab = jnp.where(sub == 0, hi, jnp.where(sub == 1, mid, jnp.where(sub == 2, lo, 0.0)))
                kaug_sc[h][rows, HEAD_DIM:] = tab.T.astype(BF16)
                kaug_sc[h][rows, :HEAD_DIM] = k_ref[rows, head_cols[h]]

    sub_q = lax.broadcasted_iota(jnp.int32, (HEAD_DIM, tq), 0)
    for h in range(hp):
        qt_sc[h][:HEAD_DIM, :] = _transpose_bf16(q_ref[:, head_cols[h]])
        qt_sc[h][HEAD_DIM:, :] = jnp.where(sub_q < 3, 1.0, 0.0).astype(BF16)
        m_sc[h][...] = jnp.full(m_sc[h].shape, -jnp.inf, F32)
        l_sc[h][...] = jnp.zeros(l_sc[h].shape, F32)
        acc_sc[h][...] = jnp.zeros(acc_sc[h].shape, F32)

    def scores(h, chunk):
        start = pl.multiple_of(chunk * tk, tk)
        return jnp.dot(kaug_sc[h][pl.ds(start, tk), :], qt_sc[h][...], preferred_element_type=F32)

    def consume(h, chunk, par, masked):
        s = s_sc[par * hp + h][...]
        if masked:
            s = jnp.where(_causal_keep(s.shape), s, -jnp.inf)
        start = pl.multiple_of(chunk * tk, tk)
        _softmax_step(s, vt_sc[h][:, pl.ds(start, tk)], m_sc[h], l_sc[h], acc_sc[h])

    for h in range(hp):
        s_sc[h][...] = scores(h, 0)

    def body(kc, carry):
        def run(par):
            for h in range(hp):
                s_sc[(1 - par) * hp + h][...] = scores(h, kc + 1)
                consume(h, kc, par, False)
        pl.when(kc % 2 == 0)(functools.partial(run, 0))
        pl.when(kc % 2 == 1)(functools.partial(run, 1))
        return carry

    lax.fori_loop(0, qi, body, 0)

    def diagonal(par):
        for h in range(hp):
            consume(h, qi, par, True)
    pl.when(qi % 2 == 0)(functools.partial(diagonal, 0))
    pl.when(qi % 2 == 1)(functools.partial(diagonal, 1))
    for h in range(hp):
        o_ref[:, head_cols[h]] = (acc_sc[h][...] / l_sc[h][...]).T.astype(o_ref.dtype)


def _fox_attn(qkv, c3, b, s, tq):
    t = b * s
    nq = s // tq
    hp = FOX_HEADS_PER_STEP
    groups = FOX_HEADS // hp
    width = hp * HEAD_DIM
    kern = functools.partial(_fox_kernel, tq=tq, tk=tq, s_len=s)

    def per_head(shape, dtype):
        return [pltpu.VMEM(shape, dtype) for _ in range(hp)]

    return pl.pallas_call(
        kern,
        grid=(b, groups, nq),
        in_specs=[
            pl.BlockSpec((tq, width), lambda bi, g, qi: (bi * nq + qi, g)),
            pl.BlockSpec((s, width), lambda bi, g, qi: (bi, groups + g)),
            pl.BlockSpec((s, width), lambda bi, g, qi: (bi, 2 * groups + g)),
            pl.BlockSpec((hp, 1, s), lambda bi, g, qi: (bi * groups + g, 0, 0)),
        ],
        out_specs=pl.BlockSpec((tq, width), lambda bi, g, qi: (bi * nq + qi, g)),
        out_shape=jax.ShapeDtypeStruct((t, FOX_HEADS * HEAD_DIM), BF16),
        scratch_shapes=(per_head((2 * HEAD_DIM, tq), BF16) + per_head((s, 2 * HEAD_DIM), BF16)
                        + per_head((HEAD_DIM, s), BF16) + per_head((1, tq), F32)
                        + per_head((1, tq), F32) + per_head((HEAD_DIM, tq), F32)
                        + per_head((tq, tq), F32) + per_head((tq, tq), F32)),
        compiler_params=_cparams(("parallel", "parallel", "arbitrary")),
    )(qkv, qkv, qkv, c3)


def _diff_kernel(q1_ref, q2_ref, k1_ref, k2_ref, v_ref, lam_ref, g_ref, o_ref,
                 q1t_sc, q2t_sc, vt_sc, m1_sc, l1_sc, a1_sc, m2_sc, l2_sc, a2_sc, *s_sc,
                 tq, tk, s_len, lam_init):
    qi = pl.program_id(2)

    @pl.when(qi == 0)
    def _():
        for ch in range(s_len // tk):
            rows = slice(ch * tk, (ch + 1) * tk)
            vt_sc[:, rows] = _transpose_bf16(v_ref[rows, :])

    q1t_sc[...] = _transpose_bf16(q1_ref[...])
    q2t_sc[...] = _transpose_bf16(q2_ref[...])
    for r in (m1_sc, m2_sc):
        r[...] = jnp.full(r.shape, -jnp.inf, F32)
    for r in (l1_sc, l2_sc, a1_sc, a2_sc):
        r[...] = jnp.zeros(r.shape, F32)

    chains = ((k1_ref, q1t_sc, m1_sc, l1_sc, a1_sc), (k2_ref, q2t_sc, m2_sc, l2_sc, a2_sc))

    def scores(c, chunk):
        start = pl.multiple_of(chunk * tk, tk)
        return jnp.dot(chains[c][0][pl.ds(start, tk), :], chains[c][1][...], preferred_element_type=F32)

    def consume(c, chunk, par, masked):
        s = s_sc[par * 2 + c][...]
        if masked:
            s = jnp.where(_causal_keep(s.shape), s, -jnp.inf)
        start = pl.multiple_of(chunk * tk, tk)
        _softmax_step(s, vt_sc[:, pl.ds(start, tk)], *chains[c][2:])

    for c in range(2):
        s_sc[c][...] = scores(c, 0)

    def body(kc, carry):
        def run(par):
            for c in range(2):
                s_sc[(1 - par) * 2 + c][...] = scores(c, kc + 1)
                consume(c, kc, par, False)
        pl.when(kc % 2 == 0)(functools.partial(run, 0))
        pl.when(kc % 2 == 1)(functools.partial(run, 1))
        return carry

    lax.fori_loop(0, qi, body, 0)

    def diagonal(par):
        for c in range(2):
            consume(c, qi, par, True)
    pl.when(qi % 2 == 0)(functools.partial(diagonal, 0))
    pl.when(qi % 2 == 1)(functools.partial(diagonal, 1))

    lp = lam_ref[...]
    lam = (jnp.exp(jnp.sum(lp[0:1, :] * lp[1:2, :], axis=-1, keepdims=True))
           - jnp.exp(jnp.sum(lp[2:3, :] * lp[3:4, :], axis=-1, keepdims=True)) + lam_init)
    y = (a1_sc[...] / l1_sc[...]).T - lam * (a2_sc[...] / l2_sc[...]).T
    ms = jnp.mean(y * y, axis=-1, keepdims=True)
    y = (y * lax.rsqrt(ms + EPS)) * g_ref[...]
    o_ref[...] = (y * (1.0 - lam_init)).astype(o_ref.dtype)


def _diff_attn(qkv, lam_p, subln_g, b, s, tq, lam_init):
    t = b * s
    nq = s // tq
    qb = 3 * FOX_HEADS
    kb = qb + 2 * DIFF_HEADS
    vb = (kb + 2 * DIFF_HEADS) // 2
    kern = functools.partial(_diff_kernel, tq=tq, tk=tq, s_len=s, lam_init=lam_init)
    return pl.pallas_call(
        kern,
        grid=(b, DIFF_HEADS, nq),
        in_specs=[
            pl.BlockSpec((tq, HEAD_DIM), lambda bi, h, qi: (bi * nq + qi, qb + 2 * h)),
            pl.BlockSpec((tq, HEAD_DIM), lambda bi, h, qi: (bi * nq + qi, qb + 2 * h + 1)),
            pl.BlockSpec((s, HEAD_DIM), lambda bi, h, qi: (bi, kb + 2 * h)),
            pl.BlockSpec((s, HEAD_DIM), lambda bi, h, qi: (bi, kb + 2 * h + 1)),
            pl.BlockSpec((s, DIFF_V_DIM), lambda bi, h, qi: (bi, vb + h)),
            pl.BlockSpec((4, HEAD_DIM), lambda bi, h, qi: (0, 0)),
            pl.BlockSpec((1, DIFF_V_DIM), lambda bi, h, qi: (0, 0)),
        ],
        out_specs=pl.BlockSpec((tq, DIFF_V_DIM), lambda bi, h, qi: (bi * nq + qi, h)),
        out_shape=jax.ShapeDtypeStruct((t, DIFF_HEADS * DIFF_V_DIM), BF16),
        scratch_shapes=[
            pltpu.VMEM((HEAD_DIM, tq), BF16), pltpu.VMEM((HEAD_DIM, tq), BF16),
            pltpu.VMEM((DIFF_V_DIM, s), BF16),
            pltpu.VMEM((1, tq), F32), pltpu.VMEM((1, tq), F32), pltpu.VMEM((DIFF_V_DIM, tq), F32),
            pltpu.VMEM((1, tq), F32), pltpu.VMEM((1, tq), F32), pltpu.VMEM((DIFF_V_DIM, tq), F32),
        ] + [pltpu.VMEM((tq, tq), F32) for _ in range(4)],
        compiler_params=_cparams(("parallel", "parallel", "arbitrary")),
    )(qkv, qkv, qkv, qkv, qkv, lam_p, subln_g)


def _pack_bf16_pairs(xb):
    w = xb.shape[1] // 2
    lo = lax.bitcast_convert_type(xb[:, :w].astype(F32), jnp.uint32)
    hi = lax.bitcast_convert_type(xb[:, w:].astype(F32), jnp.uint32)
    return (lo >> 16) | (hi & jnp.uint32(0xFFFF0000))


def _unpack_bf16_pairs(p):
    lo = lax.bitcast_convert_type(p << 16, F32).astype(BF16)
    hi = lax.bitcast_convert_type(p & jnp.uint32(0xFFFF0000), F32).astype(BF16)
    return lo, hi


def _merge_kernel(yf_ref, yd_ref, g0_ref, g1_ref, x_ref, wbf_ref, wbd_ref, wo_ref, n2_ref,
                  rw_ref, rb_ref, h_ref, u_ref, idx_ref, gw_ref):
    ya = jnp.dot(yf_ref[...], wbf_ref[...], preferred_element_type=F32)
    yb = jnp.dot(yd_ref[...], wbd_ref[...], preferred_element_type=F32)
    merged = g0_ref[...] * ya + g1_ref[...] * yb
    h = x_ref[...] + jnp.dot(merged.astype(BF16), wo_ref[...], preferred_element_type=F32)
    h_ref[...] = h
    ms = jnp.mean(h * h, axis=-1, keepdims=True)
    u = (h * lax.rsqrt(ms + EPS)) * n2_ref[...]
    ub = u.astype(BF16)
    u_ref[...] = _pack_bf16_pairs(ub)
    logits = jnp.dot(ub, rw_ref[...], preferred_element_type=F32) + rb_ref[...]

    lane = lax.broadcasted_iota(jnp.int32, logits.shape, 1).astype(F32)
    work = logits
    vals, idxs = [], []
    for _ in range(TOP_K):
        m = jnp.max(work, axis=-1, keepdims=True)
        sel = jnp.min(jnp.where(work == m, lane, float(LANES)), axis=-1, keepdims=True)
        vals.append(m)
        idxs.append(sel)
        work = jnp.where(lane == sel, -jnp.inf, work)
    es = [jnp.exp(v - vals[0]) for v in vals]
    den = es[0]
    for e in es[1:]:
        den = den + e
    idx_out = jnp.zeros(logits.shape, F32)
    gw_out = jnp.zeros(logits.shape, F32)
    for k in range(TOP_K):
        idx_out = jnp.where(lane == float(k), idxs[k], idx_out)
        gw_out = jnp.where(lane == float(k), es[k] / den, gw_out)
    idx_ref[...] = idx_out.astype(jnp.int32)
    gw_ref[...] = gw_out


def _merge(yf, yd, gate, x2, wbf, wbd, wo, n2, rw, rb, tm):
    t, d = x2.shape
    fw = yf.shape[1]
    dw = yd.shape[1]
    const = dict(pipeline_mode=pl.Buffered(1))
    return pl.pallas_call(
        _merge_kernel,
        grid=(t // tm,),
        in_specs=[
            pl.BlockSpec((tm, fw), lambda i: (i, 0)),
            pl.BlockSpec((tm, dw), lambda i: (i, 0)),
            pl.BlockSpec((tm, d), lambda i: (i, 0)),
            pl.BlockSpec((tm, d), lambda i: (i, 1)),
            pl.BlockSpec((tm, d), lambda i: (i, 0)),
            pl.BlockSpec((fw, d), lambda i: (0, 0), **const),
            pl.BlockSpec((dw, d), lambda i: (0, 0), **const),
            pl.BlockSpec((d, d), lambda i: (0, 0), **const),
            pl.BlockSpec((1, d), lambda i: (0, 0)),
            pl.BlockSpec((d, LANES), lambda i: (0, 0), **const),
            pl.BlockSpec((1, LANES), lambda i: (0, 0)),
        ],
        out_specs=[
            pl.BlockSpec((tm, d), lambda i: (i, 0)),
            pl.BlockSpec((tm, d // 2), lambda i: (i, 0)),
            pl.BlockSpec((tm, LANES), lambda i: (i, 0)),
            pl.BlockSpec((tm, LANES), lambda i: (i, 0)),
        ],
        out_shape=[
            jax.ShapeDtypeStruct((t, d), F32),
            jax.ShapeDtypeStruct((t, d // 2), jnp.uint32),
            jax.ShapeDtypeStruct((t, LANES), jnp.int32),
            jax.ShapeDtypeStruct((t, LANES), F32),
        ],
        compiler_params=_cparams(("parallel",)),
    )(yf, yd, gate, gate, x2, wbf, wbd, wo, n2, rw, rb)


def _dispatch_kernel(cnt_ref, pstart_ref, pend_ref, roa_ref, u_ref, xs_hbm, stage, zeros, sems,
                     *, tb, sub, n_rows):
    s = pl.program_id(0)
    n = pl.num_programs(0)
    slot = s % 2
    n_dma = TOP_K * tb

    groups = n_dma // SUBLANES

    def wait_slot(sl):
        def body(g, carry):
            pltpu.make_async_copy(stage.at[sl, g], xs_hbm.at[pl.ds(0, SUBLANES), :], sems.at[sl]).wait()
            return carry
        lax.fori_loop(0, groups, body, 0)

    def zero_row(r):
        return pltpu.make_async_copy(zeros.at[pl.ds(0, 1), :], xs_hbm.at[pl.ds(r, 1), :], sems.at[2])

    def zero_block(r):
        return pltpu.make_async_copy(zeros, xs_hbm.at[pl.ds(pl.multiple_of(r, sub), sub), :], sems.at[2])

    def zero_fill(start):
        def fill(first, stop, copy, step):
            def body(q, carry):
                if start:
                    copy(first + q * step).start()
                else:
                    copy(first).wait()
                return carry
            lax.fori_loop(0, (stop - first) // step, body, 0)

        def per_expert(e, carry):
            first = pstart_ref[e] + cnt_ref[e]
            edge = (first + sub - 1) // sub * sub
            fill(first, edge, zero_row, 1)
            fill(edge, pend_ref[e], zero_block, sub)
            return carry
        lax.fori_loop(0, N_EXPERTS, per_expert, 0)
        fill(pend_ref[N_EXPERTS - 1], n_rows, zero_block, sub)

    @pl.when(s == 0)
    def _():
        zeros[...] = jnp.zeros(zeros.shape, zeros.dtype)

    @pl.when(s == jnp.minimum(1, n - 1))
    def _():
        zero_fill(start=True)

    @pl.when(s >= 2)
    def _():
        wait_slot(slot)

    blk = u_ref[...].reshape(tb // SUBLANES, SUBLANES, u_ref.shape[1])
    for k in range(TOP_K):
        stage[slot, pl.ds(k * (tb // SUBLANES), tb // SUBLANES)] = blk

    def body(g, carry):
        for r in range(SUBLANES):
            q = g * SUBLANES + r
            pltpu.make_async_copy(stage.at[slot, g, pl.ds(r, 1), :],
                                  xs_hbm.at[pl.ds(roa_ref[0, 0, q], 1), :], sems.at[slot]).start()
        return carry
    lax.fori_loop(0, groups, body, 0)

    @pl.when(s == n - 1)
    def _():
        @pl.when(n >= 2)
        def _():
            wait_slot(1 - slot)
        wait_slot(slot)
        zero_fill(start=False)


def _dispatch(counts, pad_start, pad_end, roa3, u_packed, n_rows, tb, sub):
    t, w = u_packed.shape
    kern = functools.partial(_dispatch_kernel, tb=tb, sub=sub, n_rows=n_rows)
    return pl.pallas_call(
        kern,
        grid_spec=pltpu.PrefetchScalarGridSpec(
            num_scalar_prefetch=3,
            grid=(t // tb,),
            in_specs=[
                pl.BlockSpec((1, 1, TOP_K * tb), lambda s, c, ps, pe: (s, 0, 0), memory_space=pltpu.SMEM),
                pl.BlockSpec((tb, w), lambda s, c, ps, pe: (s, 0)),
            ],
            out_specs=pl.BlockSpec(memory_space=pl.ANY),
            scratch_shapes=[
                pltpu.VMEM((2, TOP_K * tb // SUBLANES, SUBLANES, w), jnp.uint32),
                pltpu.VMEM((sub, w), jnp.uint32),
                pltpu.SemaphoreType.DMA((3,)),
            ],
        ),
        out_shape=jax.ShapeDtypeStruct((n_rows, w), jnp.uint32),
        compiler_params=_cparams(("arbitrary",)),
    )(counts, pad_start, pad_end, roa3, u_packed)


def _expert_kernel(te_ref, ts_ref, nch_ref, xs_ref, wg_ref, wu_ref, bg_ref, bu_ref, wd_ref,
                   bd_ref, out_ref, x_sc, act_sc, *, chunks, nf):
    i = pl.program_id(0)
    j = pl.program_id(1)
    nch = nch_ref[i]
    half = x_sc.shape[1] // 2

    @pl.when(jnp.logical_and(j == 0, nch > 0))
    def _():
        lo, hi = _unpack_bf16_pairs(xs_ref[...])
        x_sc[:, :half] = lo
        x_sc[:, half:] = hi

    full = nch == len(chunks)

    def gate_up(rs, wg, wu):
        x = x_sc[rs, :]
        hg = jnp.dot(x, wg, preferred_element_type=F32) + bg_ref[0]
        hu = jnp.dot(x, wu, preferred_element_type=F32) + bu_ref[0]
        hg = jnp.minimum(hg, SWIGLU_LIMIT)
        hl = jnp.clip(hu, -SWIGLU_LIMIT, SWIGLU_LIMIT)
        act = hg * jax.nn.sigmoid(SWIGLU_ALPHA * hg) * (hl + 1.0)
        act_sc[j, rs, :] = act.astype(BF16)

    def down(rs, wd):
        act = jnp.concatenate([act_sc[c, rs, :] for c in range(nf)], axis=1)
        y = jnp.dot(act, wd, preferred_element_type=F32) + bd_ref[0]
        out_ref[rs, :] = _pack_bf16_pairs(y.astype(BF16))

    partial_tile = jnp.logical_and(nch > 0, nch < len(chunks))
    row_slices = [slice(a, b) for a, b in chunks]

    @pl.when(jnp.logical_and(j < nf, full))
    def _():
        gate_up(slice(None), wg_ref[0].astype(BF16), wu_ref[0].astype(BF16))

    @pl.when(jnp.logical_and(j < nf, partial_tile))
    def _():
        wg = wg_ref[0].astype(BF16)
        wu = wu_ref[0].astype(BF16)
        for c, rs in enumerate(row_slices[:-1]):
            pl.when(c < nch)(functools.partial(gate_up, rs, wg, wu))

    @pl.when(jnp.logical_and(j >= nf, full))
    def _():
        down(slice(None), wd_ref[0].astype(BF16))

    @pl.when(jnp.logical_and(j >= nf, partial_tile))
    def _():
        wd = wd_ref[0].astype(BF16)
        for c, rs in enumerate(row_slices[:-1]):
            pl.when(c < nch)(functools.partial(down, rs, wd))

    @pl.when(jnp.logical_and(j >= nf, nch < len(chunks)))
    def _():
        for c, (a, b) in enumerate(chunks):
            @pl.when(c >= nch)
            def _():
                out_ref[a:b, :] = jnp.zeros((b - a, out_ref.shape[1]), out_ref.dtype)


def _experts(tile_e, tile_src, tile_nch, xs, w_gu, b_gu3, w_down, b_down3, tm_e, chunks, tf):
    n_rows = xs.shape[0]
    n_tiles = n_rows // tm_e
    d = 2 * xs.shape[1]
    dff = w_down.shape[1]
    nf = dff // tf
    nn = d // tf
    assert nn == nf
    kern = functools.partial(_expert_kernel, chunks=chunks, nf=nf)

    def c1(i, j, ns):
        return jnp.where(ns[i] > 0, jnp.minimum(j, nf - 1), nf - 1)

    def c2(i, j, ns):
        return jnp.where(ns[i] > 0, jnp.maximum(j - nf, 0), nf - 1)

    return pl.pallas_call(
        kern,
        grid_spec=pltpu.PrefetchScalarGridSpec(
            num_scalar_prefetch=3,
            grid=(n_tiles, 2 * nf),
            in_specs=[
                pl.BlockSpec((tm_e, d // 2), lambda i, j, te, ts, ns: (ts[i], 0)),
                pl.BlockSpec((1, d, tf), lambda i, j, te, ts, ns: (te[i], 0, c1(i, j, ns))),
                pl.BlockSpec((1, d, tf), lambda i, j, te, ts, ns: (te[i], 0, nf + c1(i, j, ns))),
                pl.BlockSpec((1, 1, tf), lambda i, j, te, ts, ns: (te[i], 0, c1(i, j, ns))),
                pl.BlockSpec((1, 1, tf), lambda i, j, te, ts, ns: (te[i], 0, nf + c1(i, j, ns))),
                pl.BlockSpec((1, dff, tf), lambda i, j, te, ts, ns: (te[i], 0, c2(i, j, ns))),
                pl.BlockSpec((1, 1, tf), lambda i, j, te, ts, ns: (te[i], 0, c2(i, j, ns))),
            ],
            out_specs=pl.BlockSpec((tm_e, tf // 2), lambda i, j, te, ts, ns: (i, jnp.maximum(j - nf, 0))),
            scratch_shapes=[
                pltpu.VMEM((tm_e, d), BF16),
                pltpu.VMEM((nf, tm_e, tf), BF16),
            ],
        ),
        out_shape=jax.ShapeDtypeStruct((n_rows, d // 2), jnp.uint32),
        compiler_params=_cparams(("arbitrary", "arbitrary")),
    )(tile_e, tile_src, tile_nch, xs, w_gu, w_gu, b_gu3, b_gu3, w_down, b_down3)


def _combine_kernel(cur_ref, nxt_ref, yr_hbm, h_ref, gw_ref, gf_ref, out_ref, ysel, sems, *, tmc,
                    chunk, final_norm):
    i = pl.program_id(0)
    n = pl.num_programs(0)
    slot = i % 2
    groups = TOP_K * tmc // SUBLANES
    per_k = tmc // SUBLANES

    def issue(idx_ref, sl):
        def body(g, carry):
            for r in range(SUBLANES):
                pltpu.make_async_copy(yr_hbm.at[pl.ds(idx_ref[0, 0, g * SUBLANES + r], 1), :],
                                      ysel.at[sl, g, pl.ds(r, 1), :], sems.at[sl]).start()
            return carry
        lax.fori_loop(0, groups, body, 0)

    @pl.when(i == 0)
    def _():
        issue(cur_ref, 0)

    @pl.when(i + 1 < n)
    def _():
        issue(nxt_ref, 1 - slot)

    def wait_group(g, carry):
        pltpu.make_async_copy(yr_hbm.at[pl.ds(0, SUBLANES), :], ysel.at[slot, g], sems.at[slot]).wait()
        return carry
    lax.fori_loop(0, groups, wait_group, 0)

    gw = gw_ref[...]
    y = h_ref[...]
    half = chunk // 2
    for k in range(TOP_K):
        rows_k = ysel[slot, k * per_k:(k + 1) * per_k].reshape(tmc, ysel.shape[-1])
        lo, hi = _unpack_bf16_pairs(rows_k)
        pieces = []
        for c in range(lo.shape[1] // half):
            pieces += [lo[:, c * half:(c + 1) * half], hi[:, c * half:(c + 1) * half]]
        y = y + gw[:, k:k + 1] * jnp.concatenate(pieces, axis=1).astype(F32)
    if final_norm:
        ms = jnp.mean(y * y, axis=-1, keepdims=True)
        y = (y * lax.rsqrt(ms + EPS)) * gf_ref[...]
    out_ref[...] = y


def _combine(roa3, y_rows, h, gw, gf, tmc, chunk, final_norm):
    t, d = h.shape
    n = t // tmc
    kern = functools.partial(_combine_kernel, tmc=tmc, chunk=chunk, final_norm=final_norm)
    return pl.pallas_call(
        kern,
        grid=(n,),
        in_specs=[
            pl.BlockSpec((1, 1, TOP_K * tmc), lambda i: (i, 0, 0), memory_space=pltpu.SMEM),
            pl.BlockSpec((1, 1, TOP_K * tmc), lambda i: (jnp.minimum(i + 1, n - 1), 0, 0),
                         memory_space=pltpu.SMEM),
            pl.BlockSpec(memory_space=pl.ANY),
            pl.BlockSpec((tmc, d), lambda i: (i, 0)),
            pl.BlockSpec((tmc, LANES), lambda i: (i, 0)),
            pl.BlockSpec((1, d), lambda i: (0, 0)),
        ],
        out_specs=pl.BlockSpec((tmc, d), lambda i: (i, 0)),
        out_shape=jax.ShapeDtypeStruct((t, d), F32),
        scratch_shapes=[pltpu.VMEM((2, TOP_K * tmc // SUBLANES, SUBLANES, d // 2), jnp.uint32),
                        pltpu.SemaphoreType.DMA((2,))],
        compiler_params=_cparams(("arbitrary",)),
    )(roa3, roa3, y_rows, h, gw, gf)


def _routing(top_idx, tm_e, chunks):
    t = top_idx.shape[0]
    a = t * TOP_K
    flat_e = top_idx.reshape(a)
    onehot = (flat_e[:, None] == jnp.arange(N_EXPERTS, dtype=jnp.int32)[None, :]).astype(jnp.int32)
    csum = jnp.cumsum(onehot, axis=0)
    counts = csum[-1]
    rank = jnp.sum(onehot * csum, axis=1) - 1
    padded = (counts + tm_e - 1) // tm_e * tm_e
    pad_end = jnp.cumsum(padded)
    pad_start = pad_end - padded
    row_of_assign = (pad_start[flat_e] + rank).astype(jnp.int32)
    n_tiles = -(-a // tm_e) + N_EXPERTS
    tile_start = jnp.arange(n_tiles, dtype=jnp.int32) * tm_e
    n_used = pad_end[-1] // tm_e
    tile_e_raw = jnp.minimum(jnp.searchsorted(pad_end, tile_start, side='right'), N_EXPERTS - 1)
    used = tile_start < pad_end[-1]
    last = jnp.maximum(n_used - 1, 0)
    tile_src = jnp.where(used, jnp.arange(n_tiles, dtype=jnp.int32), last).astype(jnp.int32)
    tile_e = tile_e_raw[tile_src].astype(jnp.int32)
    valid_rows = jnp.clip(counts[tile_e_raw] - (tile_start - pad_start[tile_e_raw]), 0, tm_e)
    starts = jnp.asarray([a for a, _ in chunks], jnp.int32)
    tile_nch = jnp.sum(starts[None, :] < jnp.where(used, valid_rows, 0)[:, None], axis=1).astype(jnp.int32)
    bounds = (counts.astype(jnp.int32), pad_start.astype(jnp.int32), pad_end.astype(jnp.int32))
    return bounds, row_of_assign, tile_e, tile_src, tile_nch, n_tiles * tm_e


def kernel(x, positions, norm1_g, w_in, b_fgate, b_gate, lam_q1, lam_k1, lam_q2, lam_k2, subln_g,
           w_branch_fox, w_branch_diff, w_out, norm2_g, router_w, router_b, w_gu, b_gu, w_down,
           b_down, normf_g):
    b, s, d = x.shape
    t = b * s
    depth = norm1_g.shape[0]
    fox_w = FOX_HEADS * HEAD_DIM
    tm_a = _pick(t, 1024)
    tq = _pick(s, 512)
    tm_d = _pick(t, 256)
    tm_e, chunk, zsub, tf = 1152, 512, 128, 512
    chunks = tuple((a, min(a + chunk, tm_e)) for a in range(0, tm_e, chunk))
    tmc = _pick(t, 512)

    half = ROPE_DIM // 2
    inv_freq = ROPE_THETA ** (-jnp.arange(0, ROPE_DIM, 2, dtype=F32) / ROPE_DIM)
    zeros = jnp.zeros((LANES - ROPE_DIM,), F32)
    rope_tab = jnp.zeros((8, LANES), F32)
    rope_tab = rope_tab.at[0].set(jnp.concatenate([inv_freq, inv_freq, zeros]))
    rope_tab = rope_tab.at[1].set(jnp.concatenate([-jnp.ones((half,), F32), jnp.zeros((half,), F32), zeros]))
    rope_tab = rope_tab.at[2].set(jnp.concatenate([jnp.zeros((half,), F32), jnp.ones((half,), F32), zeros]))
    pos = positions.astype(F32).reshape(t, 1)

    h = x.reshape(t, d)
    for l in range(depth):
        wi = w_in[l]
        w_left = wi[:, :3 * fox_w].astype(BF16)
        w_right = wi[:, 3 * fox_w + FOX_HEADS:].astype(BF16)
        w_f = jnp.pad(wi[:, 3 * fox_w:3 * fox_w + FOX_HEADS], ((0, 0), (0, LANES - FOX_HEADS))).astype(BF16)
        qkv, gate, flog = _inproj(h, pos, norm1_g[l][None, :], w_left, w_right, w_f, rope_tab,
                                  b_gate[l][None, :], tm_a)

        fl = flog[:, :FOX_HEADS].reshape(b, s, FOX_HEADS).transpose(0, 2, 1).reshape(b * FOX_HEADS, s)
        bf = jnp.tile(b_fgate[l].astype(F32), b).reshape(b * FOX_HEADS, 1)
        c3 = _fgate(fl, bf).reshape(b * FOX_HEADS, 1, s)

        y_fox = _fox_attn(qkv, c3, b, s, tq)
        lam_init = 0.8 - 0.6 * math.exp(-0.3 * l)
        lam_p = jnp.stack([lam_q1[l], lam_k1[l], lam_q2[l], lam_k2[l]]).astype(F32)
        y_diff = _diff_attn(qkv, lam_p, subln_g[l][None, :].astype(F32), b, s, tq, lam_init)

        rw = jnp.pad(router_w[l], ((0, 0), (0, LANES - N_EXPERTS))).astype(BF16)
        rb = jnp.concatenate([router_b[l].astype(F32), jnp.full((LANES - N_EXPERTS,), NEG_BIG, F32)])[None, :]
        h, u2, idx128, gw128 = _merge(y_fox, y_diff, gate, h, w_branch_fox[l].astype(BF16),
                                      w_branch_diff[l].astype(BF16), w_out[l].astype(BF16),
                                      norm2_g[l][None, :], rw, rb, tm_d)

        bounds, roa, tile_e, tile_src, tile_nch, n_rows = _routing(idx128[:, :TOP_K], tm_e, chunks)
        roa3 = roa.reshape(t // tmc, tmc, TOP_K).transpose(0, 2, 1).reshape(t // tmc, 1, TOP_K * tmc)
        xs = _dispatch(*bounds, roa3, u2, n_rows, tmc, zsub)
        y_rows = _experts(tile_e, tile_src, tile_nch, xs, w_gu[l], b_gu[l][:, None, :],
                          w_down[l], b_down[l][:, None, :], tm_e, chunks, tf)
        h = _combine(roa3, y_rows, h, gw128, normf_g[None, :], tmc, tf, final_norm=(l == depth - 1))
    return h.reshape(b, s, d)
```

```python
import functools
import math

import jax
import jax.numpy as jnp
from jax import lax
from jax.experimental import pallas as pl
from jax.experimental.pallas import tpu as pltpu

HEAD_DIM = 128
FOX_HEADS = 8
DIFF_HEADS = 4
DIFF_V_DIM = 2 * HEAD_DIM
ROPE_THETA = 500000.0
ROPE_DIM = HEAD_DIM // 4
N_EXPERTS = 32
TOP_K = 4
SWIGLU_ALPHA = 1.702
SWIGLU_LIMIT = 7.0
EPS = 1e-5
LANES = 128

F32 = jnp.float32
BF16 = jnp.bfloat16
NEG_BIG = -1e30
LOG2E = math.log2(math.e)
SUBLANES = 8
VMEM_LIMIT = 56 * 1024 * 1024


def _cparams(sem, vmem=VMEM_LIMIT):
    return pltpu.CompilerParams(dimension_semantics=sem, vmem_limit_bytes=vmem)


def _pick(n, pref):
    t = min(n, pref)
    while n % t:
        t //= 2
    return t


QKV_TILE = 1024
QKV_LEFT_TILES = 3
QKV_ROTARY_TILES = (3, 4)
QKV_SCALED_TILES = (0, 3)


def _qkv_kernel(x_ref, g1_ref, wa_ref, wb_ref, wf_ref, pos_ref, tab_ref,
                qkv_ref, u_ref, flog_ref, cos_sc, sa_sc, sb_sc, *, scale):
    j = pl.program_id(1)

    @pl.when(j == 0)
    def _():
        xf = x_ref[...]
        ms = jnp.mean(xf * xf, axis=-1, keepdims=True)
        u = (xf * lax.rsqrt(ms + EPS)) * g1_ref[...]
        ub = u.astype(BF16)
        u_ref[...] = ub
        flog_ref[...] = jnp.dot(ub, wf_ref[...], preferred_element_type=F32)
        ang = pos_ref[...] * tab_ref[0:1, :]
        sn = jnp.sin(ang)
        cos_sc[...] = jnp.cos(ang)
        sa_sc[...] = sn * tab_ref[1:2, :]
        sb_sc[...] = sn * tab_ref[2:3, :]

    any_of = lambda tiles: functools.reduce(jnp.logical_or, [j == q for q in tiles])
    mult = jnp.where(any_of(QKV_SCALED_TILES), scale, 1.0).astype(F32)
    rotary = any_of(QKV_ROTARY_TILES)
    left = j < QKV_LEFT_TILES

    def plain(w_ref):
        acc = jnp.dot(u_ref[...], w_ref[...], preferred_element_type=F32)
        qkv_ref[...] = (acc * mult).astype(BF16)

    pl.when(left)(functools.partial(plain, wa_ref))
    pl.when(jnp.logical_not(jnp.logical_or(left, rotary)))(functools.partial(plain, wb_ref))

    @pl.when(rotary)
    def _():
        acc = jnp.dot(u_ref[...], wb_ref[...], preferred_element_type=F32)
        ca = cos_sc[...] * mult
        cb = sa_sc[...] * mult
        cc = sb_sc[...] * mult
        for c in range(acc.shape[1] // HEAD_DIM):
            t = acc[:, c * HEAD_DIM:(c + 1) * HEAD_DIM]
            r = (t * ca + pltpu.roll(t, HEAD_DIM - ROPE_DIM // 2, 1) * cb
                 + pltpu.roll(t, ROPE_DIM // 2, 1) * cc)
            qkv_ref[:, c * HEAD_DIM:(c + 1) * HEAD_DIM] = r.astype(BF16)


def _gates_kernel(u_ref, w_ref, bg_ref, gate_ref):
    acc = jnp.dot(u_ref[...], w_ref[...], preferred_element_type=F32)
    gate_ref[...] = jax.nn.sigmoid(acc + bg_ref[...])


def _inproj(x2, pos, g1, w_left, w_right, w_f, rope_tab, b_gate, tm):
    t, d = x2.shape
    tn = QKV_TILE
    n_qkv = 6
    n_left = QKV_LEFT_TILES
    n_gate = w_right.shape[1] // tn - (n_qkv - n_left)
    kern = functools.partial(_qkv_kernel, scale=LOG2E / math.sqrt(HEAD_DIM))
    qkv, u, flog = pl.pallas_call(
        kern,
        grid=(t // tm, n_qkv),
        in_specs=[
            pl.BlockSpec((tm, d), lambda i, j: (i, 0)),
            pl.BlockSpec((1, d), lambda i, j: (0, 0)),
            pl.BlockSpec((d, tn), lambda i, j: (0, jnp.minimum(j, n_left - 1))),
            pl.BlockSpec((d, tn), lambda i, j: (0, jnp.maximum(j - n_left, 0))),
            pl.BlockSpec((d, LANES), lambda i, j: (0, 0)),
            pl.BlockSpec((tm, 1), lambda i, j: (i, 0)),
            pl.BlockSpec((8, LANES), lambda i, j: (0, 0)),
        ],
        out_specs=[
            pl.BlockSpec((tm, tn), lambda i, j: (i, j)),
            pl.BlockSpec((tm, d), lambda i, j: (i, 0)),
            pl.BlockSpec((tm, LANES), lambda i, j: (i, 0)),
        ],
        out_shape=[
            jax.ShapeDtypeStruct((t, n_qkv * tn), BF16),
            jax.ShapeDtypeStruct((t, d), BF16),
            jax.ShapeDtypeStruct((t, LANES), F32),
        ],
        scratch_shapes=[
            pltpu.VMEM((tm, LANES), F32),
            pltpu.VMEM((tm, LANES), F32),
            pltpu.VMEM((tm, LANES), F32),
        ],
        compiler_params=_cparams(("parallel", "arbitrary")),
    )(x2, g1, w_left, w_right, w_f, pos, rope_tab)
    gate = pl.pallas_call(
        _gates_kernel,
        grid=(t // tm, n_gate),
        in_specs=[
            pl.BlockSpec((tm, d), lambda i, j: (i, 0)),
            pl.BlockSpec((d, tn), lambda i, j: (0, n_qkv - n_left + j)),
            pl.BlockSpec((1, tn), lambda i, j: (0, j)),
        ],
        out_specs=pl.BlockSpec((tm, tn), lambda i, j: (i, j)),
        out_shape=jax.ShapeDtypeStruct((t, n_gate * tn), F32),
        compiler_params=_cparams(("parallel", "arbitrary")),
    )(u, w_right, b_gate)
    return qkv, gate, flog


def _fgate_kernel(fl_ref, b_ref, c_ref):
    z = fl_ref[...] + b_ref[...]
    x = jnp.minimum(z, 0.0) - jnp.log(1.0 + jnp.exp(-jnp.abs(z)))
    n = x.shape[1]
    lane = lax.broadcasted_iota(jnp.int32, x.shape, 1)
    sh = 1
    while sh < n:
        x = x + jnp.where(lane >= sh, pltpu.roll(x, sh, 1), 0.0)
        sh *= 2
    c_ref[...] = x


def _fgate(fl, b):
    r, s = fl.shape
    return pl.pallas_call(
        _fgate_kernel,
        grid=(1,),
        in_specs=[pl.BlockSpec((r, s), lambda i: (0, 0)), pl.BlockSpec((r, 1), lambda i: (0, 0))],
        out_specs=pl.BlockSpec((r, s), lambda i: (0, 0)),
        out_shape=jax.ShapeDtypeStruct((r, s), F32),
        compiler_params=_cparams(("arbitrary",)),
    )(fl, b)


def _transpose_bf16(x):
    return x.astype(F32).T.astype(BF16)


def _softmax_step(s, vt, m_sc, l_sc, acc_sc):
    m_old = m_sc[...]
    m_new = jnp.maximum(m_old, jnp.max(s, axis=0, keepdims=True))
    alpha = jnp.exp2(m_old - m_new)
    p = jnp.exp2(s - m_new)
    l_sc[...] = alpha * l_sc[...] + jnp.sum(p, axis=0, keepdims=True)
    acc_sc[...] = alpha * acc_sc[...] + jnp.dot(vt, p.astype(BF16), preferred_element_type=F32)
    m_sc[...] = m_new


def _causal_keep(shape):
    kv = lax.broadcasted_iota(jnp.int32, shape, 0)
    qq = lax.broadcasted_iota(jnp.int32, shape, 1)
    return kv <= qq


FOX_HEADS_PER_STEP = 4


def _fox_kernel(q_ref, k_ref, v_ref, c_ref, o_ref, *scratch, tq, tk, s_len):
    hp = FOX_HEADS_PER_STEP
    qt_sc, kaug_sc, vt_sc, m_sc, l_sc, acc_sc = (scratch[n * hp:(n + 1) * hp] for n in range(6))
    s_sc = scratch[6 * hp:]
    qi = pl.program_id(2)
    head_cols = [slice(h * HEAD_DIM, (h + 1) * HEAD_DIM) for h in range(hp)]

    @pl.when(qi == 0)
    def _():
        sub = lax.broadcasted_iota(jnp.int32, (HEAD_DIM, tk), 0)
        for h in range(hp):
            for ch in range(s_len // tk):
                rows = slice(ch * tk, (ch + 1) * tk)
                vt_sc[h][:, rows] = _transpose_bf16(v_ref[rows, head_cols[h]])
                c = c_ref[h, :, rows] * (-LOG2E)
                hi = c.astype(BF16).astype(F32)
                mid = (c - hi).astype(BF16).astype(F32)
                lo = (c - hi - mid).astype(BF16).astype(F32)
                tab = jnp.where(sub == 0, hi, jnp.where(sub == 1, mid, jnp.where(sub == 2, lo, 0.0)))
                kaug_sc[h][rows, HEAD_DIM:] = tab.T.astype(BF16)
                kaug_sc[h][rows, :HEAD_DIM] = k_ref[rows, head_cols[h]]

    sub_q = lax.broadcasted_iota(jnp.int32, (HEAD_DIM, tq), 0)
    for h in range(hp):
        qt_sc[h][:HEAD_DIM, :] = _transpose_bf16(q_ref[:, head_cols[h]])
        qt_sc[h][HEAD_DIM:, :] = jnp.where(sub_q < 3, 1.0, 0.0).astype(BF16)
        m_sc[h][...] = jnp.full(m_sc[h].shape, -jnp.inf, F32)
        l_sc[h][...] = jnp.zeros(l_sc[h].shape, F32)
        acc_sc[h][...] = jnp.zeros(acc_sc[h].shape, F32)

    def scores(h, chunk):
        start = pl.multiple_of(chunk * tk, tk)
        return jnp.dot(kaug_sc[h][pl.ds(start, tk), :], qt_sc[h][...], preferred_element_type=F32)

    def consume(h, chunk, par, masked):
        s = s_sc[par * hp + h][...]
        if masked:
            s = jnp.where(_causal_keep(s.shape), s, -jnp.inf)
        start = pl.multiple_of(chunk * tk, tk)
        _softmax_step(s, vt_sc[h][:, pl.ds(start, tk)], m_sc[h], l_sc[h], acc_sc[h])

    for h in range(hp):
        s_sc[h][...] = scores(h, 0)

    def body(kc, carry):
        def run(par):
            for h in range(hp):
                s_sc[(1 - par) * hp + h][...] = scores(h, kc + 1)
                consume(h, kc, par, False)
        pl.when(kc % 2 == 0)(functools.partial(run, 0))
        pl.when(kc % 2 == 1)(functools.partial(run, 1))
        return carry

    lax.fori_loop(0, qi, body, 0)

    def diagonal(par):
        for h in range(hp):
            consume(h, qi, par, True)
    pl.when(qi % 2 == 0)(functools.partial(diagonal, 0))
    pl.when(qi % 2 == 1)(functools.partial(diagonal, 1))
    for h in range(hp):
        o_ref[:, head_cols[h]] = (acc_sc[h][...] / l_sc[h][...]).T.astype(o_ref.dtype)


def _fox_attn(qkv, c3, b, s, tq):
    t = b * s
    nq = s // tq
    hp = FOX_HEADS_PER_STEP
    groups = FOX_HEADS // hp
    width = hp * HEAD_DIM
    kern = functools.partial(_fox_kernel, tq=tq, tk=tq, s_len=s)

    def per_head(shape, dtype):
        return [pltpu.VMEM(shape, dtype) for _ in range(hp)]

    return pl.pallas_call(
        kern,
        grid=(b, groups, nq),
        in_specs=[
            pl.BlockSpec((tq, width), lambda bi, g, qi: (bi * nq + qi, g)),
            pl.BlockSpec((s, width), lambda bi, g, qi: (bi, groups + g)),
            pl.BlockSpec((s, width), lambda bi, g, qi: (bi, 2 * groups + g)),
            pl.BlockSpec((hp, 1, s), lambda bi, g, qi: (bi * groups + g, 0, 0)),
        ],
        out_specs=pl.BlockSpec((tq, width), lambda bi, g, qi: (bi * nq + qi, g)),
        out_shape=jax.ShapeDtypeStruct((t, FOX_HEADS * HEAD_DIM), BF16),
        scratch_shapes=(per_head((2 * HEAD_DIM, tq), BF16) + per_head((s, 2 * HEAD_DIM), BF16)
                        + per_head((HEAD_DIM, s), BF16) + per_head((1, tq), F32)
                        + per_head((1, tq), F32) + per_head((HEAD_DIM, tq), F32)
                        + per_head((tq, tq), F32) + per_head((tq, tq), F32)),
        compiler_params=_cparams(("parallel", "parallel", "arbitrary")),
    )(qkv, qkv, qkv, c3)


DIFF_HEADS_PER_STEP = 2


def _diff_kernel(q_ref, k_ref, v_ref, lam_ref, g_ref, o_ref, *scratch, tq, tk, s_len, lam_init):
    hp = DIFF_HEADS_PER_STEP
    nc = 2 * hp
    qt_sc, m_sc, l_sc, a_sc = (scratch[n * nc:(n + 1) * nc] for n in range(4))
    vt_sc = scratch[4 * nc:4 * nc + hp]
    s_sc = scratch[4 * nc + hp:]
    qi = pl.program_id(2)
    chain_cols = [slice(c * HEAD_DIM, (c + 1) * HEAD_DIM) for c in range(nc)]
    head_cols = [slice(h * DIFF_V_DIM, (h + 1) * DIFF_V_DIM) for h in range(hp)]

    @pl.when(qi == 0)
    def _():
        for h in range(hp):
            for ch in range(s_len // tk):
                rows = slice(ch * tk, (ch + 1) * tk)
                vt_sc[h][:, rows] = _transpose_bf16(v_ref[rows, head_cols[h]])

    for c in range(nc):
        qt_sc[c][...] = _transpose_bf16(q_ref[:, chain_cols[c]])
        m_sc[c][...] = jnp.full(m_sc[c].shape, -jnp.inf, F32)
        l_sc[c][...] = jnp.zeros(l_sc[c].shape, F32)
        a_sc[c][...] = jnp.zeros(a_sc[c].shape, F32)

    def scores(c, chunk):
        start = pl.multiple_of(chunk * tk, tk)
        return jnp.dot(k_ref[pl.ds(start, tk), chain_cols[c]], qt_sc[c][...], preferred_element_type=F32)

    def consume(c, chunk, par, masked):
        s = s_sc[par * nc + c][...]
        if masked:
            s = jnp.where(_causal_keep(s.shape), s, -jnp.inf)
        start = pl.multiple_of(chunk * tk, tk)
        _softmax_step(s, vt_sc[c // 2][:, pl.ds(start, tk)], m_sc[c], l_sc[c], a_sc[c])

    for c in range(nc):
        s_sc[c][...] = scores(c, 0)

    def body(kc, carry):
        def run(par):
            for c in range(nc):
                s_sc[(1 - par) * nc + c][...] = scores(c, kc + 1)
                consume(c, kc, par, False)
        pl.when(kc % 2 == 0)(functools.partial(run, 0))
        pl.when(kc % 2 == 1)(functools.partial(run, 1))
        return carry

    lax.fori_loop(0, qi, body, 0)

    def diagonal(par):
        for c in range(nc):
            consume(c, qi, par, True)
    pl.when(qi % 2 == 0)(functools.partial(diagonal, 0))
    pl.when(qi % 2 == 1)(functools.partial(diagonal, 1))

    lp = lam_ref[...]
    lam = (jnp.exp(jnp.sum(lp[0:1, :] * lp[1:2, :], axis=-1, keepdims=True))
           - jnp.exp(jnp.sum(lp[2:3, :] * lp[3:4, :], axis=-1, keepdims=True)) + lam_init)
    for h in range(hp):
        c1, c2 = 2 * h, 2 * h + 1
        y = (a_sc[c1][...] / l_sc[c1][...]).T - lam * (a_sc[c2][...] / l_sc[c2][...]).T
        ms = jnp.mean(y * y, axis=-1, keepdims=True)
        y = (y * lax.rsqrt(ms + EPS)) * g_ref[...]
        o_ref[:, head_cols[h]] = (y * (1.0 - lam_init)).astype(o_ref.dtype)


def _diff_attn(qkv, lam_p, subln_g, b, s, tq, lam_init):
    t = b * s
    nq = s // tq
    hp = DIFF_HEADS_PER_STEP
    nc = 2 * hp
    groups = DIFF_HEADS // hp
    qk_w = nc * HEAD_DIM
    v_w = hp * DIFF_V_DIM
    q_blk = 3 * FOX_HEADS * HEAD_DIM // qk_w
    k_blk = q_blk + groups
    v_blk = (3 * FOX_HEADS * HEAD_DIM + 2 * groups * qk_w) // v_w
    kern = functools.partial(_diff_kernel, tq=tq, tk=tq, s_len=s, lam_init=lam_init)

    def per_chain(shape, dtype):
        return [pltpu.VMEM(shape, dtype) for _ in range(nc)]

    return pl.pallas_call(
        kern,
        grid=(b, groups, nq),
        in_specs=[
            pl.BlockSpec((tq, qk_w), lambda bi, g, qi: (bi * nq + qi, q_blk + g)),
            pl.BlockSpec((s, qk_w), lambda bi, g, qi: (bi, k_blk + g)),
            pl.BlockSpec((s, v_w), lambda bi, g, qi: (bi, v_blk + g)),
            pl.BlockSpec((4, HEAD_DIM), lambda bi, g, qi: (0, 0)),
            pl.BlockSpec((1, DIFF_V_DIM), lambda bi, g, qi: (0, 0)),
        ],
        out_specs=pl.BlockSpec((tq, v_w), lambda bi, g, qi: (bi * nq + qi, g)),
        out_shape=jax.ShapeDtypeStruct((t, DIFF_HEADS * DIFF_V_DIM), BF16),
        scratch_shapes=(per_chain((HEAD_DIM, tq), BF16) + per_chain((1, tq), F32) + per_chain((1, tq), F32)
                        + per_chain((DIFF_V_DIM, tq), F32)
                        + [pltpu.VMEM((DIFF_V_DIM, s), BF16) for _ in range(hp)]
                        + per_chain((tq, tq), F32) + per_chain((tq, tq), F32)),
        compiler_params=_cparams(("parallel", "parallel", "arbitrary")),
    )(qkv, qkv, qkv, lam_p, subln_g)


def _pack_bf16_pairs(xb):
    w = xb.shape[1] // 2
    lo = lax.bitcast_convert_type(xb[:, :w].astype(F32), jnp.uint32)
    hi = lax.bitcast_convert_type(xb[:, w:].astype(F32), jnp.uint32)
    return (lo >> 16) | (hi & jnp.uint32(0xFFFF0000))


def _unpack_bf16_pairs(p):
    lo = lax.bitcast_convert_type(p << 16, F32).astype(BF16)
    hi = lax.bitcast_convert_type(p & jnp.uint32(0xFFFF0000), F32).astype(BF16)
    return lo, hi


def _merge_kernel(yf_ref, yd_ref, g0_ref, g1_ref, x_ref, wbf_ref, wbd_ref, wo_ref, n2_ref,
                  rw_ref, rb_ref, h_ref, u_ref, idx_ref, gw_ref):
    ya = jnp.dot(yf_ref[...], wbf_ref[...], preferred_element_type=F32)
    yb = jnp.dot(yd_ref[...], wbd_ref[...], preferred_element_type=F32)
    merged = g0_ref[...] * ya + g1_ref[...] * yb
    h = x_ref[...] + jnp.dot(merged.astype(BF16), wo_ref[...], preferred_element_type=F32)
    h_ref[...] = h
    ms = jnp.mean(h * h, axis=-1, keepdims=True)
    u = (h * lax.rsqrt(ms + EPS)) * n2_ref[...]
    ub = u.astype(BF16)
    u_ref[...] = _pack_bf16_pairs(ub)
    logits = jnp.dot(ub, rw_ref[...], preferred_element_type=F32) + rb_ref[...]

    lane = lax.broadcasted_iota(jnp.int32, logits.shape, 1).astype(F32)
    work = logits
    vals, idxs = [], []
    for _ in range(TOP_K):
        m = jnp.max(work, axis=-1, keepdims=True)
        sel = jnp.min(jnp.where(work == m, lane, float(LANES)), axis=-1, keepdims=True)
        vals.append(m)
        idxs.append(sel)
        work = jnp.where(lane == sel, -jnp.inf, work)
    es = [jnp.exp(v - vals[0]) for v in vals]
    den = es[0]
    for e in es[1:]:
        den = den + e
    idx_out = jnp.zeros(logits.shape, F32)
    gw_out = jnp.zeros(logits.shape, F32)
    for k in range(TOP_K):
        idx_out = jnp.where(lane == float(k), idxs[k], idx_out)
        gw_out = jnp.where(lane == float(k), es[k] / den, gw_out)
    idx_ref[...] = idx_out.astype(jnp.int32)
    gw_ref[...] = gw_out


def _merge(yf, yd, gate, x2, wbf, wbd, wo, n2, rw, rb, tm):
    t, d = x2.shape
    fw = yf.shape[1]
    dw = yd.shape[1]
    const = dict(pipeline_mode=pl.Buffered(1))
    return pl.pallas_call(
        _merge_kernel,
        grid=(t // tm,),
        in_specs=[
            pl.BlockSpec((tm, fw), lambda i: (i, 0)),
            pl.BlockSpec((tm, dw), lambda i: (i, 0)),
            pl.BlockSpec((tm, d), lambda i: (i, 0)),
            pl.BlockSpec((tm, d), lambda i: (i, 1)),
            pl.BlockSpec((tm, d), lambda i: (i, 0)),
            pl.BlockSpec((fw, d), lambda i: (0, 0), **const),
            pl.BlockSpec((dw, d), lambda i: (0, 0), **const),
            pl.BlockSpec((d, d), lambda i: (0, 0), **const),
            pl.BlockSpec((1, d), lambda i: (0, 0)),
            pl.BlockSpec((d, LANES), lambda i: (0, 0), **const),
            pl.BlockSpec((1, LANES), lambda i: (0, 0)),
        ],
        out_specs=[
            pl.BlockSpec((tm, d), lambda i: (i, 0)),
            pl.BlockSpec((tm, d // 2), lambda i: (i, 0)),
            pl.BlockSpec((tm, LANES), lambda i: (i, 0)),
            pl.BlockSpec((tm, LANES), lambda i: (i, 0)),
        ],
        out_shape=[
            jax.ShapeDtypeStruct((t, d), F32),
            jax.ShapeDtypeStruct((t, d // 2), jnp.uint32),
            jax.ShapeDtypeStruct((t, LANES), jnp.int32),
            jax.ShapeDtypeStruct((t, LANES), F32),
        ],
        compiler_params=_cparams(("parallel",)),
    )(yf, yd, gate, gate, x2, wbf, wbd, wo, n2, rw, rb)


def _dispatch_kernel(cnt_ref, pstart_ref, pend_ref, roa_ref, u_ref, xs_hbm, stage, zeros, sems,
                     *, tb, sub, n_rows):
    s = pl.program_id(0)
    n = pl.num_programs(0)
    slot = s % 2
    n_dma = TOP_K * tb

    groups = n_dma // SUBLANES

    def wait_slot(sl):
        def body(g, carry):
            pltpu.make_async_copy(stage.at[sl, g], xs_hbm.at[pl.ds(0, SUBLANES), :], sems.at[sl]).wait()
            return carry
        lax.fori_loop(0, groups, body, 0)

    def zero_row(r):
        return pltpu.make_async_copy(zeros.at[pl.ds(0, 1), :], xs_hbm.at[pl.ds(r, 1), :], sems.at[2])

    def zero_block(r):
        return pltpu.make_async_copy(zeros, xs_hbm.at[pl.ds(pl.multiple_of(r, sub), sub), :], sems.at[2])

    def zero_fill(start):
        def fill(first, stop, copy, step):
            def body(q, carry):
                if start:
                    copy(first + q * step).start()
                else:
                    copy(first).wait()
                return carry
            lax.fori_loop(0, (stop - first) // step, body, 0)

        def per_expert(e, carry):
            first = pstart_ref[e] + cnt_ref[e]
            edge = (first + sub - 1) // sub * sub
            fill(first, edge, zero_row, 1)
            fill(edge, pend_ref[e], zero_block, sub)
            return carry
        lax.fori_loop(0, N_EXPERTS, per_expert, 0)
        fill(pend_ref[N_EXPERTS - 1], n_rows, zero_block, sub)

    @pl.when(s == 0)
    def _():
        zeros[...] = jnp.zeros(zeros.shape, zeros.dtype)

    @pl.when(s == jnp.minimum(1, n - 1))
    def _():
        zero_fill(start=True)

    @pl.when(s >= 2)
    def _():
        wait_slot(slot)

    blk = u_ref[...].reshape(tb // SUBLANES, SUBLANES, u_ref.shape[1])
    for k in range(TOP_K):
        stage[slot, pl.ds(k * (tb // SUBLANES), tb // SUBLANES)] = blk

    def body(g, carry):
        for r in range(SUBLANES):
            q = g * SUBLANES + r
            pltpu.make_async_copy(stage.at[slot, g, pl.ds(r, 1), :],
                                  xs_hbm.at[pl.ds(roa_ref[0, 0, q], 1), :], sems.at[slot]).start()
        return carry
    lax.fori_loop(0, groups, body, 0)

    @pl.when(s == n - 1)
    def _():
        @pl.when(n >= 2)
        def _():
            wait_slot(1 - slot)
        wait_slot(slot)
        zero_fill(start=False)


def _dispatch(counts, pad_start, pad_end, roa3, u_packed, n_rows, tb, sub):
    t, w = u_packed.shape
    kern = functools.partial(_dispatch_kernel, tb=tb, sub=sub, n_rows=n_rows)
    return pl.pallas_call(
        kern,
        grid_spec=pltpu.PrefetchScalarGridSpec(
            num_scalar_prefetch=3,
            grid=(t // tb,),
            in_specs=[
                pl.BlockSpec((1, 1, TOP_K * tb), lambda s, c, ps, pe: (s, 0, 0), memory_space=pltpu.SMEM),
                pl.BlockSpec((tb, w), lambda s, c, ps, pe: (s, 0)),
            ],
            out_specs=pl.BlockSpec(memory_space=pl.ANY),
            scratch_shapes=[
                pltpu.VMEM((2, TOP_K * tb // SUBLANES, SUBLANES, w), jnp.uint32),
                pltpu.VMEM((sub, w), jnp.uint32),
                pltpu.SemaphoreType.DMA((3,)),
            ],
        ),
        out_shape=jax.ShapeDtypeStruct((n_rows, w), jnp.uint32),
        compiler_params=_cparams(("arbitrary",)),
    )(counts, pad_start, pad_end, roa3, u_packed)


def _expert_kernel(te_ref, ts_ref, nch_ref, xs_ref, wg_ref, wu_ref, bg_ref, bu_ref, wd_ref,
                   bd_ref, out_ref, x_sc, act_sc, *, chunks, nf):
    i = pl.program_id(0)
    j = pl.program_id(1)
    nch = nch_ref[i]
    half = x_sc.shape[1] // 2

    @pl.when(jnp.logical_and(j == 0, nch > 0))
    def _():
        lo, hi = _unpack_bf16_pairs(xs_ref[...])
        x_sc[:, :half] = lo
        x_sc[:, half:] = hi

    full = nch == len(chunks)

    def gate_up(rs, wg, wu):
        x = x_sc[rs, :]
        hg = jnp.dot(x, wg, preferred_element_type=F32) + bg_ref[0]
        hu = jnp.dot(x, wu, preferred_element_type=F32) + bu_ref[0]
        hg = jnp.minimum(hg, SWIGLU_LIMIT)
        hl = jnp.clip(hu, -SWIGLU_LIMIT, SWIGLU_LIMIT)
        act = hg * jax.nn.sigmoid(SWIGLU_ALPHA * hg) * (hl + 1.0)
        act_sc[j, rs, :] = act.astype(BF16)

    def down(rs, wd):
        act = jnp.concatenate([act_sc[c, rs, :] for c in range(nf)], axis=1)
        y = jnp.dot(act, wd, preferred_element_type=F32) + bd_ref[0]
        out_ref[rs, :] = _pack_bf16_pairs(y.astype(BF16))

    partial_tile = jnp.logical_and(nch > 0, nch < len(chunks))
    row_slices = [slice(a, b) for a, b in chunks]

    @pl.when(jnp.logical_and(j < nf, full))
    def _():
        gate_up(slice(None), wg_ref[0].astype(BF16), wu_ref[0].astype(BF16))

    @pl.when(jnp.logical_and(j < nf, partial_tile))
    def _():
        wg = wg_ref[0].astype(BF16)
        wu = wu_ref[0].astype(BF16)
        for c, rs in enumerate(row_slices[:-1]):
            pl.when(c < nch)(functools.partial(gate_up, rs, wg, wu))

    @pl.when(jnp.logical_and(j >= nf, full))
    def _():
        down(slice(None), wd_ref[0].astype(BF16))

    @pl.when(jnp.logical_and(j >= nf, partial_tile))
    def _():
        wd = wd_ref[0].astype(BF16)
        for c, rs in enumerate(row_slices[:-1]):
            pl.when(c < nch)(functools.partial(down, rs, wd))

    @pl.when(jnp.logical_and(j >= nf, nch < len(chunks)))
    def _():
        for c, (a, b) in enumerate(chunks):
            @pl.when(c >= nch)
            def _():
                out_ref[a:b, :] = jnp.zeros((b - a, out_ref.shape[1]), out_ref.dtype)


def _experts(tile_e, tile_src, tile_nch, xs, w_gu, b_gu3, w_down, b_down3, tm_e, chunks, tf):
    n_rows = xs.shape[0]
    n_tiles = n_rows // tm_e
    d = 2 * xs.shape[1]
    dff = w_down.shape[1]
    nf = dff // tf
    nn = d // tf
    assert nn == nf
    kern = functools.partial(_expert_kernel, chunks=chunks, nf=nf)

    def c1(i, j, ns):
        return jnp.where(ns[i] > 0, jnp.minimum(j, nf - 1), nf - 1)

    def c2(i, j, ns):
        return jnp.where(ns[i] > 0, jnp.maximum(j - nf, 0), nf - 1)

    return pl.pallas_call(
        kern,
        grid_spec=pltpu.PrefetchScalarGridSpec(
            num_scalar_prefetch=3,
            grid=(n_tiles, 2 * nf),
            in_specs=[
                pl.BlockSpec((tm_e, d // 2), lambda i, j, te, ts, ns: (ts[i], 0)),
                pl.BlockSpec((1, d, tf), lambda i, j, te, ts, ns: (te[i], 0, c1(i, j, ns))),
                pl.BlockSpec((1, d, tf), lambda i, j, te, ts, ns: (te[i], 0, nf + c1(i, j, ns))),
                pl.BlockSpec((1, 1, tf), lambda i, j, te, ts, ns: (te[i], 0, c1(i, j, ns))),
                pl.BlockSpec((1, 1, tf), lambda i, j, te, ts, ns: (te[i], 0, nf + c1(i, j, ns))),
                pl.BlockSpec((1, dff, tf), lambda i, j, te, ts, ns: (te[i], 0, c2(i, j, ns))),
                pl.BlockSpec((1, 1, tf), lambda i, j, te, ts, ns: (te[i], 0, c2(i, j, ns))),
            ],
            out_specs=pl.BlockSpec((tm_e, tf // 2), lambda i, j, te, ts, ns: (i, jnp.maximum(j - nf, 0))),
            scratch_shapes=[
                pltpu.VMEM((tm_e, d), BF16),
                pltpu.VMEM((nf, tm_e, tf), BF16),
            ],
        ),
        out_shape=jax.ShapeDtypeStruct((n_rows, d // 2), jnp.uint32),
        compiler_params=_cparams(("arbitrary", "arbitrary")),
    )(tile_e, tile_src, tile_nch, xs, w_gu, w_gu, b_gu3, b_gu3, w_down, b_down3)


def _combine_kernel(cur_ref, nxt_ref, yr_hbm, h_ref, gw_ref, gf_ref, out_ref, ysel, sems, *, tmc,
                    chunk, final_norm):
    i = pl.program_id(0)
    n = pl.num_programs(0)
    slot = i % 2
    groups = TOP_K * tmc // SUBLANES
    per_k = tmc // SUBLANES

    def issue(idx_ref, sl):
        def body(g, carry):
            for r in range(SUBLANES):
                pltpu.make_async_copy(yr_hbm.at[pl.ds(idx_ref[0, 0, g * SUBLANES + r], 1), :],
                                      ysel.at[sl, g, pl.ds(r, 1), :], sems.at[sl]).start()
            return carry
        lax.fori_loop(0, groups, body, 0)

    @pl.when(i == 0)
    def _():
        issue(cur_ref, 0)

    @pl.when(i + 1 < n)
    def _():
        issue(nxt_ref, 1 - slot)

    def wait_group(g, carry):
        pltpu.make_async_copy(yr_hbm.at[pl.ds(0, SUBLANES), :], ysel.at[slot, g], sems.at[slot]).wait()
        return carry
    lax.fori_loop(0, groups, wait_group, 0)

    gw = gw_ref[...]
    y = h_ref[...]
    half = chunk // 2
    for k in range(TOP_K):
        rows_k = ysel[slot, k * per_k:(k + 1) * per_k].reshape(tmc, ysel.shape[-1])
        lo, hi = _unpack_bf16_pairs(rows_k)
        pieces = []
        for c in range(lo.shape[1] // half):
            pieces += [lo[:, c * half:(c + 1) * half], hi[:, c * half:(c + 1) * half]]
        y = y + gw[:, k:k + 1] * jnp.concatenate(pieces, axis=1).astype(F32)
    if final_norm:
        ms = jnp.mean(y * y, axis=-1, keepdims=True)
        y = (y * lax.rsqrt(ms + EPS)) * gf_ref[...]
    out_ref[...] = y


def _combine(roa3, y_rows, h, gw, gf, tmc, chunk, final_norm):
    t, d = h.shape
    n = t // tmc
    kern = functools.partial(_combine_kernel, tmc=tmc, chunk=chunk, final_norm=final_norm)
    return pl.pallas_call(
        kern,
        grid=(n,),
        in_specs=[
            pl.BlockSpec((1, 1, TOP_K * tmc), lambda i: (i, 0, 0), memory_space=pltpu.SMEM),
            pl.BlockSpec((1, 1, TOP_K * tmc), lambda i: (jnp.minimum(i + 1, n - 1), 0, 0),
                         memory_space=pltpu.SMEM),
            pl.BlockSpec(memory_space=pl.ANY),
            pl.BlockSpec((tmc, d), lambda i: (i, 0)),
            pl.BlockSpec((tmc, LANES), lambda i: (i, 0)),
            pl.BlockSpec((1, d), lambda i: (0, 0)),
        ],
        out_specs=pl.BlockSpec((tmc, d), lambda i: (i, 0)),
        out_shape=jax.ShapeDtypeStruct((t, d), F32),
        scratch_shapes=[pltpu.VMEM((2, TOP_K * tmc // SUBLANES, SUBLANES, d // 2), jnp.uint32),
                        pltpu.SemaphoreType.DMA((2,))],
        compiler_params=_cparams(("arbitrary",)),
    )(roa3, roa3, y_rows, h, gw, gf)


def _routing(top_idx, tm_e, chunks):
    t = top_idx.shape[0]
    a = t * TOP_K
    flat_e = top_idx.reshape(a)
    onehot = (flat_e[:, None] == jnp.arange(N_EXPERTS, dtype=jnp.int32)[None, :]).astype(jnp.int32)
    csum = jnp.cumsum(onehot, axis=0)
    counts = csum[-1]
    rank = jnp.sum(onehot * csum, axis=1) - 1
    padded = (counts + tm_e - 1) // tm_e * tm_e
    pad_end = jnp.cumsum(padded)
    pad_start = pad_end - padded
    row_of_assign = (pad_start[flat_e] + rank).astype(jnp.int32)
    n_tiles = -(-a // tm_e) + N_EXPERTS
    tile_start = jnp.arange(n_tiles, dtype=jnp.int32) * tm_e
    n_used = pad_end[-1] // tm_e
    tile_e_raw = jnp.minimum(jnp.searchsorted(pad_end, tile_start, side='right'), N_EXPERTS - 1)
    used = tile_start < pad_end[-1]
    last = jnp.maximum(n_used - 1, 0)
    tile_src = jnp.where(used, jnp.arange(n_tiles, dtype=jnp.int32), last).astype(jnp.int32)
    tile_e = tile_e_raw[tile_src].astype(jnp.int32)
    valid_rows = jnp.clip(counts[tile_e_raw] - (tile_start - pad_start[tile_e_raw]), 0, tm_e)
    starts = jnp.asarray([a for a, _ in chunks], jnp.int32)
    tile_nch = jnp.sum(starts[None, :] < jnp.where(used, valid_rows, 0)[:, None], axis=1).astype(jnp.int32)
    bounds = (counts.astype(jnp.int32), pad_start.astype(jnp.int32), pad_end.astype(jnp.int32))
    return bounds, row_of_assign, tile_e, tile_src, tile_nch, n_tiles * tm_e


def kernel(x, positions, norm1_g, w_in, b_fgate, b_gate, lam_q1, lam_k1, lam_q2, lam_k2, subln_g,
           w_branch_fox, w_branch_diff, w_out, norm2_g, router_w, router_b, w_gu, b_gu, w_down,
           b_down, normf_g):
    b, s, d = x.shape
    t = b * s
    depth = norm1_g.shape[0]
    fox_w = FOX_HEADS * HEAD_DIM
    tm_a = _pick(t, 1024)
    tq = _pick(s, 512)
    tm_d = _pick(t, 256)
    tm_e, chunk, zsub, tf = 1152, 512, 128, 512
    chunks = tuple((a, min(a + chunk, tm_e)) for a in range(0, tm_e, chunk))
    tmc = _pick(t, 512)

    half = ROPE_DIM // 2
    inv_freq = ROPE_THETA ** (-jnp.arange(0, ROPE_DIM, 2, dtype=F32) / ROPE_DIM)
    zeros = jnp.zeros((LANES - ROPE_DIM,), F32)
    rope_tab = jnp.zeros((8, LANES), F32)
    rope_tab = rope_tab.at[0].set(jnp.concatenate([inv_freq, inv_freq, zeros]))
    rope_tab = rope_tab.at[1].set(jnp.concatenate([-jnp.ones((half,), F32), jnp.zeros((half,), F32), zeros]))
    rope_tab = rope_tab.at[2].set(jnp.concatenate([jnp.zeros((half,), F32), jnp.ones((half,), F32), zeros]))
    pos = positions.astype(F32).reshape(t, 1)

    h = x.reshape(t, d)
    for l in range(depth):
        wi = w_in[l]
        w_left = wi[:, :3 * fox_w].astype(BF16)
        w_right = wi[:, 3 * fox_w + FOX_HEADS:].astype(BF16)
        w_f = jnp.pad(wi[:, 3 * fox_w:3 * fox_w + FOX_HEADS], ((0, 0), (0, LANES - FOX_HEADS))).astype(BF16)
        qkv, gate, flog = _inproj(h, pos, norm1_g[l][None, :], w_left, w_right, w_f, rope_tab,
                                  b_gate[l][None, :], tm_a)

        fl = flog[:, :FOX_HEADS].reshape(b, s, FOX_HEADS).transpose(0, 2, 1).reshape(b * FOX_HEADS, s)
        bf = jnp.tile(b_fgate[l].astype(F32), b).reshape(b * FOX_HEADS, 1)
        c3 = _fgate(fl, bf).reshape(b * FOX_HEADS, 1, s)

        y_fox = _fox_attn(qkv, c3, b, s, tq)
        lam_init = 0.8 - 0.6 * math.exp(-0.3 * l)
        lam_p = jnp.stack([lam_q1[l], lam_k1[l], lam_q2[l], lam_k2[l]]).astype(F32)
        y_diff = _diff_attn(qkv, lam_p, subln_g[l][None, :].astype(F32), b, s, tq, lam_init)

        rw = jnp.pad(router_w[l], ((0, 0), (0, LANES - N_EXPERTS))).astype(BF16)
        rb = jnp.concatenate([router_b[l].astype(F32), jnp.full((LANES - N_EXPERTS,), NEG_BIG, F32)])[None, :]
        h, u2, idx128, gw128 = _merge(y_fox, y_diff, gate, h, w_branch_fox[l].astype(BF16),
                                      w_branch_diff[l].astype(BF16), w_out[l].astype(BF16),
                                      norm2_g[l][None, :], rw, rb, tm_d)

        bounds, roa, tile_e, tile_src, tile_nch, n_rows = _routing(idx128[:, :TOP_K], tm_e, chunks)
        roa3 = roa.reshape(t // tmc, tmc, TOP_K).transpose(0, 2, 1).reshape(t // tmc, 1, TOP_K * tmc)
        xs = _dispatch(*bounds, roa3, u2, n_rows, tmc, zsub)
        y_rows = _experts(tile_e, tile_src, tile_nch, xs, w_gu[l], b_gu[l][:, None, :],
                          w_down[l], b_down[l][:, None, :], tm_e, chunks, tf)
        h = _combine(roa3, y_rows, h, gw128, normf_g[None, :], tmc, tf, final_norm=(l == depth - 1))
    return h.reshape(b, s, d)
```

```python
import functools
import math

import jax
import jax.numpy as jnp
from jax import lax
from jax.experimental import pallas as pl
from jax.experimental.pallas import tpu as pltpu

HEAD_DIM = 128
FOX_HEADS = 8
DIFF_HEADS = 4
DIFF_V_DIM = 2 * HEAD_DIM
ROPE_THETA = 500000.0
ROPE_DIM = HEAD_DIM // 4
N_EXPERTS = 32
TOP_K = 4
SWIGLU_ALPHA = 1.702
SWIGLU_LIMIT = 7.0
EPS = 1e-5
LANES = 128

F32 = jnp.float32
BF16 = jnp.bfloat16
NEG_BIG = -1e30
LOG2E = math.log2(math.e)
SUBLANES = 8
VMEM_LIMIT = 56 * 1024 * 1024


def _cparams(sem, vmem=VMEM_LIMIT):
    return pltpu.CompilerParams(dimension_semantics=sem, vmem_limit_bytes=vmem)


def _pick(n, pref):
    t = min(n, pref)
    while n % t:
        t //= 2
    return t


QKV_TILE = 1024
QKV_LEFT_TILES = 3
QKV_ROTARY_TILES = (3, 4)
QKV_SCALED_TILES = (0, 3)


def _qkv_kernel(x_ref, g1_ref, wa_ref, wb_ref, wf_ref, pos_ref, tab_ref,
                qkv_ref, u_ref, flog_ref, cos_sc, sa_sc, sb_sc, *, scale):
    j = pl.program_id(1)

    @pl.when(j == 0)
    def _():
        xf = x_ref[...]
        ms = jnp.mean(xf * xf, axis=-1, keepdims=True)
        u = (xf * lax.rsqrt(ms + EPS)) * g1_ref[...]
        ub = u.astype(BF16)
        u_ref[...] = ub
        flog_ref[...] = jnp.dot(ub, wf_ref[...], preferred_element_type=F32)
        ang = pos_ref[...] * tab_ref[0:1, :]
        sn = jnp.sin(ang)
        cos_sc[...] = jnp.cos(ang)
        sa_sc[...] = sn * tab_ref[1:2, :]
        sb_sc[...] = sn * tab_ref[2:3, :]

    any_of = lambda tiles: functools.reduce(jnp.logical_or, [j == q for q in tiles])
    mult = jnp.where(any_of(QKV_SCALED_TILES), scale, 1.0).astype(F32)
    rotary = any_of(QKV_ROTARY_TILES)
    left = j < QKV_LEFT_TILES

    def plain(w_ref):
        acc = jnp.dot(u_ref[...], w_ref[...], preferred_element_type=F32)
        qkv_ref[...] = (acc * mult).astype(BF16)

    pl.when(left)(functools.partial(plain, wa_ref))
    pl.when(jnp.logical_not(jnp.logical_or(left, rotary)))(functools.partial(plain, wb_ref))

    @pl.when(rotary)
    def _():
        acc = jnp.dot(u_ref[...], wb_ref[...], preferred_element_type=F32)
        ca = cos_sc[...] * mult
        cb = sa_sc[...] * mult
        cc = sb_sc[...] * mult
        for c in range(acc.shape[1] // HEAD_DIM):
            t = acc[:, c * HEAD_DIM:(c + 1) * HEAD_DIM]
            r = (t * ca + pltpu.roll(t, HEAD_DIM - ROPE_DIM // 2, 1) * cb
                 + pltpu.roll(t, ROPE_DIM // 2, 1) * cc)
            qkv_ref[:, c * HEAD_DIM:(c + 1) * HEAD_DIM] = r.astype(BF16)


def _gates_kernel(u_ref, w_ref, bg_ref, gate_ref):
    acc = jnp.dot(u_ref[...], w_ref[...], preferred_element_type=F32)
    gate_ref[...] = jax.nn.sigmoid(acc + bg_ref[...])


def _inproj(x2, pos, g1, w_left, w_right, w_f, rope_tab, b_gate, tm):
    t, d = x2.shape
    tn = QKV_TILE
    n_qkv = 6
    n_left = QKV_LEFT_TILES
    n_gate = w_right.shape[1] // tn - (n_qkv - n_left)
    kern = functools.partial(_qkv_kernel, scale=LOG2E / math.sqrt(HEAD_DIM))
    qkv, u, flog = pl.pallas_call(
        kern,
        grid=(t // tm, n_qkv),
        in_specs=[
            pl.BlockSpec((tm, d), lambda i, j: (i, 0)),
            pl.BlockSpec((1, d), lambda i, j: (0, 0)),
            pl.BlockSpec((d, tn), lambda i, j: (0, jnp.minimum(j, n_left - 1))),
            pl.BlockSpec((d, tn), lambda i, j: (0, jnp.maximum(j - n_left, 0))),
            pl.BlockSpec((d, LANES), lambda i, j: (0, 0)),
            pl.BlockSpec((tm, 1), lambda i, j: (i, 0)),
            pl.BlockSpec((8, LANES), lambda i, j: (0, 0)),
        ],
        out_specs=[
            pl.BlockSpec((tm, tn), lambda i, j: (i, j)),
            pl.BlockSpec((tm, d), lambda i, j: (i, 0)),
            pl.BlockSpec((tm, LANES), lambda i, j: (i, 0)),
        ],
        out_shape=[
            jax.ShapeDtypeStruct((t, n_qkv * tn), BF16),
            jax.ShapeDtypeStruct((t, d), BF16),
            jax.ShapeDtypeStruct((t, LANES), F32),
        ],
        scratch_shapes=[
            pltpu.VMEM((tm, LANES), F32),
            pltpu.VMEM((tm, LANES), F32),
            pltpu.VMEM((tm, LANES), F32),
        ],
        compiler_params=_cparams(("parallel", "arbitrary")),
    )(x2, g1, w_left, w_right, w_f, pos, rope_tab)
    gate = pl.pallas_call(
        _gates_kernel,
        grid=(t // tm, n_gate),
        in_specs=[
            pl.BlockSpec((tm, d), lambda i, j: (i, 0)),
            pl.BlockSpec((d, tn), lambda i, j: (0, n_qkv - n_left + j)),
            pl.BlockSpec((1, tn), lambda i, j: (0, j)),
        ],
        out_specs=pl.BlockSpec((tm, tn), lambda i, j: (i, j)),
        out_shape=jax.ShapeDtypeStruct((t, n_gate * tn), F32),
        compiler_params=_cparams(("parallel", "arbitrary")),
    )(u, w_right, b_gate)
    return qkv, gate, flog


def _fgate_kernel(fl_ref, b_ref, c_ref):
    z = fl_ref[...] + b_ref[...]
    x = jnp.minimum(z, 0.0) - jnp.log(1.0 + jnp.exp(-jnp.abs(z)))
    n = x.shape[1]
    lane = lax.broadcasted_iota(jnp.int32, x.shape, 1)
    sh = 1
    while sh < n:
        x = x + jnp.where(lane >= sh, pltpu.roll(x, sh, 1), 0.0)
        sh *= 2
    c_ref[...] = x


def _fgate(fl, b):
    r, s = fl.shape
    return pl.pallas_call(
        _fgate_kernel,
        grid=(1,),
        in_specs=[pl.BlockSpec((r, s), lambda i: (0, 0)), pl.BlockSpec((r, 1), lambda i: (0, 0))],
        out_specs=pl.BlockSpec((r, s), lambda i: (0, 0)),
        out_shape=jax.ShapeDtypeStruct((r, s), F32),
        compiler_params=_cparams(("arbitrary",)),
    )(fl, b)


def _transpose_bf16(x):
    return x.astype(F32).T.astype(BF16)


def _softmax_step(s, vt, m_sc, l_sc, acc_sc):
    m_old = m_sc[...]
    m_new = jnp.maximum(m_old, jnp.max(s, axis=0, keepdims=True))
    alpha = jnp.exp2(m_old - m_new)
    p = jnp.exp2(s - m_new)
    l_sc[...] = alpha * l_sc[...] + jnp.sum(p, axis=0, keepdims=True)
    acc_sc[...] = alpha * acc_sc[...] + jnp.dot(vt, p.astype(BF16), preferred_element_type=F32)
    m_sc[...] = m_new


def _causal_keep(shape):
    kv = lax.broadcasted_iota(jnp.int32, shape, 0)
    qq = lax.broadcasted_iota(jnp.int32, shape, 1)
    return kv <= qq


FOX_HEADS_PER_STEP = 4


def _fox_kernel(q_ref, k_ref, v_ref, c_ref, o_ref, *scratch, tq, tk, s_len):
    hp = FOX_HEADS_PER_STEP
    qt_sc, kaug_sc, vt_sc, m_sc, l_sc, acc_sc = (scratch[n * hp:(n + 1) * hp] for n in range(6))
    s_sc = scratch[6 * hp:]
    qi = pl.program_id(2)
    head_cols = [slice(h * HEAD_DIM, (h + 1) * HEAD_DIM) for h in range(hp)]

    @pl.when(qi == 0)
    def _():
        sub = lax.broadcasted_iota(jnp.int32, (HEAD_DIM, tk), 0)
        for h in range(hp):
            for ch in range(s_len // tk):
                rows = slice(ch * tk, (ch + 1) * tk)
                vt_sc[h][:, rows] = _transpose_bf16(v_ref[rows, head_cols[h]])
                c = c_ref[h, :, rows] * (-LOG2E)
                hi = c.astype(BF16).astype(F32)
                mid = (c - hi).astype(BF16).astype(F32)
                lo = (c - hi - mid).astype(BF16).astype(F32)
                tab = jnp.where(sub == 0, hi, jnp.where(sub == 1, mid, jnp.where(sub == 2, lo, 0.0)))
                kaug_sc[h][rows, HEAD_DIM:] = tab.T.astype(BF16)
                kaug_sc[h][rows, :HEAD_DIM] = k_ref[rows, head_cols[h]]

    sub_q = lax.broadcasted_iota(jnp.int32, (HEAD_DIM, tq), 0)
    for h in range(hp):
        qt_sc[h][:HEAD_DIM, :] = _transpose_bf16(q_ref[:, head_cols[h]])
        qt_sc[h][HEAD_DIM:, :] = jnp.where(sub_q < 3, 1.0, 0.0).astype(BF16)
        m_sc[h][...] = jnp.full(m_sc[h].shape, -jnp.inf, F32)
        l_sc[h][...] = jnp.zeros(l_sc[h].shape, F32)
        acc_sc[h][...] = jnp.zeros(acc_sc[h].shape, F32)

    def scores(h, chunk):
        start = pl.multiple_of(chunk * tk, tk)
        return jnp.dot(kaug_sc[h][pl.ds(start, tk), :], qt_sc[h][...], preferred_element_type=F32)

    def consume(h, chunk, par, masked):
        s = s_sc[par * hp + h][...]
        if masked:
            s = jnp.where(_causal_keep(s.shape), s, -jnp.inf)
        start = pl.multiple_of(chunk * tk, tk)
        _softmax_step(s, vt_sc[h][:, pl.ds(start, tk)], m_sc[h], l_sc[h], acc_sc[h])

    for h in range(hp):
        s_sc[h][...] = scores(h, 0)

    def body(kc, carry):
        def run(par):
            for h in range(hp):
                s_sc[(1 - par) * hp + h][...] = scores(h, kc + 1)
                consume(h, kc, par, False)
        pl.when(kc % 2 == 0)(functools.partial(run, 0))
        pl.when(kc % 2 == 1)(functools.partial(run, 1))
        return carry

    lax.fori_loop(0, qi, body, 0)

    def diagonal(par):
        for h in range(hp):
            consume(h, qi, par, True)
    pl.when(qi % 2 == 0)(functools.partial(diagonal, 0))
    pl.when(qi % 2 == 1)(functools.partial(diagonal, 1))
    for h in range(hp):
        o_ref[:, head_cols[h]] = (acc_sc[h][...] / l_sc[h][...]).T.astype(o_ref.dtype)


def _fox_attn(qkv, c3, b, s, tq):
    t = b * s
    nq = s // tq
    hp = FOX_HEADS_PER_STEP
    groups = FOX_HEADS // hp
    width = hp * HEAD_DIM
    kern = functools.partial(_fox_kernel, tq=tq, tk=tq, s_len=s)

    def per_head(shape, dtype):
        return [pltpu.VMEM(shape, dtype) for _ in range(hp)]

    return pl.pallas_call(
        kern,
        grid=(b, groups, nq),
        in_specs=[
            pl.BlockSpec((tq, width), lambda bi, g, qi: (bi * nq + qi, g)),
            pl.BlockSpec((s, width), lambda bi, g, qi: (bi, groups + g)),
            pl.BlockSpec((s, width), lambda bi, g, qi: (bi, 2 * groups + g)),
            pl.BlockSpec((hp, 1, s), lambda bi, g, qi: (bi * groups + g, 0, 0)),
        ],
        out_specs=pl.BlockSpec((tq, width), lambda bi, g, qi: (bi * nq + qi, g)),
        out_shape=jax.ShapeDtypeStruct((t, FOX_HEADS * HEAD_DIM), BF16),
        scratch_shapes=(per_head((2 * HEAD_DIM, tq), BF16) + per_head((s, 2 * HEAD_DIM), BF16)
                        + per_head((HEAD_DIM, s), BF16) + per_head((1, tq), F32)
                        + per_head((1, tq), F32) + per_head((HEAD_DIM, tq), F32)
                        + per_head((tq, tq), F32) + per_head((tq, tq), F32)),
        compiler_params=_cparams(("parallel", "parallel", "arbitrary")),
    )(qkv, qkv, qkv, c3)


DIFF_HEADS_PER_STEP = 2


def _diff_kernel(q_ref, k_ref, v_ref, lam_ref, g_ref, o_ref, *scratch, tq, tk, s_len, lam_init):
    hp = DIFF_HEADS_PER_STEP
    nc = 2 * hp
    qt_sc, m_sc, l_sc, a_sc = (scratch[n * nc:(n + 1) * nc] for n in range(4))
    vt_sc = scratch[4 * nc:4 * nc + hp]
    s_sc = scratch[4 * nc + hp:]
    qi = pl.program_id(2)
    chain_cols = [slice(c * HEAD_DIM, (c + 1) * HEAD_DIM) for c in range(nc)]
    head_cols = [slice(h * DIFF_V_DIM, (h + 1) * DIFF_V_DIM) for h in range(hp)]

    @pl.when(qi == 0)
    def _():
        for h in range(hp):
            for ch in range(s_len // tk):
                rows = slice(ch * tk, (ch + 1) * tk)
                vt_sc[h][:, rows] = _transpose_bf16(v_ref[rows, head_cols[h]])

    for c in range(nc):
        qt_sc[c][...] = _transpose_bf16(q_ref[:, chain_cols[c]])
        m_sc[c][...] = jnp.full(m_sc[c].shape, -jnp.inf, F32)
        l_sc[c][...] = jnp.zeros(l_sc[c].shape, F32)
        a_sc[c][...] = jnp.zeros(a_sc[c].shape, F32)

    def scores(c, chunk):
        start = pl.multiple_of(chunk * tk, tk)
        return jnp.dot(k_ref[pl.ds(start, tk), chain_cols[c]], qt_sc[c][...], preferred_element_type=F32)

    def consume(c, chunk, par, masked):
        s = s_sc[par * nc + c][...]
        if masked:
            s = jnp.where(_causal_keep(s.shape), s, -jnp.inf)
        start = pl.multiple_of(chunk * tk, tk)
        _softmax_step(s, vt_sc[c // 2][:, pl.ds(start, tk)], m_sc[c], l_sc[c], a_sc[c])

    for c in range(nc):
        s_sc[c][...] = scores(c, 0)

    def body(kc, carry):
        def run(par):
            for c in range(nc):
                s_sc[(1 - par) * nc + c][...] = scores(c, kc + 1)
                consume(c, kc, par, False)
        pl.when(kc % 2 == 0)(functools.partial(run, 0))
        pl.when(kc % 2 == 1)(functools.partial(run, 1))
        return carry

    lax.fori_loop(0, qi, body, 0)

    def diagonal(par):
        for c in range(nc):
            consume(c, qi, par, True)
    pl.when(qi % 2 == 0)(functools.partial(diagonal, 0))
    pl.when(qi % 2 == 1)(functools.partial(diagonal, 1))

    lp = lam_ref[...]
    lam = (jnp.exp(jnp.sum(lp[0:1, :] * lp[1:2, :], axis=-1, keepdims=True))
           - jnp.exp(jnp.sum(lp[2:3, :] * lp[3:4, :], axis=-1, keepdims=True)) + lam_init)
    for h in range(hp):
        c1, c2 = 2 * h, 2 * h + 1
        y = (a_sc[c1][...] / l_sc[c1][...]).T - lam * (a_sc[c2][...] / l_sc[c2][...]).T
        ms = jnp.mean(y * y, axis=-1, keepdims=True)
        y = (y * lax.rsqrt(ms + EPS)) * g_ref[...]
        o_ref[:, head_cols[h]] = (y * (1.0 - lam_init)).astype(o_ref.dtype)


def _diff_attn(qkv, lam_p, subln_g, b, s, tq, lam_init):
    t = b * s
    nq = s // tq
    hp = DIFF_HEADS_PER_STEP
    nc = 2 * hp
    groups = DIFF_HEADS // hp
    qk_w = nc * HEAD_DIM
    v_w = hp * DIFF_V_DIM
    q_blk = 3 * FOX_HEADS * HEAD_DIM // qk_w
    k_blk = q_blk + groups
    v_blk = (3 * FOX_HEADS * HEAD_DIM + 2 * groups * qk_w) // v_w
    kern = functools.partial(_diff_kernel, tq=tq, tk=tq, s_len=s, lam_init=lam_init)

    def per_chain(shape, dtype):
        return [pltpu.VMEM(shape, dtype) for _ in range(nc)]

    return pl.pallas_call(
        kern,
        grid=(b, groups, nq),
        in_specs=[
            pl.BlockSpec((tq, qk_w), lambda bi, g, qi: (bi * nq + qi, q_blk + g)),
            pl.BlockSpec((s, qk_w), lambda bi, g, qi: (bi, k_blk + g)),
            pl.BlockSpec((s, v_w), lambda bi, g, qi: (bi, v_blk + g)),
            pl.BlockSpec((4, HEAD_DIM), lambda bi, g, qi: (0, 0)),
            pl.BlockSpec((1, DIFF_V_DIM), lambda bi, g, qi: (0, 0)),
        ],
        out_specs=pl.BlockSpec((tq, v_w), lambda bi, g, qi: (bi * nq + qi, g)),
        out_shape=jax.ShapeDtypeStruct((t, DIFF_HEADS * DIFF_V_DIM), BF16),
        scratch_shapes=(per_chain((HEAD_DIM, tq), BF16) + per_chain((1, tq), F32) + per_chain((1, tq), F32)
                        + per_chain((DIFF_V_DIM, tq), F32)
                        + [pltpu.VMEM((DIFF_V_DIM, s), BF16) for _ in range(hp)]
                        + per_chain((tq, tq), F32) + per_chain((tq, tq), F32)),
        compiler_params=_cparams(("parallel", "parallel", "arbitrary")),
    )(qkv, qkv, qkv, lam_p, subln_g)


def _pack_bf16_pairs(xb):
    w = xb.shape[1] // 2
    lo = lax.bitcast_convert_type(xb[:, :w].astype(F32), jnp.uint32)
    hi = lax.bitcast_convert_type(xb[:, w:].astype(F32), jnp.uint32)
    return (lo >> 16) | (hi & jnp.uint32(0xFFFF0000))


def _unpack_bf16_pairs(p):
    lo = lax.bitcast_convert_type(p << 16, F32).astype(BF16)
    hi = lax.bitcast_convert_type(p & jnp.uint32(0xFFFF0000), F32).astype(BF16)
    return lo, hi


def _merge_kernel(yf_ref, yd_ref, g0_ref, g1_ref, x_ref, wbf_ref, wbd_ref, wo_ref, n2_ref,
                  rw_ref, rb_ref, h_ref, u_ref, idx_ref, gw_ref):
    ya = jnp.dot(yf_ref[...], wbf_ref[...], preferred_element_type=F32)
    yb = jnp.dot(yd_ref[...], wbd_ref[...], preferred_element_type=F32)
    merged = g0_ref[...] * ya + g1_ref[...] * yb
    h = x_ref[...] + jnp.dot(merged.astype(BF16), wo_ref[...], preferred_element_type=F32)
    h_ref[...] = h
    ms = jnp.mean(h * h, axis=-1, keepdims=True)
    u = (h * lax.rsqrt(ms + EPS)) * n2_ref[...]
    ub = u.astype(BF16)
    u_ref[...] = _pack_bf16_pairs(ub)
    logits = jnp.dot(ub, rw_ref[...], preferred_element_type=F32) + rb_ref[...]

    lane = lax.broadcasted_iota(jnp.int32, logits.shape, 1).astype(F32)
    work = logits
    vals, idxs = [], []
    for _ in range(TOP_K):
        m = jnp.max(work, axis=-1, keepdims=True)
        sel = jnp.min(jnp.where(work == m, lane, float(LANES)), axis=-1, keepdims=True)
        vals.append(m)
        idxs.append(sel)
        work = jnp.where(lane == sel, -jnp.inf, work)
    es = [jnp.exp(v - vals[0]) for v in vals]
    den = es[0]
    for e in es[1:]:
        den = den + e
    idx_out = jnp.zeros(logits.shape, F32)
    gw_out = jnp.zeros(logits.shape, F32)
    for k in range(TOP_K):
        idx_out = jnp.where(lane == float(k), idxs[k], idx_out)
        gw_out = jnp.where(lane == float(k), es[k] / den, gw_out)
    idx_ref[...] = idx_out.astype(jnp.int32)
    gw_ref[...] = gw_out


def _merge(yf, yd, gate, x2, wbf, wbd, wo, n2, rw, rb, tm):
    t, d = x2.shape
    fw = yf.shape[1]
    dw = yd.shape[1]
    const = dict(pipeline_mode=pl.Buffered(1))
    return pl.pallas_call(
        _merge_kernel,
        grid=(t // tm,),
        in_specs=[
            pl.BlockSpec((tm, fw), lambda i: (i, 0)),
            pl.BlockSpec((tm, dw), lambda i: (i, 0)),
            pl.BlockSpec((tm, d), lambda i: (i, 0)),
            pl.BlockSpec((tm, d), lambda i: (i, 1)),
            pl.BlockSpec((tm, d), lambda i: (i, 0)),
            pl.BlockSpec((fw, d), lambda i: (0, 0), **const),
            pl.BlockSpec((dw, d), lambda i: (0, 0), **const),
            pl.BlockSpec((d, d), lambda i: (0, 0), **const),
            pl.BlockSpec((1, d), lambda i: (0, 0)),
            pl.BlockSpec((d, LANES), lambda i: (0, 0), **const),
            pl.BlockSpec((1, LANES), lambda i: (0, 0)),
        ],
        out_specs=[
            pl.BlockSpec((tm, d), lambda i: (i, 0)),
            pl.BlockSpec((tm, d // 2), lambda i: (i, 0)),
            pl.BlockSpec((tm, LANES), lambda i: (i, 0)),
            pl.BlockSpec((tm, LANES), lambda i: (i, 0)),
        ],
        out_shape=[
            jax.ShapeDtypeStruct((t, d), F32),
            jax.ShapeDtypeStruct((t, d // 2), jnp.uint32),
            jax.ShapeDtypeStruct((t, LANES), jnp.int32),
            jax.ShapeDtypeStruct((t, LANES), F32),
        ],
        compiler_params=_cparams(("parallel",)),
    )(yf, yd, gate, gate, x2, wbf, wbd, wo, n2, rw, rb)


def _dispatch_kernel(cnt_ref, pstart_ref, pend_ref, roa_ref, u_ref, xs_hbm, stage, zeros, sems,
                     *, tb, sub, n_rows):
    s = pl.program_id(0)
    n = pl.num_programs(0)
    slot = s % 2
    n_dma = TOP_K * tb

    groups = n_dma // SUBLANES

    def wait_slot(sl):
        def body(g, carry):
            pltpu.make_async_copy(stage.at[sl, g], xs_hbm.at[pl.ds(0, SUBLANES), :], sems.at[sl]).wait()
            return carry
        lax.fori_loop(0, groups, body, 0)

    def zero_row(r):
        return pltpu.make_async_copy(zeros.at[pl.ds(0, 1), :], xs_hbm.at[pl.ds(r, 1), :], sems.at[2])

    def zero_block(r):
        return pltpu.make_async_copy(zeros, xs_hbm.at[pl.ds(pl.multiple_of(r, sub), sub), :], sems.at[2])

    def zero_fill(start):
        def fill(first, stop, copy, step):
            def body(q, carry):
                if start:
                    copy(first + q * step).start()
                else:
                    copy(first).wait()
                return carry
            lax.fori_loop(0, (stop - first) // step, body, 0)

        def per_expert(e, carry):
            first = pstart_ref[e] + cnt_ref[e]
            edge = (first + sub - 1) // sub * sub
            fill(first, edge, zero_row, 1)
            fill(edge, pend_ref[e], zero_block, sub)
            return carry
        lax.fori_loop(0, N_EXPERTS, per_expert, 0)
        fill(pend_ref[N_EXPERTS - 1], n_rows, zero_block, sub)

    @pl.when(s == 0)
    def _():
        zeros[...] = jnp.zeros(zeros.shape, zeros.dtype)

    @pl.when(s == jnp.minimum(1, n - 1))
    def _():
        zero_fill(start=True)

    @pl.when(s >= 2)
    def _():
        wait_slot(slot)

    blk = u_ref[...].reshape(tb // SUBLANES, SUBLANES, u_ref.shape[1])
    for k in range(TOP_K):
        stage[slot, pl.ds(k * (tb // SUBLANES), tb // SUBLANES)] = blk

    def body(g, carry):
        for r in range(SUBLANES):
            q = g * SUBLANES + r
            pltpu.make_async_copy(stage.at[slot, g, pl.ds(r, 1), :],
                                  xs_hbm.at[pl.ds(roa_ref[0, 0, q], 1), :], sems.at[slot]).start()
        return carry
    lax.fori_loop(0, groups, body, 0)

    @pl.when(s == n - 1)
    def _():
        @pl.when(n >= 2)
        def _():
            wait_slot(1 - slot)
        wait_slot(slot)
        zero_fill(start=False)


def _dispatch(counts, pad_start, pad_end, roa3, u_packed, n_rows, tb, sub):
    t, w = u_packed.shape
    kern = functools.partial(_dispatch_kernel, tb=tb, sub=sub, n_rows=n_rows)
    return pl.pallas_call(
        kern,
        grid_spec=pltpu.PrefetchScalarGridSpec(
            num_scalar_prefetch=3,
            grid=(t // tb,),
            in_specs=[
                pl.BlockSpec((1, 1, TOP_K * tb), lambda s, c, ps, pe: (s, 0, 0), memory_space=pltpu.SMEM),
                pl.BlockSpec((tb, w), lambda s, c, ps, pe: (s, 0)),
            ],
            out_specs=pl.BlockSpec(memory_space=pl.ANY),
            scratch_shapes=[
                pltpu.VMEM((2, TOP_K * tb // SUBLANES, SUBLANES, w), jnp.uint32),
                pltpu.VMEM((sub, w), jnp.uint32),
                pltpu.SemaphoreType.DMA((3,)),
            ],
        ),
        out_shape=jax.ShapeDtypeStruct((n_rows, w), jnp.uint32),
        compiler_params=_cparams(("arbitrary",)),
    )(counts, pad_start, pad_end, roa3, u_packed)


def _expert_kernel(te_ref, ts_ref, nch_ref, xs_ref, wg_ref, wu_ref, bg_ref, bu_ref, wd_ref,
                   bd_ref, out_ref, x_sc, act_sc, *, chunks, nf):
    i = pl.program_id(0)
    j = pl.program_id(1)
    nch = nch_ref[i]
    half = x_sc.shape[1] // 2

    @pl.when(jnp.logical_and(j == 0, nch > 0))
    def _():
        lo, hi = _unpack_bf16_pairs(xs_ref[...])
        x_sc[:, :half] = lo
        x_sc[:, half:] = hi

    full = nch == len(chunks)

    def gate_up(rs, wg, wu):
        x = x_sc[rs, :]
        hg = jnp.dot(x, wg, preferred_element_type=F32) + bg_ref[0]
        hu = jnp.dot(x, wu, preferred_element_type=F32) + bu_ref[0]
        hg = jnp.minimum(hg, SWIGLU_LIMIT)
        hl = jnp.clip(hu, -SWIGLU_LIMIT, SWIGLU_LIMIT)
        act = hg * jax.nn.sigmoid(SWIGLU_ALPHA * hg) * (hl + 1.0)
        act_sc[j, rs, :] = act.astype(BF16)

    def down(rs, wd):
        act = jnp.concatenate([act_sc[c, rs, :] for c in range(nf)], axis=1)
        y = jnp.dot(act, wd, preferred_element_type=F32) + bd_ref[0]
        out_ref[rs, :] = _pack_bf16_pairs(y.astype(BF16))

    partial_tile = jnp.logical_and(nch > 0, nch < len(chunks))
    row_slices = [slice(a, b) for a, b in chunks]

    @pl.when(jnp.logical_and(j < nf, full))
    def _():
        gate_up(slice(None), wg_ref[0].astype(BF16), wu_ref[0].astype(BF16))

    @pl.when(jnp.logical_and(j < nf, partial_tile))
    def _():
        wg = wg_ref[0].astype(BF16)
        wu = wu_ref[0].astype(BF16)
        for c, rs in enumerate(row_slices[:-1]):
            pl.when(c < nch)(functools.partial(gate_up, rs, wg, wu))

    @pl.when(jnp.logical_and(j >= nf, full))
    def _():
        down(slice(None), wd_ref[0].astype(BF16))

    @pl.when(jnp.logical_and(j >= nf, partial_tile))
    def _():
        wd = wd_ref[0].astype(BF16)
        for c, rs in enumerate(row_slices[:-1]):
            pl.when(c < nch)(functools.partial(down, rs, wd))

    @pl.when(jnp.logical_and(j >= nf, nch < len(chunks)))
    def _():
        for c, (a, b) in enumerate(chunks):
            @pl.when(c >= nch)
            def _():
                out_ref[a:b, :] = jnp.zeros((b - a, out_ref.shape[1]), out_ref.dtype)


def _experts(tile_e, tile_src, tile_nch, xs, w_gu, b_gu3, w_down, b_down3, tm_e, chunks, tf):
    n_rows = xs.shape[0]
    n_tiles = n_rows // tm_e
    d = 2 * xs.shape[1]
    dff = w_down.shape[1]
    nf = dff // tf
    nn = d // tf
    assert nn == nf
    kern = functools.partial(_expert_kernel, chunks=chunks, nf=nf)

    def c1(i, j, ns):
        return jnp.where(ns[i] > 0, jnp.minimum(j, nf - 1), nf - 1)

    def c2(i, j, ns):
        return jnp.where(ns[i] > 0, jnp.maximum(j - nf, 0), nf - 1)

    return pl.pallas_call(
        kern,
        grid_spec=pltpu.PrefetchScalarGridSpec(
            num_scalar_prefetch=3,
            grid=(n_tiles, 2 * nf),
            in_specs=[
                pl.BlockSpec((tm_e, d // 2), lambda i, j, te, ts, ns: (ts[i], 0)),
                pl.BlockSpec((1, d, tf), lambda i, j, te, ts, ns: (te[i], 0, c1(i, j, ns))),
                pl.BlockSpec((1, d, tf), lambda i, j, te, ts, ns: (te[i], 0, nf + c1(i, j, ns))),
                pl.BlockSpec((1, 1, tf), lambda i, j, te, ts, ns: (te[i], 0, c1(i, j, ns))),
                pl.BlockSpec((1, 1, tf), lambda i, j, te, ts, ns: (te[i], 0, nf + c1(i, j, ns))),
                pl.BlockSpec((1, dff, tf), lambda i, j, te, ts, ns: (te[i], 0, c2(i, j, ns))),
                pl.BlockSpec((1, 1, tf), lambda i, j, te, ts, ns: (te[i], 0, c2(i, j, ns))),
            ],
            out_specs=pl.BlockSpec((tm_e, tf // 2), lambda i, j, te, ts, ns: (i, jnp.maximum(j - nf, 0))),
            scratch_shapes=[
                pltpu.VMEM((tm_e, d), BF16),
                pltpu.VMEM((nf, tm_e, tf), BF16),
            ],
        ),
        out_shape=jax.ShapeDtypeStruct((n_rows, d // 2), jnp.uint32),
        compiler_params=_cparams(("arbitrary", "arbitrary")),
    )(tile_e, tile_src, tile_nch, xs, w_gu, w_gu, b_gu3, b_gu3, w_down, b_down3)


def _combine_kernel(cur_ref, nxt_ref, yr_hbm, h_ref, gw_ref, gf_ref, out_ref, ysel, sems, *, tmc,
                    chunk, final_norm):
    i = pl.program_id(0)
    n = pl.num_programs(0)
    slot = i % 2
    groups = TOP_K * tmc // SUBLANES
    per_k = tmc // SUBLANES

    def issue(idx_ref, sl):
        def body(g, carry):
            for r in range(SUBLANES):
                pltpu.make_async_copy(yr_hbm.at[pl.ds(idx_ref[0, 0, g * SUBLANES + r], 1), :],
                                      ysel.at[sl, g, pl.ds(r, 1), :], sems.at[sl]).start()
            return carry
        lax.fori_loop(0, groups, body, 0)

    @pl.when(i == 0)
    def _():
        issue(cur_ref, 0)

    @pl.when(i + 1 < n)
    def _():
        issue(nxt_ref, 1 - slot)

    def wait_group(g, carry):
        pltpu.make_async_copy(yr_hbm.at[pl.ds(0, SUBLANES), :], ysel.at[slot, g], sems.at[slot]).wait()
        return carry
    lax.fori_loop(0, groups, wait_group, 0)

    gw = gw_ref[...]
    y = h_ref[...]
    half = chunk // 2
    for k in range(TOP_K):
        rows_k = ysel[slot, k * per_k:(k + 1) * per_k].reshape(tmc, ysel.shape[-1])
        lo, hi = _unpack_bf16_pairs(rows_k)
        pieces = []
        for c in range(lo.shape[1] // half):
            pieces += [lo[:, c * half:(c + 1) * half], hi[:, c * half:(c + 1) * half]]
        y = y + gw[:, k:k + 1] * jnp.concatenate(pieces, axis=1).astype(F32)
    if final_norm:
        ms = jnp.mean(y * y, axis=-1, keepdims=True)
        y = (y * lax.rsqrt(ms + EPS)) * gf_ref[...]
    out_ref[...] = y


def _combine(roa3, y_rows, h, gw, gf, tmc, chunk, final_norm):
    t, d = h.shape
    n = t // tmc
    kern = functools.partial(_combine_kernel, tmc=tmc, chunk=chunk, final_norm=final_norm)
    return pl.pallas_call(
        kern,
        grid=(n,),
        in_specs=[
            pl.BlockSpec((1, 1, TOP_K * tmc), lambda i: (i, 0, 0), memory_space=pltpu.SMEM),
            pl.BlockSpec((1, 1, TOP_K * tmc), lambda i: (jnp.minimum(i + 1, n - 1), 0, 0),
                         memory_space=pltpu.SMEM),
            pl.BlockSpec(memory_space=pl.ANY),
            pl.BlockSpec((tmc, d), lambda i: (i, 0)),
            pl.BlockSpec((tmc, LANES), lambda i: (i, 0)),
            pl.BlockSpec((1, d), lambda i: (0, 0)),
        ],
        out_specs=pl.BlockSpec((tmc, d), lambda i: (i, 0)),
        out_shape=jax.ShapeDtypeStruct((t, d), F32),
        scratch_shapes=[pltpu.VMEM((2, TOP_K * tmc // SUBLANES, SUBLANES, d // 2), jnp.uint32),
                        pltpu.SemaphoreType.DMA((2,))],
        compiler_params=_cparams(("arbitrary",)),
    )(roa3, roa3, y_rows, h, gw, gf)


def _routing(top_idx, tm_e, chunks):
    t = top_idx.shape[0]
    a = t * TOP_K
    flat_e = top_idx.reshape(a)
    onehot = (flat_e[:, None] == jnp.arange(N_EXPERTS, dtype=jnp.int32)[None, :]).astype(jnp.int32)
    csum = jnp.cumsum(onehot, axis=0)
    counts = csum[-1]
    rank = jnp.sum(onehot * csum, axis=1) - 1
    padded = (counts + tm_e - 1) // tm_e * tm_e
    pad_end = jnp.cumsum(padded)
    pad_start = pad_end - padded
    row_of_assign = (pad_start[flat_e] + rank).astype(jnp.int32)
    n_tiles = -(-a // tm_e) + N_EXPERTS
    tile_start = jnp.arange(n_tiles, dtype=jnp.int32) * tm_e
    n_used = pad_end[-1] // tm_e
    tile_e_raw = jnp.minimum(jnp.searchsorted(pad_end, tile_start, side='right'), N_EXPERTS - 1)
    used = tile_start < pad_end[-1]
    last = jnp.maximum(n_used - 1, 0)
    tile_src = jnp.where(used, jnp.arange(n_tiles, dtype=jnp.int32), last).astype(jnp.int32)
    tile_e = tile_e_raw[tile_src].astype(jnp.int32)
    valid_rows = jnp.clip(counts[tile_e_raw] - (tile_start - pad_start[tile_e_raw]), 0, tm_e)
    starts = jnp.asarray([a for a, _ in chunks], jnp.int32)
    tile_nch = jnp.sum(starts[None, :] < jnp.where(used, valid_rows, 0)[:, None], axis=1).astype(jnp.int32)
    bounds = (counts.astype(jnp.int32), pad_start.astype(jnp.int32), pad_end.astype(jnp.int32))
    return bounds, row_of_assign, tile_e, tile_src, tile_nch, n_tiles * tm_e


def kernel(x, positions, norm1_g, w_in, b_fgate, b_gate, lam_q1, lam_k1, lam_q2, lam_k2, subln_g,
           w_branch_fox, w_branch_diff, w_out, norm2_g, router_w, router_b, w_gu, b_gu, w_down,
           b_down, normf_g):
    b, s, d = x.shape
    t = b * s
    depth = norm1_g.shape[0]
    fox_w = FOX_HEADS * HEAD_DIM
    tm_a = _pick(t, 1024)
    tq = _pick(s, 512)
    tm_d = _pick(t, 256)
    tm_e, chunk, zsub, tf = 1152, 256, 128, 512
    chunks = tuple((a, min(a + chunk, tm_e)) for a in range(0, tm_e, chunk))
    tmc = _pick(t, 256)

    half = ROPE_DIM // 2
    inv_freq = ROPE_THETA ** (-jnp.arange(0, ROPE_DIM, 2, dtype=F32) / ROPE_DIM)
    zeros = jnp.zeros((LANES - ROPE_DIM,), F32)
    rope_tab = jnp.zeros((8, LANES), F32)
    rope_tab = rope_tab.at[0].set(jnp.concatenate([inv_freq, inv_freq, zeros]))
    rope_tab = rope_tab.at[1].set(jnp.concatenate([-jnp.ones((half,), F32), jnp.zeros((half,), F32), zeros]))
    rope_tab = rope_tab.at[2].set(jnp.concatenate([jnp.zeros((half,), F32), jnp.ones((half,), F32), zeros]))
    pos = positions.astype(F32).reshape(t, 1)

    h = x.reshape(t, d)
    for l in range(depth):
        wi = w_in[l]
        w_left = wi[:, :3 * fox_w].astype(BF16)
        w_right = wi[:, 3 * fox_w + FOX_HEADS:].astype(BF16)
        w_f = jnp.pad(wi[:, 3 * fox_w:3 * fox_w + FOX_HEADS], ((0, 0), (0, LANES - FOX_HEADS))).astype(BF16)
        qkv, gate, flog = _inproj(h, pos, norm1_g[l][None, :], w_left, w_right, w_f, rope_tab,
                                  b_gate[l][None, :], tm_a)

        fl = flog[:, :FOX_HEADS].reshape(b, s, FOX_HEADS).transpose(0, 2, 1).reshape(b * FOX_HEADS, s)
        bf = jnp.tile(b_fgate[l].astype(F32), b).reshape(b * FOX_HEADS, 1)
        c3 = _fgate(fl, bf).reshape(b * FOX_HEADS, 1, s)

        y_fox = _fox_attn(qkv, c3, b, s, tq)
        lam_init = 0.8 - 0.6 * math.exp(-0.3 * l)
        lam_p = jnp.stack([lam_q1[l], lam_k1[l], lam_q2[l], lam_k2[l]]).astype(F32)
        y_diff = _diff_attn(qkv, lam_p, subln_g[l][None, :].astype(F32), b, s, tq, lam_init)

        rw = jnp.pad(router_w[l], ((0, 0), (0, LANES - N_EXPERTS))).astype(BF16)
        rb = jnp.concatenate([router_b[l].astype(F32), jnp.full((LANES - N_EXPERTS,), NEG_BIG, F32)])[None, :]
        h, u2, idx128, gw128 = _merge(y_fox, y_diff, gate, h, w_branch_fox[l].astype(BF16),
                                      w_branch_diff[l].astype(BF16), w_out[l].astype(BF16),
                                      norm2_g[l][None, :], rw, rb, tm_d)

        bounds, roa, tile_e, tile_src, tile_nch, n_rows = _routing(idx128[:, :TOP_K], tm_e, chunks)
        roa3 = roa.reshape(t // tmc, tmc, TOP_K).transpose(0, 2, 1).reshape(t // tmc, 1, TOP_K * tmc)
        xs = _dispatch(*bounds, roa3, u2, n_rows, tmc, zsub)
        y_rows = _experts(tile_e, tile_src, tile_nch, xs, w_gu[l], b_gu[l][:, None, :],
                          w_down[l], b_down[l][:, None, :], tm_e, chunks, tf)
        h = _combine(roa3, y_rows, h, gw128, normf_g[None, :], tmc, tf, final_norm=(l == depth - 1))
    return h.reshape(b, s, d)
```
